```python
import jax, jax.numpy as jnp
from jax import lax
import numpy as np

D_MODEL = 2048
BATCH = 8
SEQ = 2048
DEPTH = 1

POOL_WIDTH = D_MODEL // 2
POOL_WINDOWS = (2, 4, 8, 16)
N_POOL_GROUPS = len(POOL_WINDOWS)
POOL_GROUP = POOL_WIDTH // N_POOL_GROUPS
HEAD_DIM = 128
N_Q_HEADS = (D_MODEL - POOL_WIDTH) // HEAD_DIM
N_KV_HEADS = 2
Q_PER_KV = N_Q_HEADS // N_KV_HEADS
ATTN_WIDTH = N_Q_HEADS * HEAD_DIM
KV_WIDTH = N_KV_HEADS * HEAD_DIM
IN_PROJ_WIDTH = POOL_WIDTH + ATTN_WIDTH + 2 * KV_WIDTH
MIX_WIDTH = POOL_WIDTH + ATTN_WIDTH
WINDOW = 128
BLOCK = 128
ROPE_THETA = 10000.0
N_GROUPS = 8
EXPERTS_PER_GROUP = 8
N_EXPERTS = N_GROUPS * EXPERTS_PER_GROUP
TOP_K = 2
D_EXPERT = 512
MOE_BLOCK = 128
EPS = 1e-6

kernel_name = 'hybrid_pool_wingqa_hiermoe_block'


def rmsnorm(x, w):
    xf = x.astype(jnp.float32)
    y = xf * lax.rsqrt(jnp.mean(xf * xf, axis=-1, keepdims=True) + EPS) * w.astype(jnp.float32)
    return y.astype(x.dtype)


def rope_tables(S):
    inv = ROPE_THETA ** (-jnp.arange(0, HEAD_DIM, 2, dtype=jnp.float32) / HEAD_DIM)
    ang = jnp.arange(S, dtype=jnp.float32)[:, None] * inv[None, :]
    return jnp.cos(ang)[:, None, :], jnp.sin(ang)[:, None, :]


def apply_rope(t, cos, sin):
    half = HEAD_DIM // 2
    tf = t.astype(jnp.float32)
    t1, t2 = tf[..., :half], tf[..., half:]
    out = jnp.concatenate([t1 * cos - t2 * sin, t2 * cos + t1 * sin], axis=-1)
    return out.astype(t.dtype)


def pool_mixer(u, w_pool, pool_scale):
    B, S, _ = u.shape
    uf = u.astype(jnp.float32).reshape(B, S, N_POOL_GROUPS, POOL_GROUP)
    cs = jnp.concatenate([jnp.zeros((B, 1, N_POOL_GROUPS, POOL_GROUP), jnp.float32),
                          jnp.cumsum(uf, axis=1)], axis=1)
    t = jnp.arange(S, dtype=jnp.int32)
    half = jnp.array(POOL_WINDOWS, jnp.int32) // 2
    lo = jnp.clip(t[:, None] - half[None, :], 0, S)
    hi = jnp.clip(t[:, None] + half[None, :], 0, S)
    gi = jnp.arange(N_POOL_GROUPS)[None, :]
    win_sum = cs[:, hi, gi, :] - cs[:, lo, gi, :]
    count = (hi - lo).astype(jnp.float32)[None, :, :, None]
    d = win_sum / count - uf
    y = jnp.einsum('bsgc,gcd->bsgd', d, w_pool.astype(jnp.float32))
    y = y * pool_scale.astype(jnp.float32).reshape(N_POOL_GROUPS, POOL_GROUP)
    return y.reshape(B, S, POOL_WIDTH).astype(u.dtype)


def windowed_gqa(q, k, v, sink):
    B, S = q.shape[0], q.shape[1]
    NB = S // BLOCK
    qb = q.reshape(B, NB, BLOCK, N_KV_HEADS, Q_PER_KV, HEAD_DIM)

    def band(t):
        tp = jnp.pad(t, ((0, 0), (BLOCK, BLOCK), (0, 0), (0, 0)))
        tb = tp.reshape(B, NB + 2, BLOCK, N_KV_HEADS, HEAD_DIM)
        return jnp.concatenate([tb[:, :-2], tb[:, 1:-1], tb[:, 2:]], axis=2)

    kb, vb = band(k), band(v)
    scores = jnp.einsum('bnqhgd,bnkhd->bnhgqk', qb, kb).astype(jnp.float32) * (HEAD_DIM ** -0.5)
    n = jnp.arange(NB)[:, None, None]
    qpos = n * BLOCK + jnp.arange(BLOCK)[None, :, None]
    kpos = (n - 1) * BLOCK + jnp.arange(3 * BLOCK)[None, None, :]
    valid = (jnp.abs(kpos - qpos) <= WINDOW) & (kpos >= 0) & (kpos < S)
    scores = jnp.where(valid[None, :, None, None], scores, -jnp.inf)
    sink_l = sink.astype(jnp.float32).reshape(N_KV_HEADS, Q_PER_KV)
    sink_col = jnp.broadcast_to(sink_l[None, None, :, :, None, None], scores.shape[:-1] + (1,))
    probs = jax.nn.softmax(jnp.concatenate([scores, sink_col], axis=-1), axis=-1)[..., :-1]
    out = jnp.einsum('bnhgqk,bnkhd->bnqhgd', probs.astype(v.dtype), vb)
    return out.reshape(B, S, ATTN_WIDTH)


def hier_moe(x, w_group, b_group, w_erouter, b_erouter, w_gate, w_up, w_down):
    B, S, D = x.shape
    T = B * S
    xt = x.reshape(T, D)
    g_prob = jax.nn.softmax((xt @ w_group).astype(jnp.float32) + b_group.astype(jnp.float32), axis=-1)
    g_w, g_idx = lax.top_k(g_prob, 1)
    e_logits = ((xt @ w_erouter).astype(jnp.float32) + b_erouter.astype(jnp.float32))
    e_logits = e_logits.reshape(T, N_GROUPS, EXPERTS_PER_GROUP)
    e_in_group = jnp.take_along_axis(e_logits, g_idx[:, :, None], axis=1)[:, 0]
    top_logit, top_j = lax.top_k(e_in_group, TOP_K)
    gate = g_w * jax.nn.softmax(top_logit, axis=-1)
    expert_id = g_idx * EXPERTS_PER_GROUP + top_j

    A = T * TOP_K
    e_flat = expert_id.reshape(A).astype(jnp.int32)
    tok_flat = jnp.repeat(jnp.arange(T, dtype=jnp.int32), TOP_K)
    w_flat = gate.reshape(A)
    order = jnp.argsort(e_flat)
    se, stok, sw = e_flat[order], tok_flat[order], w_flat[order]
    counts = jnp.zeros((N_EXPERTS,), jnp.int32).at[e_flat].add(1)
    padded = (counts + MOE_BLOCK - 1) // MOE_BLOCK * MOE_BLOCK
    starts = jnp.cumsum(counts) - counts
    pends = jnp.cumsum(padded)
    pstarts = pends - padded
    dest = pstarts[se] + jnp.arange(A, dtype=jnp.int32) - starts[se]
    n_blocks = -(-(A + N_EXPERTS * (MOE_BLOCK - 1)) // MOE_BLOCK)
    P = n_blocks * MOE_BLOCK
    slot_tok = jnp.full((P,), T, jnp.int32).at[dest].set(stok)
    slot_w = jnp.zeros((P,), jnp.float32).at[dest].set(sw)
    block_start = jnp.arange(n_blocks, dtype=jnp.int32) * MOE_BLOCK
    block_expert = jnp.minimum(jnp.searchsorted(pends, block_start, side='right'), N_EXPERTS - 1)
    x_pad = jnp.concatenate([xt, jnp.zeros((1, D), xt.dtype)], axis=0)
    xs = x_pad[slot_tok].reshape(n_blocks, MOE_BLOCK, D)

    def expert_block(args):
        xb, e = args
        hb = jax.nn.silu(xb @ w_gate[e]) * (xb @ w_up[e])
        return hb @ w_down[e]

    ys = lax.map(expert_block, (xs, block_expert)).reshape(P, D)
    ys = ys * slot_w[:, None].astype(ys.dtype)
    out = jnp.zeros((T + 1, D), ys.dtype).at[slot_tok].add(ys)[:T]
    return out.reshape(B, S, D).astype(x.dtype)


def setup_inputs(seed: int = 0) -> dict:
    key = jax.random.key(seed)
    ks = jax.random.split(key, 20)
    f32 = jnp.float32
    L = DEPTH
    nrm = lambda k, shape, s: jax.random.normal(k, shape, f32) * s
    return {
        'x': jax.random.normal(ks[0], (BATCH, SEQ, D_MODEL), f32),
        'norm_mix_w': 1.0 + nrm(ks[1], (L, D_MODEL), 0.02),
        'w_in': nrm(ks[2], (L, D_MODEL, IN_PROJ_WIDTH), D_MODEL ** -0.5),
        'w_pool': nrm(ks[3], (L, N_POOL_GROUPS, POOL_GROUP, POOL_GROUP), POOL_GROUP ** -0.5),
        'pool_scale': 1.0 + nrm(ks[4], (L, POOL_WIDTH), 0.02),
        'q_norm_w': 1.0 + nrm(ks[5], (L, HEAD_DIM), 0.02),
        'k_norm_w': 1.0 + nrm(ks[6], (L, HEAD_DIM), 0.02),
        'sink_logits': nrm(ks[7], (L, N_Q_HEADS), 0.5),
        'w_out': nrm(ks[8], (L, MIX_WIDTH, D_MODEL), MIX_WIDTH ** -0.5),
        'norm_ffn_w': 1.0 + nrm(ks[9], (L, D_MODEL), 0.02),
        'w_group_router': nrm(ks[10], (L, D_MODEL, N_GROUPS), D_MODEL ** -0.5),
        'b_group_router': nrm(ks[11], (L, N_GROUPS), 0.01),
        'w_expert_router': nrm(ks[12], (L, D_MODEL, N_EXPERTS), D_MODEL ** -0.5),
        'b_expert_router': nrm(ks[13], (L, N_EXPERTS), 0.01),
        'w_gate': nrm(ks[14], (L, N_EXPERTS, D_MODEL, D_EXPERT), D_MODEL ** -0.5),
        'w_up': nrm(ks[15], (L, N_EXPERTS, D_MODEL, D_EXPERT), D_MODEL ** -0.5),
        'w_down': nrm(ks[16], (L, N_EXPERTS, D_EXPERT, D_MODEL), D_EXPERT ** -0.5),
    }


def reference(x, norm_mix_w, w_in, w_pool, pool_scale, q_norm_w, k_norm_w, sink_logits, w_out,
              norm_ffn_w, w_group_router, b_group_router, w_expert_router, b_expert_router,
              w_gate, w_up, w_down):
    B, S, _ = x.shape
    cos, sin = rope_tables(S)
    o_q = POOL_WIDTH
    o_k = o_q + ATTN_WIDTH
    o_v = o_k + KV_WIDTH
    h = x
    for l in range(DEPTH):
        xn = rmsnorm(h, norm_mix_w[l])
        z = xn @ w_in[l]
        u_pool = z[..., :o_q]
        q = z[..., o_q:o_k].reshape(B, S, N_Q_HEADS, HEAD_DIM)
        k = z[..., o_k:o_v].reshape(B, S, N_KV_HEADS, HEAD_DIM)
        v = z[..., o_v:].reshape(B, S, N_KV_HEADS, HEAD_DIM)
        q = apply_rope(rmsnorm(q, q_norm_w[l]), cos, sin)
        k = apply_rope(rmsnorm(k, k_norm_w[l]), cos, sin)
        a_out = pool_mixer(u_pool, w_pool[l], pool_scale[l])
        b_out = windowed_gqa(q, k, v, sink_logits[l])
        h = h + jnp.concatenate([a_out, b_out], axis=-1) @ w_out[l]
        h = h + hier_moe(rmsnorm(h, norm_ffn_w[l]), w_group_router[l], b_group_router[l],
                         w_expert_router[l], b_expert_router[l], w_gate[l], w_up[l], w_down[l])
    return h
```

```python
import functools

import numpy as np
import jax
import jax.numpy as jnp
from jax import lax
from jax.experimental import pallas as pl
from jax.experimental.pallas import tpu as pltpu

D_MODEL = 2048
POOL_WIDTH = 1024
POOL_WINDOWS = (2, 4, 8, 16)
N_POOL_GROUPS = 4
POOL_GROUP = 256
HEAD_DIM = 128
N_Q_HEADS = 8
N_KV_HEADS = 2
Q_PER_KV = 4
ATTN_WIDTH = 1024
KV_WIDTH = 256
IN_PROJ_WIDTH = 2560
MIX_WIDTH = 2048
WINDOW = 128
BLOCK = 128
BAND = 3 * BLOCK
ROPE_THETA = 10000.0
N_GROUPS = 8
EXPERTS_PER_GROUP = 8
N_EXPERTS = 64
TOP_K = 2
D_EXPERT = 512
MOE_BLOCK = 128
EPS = 1e-6

LANES = 128
NEG_BIG = -1e30
VMEM_LIMIT = 56 * 1024 * 1024

BF16 = jnp.bfloat16
F32 = jnp.float32


def _cparams(n_axes):
    return pltpu.CompilerParams(dimension_semantics=("arbitrary",) * n_axes,
                                vmem_limit_bytes=VMEM_LIMIT)


IN_TM = 256


def _in_proj_kernel(x_ref, nw_ref, w_ref, qnw_ref, knw_ref, cos_ref, sin_ref,
                    u_ref, q_ref, k_ref, v_ref):
    x = x_ref[...]
    ms = jnp.mean(x * x, axis=-1, keepdims=True)
    xn = (x * lax.rsqrt(ms + EPS) * nw_ref[...]).astype(BF16)
    cos = cos_ref[...]
    sin = sin_ref[...]

    def head_norm_rope(t, w, scale):
        hms = jnp.mean(t * t, axis=-1, keepdims=True)
        t = t * lax.rsqrt(hms + EPS) * w
        t = t * cos + pltpu.roll(t, HEAD_DIM // 2, axis=1) * sin
        return t * scale

    u_ref[...] = jnp.dot(xn, w_ref[:, :POOL_WIDTH], preferred_element_type=F32).astype(BF16)
    zq = jnp.dot(xn, w_ref[:, POOL_WIDTH:POOL_WIDTH + ATTN_WIDTH], preferred_element_type=F32)
    qnw = qnw_ref[...]
    for h in range(N_Q_HEADS):
        sl = slice(h * HEAD_DIM, (h + 1) * HEAD_DIM)
        q_ref[:, sl] = head_norm_rope(zq[:, sl], qnw, HEAD_DIM ** -0.5).astype(BF16)
    o_k = POOL_WIDTH + ATTN_WIDTH
    zk = jnp.dot(xn, w_ref[:, o_k:o_k + KV_WIDTH], preferred_element_type=F32)
    knw = knw_ref[...]
    for h in range(N_KV_HEADS):
        sl = slice(h * HEAD_DIM, (h + 1) * HEAD_DIM)
        k_ref[:, sl] = head_norm_rope(zk[:, sl], knw, 1.0).astype(BF16)
    v_ref[...] = jnp.dot(xn, w_ref[:, o_k + KV_WIDTH:], preferred_element_type=F32).astype(BF16)


def _in_proj(x2, nw, w_in, qnw, knw, cos, sin, seq):
    T = x2.shape[0]
    tm = IN_TM
    pos_blocks = seq // tm
    full = lambda shape: pl.BlockSpec(shape, lambda i: (0,) * len(shape))
    return pl.pallas_call(
        _in_proj_kernel,
        grid=(T // tm,),
        in_specs=[
            pl.BlockSpec((tm, D_MODEL), lambda i: (i, 0)),
            full((1, D_MODEL)),
            full((D_MODEL, IN_PROJ_WIDTH)),
            full((1, HEAD_DIM)),
            full((1, HEAD_DIM)),
            pl.BlockSpec((tm, HEAD_DIM), lambda i: (i % pos_blocks, 0)),
            pl.BlockSpec((tm, HEAD_DIM), lambda i: (i % pos_blocks, 0)),
        ],
        out_specs=[
            pl.BlockSpec((tm, POOL_WIDTH), lambda i: (i, 0)),
            pl.BlockSpec((tm, ATTN_WIDTH), lambda i: (i, 0)),
            pl.BlockSpec((tm, KV_WIDTH), lambda i: (i, 0)),
            pl.BlockSpec((tm, KV_WIDTH), lambda i: (i, 0)),
        ],
        out_shape=[
            jax.ShapeDtypeStruct((T, POOL_WIDTH), BF16),
            jax.ShapeDtypeStruct((T, ATTN_WIDTH), BF16),
            jax.ShapeDtypeStruct((T, KV_WIDTH), BF16),
            jax.ShapeDtypeStruct((T, KV_WIDTH), BF16),
        ],
        compiler_params=_cparams(1),
        name="in_proj",
    )(x2, nw, w_in, qnw, knw, cos, sin)


MIX_TQ = 256


def _band_constants(seq):
    nb = seq // BLOCK
    coef = np.zeros((3, N_POOL_GROUPS, BLOCK, BAND), np.float32)
    inv_count = np.zeros((3, N_POOL_GROUPS, BLOCK, 1), np.float32)
    bias = np.zeros((3, BLOCK, BAND), np.float32)
    for kind, n in enumerate((0, 1, nb - 1)):
        start = min(max((n - 1) * BLOCK, 0), seq - BAND)
        t = n * BLOCK + np.arange(BLOCK)[:, None]
        s = start + np.arange(BAND)[None, :]
        bias[kind] = np.where(np.abs(s - t) <= WINDOW, 0.0, NEG_BIG)
        for g, win in enumerate(POOL_WINDOWS):
            half = win // 2
            lo = np.clip(t - half, 0, seq)
            hi = np.clip(t + half, 0, seq)
            count = (hi - lo).astype(np.float32)
            inside = ((s >= lo) & (s < hi)).astype(np.float32)
            coef[kind, g] = inside - count * (s == t)
            inv_count[kind, g] = 1.0 / count
    return coef, inv_count, bias


def _mixers_kernel(sink_ref, u_ref, q_ref, k_ref, v_ref, coef_ref, invc_ref, bias_ref,
                   wpool_ref, pscale_ref, ab_ref, *, seq):
    nb = seq // BLOCK
    j = pl.program_id(1)
    for r in range(MIX_TQ // BLOCK):
        n = j * (MIX_TQ // BLOCK) + r
        start = pl.multiple_of(jnp.clip((n - 1) * BLOCK, 0, seq - BAND), BLOCK)
        kind = jnp.where(n == 0, 0, jnp.where(n == nb - 1, 2, 1))
        rows = slice(r * BLOCK, (r + 1) * BLOCK)

        for g in range(N_POOL_GROUPS):
            cols = slice(g * POOL_GROUP, (g + 1) * POOL_GROUP)
            ub = u_ref[0, pl.ds(start, BAND), cols]
            d = jnp.dot(coef_ref[kind, g], ub, preferred_element_type=F32) * invc_ref[kind, g]
            y = jnp.dot(d.astype(BF16), wpool_ref[g], preferred_element_type=F32)
            ab_ref[0, rows, cols] = (y * pscale_ref[:, cols]).astype(BF16)

        bias = bias_ref[kind]
        for hk in range(N_KV_HEADS):
            kcols = slice(hk * HEAD_DIM, (hk + 1) * HEAD_DIM)
            kb = k_ref[0, pl.ds(start, BAND), kcols]
            vb = v_ref[0, pl.ds(start, BAND), kcols]
            heads = [hk * Q_PER_KV + g for g in range(Q_PER_KV)]
            qs = jnp.concatenate(
                [q_ref[0, rows, h * HEAD_DIM:(h + 1) * HEAD_DIM] for h in heads], axis=0)
            s = lax.dot_general(qs, kb, (((1,), (1,)), ((), ())), preferred_element_type=F32)
            for gi, h in enumerate(heads):
                sh = s[gi * BLOCK:(gi + 1) * BLOCK] + bias
                sink = sink_ref[h]
                m = jnp.maximum(jnp.max(sh, axis=-1, keepdims=True), sink)
                p = jnp.exp(sh - m)
                denom = jnp.sum(p, axis=-1, keepdims=True) + jnp.exp(sink - m)
                o = jnp.dot(p.astype(BF16), vb, preferred_element_type=F32) / denom
                ocols = slice(POOL_WIDTH + h * HEAD_DIM, POOL_WIDTH + (h + 1) * HEAD_DIM)
                ab_ref[0, rows, ocols] = o.astype(BF16)


def _mixers(sink, u, q, k, v, coef, invc, bias, wpool, pscale):
    B, S, _ = u.shape
    tq = MIX_TQ
    full = lambda shape: pl.BlockSpec(shape, lambda b, j: (0,) * len(shape))
    return pl.pallas_call(
        functools.partial(_mixers_kernel, seq=S),
        grid=(B, S // tq),
        in_specs=[
            pl.BlockSpec(memory_space=pltpu.SMEM),
            pl.BlockSpec((1, S, POOL_WIDTH), lambda b, j: (b, 0, 0)),
            pl.BlockSpec((1, tq, ATTN_WIDTH), lambda b, j: (b, j, 0)),
            pl.BlockSpec((1, S, KV_WIDTH), lambda b, j: (b, 0, 0)),
            pl.BlockSpec((1, S, KV_WIDTH), lambda b, j: (b, 0, 0)),
            full((3, N_POOL_GROUPS, BLOCK, BAND)),
            full((3, N_POOL_GROUPS, BLOCK, 1)),
            full((3, BLOCK, BAND)),
            full((N_POOL_GROUPS, POOL_GROUP, POOL_GROUP)),
            full((1, POOL_WIDTH)),
        ],
        out_specs=pl.BlockSpec((1, tq, MIX_WIDTH), lambda b, j: (b, j, 0)),
        out_shape=jax.ShapeDtypeStruct((B, S, MIX_WIDTH), BF16),
        compiler_params=_cparams(2),
        name="mixers",
    )(sink, u, q, k, v, coef, invc, bias, wpool, pscale)


OUT_TM = 256
ROUTE_E0, ROUTE_E1, ROUTE_W0, ROUTE_W1 = 0, 1, 2, 3


def _out_proj_kernel(ab_ref, x_ref, w_ref, nw_ref, wr_ref, br_ref, h_ref, xn_ref, route_ref):
    h = x_ref[...] + jnp.dot(ab_ref[...], w_ref[...], preferred_element_type=F32)
    h_ref[...] = h
    ms = jnp.mean(h * h, axis=-1, keepdims=True)
    xn = h * lax.rsqrt(ms + EPS) * nw_ref[...]
    xn_ref[...] = xn
    logits = jnp.dot(xn.astype(BF16), wr_ref[...], preferred_element_type=F32) + br_ref[...]

    lane = lax.broadcasted_iota(jnp.int32, logits.shape, 1)

    def first_argmax(vals):
        m = jnp.max(vals, axis=-1, keepdims=True)
        idx = jnp.min(jnp.where(vals == m, lane, LANES), axis=-1, keepdims=True)
        return m, idx

    gl = jnp.where(lane < N_GROUPS, logits, NEG_BIG)
    gmax, gidx = first_argmax(gl)
    gsum = jnp.sum(jnp.where(lane < N_GROUPS, jnp.exp(gl - gmax), 0.0), axis=-1, keepdims=True)
    g_w = 1.0 / gsum
    e_lo = N_GROUPS + gidx * EXPERTS_PER_GROUP
    el = jnp.where((lane >= e_lo) & (lane < e_lo + EXPERTS_PER_GROUP), logits, NEG_BIG)
    m1, i1 = first_argmax(el)
    m2, i2 = first_argmax(jnp.where(lane == i1, NEG_BIG, el))
    t = jnp.exp(m2 - m1)
    p1 = 1.0 / (1.0 + t)
    p2 = t * p1
    e1 = (i1 - N_GROUPS).astype(F32)
    e2 = (i2 - N_GROUPS).astype(F32)
    route_ref[...] = jnp.where(
        lane == ROUTE_E0, e1,
        jnp.where(lane == ROUTE_E1, e2,
                  jnp.where(lane == ROUTE_W0, g_w * p1,
                            jnp.where(lane == ROUTE_W1, g_w * p2, 0.0))))


def _out_proj(ab, x2, w_out, nw, w_router, b_router):
    T = x2.shape[0]
    tm = OUT_TM
    full = lambda shape: pl.BlockSpec(shape, lambda i: (0,) * len(shape))
    row = lambda width: pl.BlockSpec((tm, width), lambda i: (i, 0))
    return pl.pallas_call(
        _out_proj_kernel,
        grid=(T // tm,),
        in_specs=[row(MIX_WIDTH), row(D_MODEL), full((MIX_WIDTH, D_MODEL)), full((1, D_MODEL)),
                  full((D_MODEL, LANES)), full((1, LANES))],
        out_specs=[row(D_MODEL), row(D_MODEL), row(LANES)],
        out_shape=[jax.ShapeDtypeStruct((T, D_MODEL), F32),
                   jax.ShapeDtypeStruct((T, D_MODEL), F32),
                   jax.ShapeDtypeStruct((T, LANES), F32)],
        compiler_params=_cparams(1),
        name="out_proj",
    )(ab, x2, w_out, nw, w_router, b_router)


def _row_gather(src_hbm, idx_ref, dst_ref, sem, n_rows, start):
    def body(r, carry):
        cp = pltpu.make_async_copy(src_hbm.at[pl.ds(idx_ref[r], 1)], dst_ref.at[pl.ds(r, 1)], sem)
        if start:
            cp.start()
        else:
            cp.wait()
        return carry
    lax.fori_loop(0, n_rows, body, 0)


def _experts_kernel(be_ref, nused_ref, tok_cur_ref, tok_next_ref, xn_hbm, wg_ref, wu_ref, wd_ref,
                    ys_ref, xbuf, sem):
    b = pl.program_id(0)
    n_used = nused_ref[0]
    slot = b % 2

    @pl.when(b == 0)
    def _():
        _row_gather(xn_hbm, tok_cur_ref.at[0, 0], xbuf.at[0], sem.at[0], MOE_BLOCK, True)

    @pl.when(b + 1 < n_used)
    def _():
        _row_gather(xn_hbm, tok_next_ref.at[0, 0], xbuf.at[1 - slot], sem.at[1 - slot],
                    MOE_BLOCK, True)

    @pl.when(b < n_used)
    def _():
        _row_gather(xn_hbm, tok_cur_ref.at[0, 0], xbuf.at[slot], sem.at[slot], MOE_BLOCK, False)
        xb = xbuf[slot].astype(BF16)
        g = jnp.dot(xb, wg_ref[0].astype(BF16), preferred_element_type=F32)
        u = jnp.dot(xb, wu_ref[0].astype(BF16), preferred_element_type=F32)
        hmid = (g * jax.nn.sigmoid(g) * u).astype(BF16)
        ys_ref[...] = jnp.dot(hmid, wd_ref[0].astype(BF16), preferred_element_type=F32)

    @pl.when(b >= n_used)
    def _():
        ys_ref[...] = jnp.zeros_like(ys_ref)


def _experts(block_expert, n_used, slot_tok3, xn, w_gate, w_up, w_down):
    n_blocks = slot_tok3.shape[0]
    P = n_blocks * MOE_BLOCK
    tok_spec = lambda fn: pl.BlockSpec((1, 1, MOE_BLOCK), fn, memory_space=pltpu.SMEM)
    grid_spec = pltpu.PrefetchScalarGridSpec(
        num_scalar_prefetch=2,
        grid=(n_blocks,),
        in_specs=[
            tok_spec(lambda b, be, nu: (b, 0, 0)),
            tok_spec(lambda b, be, nu: (jnp.minimum(b + 1, n_blocks - 1), 0, 0)),
            pl.BlockSpec(memory_space=pl.ANY),
            pl.BlockSpec((1, D_MODEL, D_EXPERT), lambda b, be, nu: (be[b], 0, 0)),
            pl.BlockSpec((1, D_MODEL, D_EXPERT), lambda b, be, nu: (be[b], 0, 0)),
            pl.BlockSpec((1, D_EXPERT, D_MODEL), lambda b, be, nu: (be[b], 0, 0)),
        ],
        out_specs=pl.BlockSpec((MOE_BLOCK, D_MODEL), lambda b, be, nu: (b, 0)),
        scratch_shapes=[pltpu.VMEM((2, MOE_BLOCK, D_MODEL), F32),
                        pltpu.SemaphoreType.DMA((2,))],
    )
    return pl.pallas_call(
        _experts_kernel,
        grid_spec=grid_spec,
        out_shape=jax.ShapeDtypeStruct((P, D_MODEL), F32),
        compiler_params=_cparams(1),
        name="experts",
    )(block_expert, n_used, slot_tok3, slot_tok3, xn, w_gate, w_up, w_down)


CMB_TM = 128


def _combine_kernel(d_cur_ref, d_next_ref, h_ref, route_ref, ys_hbm, o_ref, ybuf, sem):
    i = pl.program_id(0)
    n = pl.num_programs(0)
    slot = i % 2
    n_rows = TOP_K * CMB_TM

    @pl.when(i == 0)
    def _():
        _row_gather(ys_hbm, d_cur_ref.at[0, 0], ybuf.at[0], sem.at[0], n_rows, True)

    @pl.when(i + 1 < n)
    def _():
        _row_gather(ys_hbm, d_next_ref.at[0, 0], ybuf.at[1 - slot], sem.at[1 - slot], n_rows, True)

    _row_gather(ys_hbm, d_cur_ref.at[0, 0], ybuf.at[slot], sem.at[slot], n_rows, False)
    route = route_ref[...]
    w0 = route[:, ROUTE_W0:ROUTE_W0 + 1]
    w1 = route[:, ROUTE_W1:ROUTE_W1 + 1]
    y0 = ybuf[slot, :CMB_TM]
    y1 = ybuf[slot, CMB_TM:]
    o_ref[...] = h_ref[...] + (w0 * y0 + w1 * y1)


def _combine(dest3, h, route, ys):
    T = h.shape[0]
    tm = CMB_TM
    n = T // tm
    d_spec = lambda fn: pl.BlockSpec((1, 1, TOP_K * tm), fn, memory_space=pltpu.SMEM)
    return pl.pallas_call(
        _combine_kernel,
        grid=(n,),
        in_specs=[
            d_spec(lambda i: (i, 0, 0)),
            d_spec(lambda i: (jnp.minimum(i + 1, n - 1), 0, 0)),
            pl.BlockSpec((tm, D_MODEL), lambda i: (i, 0)),
            pl.BlockSpec((tm, LANES), lambda i: (i, 0)),
            pl.BlockSpec(memory_space=pl.ANY),
        ],
        out_specs=pl.BlockSpec((tm, D_MODEL), lambda i: (i, 0)),
        out_shape=jax.ShapeDtypeStruct((T, D_MODEL), F32),
        scratch_shapes=[pltpu.VMEM((2, TOP_K * tm, D_MODEL), F32),
                        pltpu.SemaphoreType.DMA((2,))],
        compiler_params=_cparams(1),
        name="combine",
    )(dest3, dest3, h, route, ys)


def _dispatch_plan(route, T):
    e = route[:, :TOP_K].astype(jnp.int32)
    A = T * TOP_K
    e_flat = e.reshape(A)
    onehot = (e_flat[:, None] == jnp.arange(N_EXPERTS, dtype=jnp.int32)[None, :]).astype(jnp.int32)
    csum = jnp.cumsum(onehot, axis=0)
    rank = jnp.sum(csum * onehot, axis=1) - 1
    counts = csum[-1]
    padded = (counts + MOE_BLOCK - 1) // MOE_BLOCK * MOE_BLOCK
    pends = jnp.cumsum(padded)
    pstarts = pends - padded
    dest = pstarts[e_flat] + rank
    n_blocks = -(-(A + N_EXPERTS * (MOE_BLOCK - 1)) // MOE_BLOCK)
    P = n_blocks * MOE_BLOCK
    tok_flat = jnp.arange(A, dtype=jnp.int32) // TOP_K
    slot_tok = jnp.zeros((P,), jnp.int32).at[dest].set(tok_flat)
    block_start = jnp.arange(n_blocks, dtype=jnp.int32) * MOE_BLOCK
    block_expert = jnp.minimum(jnp.searchsorted(pends, block_start, side='right'),
                               N_EXPERTS - 1).astype(jnp.int32)
    n_used = (pends[-1] // MOE_BLOCK).astype(jnp.int32).reshape(1)
    return dest.reshape(T, TOP_K), slot_tok.reshape(n_blocks, 1, MOE_BLOCK), block_expert, n_used


def _rope_tables(S):
    inv = ROPE_THETA ** (-jnp.arange(0, HEAD_DIM, 2, dtype=F32) / HEAD_DIM)
    ang = jnp.arange(S, dtype=F32)[:, None] * inv[None, :]
    cos, sin = jnp.cos(ang), jnp.sin(ang)
    return jnp.concatenate([cos, cos], axis=-1), jnp.concatenate([-sin, sin], axis=-1)


def kernel(x, norm_mix_w, w_in, w_pool, pool_scale, q_norm_w, k_norm_w, sink_logits, w_out,
           norm_ffn_w, w_group_router, b_group_router, w_expert_router, b_expert_router,
           w_gate, w_up, w_down):
    B, S, D = x.shape
    T = B * S
    depth = w_in.shape[0]
    cos, sin = _rope_tables(S)
    coef_np, invc_np, bias_np = _band_constants(S)
    coef = jnp.asarray(coef_np, BF16)
    invc = jnp.asarray(invc_np, F32)
    bias = jnp.asarray(bias_np, F32)
    pad_lanes = LANES - N_GROUPS - N_EXPERTS

    h = x.reshape(T, D)
    for l in range(depth):
        u, q, k, v = _in_proj(h, norm_mix_w[l].reshape(1, D), w_in[l].astype(BF16),
                              q_norm_w[l].reshape(1, HEAD_DIM), k_norm_w[l].reshape(1, HEAD_DIM),
                              cos, sin, S)
        ab = _mixers(sink_logits[l], u.reshape(B, S, -1), q.reshape(B, S, -1),
                     k.reshape(B, S, -1), v.reshape(B, S, -1), coef, invc, bias,
                     w_pool[l].astype(BF16), pool_scale[l].reshape(1, POOL_WIDTH))
        w_router = jnp.concatenate(
            [w_group_router[l], w_expert_router[l], jnp.zeros((D, pad_lanes), F32)], axis=1)
        b_router = jnp.concatenate(
            [b_group_router[l], b_expert_router[l], jnp.zeros((pad_lanes,), F32)]).reshape(1, LANES)
        hmix, xn, route = _out_proj(ab.reshape(T, MIX_WIDTH), h, w_out[l].astype(BF16),
                                    norm_ffn_w[l].reshape(1, D), w_router.astype(BF16), b_router)
        dest, slot_tok3, block_expert, n_used = _dispatch_plan(route, T)
        ys = _experts(block_expert, n_used, slot_tok3, xn, w_gate[l], w_up[l], w_down[l])
        dest3 = jnp.transpose(dest.reshape(T // CMB_TM, CMB_TM, TOP_K), (0, 2, 1)).reshape(
            T // CMB_TM, 1, TOP_K * CMB_TM)
        h = _combine(dest3, hmix, route, ys)
    return h.reshape(B, S, D)
```

```python
import functools

import numpy as np
import jax
import jax.numpy as jnp
from jax import lax
from jax.experimental import pallas as pl
from jax.experimental.pallas import tpu as pltpu

D_MODEL = 2048
POOL_WIDTH = 1024
POOL_WINDOWS = (2, 4, 8, 16)
N_POOL_GROUPS = 4
POOL_GROUP = 256
HEAD_DIM = 128
N_Q_HEADS = 8
N_KV_HEADS = 2
Q_PER_KV = 4
ATTN_WIDTH = 1024
KV_WIDTH = 256
IN_PROJ_WIDTH = 2560
MIX_WIDTH = 2048
WINDOW = 128
BLOCK = 128
BAND = 3 * BLOCK
ROPE_THETA = 10000.0
N_GROUPS = 8
EXPERTS_PER_GROUP = 8
N_EXPERTS = 64
TOP_K = 2
D_EXPERT = 512
MOE_BLOCK = 128
EPS = 1e-6

LANES = 128
SUBLANES = 8
HALF = D_MODEL // 2
PACK_CHUNKS = HALF // LANES
NEG_BIG = -1e30
VMEM_LIMIT = 56 * 1024 * 1024

BF16 = jnp.bfloat16
F32 = jnp.float32
U32 = jnp.uint32
HI_MASK = 0xFFFF0000


def _cparams(n_axes):
    return pltpu.CompilerParams(dimension_semantics=("arbitrary",) * n_axes,
                                vmem_limit_bytes=VMEM_LIMIT)


def _pack_pair(lo, hi):
    lo_bits = lax.bitcast_convert_type(lo.astype(BF16).astype(F32), U32) >> 16
    hi_bits = lax.bitcast_convert_type(hi.astype(BF16).astype(F32), U32) & jnp.uint32(HI_MASK)
    return lo_bits | hi_bits


def _unpack_pair(words):
    lo = lax.bitcast_convert_type(words << 16, F32)
    hi = lax.bitcast_convert_type(words & jnp.uint32(HI_MASK), F32)
    return lo, hi


IN_TM = 256


def _in_proj_kernel(x_ref, nw_ref, w_ref, qnw_ref, knw_ref, cos_ref, sin_ref,
                    u_ref, q_ref, k_ref, v_ref):
    x = x_ref[...]
    ms = jnp.mean(x * x, axis=-1, keepdims=True)
    xn = (x * lax.rsqrt(ms + EPS) * nw_ref[...]).astype(BF16)
    cos = cos_ref[...]
    sin = sin_ref[...]

    def head_norm_rope(t, w, scale):
        hms = jnp.mean(t * t, axis=-1, keepdims=True)
        t = t * lax.rsqrt(hms + EPS) * w
        t = t * cos + pltpu.roll(t, HEAD_DIM // 2, axis=1) * sin
        return t * scale

    u_ref[...] = jnp.dot(xn, w_ref[:, :POOL_WIDTH], preferred_element_type=F32).astype(BF16)
    zq = jnp.dot(xn, w_ref[:, POOL_WIDTH:POOL_WIDTH + ATTN_WIDTH], preferred_element_type=F32)
    qnw = qnw_ref[...]
    for h in range(N_Q_HEADS):
        sl = slice(h * HEAD_DIM, (h + 1) * HEAD_DIM)
        q_ref[:, sl] = head_norm_rope(zq[:, sl], qnw, HEAD_DIM ** -0.5).astype(BF16)
    o_k = POOL_WIDTH + ATTN_WIDTH
    zk = jnp.dot(xn, w_ref[:, o_k:o_k + KV_WIDTH], preferred_element_type=F32)
    knw = knw_ref[...]
    for h in range(N_KV_HEADS):
        sl = slice(h * HEAD_DIM, (h + 1) * HEAD_DIM)
        k_ref[:, sl] = head_norm_rope(zk[:, sl], knw, 1.0).astype(BF16)
    v_ref[...] = jnp.dot(xn, w_ref[:, o_k + KV_WIDTH:], preferred_element_type=F32).astype(BF16)


def _in_proj(x2, nw, w_in, qnw, knw, cos, sin, seq):
    T = x2.shape[0]
    tm = IN_TM
    pos_blocks = seq // tm
    full = lambda shape: pl.BlockSpec(shape, lambda i: (0,) * len(shape))
    return pl.pallas_call(
        _in_proj_kernel,
        grid=(T // tm,),
        in_specs=[
            pl.BlockSpec((tm, D_MODEL), lambda i: (i, 0)),
            full((1, D_MODEL)),
            full((D_MODEL, IN_PROJ_WIDTH)),
            full((1, HEAD_DIM)),
            full((1, HEAD_DIM)),
            pl.BlockSpec((tm, HEAD_DIM), lambda i: (i % pos_blocks, 0)),
            pl.BlockSpec((tm, HEAD_DIM), lambda i: (i % pos_blocks, 0)),
        ],
        out_specs=[
            pl.BlockSpec((tm, POOL_WIDTH), lambda i: (i, 0)),
            pl.BlockSpec((tm, ATTN_WIDTH), lambda i: (i, 0)),
            pl.BlockSpec((tm, KV_WIDTH), lambda i: (i, 0)),
            pl.BlockSpec((tm, KV_WIDTH), lambda i: (i, 0)),
        ],
        out_shape=[
            jax.ShapeDtypeStruct((T, POOL_WIDTH), BF16),
            jax.ShapeDtypeStruct((T, ATTN_WIDTH), BF16),
            jax.ShapeDtypeStruct((T, KV_WIDTH), BF16),
            jax.ShapeDtypeStruct((T, KV_WIDTH), BF16),
        ],
        compiler_params=_cparams(1),
        name="in_proj",
    )(x2, nw, w_in, qnw, knw, cos, sin)


MIX_TQ = 256


def _band_constants(seq):
    nb = seq // BLOCK
    coef = np.zeros((3, N_POOL_GROUPS, BLOCK, BAND), np.float32)
    inv_count = np.zeros((3, N_POOL_GROUPS, BLOCK, 1), np.float32)
    bias = np.zeros((3, BLOCK, BAND), np.float32)
    for kind, n in enumerate((0, 1, nb - 1)):
        start = min(max((n - 1) * BLOCK, 0), seq - BAND)
        t = n * BLOCK + np.arange(BLOCK)[:, None]
        s = start + np.arange(BAND)[None, :]
        bias[kind] = np.where(np.abs(s - t) <= WINDOW, 0.0, NEG_BIG)
        for g, win in enumerate(POOL_WINDOWS):
            half = win // 2
            lo = np.clip(t - half, 0, seq)
            hi = np.clip(t + half, 0, seq)
            count = (hi - lo).astype(np.float32)
            inside = ((s >= lo) & (s < hi)).astype(np.float32)
            coef[kind, g] = inside - count * (s == t)
            inv_count[kind, g] = 1.0 / count
    return coef, inv_count, bias


def _mixers_kernel(sink_ref, u_ref, q_ref, k_ref, v_ref, coef_ref, invc_ref, bias_ref,
                   wpool_ref, pscale_ref, ab_ref, *, seq):
    nb = seq // BLOCK
    j = pl.program_id(1)
    for r in range(MIX_TQ // BLOCK):
        n = j * (MIX_TQ // BLOCK) + r
        start = pl.multiple_of(jnp.clip((n - 1) * BLOCK, 0, seq - BAND), BLOCK)
        kind = jnp.where(n == 0, 0, jnp.where(n == nb - 1, 2, 1))
        rows = slice(r * BLOCK, (r + 1) * BLOCK)

        for g in range(N_POOL_GROUPS):
            cols = slice(g * POOL_GROUP, (g + 1) * POOL_GROUP)
            ub = u_ref[0, pl.ds(start, BAND), cols]
            d = jnp.dot(coef_ref[kind, g], ub, preferred_element_type=F32) * invc_ref[kind, g]
            y = jnp.dot(d.astype(BF16), wpool_ref[g], preferred_element_type=F32)
            ab_ref[0, rows, cols] = (y * pscale_ref[:, cols]).astype(BF16)

        bias = bias_ref[kind]
        for hk in range(N_KV_HEADS):
            kcols = slice(hk * HEAD_DIM, (hk + 1) * HEAD_DIM)
            kb = k_ref[0, pl.ds(start, BAND), kcols]
            vb = v_ref[0, pl.ds(start, BAND), kcols]
            heads = [hk * Q_PER_KV + g for g in range(Q_PER_KV)]
            qs = jnp.concatenate(
                [q_ref[0, rows, h * HEAD_DIM:(h + 1) * HEAD_DIM] for h in heads], axis=0)
            s = lax.dot_general(qs, kb, (((1,), (1,)), ((), ())), preferred_element_type=F32)
            for gi, h in enumerate(heads):
                sh = s[gi * BLOCK:(gi + 1) * BLOCK] + bias
                sink = sink_ref[h]
                m = jnp.maximum(jnp.max(sh, axis=-1, keepdims=True), sink)
                p = jnp.exp(sh - m)
                denom = jnp.sum(p, axis=-1, keepdims=True) + jnp.exp(sink - m)
                o = jnp.dot(p.astype(BF16), vb, preferred_element_type=F32) / denom
                ocols = slice(POOL_WIDTH + h * HEAD_DIM, POOL_WIDTH + (h + 1) * HEAD_DIM)
                ab_ref[0, rows, ocols] = o.astype(BF16)


def _mixers(sink, u, q, k, v, coef, invc, bias, wpool, pscale):
    B, S, _ = u.shape
    tq = MIX_TQ
    full = lambda shape: pl.BlockSpec(shape, lambda b, j: (0,) * len(shape))
    return pl.pallas_call(
        functools.partial(_mixers_kernel, seq=S),
        grid=(B, S // tq),
        in_specs=[
            pl.BlockSpec(memory_space=pltpu.SMEM),
            pl.BlockSpec((1, S, POOL_WIDTH), lambda b, j: (b, 0, 0)),
            pl.BlockSpec((1, tq, ATTN_WIDTH), lambda b, j: (b, j, 0)),
            pl.BlockSpec((1, S, KV_WIDTH), lambda b, j: (b, 0, 0)),
            pl.BlockSpec((1, S, KV_WIDTH), lambda b, j: (b, 0, 0)),
            full((3, N_POOL_GROUPS, BLOCK, BAND)),
            full((3, N_POOL_GROUPS, BLOCK, 1)),
            full((3, BLOCK, BAND)),
            full((N_POOL_GROUPS, POOL_GROUP, POOL_GROUP)),
            full((1, POOL_WIDTH)),
        ],
        out_specs=pl.BlockSpec((1, tq, MIX_WIDTH), lambda b, j: (b, j, 0)),
        out_shape=jax.ShapeDtypeStruct((B, S, MIX_WIDTH), BF16),
        compiler_params=_cparams(2),
        name="mixers",
    )(sink, u, q, k, v, coef, invc, bias, wpool, pscale)


OUT_TM = 256
ROUTE_E0, ROUTE_E1, ROUTE_W0, ROUTE_W1 = 0, 1, 2, 3


def _out_proj_kernel(ab_ref, x_ref, w_ref, nw_ref, wr_ref, br_ref,
                     h_ref, xn_ref, route_ref, counts_ref):
    h = x_ref[...] + jnp.dot(ab_ref[...], w_ref[...], preferred_element_type=F32)
    h_ref[...] = h
    ms = jnp.mean(h * h, axis=-1, keepdims=True)
    xn = h * lax.rsqrt(ms + EPS) * nw_ref[...]
    packed = _pack_pair(xn[:, :HALF], xn[:, HALF:])
    for c in range(PACK_CHUNKS):
        xn_ref[pl.ds(c, OUT_TM, stride=SUBLANES), :] = packed[:, c * LANES:(c + 1) * LANES]
    logits = jnp.dot(xn.astype(BF16), wr_ref[...], preferred_element_type=F32) + br_ref[...]

    lane = lax.broadcasted_iota(jnp.int32, logits.shape, 1)

    def first_argmax(vals):
        m = jnp.max(vals, axis=-1, keepdims=True)
        idx = jnp.min(jnp.where(vals == m, lane, LANES), axis=-1, keepdims=True)
        return m, idx

    gl = jnp.where(lane < N_GROUPS, logits, NEG_BIG)
    gmax, gidx = first_argmax(gl)
    gsum = jnp.sum(jnp.where(lane < N_GROUPS, jnp.exp(gl - gmax), 0.0), axis=-1, keepdims=True)
    g_w = 1.0 / gsum
    e_lo = N_GROUPS + gidx * EXPERTS_PER_GROUP
    el = jnp.where((lane >= e_lo) & (lane < e_lo + EXPERTS_PER_GROUP), logits, NEG_BIG)
    m1, i1 = first_argmax(el)
    m2, i2 = first_argmax(jnp.where(lane == i1, NEG_BIG, el))
    t = jnp.exp(m2 - m1)
    p1 = 1.0 / (1.0 + t)
    p2 = t * p1
    e1 = i1 - N_GROUPS
    e2 = i2 - N_GROUPS
    route_ref[...] = jnp.where(
        lane == ROUTE_E0, e1.astype(F32),
        jnp.where(lane == ROUTE_E1, e2.astype(F32),
                  jnp.where(lane == ROUTE_W0, g_w * p1,
                            jnp.where(lane == ROUTE_W1, g_w * p2, 0.0))))

    @pl.when(pl.program_id(0) == 0)
    def _():
        counts_ref[...] = jnp.zeros_like(counts_ref)

    chosen = ((lane == e1) | (lane == e2)).astype(F32)
    counts_ref[...] += jnp.sum(chosen, axis=0, keepdims=True)


def _out_proj(ab, x2, w_out, nw, w_router, b_router):
    T = x2.shape[0]
    tm = OUT_TM
    full = lambda shape: pl.BlockSpec(shape, lambda i: (0,) * len(shape))
    row = lambda width: pl.BlockSpec((tm, width), lambda i: (i, 0))
    return pl.pallas_call(
        _out_proj_kernel,
        grid=(T // tm,),
        in_specs=[row(MIX_WIDTH), row(D_MODEL), full((MIX_WIDTH, D_MODEL)), full((1, D_MODEL)),
                  full((D_MODEL, LANES)), full((1, LANES))],
        out_specs=[row(D_MODEL),
                   pl.BlockSpec((tm * SUBLANES, LANES), lambda i: (i, 0)),
                   row(LANES),
                   full((1, LANES))],
        out_shape=[jax.ShapeDtypeStruct((T, D_MODEL), F32),
                   jax.ShapeDtypeStruct((T * SUBLANES, LANES), U32),
                   jax.ShapeDtypeStruct((T, LANES), F32),
                   jax.ShapeDtypeStruct((1, LANES), F32)],
        compiler_params=_cparams(1),
        name="out_proj",
    )(ab, x2, w_out, nw, w_router, b_router)


PLAN_TM = 512


def _plan_kernel(route_ref, pstart_ref, tri_ref, dest_ref, carry_ref):
    @pl.when(pl.program_id(0) == 0)
    def _():
        carry_ref[...] = jnp.zeros_like(carry_ref)

    route = route_ref[...]
    lane = lax.broadcasted_iota(jnp.int32, route.shape, 1)
    e0 = route[:, ROUTE_E0:ROUTE_E0 + 1].astype(jnp.int32)
    e1 = route[:, ROUTE_E1:ROUTE_E1 + 1].astype(jnp.int32)
    oh0 = lane == e0
    oh1 = lane == e1
    both = (oh0 | oh1).astype(F32)
    earlier = jnp.dot(tri_ref[...], both.astype(BF16), preferred_element_type=F32)
    base = pstart_ref[...] + carry_ref[...] + earlier
    d0 = jnp.sum(jnp.where(oh0, base, 0.0), axis=-1, keepdims=True)
    d1 = jnp.sum(jnp.where(oh1, base, 0.0), axis=-1, keepdims=True)
    dest_ref[...] = jnp.where(lane == 0, d0, jnp.where(lane == 1, d1, 0.0)).astype(jnp.int32)
    carry_ref[...] += jnp.sum(both, axis=0, keepdims=True)


def _plan(route, pstart_row):
    T = route.shape[0]
    tm = PLAN_TM
    tri = jnp.asarray(np.tril(np.ones((tm, tm), np.float32), -1), BF16)
    return pl.pallas_call(
        _plan_kernel,
        grid=(T // tm,),
        in_specs=[pl.BlockSpec((tm, LANES), lambda i: (i, 0)),
                  pl.BlockSpec((1, LANES), lambda i: (0, 0)),
                  pl.BlockSpec((tm, tm), lambda i: (0, 0))],
        out_specs=pl.BlockSpec((tm, LANES), lambda i: (i, 0)),
        out_shape=jax.ShapeDtypeStruct((T, LANES), jnp.int32),
        scratch_shapes=[pltpu.VMEM((1, LANES), F32)],
        compiler_params=_cparams(1),
        name="plan",
    )(route, pstart_row, tri)


DISP_TM = 256
DMA_UNROLL = 8


def _row_copy(src, src_row, dst, dst_row, sem):
    return pltpu.make_async_copy(
        src.at[pl.ds(pl.multiple_of(src_row * SUBLANES, SUBLANES), SUBLANES)],
        dst.at[pl.ds(pl.multiple_of(dst_row * SUBLANES, SUBLANES), SUBLANES)], sem)


def _dispatch_kernel(pad_lo_ref, pad_hi_ref, nused_ref, dest_ref, xn_hbm, zero_hbm, xs_hbm,
                     sem, tail_sem, *, experts_per_step, tails_per_step, n_blocks):
    i = pl.program_id(0)
    n = pl.num_programs(0)
    slot = i % 2
    n_rows = TOP_K * DISP_TM

    def start_row(a, carry):
        tok = i * DISP_TM + jnp.bitwise_and(a, DISP_TM - 1)
        _row_copy(xn_hbm, tok, xs_hbm, dest_ref[0, 0, a], sem.at[slot]).start()
        return carry

    lax.fori_loop(0, n_rows, start_row, 0, unroll=DMA_UNROLL)

    def pad_ranges(step):
        for j in range(experts_per_step):
            e = jnp.minimum(step * experts_per_step + j, N_EXPERTS - 1)
            live = step * experts_per_step + j < N_EXPERTS
            yield pad_lo_ref[e], jnp.where(live, pad_hi_ref[e], pad_lo_ref[e])

    for lo, hi in pad_ranges(i):
        def start_pad(r, carry):
            _row_copy(zero_hbm, 0, xs_hbm, r, sem.at[slot]).start()
            return carry
        lax.fori_loop(lo, hi, start_pad, 0)

    def tail_copies(step, s, start):
        for j in range(tails_per_step):
            blk = nused_ref[0] + step * tails_per_step + j
            rows = pl.ds(pl.multiple_of(jnp.minimum(blk, n_blocks - 1) * (MOE_BLOCK * SUBLANES),
                                        MOE_BLOCK * SUBLANES), MOE_BLOCK * SUBLANES)
            cp = pltpu.make_async_copy(zero_hbm, xs_hbm.at[rows], tail_sem.at[s])

            @pl.when(blk < n_blocks)
            def _():
                cp.start() if start else cp.wait()

    tail_copies(i, slot, True)

    def wait_step(step, s):
        one_row = _row_copy(zero_hbm, 0, xs_hbm, 0, sem.at[s])

        def wait(a, carry):
            one_row.wait()
            return carry
        lax.fori_loop(0, n_rows, wait, 0, unroll=DMA_UNROLL)
        for lo, hi in pad_ranges(step):
            lax.fori_loop(lo, hi, wait, 0)

        tail_copies(step, s, False)

    @pl.when(i > 0)
    def _():
        wait_step(i - 1, 1 - slot)

    @pl.when(i == n - 1)
    def _():
        wait_step(i, slot)


def _dispatch(pad_lo, pad_hi, n_used, dest3, xn_packed, n_blocks):
    n = dest3.shape[0]
    min_used = (n * DISP_TM * TOP_K) // MOE_BLOCK
    tails_per_step = -(-(n_blocks - min_used) // n)
    zero_block = jnp.zeros((MOE_BLOCK * SUBLANES, LANES), U32)
    grid_spec = pltpu.PrefetchScalarGridSpec(
        num_scalar_prefetch=3,
        grid=(n,),
        in_specs=[pl.BlockSpec((1, 1, TOP_K * DISP_TM), lambda i, *_: (i, 0, 0),
                               memory_space=pltpu.SMEM),
                  pl.BlockSpec(memory_space=pl.ANY),
                  pl.BlockSpec(memory_space=pl.ANY)],
        out_specs=pl.BlockSpec(memory_space=pl.ANY),
        scratch_shapes=[pltpu.SemaphoreType.DMA((2,)), pltpu.SemaphoreType.DMA((2,))],
    )
    return pl.pallas_call(
        functools.partial(_dispatch_kernel, experts_per_step=-(-N_EXPERTS // n),
                          tails_per_step=tails_per_step, n_blocks=n_blocks),
        grid_spec=grid_spec,
        out_shape=jax.ShapeDtypeStruct((n_blocks * MOE_BLOCK * SUBLANES, LANES), U32),
        compiler_params=_cparams(1),
        name="dispatch",
    )(pad_lo, pad_hi, n_used, dest3, xn_packed, zero_block)


CAST_ROWS = 256


def _experts_kernel(be_ref, first_ref, par_ref, nxt_ref, nused_ref,
                    xs_ref, wg_hbm, wu_hbm, wd_hbm, ys_ref,
                    fg, fu, fd, wg, wu, wd, sem):
    b = pl.program_id(0)
    n_used = nused_ref[0]

    def fetch(e, p):
        return (pltpu.make_async_copy(wg_hbm.at[e], fg.at[p], sem.at[p, 0]),
                pltpu.make_async_copy(wu_hbm.at[e], fu.at[p], sem.at[p, 1]),
                pltpu.make_async_copy(wd_hbm.at[e], fd.at[p], sem.at[p, 2]))

    @pl.when(b == 0)
    def _():
        for cp in fetch(be_ref[0], 0):
            cp.start()

    @pl.when((b < n_used) & (first_ref[b] == 1))
    def _():
        p = par_ref[b]
        for cp in fetch(be_ref[b], p):
            cp.wait()

        @pl.when(nxt_ref[b] >= 0)
        def _():
            for cp in fetch(nxt_ref[b], 1 - p):
                cp.start()

        def cast(src, dst, n_rows):
            def body(i, carry):
                rows = pl.ds(pl.multiple_of(i * CAST_ROWS, CAST_ROWS), CAST_ROWS)
                dst[rows, :] = src[p, rows, :].astype(BF16)
                return carry
            lax.fori_loop(0, n_rows // CAST_ROWS, body, 0)

        cast(fg, wg, D_MODEL)
        cast(fu, wu, D_MODEL)
        cast(fd, wd, D_EXPERT)

    @pl.when(b < n_used)
    def _():
        lo_parts, hi_parts = [], []
        for c in range(PACK_CHUNKS):
            lo, hi = _unpack_pair(xs_ref[pl.ds(c, MOE_BLOCK, stride=SUBLANES), :])
            lo_parts.append(lo.astype(BF16))
            hi_parts.append(hi.astype(BF16))
        xb = jnp.concatenate(lo_parts + hi_parts, axis=1)
        g = jnp.dot(xb, wg[...], preferred_element_type=F32)
        u = jnp.dot(xb, wu[...], preferred_element_type=F32)
        hmid = (g * jax.nn.sigmoid(g) * u).astype(BF16)
        y = jnp.dot(hmid, wd[...], preferred_element_type=F32)
        packed = _pack_pair(y[:, :HALF], y[:, HALF:])
        for c in range(PACK_CHUNKS):
            ys_ref[pl.ds(c, MOE_BLOCK, stride=SUBLANES), :] = packed[:, c * LANES:(c + 1) * LANES]

    @pl.when(b >= n_used)
    def _():
        ys_ref[...] = jnp.zeros_like(ys_ref)


def _experts(tables, xs, w_gate, w_up, w_down):
    n_blocks = tables[0].shape[0]
    blk = lambda: pl.BlockSpec((MOE_BLOCK * SUBLANES, LANES), lambda b, *_: (b, 0))
    grid_spec = pltpu.PrefetchScalarGridSpec(
        num_scalar_prefetch=len(tables),
        grid=(n_blocks,),
        in_specs=[blk(),
                  pl.BlockSpec(memory_space=pl.ANY),
                  pl.BlockSpec(memory_space=pl.ANY),
                  pl.BlockSpec(memory_space=pl.ANY)],
        out_specs=blk(),
        scratch_shapes=[pltpu.VMEM((2, D_MODEL, D_EXPERT), F32),
                        pltpu.VMEM((2, D_MODEL, D_EXPERT), F32),
                        pltpu.VMEM((2, D_EXPERT, D_MODEL), F32),
                        pltpu.VMEM((D_MODEL, D_EXPERT), BF16),
                        pltpu.VMEM((D_MODEL, D_EXPERT), BF16),
                        pltpu.VMEM((D_EXPERT, D_MODEL), BF16),
                        pltpu.SemaphoreType.DMA((2, 3))],
    )
    return pl.pallas_call(
        _experts_kernel,
        grid_spec=grid_spec,
        out_shape=jax.ShapeDtypeStruct((n_blocks * MOE_BLOCK * SUBLANES, LANES), U32),
        compiler_params=_cparams(1),
        name="experts",
    )(*tables, xs, w_gate, w_up, w_down)


CMB_TM = 128


def _combine_kernel(d_cur_ref, d_next_ref, h_ref, route_ref, ys_hbm, o_ref, ybuf, sem):
    i = pl.program_id(0)
    n = pl.num_programs(0)
    slot = i % 2
    n_rows = TOP_K * CMB_TM

    def gather(idx_ref, s, start):
        def body(a, carry):
            cp = _row_copy(ys_hbm, idx_ref[0, 0, a], ybuf.at[s], a, sem.at[s])
            if start:
                cp.start()
            else:
                cp.wait()
            return carry
        lax.fori_loop(0, n_rows, body, 0, unroll=DMA_UNROLL)

    @pl.when(i == 0)
    def _():
        gather(d_cur_ref, 0, True)

    @pl.when(i + 1 < n)
    def _():
        gather(d_next_ref, 1 - slot, True)

    gather(d_cur_ref, slot, False)
    route = route_ref[...]
    w0 = route[:, ROUTE_W0:ROUTE_W0 + 1]
    w1 = route[:, ROUTE_W1:ROUTE_W1 + 1]
    yb = ybuf.at[slot]
    for c in range(PACK_CHUNKS):
        lo0, hi0 = _unpack_pair(yb[pl.ds(c, CMB_TM, stride=SUBLANES), :])
        lo1, hi1 = _unpack_pair(yb[pl.ds(CMB_TM * SUBLANES + c, CMB_TM, stride=SUBLANES), :])
        lo_cols = slice(c * LANES, (c + 1) * LANES)
        hi_cols = slice(HALF + c * LANES, HALF + (c + 1) * LANES)
        o_ref[:, lo_cols] = h_ref[:, lo_cols] + (w0 * lo0 + w1 * lo1)
        o_ref[:, hi_cols] = h_ref[:, hi_cols] + (w0 * hi0 + w1 * hi1)


def _combine(dest3, h, route, ys):
    T = h.shape[0]
    tm = CMB_TM
    n = T // tm
    d_spec = lambda fn: pl.BlockSpec((1, 1, TOP_K * tm), fn, memory_space=pltpu.SMEM)
    return pl.pallas_call(
        _combine_kernel,
        grid=(n,),
        in_specs=[
            d_spec(lambda i: (i, 0, 0)),
            d_spec(lambda i: (jnp.minimum(i + 1, n - 1), 0, 0)),
            pl.BlockSpec((tm, D_MODEL), lambda i: (i, 0)),
            pl.BlockSpec((tm, LANES), lambda i: (i, 0)),
            pl.BlockSpec(memory_space=pl.ANY),
        ],
        out_specs=pl.BlockSpec((tm, D_MODEL), lambda i: (i, 0)),
        out_shape=jax.ShapeDtypeStruct((T, D_MODEL), F32),
        scratch_shapes=[pltpu.VMEM((2, TOP_K * tm * SUBLANES, LANES), U32),
                        pltpu.SemaphoreType.DMA((2,))],
        compiler_params=_cparams(1),
        name="combine",
    )(dest3, dest3, h, route, ys)


def _block_tables(counts_row, T):
    counts = counts_row[0, :N_EXPERTS].astype(jnp.int32)
    padded = (counts + MOE_BLOCK - 1) // MOE_BLOCK * MOE_BLOCK
    pends = jnp.cumsum(padded)
    pstarts = pends - padded
    n_blocks = -(-(T * TOP_K + N_EXPERTS * (MOE_BLOCK - 1)) // MOE_BLOCK)
    n_used = pends[-1] // MOE_BLOCK
    block_start = jnp.arange(n_blocks, dtype=jnp.int32) * MOE_BLOCK
    be = jnp.minimum(jnp.searchsorted(pends, block_start, side='right'),
                     N_EXPERTS - 1).astype(jnp.int32)
    first =jnp.concatenate([jnp.ones((1,), jnp.int32), (be[1:] != be[:-1]).astype(jnp.int32)])
    par = (jnp.cumsum(first) - 1) % 2
    run_end_block = pends[be] // MOE_BLOCK
    nxt = jnp.where(run_end_block < n_used, be[jnp.minimum(run_end_block, n_blocks - 1)], -1)
    pstart_row = jnp.zeros((1, LANES), F32).at[0, :N_EXPERTS].set(pstarts.astype(F32))
    n_used = n_used.astype(jnp.int32).reshape(1)
    tables = tuple(t.astype(jnp.int32) for t in (be, first, par, nxt)) + (n_used,)
    pads = ((pstarts + counts).astype(jnp.int32), pends.astype(jnp.int32), n_used)
    return pstart_row, tables, pads, n_blocks


def _choice_major(dest, tile):
    T = dest.shape[0]
    return jnp.transpose(dest.reshape(T // tile, tile, TOP_K), (0, 2, 1)).reshape(
        T // tile, 1, TOP_K * tile)


def _rope_tables(S):
    inv = ROPE_THETA ** (-jnp.arange(0, HEAD_DIM, 2, dtype=F32) / HEAD_DIM)
    ang = jnp.arange(S, dtype=F32)[:, None] * inv[None, :]
    cos, sin = jnp.cos(ang), jnp.sin(ang)
    return jnp.concatenate([cos, cos], axis=-1), jnp.concatenate([-sin, sin], axis=-1)


def kernel(x, norm_mix_w, w_in, w_pool, pool_scale, q_norm_w, k_norm_w, sink_logits, w_out,
           norm_ffn_w, w_group_router, b_group_router, w_expert_router, b_expert_router,
           w_gate, w_up, w_down):
    B, S, D = x.shape
    T = B * S
    depth = w_in.shape[0]
    cos, sin = _rope_tables(S)
    coef_np, invc_np, bias_np = _band_constants(S)
    coef = jnp.asarray(coef_np, BF16)
    invc = jnp.asarray(invc_np, F32)
    bias = jnp.asarray(bias_np, F32)
    pad_lanes = LANES - N_GROUPS - N_EXPERTS

    h = x.reshape(T, D)
    for l in range(depth):
        u, q, k, v = _in_proj(h, norm_mix_w[l].reshape(1, D), w_in[l].astype(BF16),
                              q_norm_w[l].reshape(1, HEAD_DIM), k_norm_w[l].reshape(1, HEAD_DIM),
                              cos, sin, S)
        ab = _mixers(sink_logits[l], u.reshape(B, S, -1), q.reshape(B, S, -1),
                     k.reshape(B, S, -1), v.reshape(B, S, -1), coef, invc, bias,
                     w_pool[l].astype(BF16), pool_scale[l].reshape(1, POOL_WIDTH))
        w_router = jnp.concatenate(
            [w_group_router[l], w_expert_router[l], jnp.zeros((D, pad_lanes), F32)], axis=1)
        b_router = jnp.concatenate(
            [b_group_router[l], b_expert_router[l], jnp.zeros((pad_lanes,), F32)]).reshape(1, LANES)
        hmix, xn_packed, route, counts = _out_proj(
            ab.reshape(T, MIX_WIDTH), h, w_out[l].astype(BF16), norm_ffn_w[l].reshape(1, D),
            w_router.astype(BF16), b_router)
        pstart_row, tables, pads, n_blocks = _block_tables(counts, T)
        dest = _plan(route, pstart_row)[:, :TOP_K]
        xs = _dispatch(*pads, _choice_major(dest, DISP_TM), xn_packed, n_blocks)
        ys = _experts(tables, xs, w_gate[l], w_up[l], w_down[l])
        h = _combine(_choice_major(dest, CMB_TM), hmix, route, ys)
    return h.reshape(B, S, D)
```

```python
import functools

import numpy as np
import jax
import jax.numpy as jnp
from jax import lax
from jax.experimental import pallas as pl
from jax.experimental.pallas import tpu as pltpu

D_MODEL = 2048
POOL_WIDTH = 1024
POOL_WINDOWS = (2, 4, 8, 16)
N_POOL_GROUPS = 4
POOL_GROUP = 256
HEAD_DIM = 128
N_Q_HEADS = 8
N_KV_HEADS = 2
Q_PER_KV = 4
ATTN_WIDTH = 1024
KV_WIDTH = 256
IN_PROJ_WIDTH = 2560
MIX_WIDTH = 2048
WINDOW = 128
BLOCK = 128
BAND = 3 * BLOCK
ROPE_THETA = 10000.0
N_GROUPS = 8
EXPERTS_PER_GROUP = 8
N_EXPERTS = 64
TOP_K = 2
D_EXPERT = 512
MOE_BLOCK = 128
EPS = 1e-6

LANES = 128
SUBLANES = 8
HALF = D_MODEL // 2
PACK_CHUNKS = HALF // LANES
NEG_BIG = -1e30
VMEM_LIMIT = 56 * 1024 * 1024

BF16 = jnp.bfloat16
F32 = jnp.float32
U32 = jnp.uint32
HI_MASK = 0xFFFF0000


def _cparams(n_axes):
    return pltpu.CompilerParams(dimension_semantics=("arbitrary",) * n_axes,
                                vmem_limit_bytes=VMEM_LIMIT)


def _pack_pair(lo, hi):
    lo_bits = lax.bitcast_convert_type(lo.astype(BF16).astype(F32), U32) >> 16
    hi_bits = lax.bitcast_convert_type(hi.astype(BF16).astype(F32), U32) & jnp.uint32(HI_MASK)
    return lo_bits | hi_bits


def _unpack_pair(words):
    lo = lax.bitcast_convert_type(words << 16, F32)
    hi = lax.bitcast_convert_type(words & jnp.uint32(HI_MASK), F32)
    return lo, hi


IN_TM = 256


def _in_proj_kernel(x_ref, nw_ref, w_ref, qnw_ref, knw_ref, cos_ref, sin_ref,
                    u_ref, q_ref, k_ref, v_ref):
    x = x_ref[...]
    ms = jnp.mean(x * x, axis=-1, keepdims=True)
    xn = (x * lax.rsqrt(ms + EPS) * nw_ref[...]).astype(BF16)
    cos = cos_ref[...]
    sin = sin_ref[...]

    def head_norm_rope(t, w, scale):
        hms = jnp.mean(t * t, axis=-1, keepdims=True)
        t = t * lax.rsqrt(hms + EPS) * w
        t = t * cos + pltpu.roll(t, HEAD_DIM // 2, axis=1) * sin
        return t * scale

    u_ref[...] = jnp.dot(xn, w_ref[:, :POOL_WIDTH], preferred_element_type=F32).astype(BF16)
    zq = jnp.dot(xn, w_ref[:, POOL_WIDTH:POOL_WIDTH + ATTN_WIDTH], preferred_element_type=F32)
    qnw = qnw_ref[...]
    for h in range(N_Q_HEADS):
        sl = slice(h * HEAD_DIM, (h + 1) * HEAD_DIM)
        q_ref[:, sl] = head_norm_rope(zq[:, sl], qnw, HEAD_DIM ** -0.5).astype(BF16)
    o_k = POOL_WIDTH + ATTN_WIDTH
    zk = jnp.dot(xn, w_ref[:, o_k:o_k + KV_WIDTH], preferred_element_type=F32)
    knw = knw_ref[...]
    for h in range(N_KV_HEADS):
        sl = slice(h * HEAD_DIM, (h + 1) * HEAD_DIM)
        k_ref[:, sl] = head_norm_rope(zk[:, sl], knw, 1.0).astype(BF16)
    v_ref[...] = jnp.dot(xn, w_ref[:, o_k + KV_WIDTH:], preferred_element_type=F32).astype(BF16)


def _in_proj(x2, nw, w_in, qnw, knw, cos, sin, seq):
    T = x2.shape[0]
    tm = IN_TM
    pos_blocks = seq // tm
    full = lambda shape: pl.BlockSpec(shape, lambda i: (0,) * len(shape))
    return pl.pallas_call(
        _in_proj_kernel,
        grid=(T // tm,),
        in_specs=[
            pl.BlockSpec((tm, D_MODEL), lambda i: (i, 0)),
            full((1, D_MODEL)),
            full((D_MODEL, IN_PROJ_WIDTH)),
            full((1, HEAD_DIM)),
            full((1, HEAD_DIM)),
            pl.BlockSpec((tm, HEAD_DIM), lambda i: (i % pos_blocks, 0)),
            pl.BlockSpec((tm, HEAD_DIM), lambda i: (i % pos_blocks, 0)),
        ],
        out_specs=[
            pl.BlockSpec((tm, POOL_WIDTH), lambda i: (i, 0)),
            pl.BlockSpec((tm, ATTN_WIDTH), lambda i: (i, 0)),
            pl.BlockSpec((tm, KV_WIDTH), lambda i: (i, 0)),
            pl.BlockSpec((tm, KV_WIDTH), lambda i: (i, 0)),
        ],
        out_shape=[
            jax.ShapeDtypeStruct((T, POOL_WIDTH), BF16),
            jax.ShapeDtypeStruct((T, ATTN_WIDTH), BF16),
            jax.ShapeDtypeStruct((T, KV_WIDTH), BF16),
            jax.ShapeDtypeStruct((T, KV_WIDTH), BF16),
        ],
        compiler_params=_cparams(1),
        name="in_proj",
    )(x2, nw, w_in, qnw, knw, cos, sin)


MIX_TQ = 256


def _band_constants(seq):
    nb = seq // BLOCK
    coef = np.zeros((3, N_POOL_GROUPS, BLOCK, BAND), np.float32)
    inv_count = np.zeros((3, N_POOL_GROUPS, BLOCK, 1), np.float32)
    bias = np.zeros((3, BLOCK, BAND), np.float32)
    for kind, n in enumerate((0, 1, nb - 1)):
        start = min(max((n - 1) * BLOCK, 0), seq - BAND)
        t = n * BLOCK + np.arange(BLOCK)[:, None]
        s = start + np.arange(BAND)[None, :]
        bias[kind] = np.where(np.abs(s - t) <= WINDOW, 0.0, NEG_BIG)
        for g, win in enumerate(POOL_WINDOWS):
            half = win // 2
            lo = np.clip(t - half, 0, seq)
            hi = np.clip(t + half, 0, seq)
            count = (hi - lo).astype(np.float32)
            inside = ((s >= lo) & (s < hi)).astype(np.float32)
            coef[kind, g] = inside - count * (s == t)
            inv_count[kind, g] = 1.0 / count
    return coef, inv_count, bias


def _mixers_kernel(sink_ref, u_ref, q_ref, k_ref, v_ref, coef_ref, invc_ref, bias_ref,
                   wpool_ref, pscale_ref, ab_ref, *, seq):
    nb = seq // BLOCK
    j = pl.program_id(1)
    for r in range(MIX_TQ // BLOCK):
        n = j * (MIX_TQ // BLOCK) + r
        start = pl.multiple_of(jnp.clip((n - 1) * BLOCK, 0, seq - BAND), BLOCK)
        kind = jnp.where(n == 0, 0, jnp.where(n == nb - 1, 2, 1))
        rows = slice(r * BLOCK, (r + 1) * BLOCK)

        for g in range(N_POOL_GROUPS):
            cols = slice(g * POOL_GROUP, (g + 1) * POOL_GROUP)
            ub = u_ref[0, pl.ds(start, BAND), cols]
            d = jnp.dot(coef_ref[kind, g], ub, preferred_element_type=F32) * invc_ref[kind, g]
            y = jnp.dot(d.astype(BF16), wpool_ref[g], preferred_element_type=F32)
            ab_ref[0, rows, cols] = (y * pscale_ref[:, cols]).astype(BF16)

        bias = bias_ref[kind]
        for hk in range(N_KV_HEADS):
            kcols = slice(hk * HEAD_DIM, (hk + 1) * HEAD_DIM)
            kb = k_ref[0, pl.ds(start, BAND), kcols]
            vb = v_ref[0, pl.ds(start, BAND), kcols]
            heads = [hk * Q_PER_KV + g for g in range(Q_PER_KV)]
            qs = jnp.concatenate(
                [q_ref[0, rows, h * HEAD_DIM:(h + 1) * HEAD_DIM] for h in heads], axis=0)
            s = lax.dot_general(qs, kb, (((1,), (1,)), ((), ())), preferred_element_type=F32)
            for gi, h in enumerate(heads):
                sh = s[gi * BLOCK:(gi + 1) * BLOCK] + bias
                sink = sink_ref[h]
                m = jnp.maximum(jnp.max(sh, axis=-1, keepdims=True), sink)
                p = jnp.exp(sh - m)
                denom = jnp.sum(p, axis=-1, keepdims=True) + jnp.exp(sink - m)
                o = jnp.dot(p.astype(BF16), vb, preferred_element_type=F32) / denom
                ocols = slice(POOL_WIDTH + h * HEAD_DIM, POOL_WIDTH + (h + 1) * HEAD_DIM)
                ab_ref[0, rows, ocols] = o.astype(BF16)


def _mixers(sink, u, q, k, v, coef, invc, bias, wpool, pscale):
    B, S, _ = u.shape
    tq = MIX_TQ
    full = lambda shape: pl.BlockSpec(shape, lambda b, j: (0,) * len(shape))
    return pl.pallas_call(
        functools.partial(_mixers_kernel, seq=S),
        grid=(B, S // tq),
        in_specs=[
            pl.BlockSpec(memory_space=pltpu.SMEM),
            pl.BlockSpec((1, S, POOL_WIDTH), lambda b, j: (b, 0, 0)),
            pl.BlockSpec((1, tq, ATTN_WIDTH), lambda b, j: (b, j, 0)),
            pl.BlockSpec((1, S, KV_WIDTH), lambda b, j: (b, 0, 0)),
            pl.BlockSpec((1, S, KV_WIDTH), lambda b, j: (b, 0, 0)),
            full((3, N_POOL_GROUPS, BLOCK, BAND)),
            full((3, N_POOL_GROUPS, BLOCK, 1)),
            full((3, BLOCK, BAND)),
            full((N_POOL_GROUPS, POOL_GROUP, POOL_GROUP)),
            full((1, POOL_WIDTH)),
        ],
        out_specs=pl.BlockSpec((1, tq, MIX_WIDTH), lambda b, j: (b, j, 0)),
        out_shape=jax.ShapeDtypeStruct((B, S, MIX_WIDTH), BF16),
        compiler_params=_cparams(2),
        name="mixers",
    )(sink, u, q, k, v, coef, invc, bias, wpool, pscale)


OUT_TM = 256
ROUTE_E0, ROUTE_E1, ROUTE_W0, ROUTE_W1 = 0, 1, 2, 3


def _out_proj_kernel(ab_ref, x_ref, w_ref, nw_ref, wr_ref, br_ref,
                     h_ref, xn_ref, route_ref, counts_ref):
    h = x_ref[...] + jnp.dot(ab_ref[...], w_ref[...], preferred_element_type=F32)
    h_ref[...] = h
    ms = jnp.mean(h * h, axis=-1, keepdims=True)
    xn = h * lax.rsqrt(ms + EPS) * nw_ref[...]
    packed = _pack_pair(xn[:, :HALF], xn[:, HALF:])
    for c in range(PACK_CHUNKS):
        xn_ref[pl.ds(c, OUT_TM, stride=SUBLANES), :] = packed[:, c * LANES:(c + 1) * LANES]
    logits = jnp.dot(xn.astype(BF16), wr_ref[...], preferred_element_type=F32) + br_ref[...]

    lane = lax.broadcasted_iota(jnp.int32, logits.shape, 1)

    def first_argmax(vals):
        m = jnp.max(vals, axis=-1, keepdims=True)
        idx = jnp.min(jnp.where(vals == m, lane, LANES), axis=-1, keepdims=True)
        return m, idx

    gl = jnp.where(lane < N_GROUPS, logits, NEG_BIG)
    gmax, gidx = first_argmax(gl)
    gsum = jnp.sum(jnp.where(lane < N_GROUPS, jnp.exp(gl - gmax), 0.0), axis=-1, keepdims=True)
    g_w = 1.0 / gsum
    e_lo = N_GROUPS + gidx * EXPERTS_PER_GROUP
    el = jnp.where((lane >= e_lo) & (lane < e_lo + EXPERTS_PER_GROUP), logits, NEG_BIG)
    m1, i1 = first_argmax(el)
    m2, i2 = first_argmax(jnp.where(lane == i1, NEG_BIG, el))
    t = jnp.exp(m2 - m1)
    p1 = 1.0 / (1.0 + t)
    p2 = t * p1
    e1 = i1 - N_GROUPS
    e2 = i2 - N_GROUPS
    route_ref[...] = jnp.where(
        lane == ROUTE_E0, e1.astype(F32),
        jnp.where(lane == ROUTE_E1, e2.astype(F32),
                  jnp.where(lane == ROUTE_W0, g_w * p1,
                            jnp.where(lane == ROUTE_W1, g_w * p2, 0.0))))

    @pl.when(pl.program_id(0) == 0)
    def _():
        counts_ref[...] = jnp.zeros_like(counts_ref)

    chosen = ((lane == e1) | (lane == e2)).astype(F32)
    counts_ref[...] += jnp.sum(chosen, axis=0, keepdims=True)


def _out_proj(ab, x2, w_out, nw, w_router, b_router):
    T = x2.shape[0]
    tm = OUT_TM
    full = lambda shape: pl.BlockSpec(shape, lambda i: (0,) * len(shape))
    row = lambda width: pl.BlockSpec((tm, width), lambda i: (i, 0))
    return pl.pallas_call(
        _out_proj_kernel,
        grid=(T // tm,),
        in_specs=[row(MIX_WIDTH), row(D_MODEL), full((MIX_WIDTH, D_MODEL)), full((1, D_MODEL)),
                  full((D_MODEL, LANES)), full((1, LANES))],
        out_specs=[row(D_MODEL),
                   pl.BlockSpec((tm * SUBLANES, LANES), lambda i: (i, 0)),
                   row(LANES),
                   full((1, LANES))],
        out_shape=[jax.ShapeDtypeStruct((T, D_MODEL), F32),
                   jax.ShapeDtypeStruct((T * SUBLANES, LANES), U32),
                   jax.ShapeDtypeStruct((T, LANES), F32),
                   jax.ShapeDtypeStruct((1, LANES), F32)],
        compiler_params=_cparams(1),
        name="out_proj",
    )(ab, x2, w_out, nw, w_router, b_router)


PLAN_TM = 512


def _plan_kernel(route_ref, pstart_ref, tri_ref, dest_ref, carry_ref):
    @pl.when(pl.program_id(0) == 0)
    def _():
        carry_ref[...] = jnp.zeros_like(carry_ref)

    route = route_ref[...]
    lane = lax.broadcasted_iota(jnp.int32, route.shape, 1)
    e0 = route[:, ROUTE_E0:ROUTE_E0 + 1].astype(jnp.int32)
    e1 = route[:, ROUTE_E1:ROUTE_E1 + 1].astype(jnp.int32)
    oh0 = lane == e0
    oh1 = lane == e1
    both = (oh0 | oh1).astype(F32)
    earlier = jnp.dot(tri_ref[...], both.astype(BF16), preferred_element_type=F32)
    base = pstart_ref[...] + carry_ref[...] + earlier
    d0 = jnp.sum(jnp.where(oh0, base, 0.0), axis=-1, keepdims=True)
    d1 = jnp.sum(jnp.where(oh1, base, 0.0), axis=-1, keepdims=True)
    dest_ref[...] = jnp.where(lane == 0, d0, jnp.where(lane == 1, d1, 0.0)).astype(jnp.int32)
    carry_ref[...] += jnp.sum(both, axis=0, keepdims=True)


def _plan(route, pstart_row):
    T = route.shape[0]
    tm = PLAN_TM
    tri = jnp.asarray(np.tril(np.ones((tm, tm), np.float32), -1), BF16)
    return pl.pallas_call(
        _plan_kernel,
        grid=(T // tm,),
        in_specs=[pl.BlockSpec((tm, LANES), lambda i: (i, 0)),
                  pl.BlockSpec((1, LANES), lambda i: (0, 0)),
                  pl.BlockSpec((tm, tm), lambda i: (0, 0))],
        out_specs=pl.BlockSpec((tm, LANES), lambda i: (i, 0)),
        out_shape=jax.ShapeDtypeStruct((T, LANES), jnp.int32),
        scratch_shapes=[pltpu.VMEM((1, LANES), F32)],
        compiler_params=_cparams(1),
        name="plan",
    )(route, pstart_row, tri)


DISP_TM = 512
DMA_GROUP = 8
ROW_WORDS = SUBLANES


def _packed_row(ref, row):
    return ref.at[pl.ds(pl.multiple_of(row * ROW_WORDS, ROW_WORDS), ROW_WORDS)]


def _grouped(n, fn):
    def trip(t, carry):
        for k in range(DMA_GROUP):
            fn(t * DMA_GROUP + k, k)
        return carry
    lax.fori_loop(0, n // DMA_GROUP, trip, 0)


def _dispatch_kernel(pad_lo_ref, pad_hi_ref, nused_ref, dest_ref, xn_ref, xs_hbm,
                     zeros, sem, tail_sem, *, experts_per_step, tails_per_step, n_blocks):
    i = pl.program_id(0)
    n_rows = TOP_K * DISP_TM

    @pl.when(i == 0)
    def _():
        zeros[...] = jnp.zeros_like(zeros)

    def row_copy(a):
        src = _packed_row(xn_ref, jnp.bitwise_and(a, DISP_TM - 1))
        return pltpu.make_async_copy(src, _packed_row(xs_hbm, dest_ref[0, 0, a]), sem)

    _grouped(n_rows, lambda a, k: row_copy(a).start(priority=k % 2))

    def pad_copy(slot_row):
        return pltpu.make_async_copy(_packed_row(zeros, 0), _packed_row(xs_hbm, slot_row), sem)

    def pad_ranges():
        for j in range(experts_per_step):
            e = jnp.minimum(i * experts_per_step + j, N_EXPERTS - 1)
            live = i * experts_per_step + j < N_EXPERTS
            yield pad_lo_ref[e], jnp.where(live, pad_hi_ref[e], pad_lo_ref[e])

    def tail_copies(start):
        for j in range(tails_per_step):
            blk = nused_ref[0] + i * tails_per_step + j
            rows = pl.ds(pl.multiple_of(jnp.minimum(blk, n_blocks - 1) * (MOE_BLOCK * ROW_WORDS),
                                        MOE_BLOCK * ROW_WORDS), MOE_BLOCK * ROW_WORDS)
            cp = pltpu.make_async_copy(zeros, xs_hbm.at[rows], tail_sem)

            @pl.when(blk < n_blocks)
            def _():
                cp.start() if start else cp.wait()

    for lo, hi in pad_ranges():
        lax.fori_loop(lo, hi, lambda r, c: (pad_copy(r).start(), c)[1], 0)
    tail_copies(True)

    _grouped(n_rows, lambda a, k: pad_copy(0).wait())
    for lo, hi in pad_ranges():
        lax.fori_loop(lo, hi, lambda r, c: (pad_copy(0).wait(), c)[1], 0)
    tail_copies(False)


def _dispatch(pad_lo, pad_hi, n_used, dest3, xn_packed, n_blocks):
    n = dest3.shape[0]
    min_used = (n * DISP_TM * TOP_K) // MOE_BLOCK
    tails_per_step = -(-(n_blocks - min_used) // n)
    grid_spec = pltpu.PrefetchScalarGridSpec(
        num_scalar_prefetch=3,
        grid=(n,),
        in_specs=[pl.BlockSpec((1, 1, TOP_K * DISP_TM), lambda i, *_: (i, 0, 0),
                               memory_space=pltpu.SMEM),
                  pl.BlockSpec((DISP_TM * ROW_WORDS, LANES), lambda i, *_: (i, 0))],
        out_specs=pl.BlockSpec(memory_space=pl.ANY),
        scratch_shapes=[pltpu.VMEM((MOE_BLOCK * ROW_WORDS, LANES), U32),
                        pltpu.SemaphoreType.DMA(()), pltpu.SemaphoreType.DMA(())],
    )
    return pl.pallas_call(
        functools.partial(_dispatch_kernel, experts_per_step=-(-N_EXPERTS // n),
                          tails_per_step=tails_per_step, n_blocks=n_blocks),
        grid_spec=grid_spec,
        out_shape=jax.ShapeDtypeStruct((n_blocks * MOE_BLOCK * ROW_WORDS, LANES), U32),
        compiler_params=_cparams(1),
        name="dispatch",
    )(pad_lo, pad_hi, n_used, dest3, xn_packed)


CAST_ROWS = 256


def _experts_kernel(be_ref, first_ref, par_ref, nxt_ref, nused_ref,
                    xs_ref, wg_hbm, wu_hbm, wd_hbm, ys_ref,
                    fg, fu, fd, wg, wu, wd, sem):
    b = pl.program_id(0)
    n_used = nused_ref[0]

    def fetch(e, p):
        return (pltpu.make_async_copy(wg_hbm.at[e], fg.at[p], sem.at[p, 0]),
                pltpu.make_async_copy(wu_hbm.at[e], fu.at[p], sem.at[p, 1]),
                pltpu.make_async_copy(wd_hbm.at[e], fd.at[p], sem.at[p, 2]))

    @pl.when(b == 0)
    def _():
        for cp in fetch(be_ref[0], 0):
            cp.start()

    @pl.when((b < n_used) & (first_ref[b] == 1))
    def _():
        p = par_ref[b]
        for cp in fetch(be_ref[b], p):
            cp.wait()

        @pl.when(nxt_ref[b] >= 0)
        def _():
            for cp in fetch(nxt_ref[b], 1 - p):
                cp.start()

        def cast(src, dst, n_rows):
            def body(i, carry):
                rows = pl.ds(pl.multiple_of(i * CAST_ROWS, CAST_ROWS), CAST_ROWS)
                dst[rows, :] = src[p, rows, :].astype(BF16)
                return carry
            lax.fori_loop(0, n_rows // CAST_ROWS, body, 0)

        cast(fg, wg, D_MODEL)
        cast(fu, wu, D_MODEL)
        cast(fd, wd, D_EXPERT)

    @pl.when(b < n_used)
    def _():
        lo_parts, hi_parts = [], []
        for c in range(PACK_CHUNKS):
            lo, hi = _unpack_pair(xs_ref[pl.ds(c, MOE_BLOCK, stride=SUBLANES), :])
            lo_parts.append(lo.astype(BF16))
            hi_parts.append(hi.astype(BF16))
        xb = jnp.concatenate(lo_parts + hi_parts, axis=1)
        g = jnp.dot(xb, wg[...], preferred_element_type=F32)
        u = jnp.dot(xb, wu[...], preferred_element_type=F32)
        hmid = (g * jax.nn.sigmoid(g) * u).astype(BF16)
        y = jnp.dot(hmid, wd[...], preferred_element_type=F32)
        packed = _pack_pair(y[:, :HALF], y[:, HALF:])
        for c in range(PACK_CHUNKS):
            ys_ref[pl.ds(c, MOE_BLOCK, stride=SUBLANES), :] = packed[:, c * LANES:(c + 1) * LANES]

    @pl.when(b >= n_used)
    def _():
        ys_ref[...] = jnp.zeros_like(ys_ref)


def _experts(tables, xs, w_gate, w_up, w_down):
    n_blocks = tables[0].shape[0]
    blk = lambda: pl.BlockSpec((MOE_BLOCK * SUBLANES, LANES), lambda b, *_: (b, 0))
    grid_spec = pltpu.PrefetchScalarGridSpec(
        num_scalar_prefetch=len(tables),
        grid=(n_blocks,),
        in_specs=[blk(),
                  pl.BlockSpec(memory_space=pl.ANY),
                  pl.BlockSpec(memory_space=pl.ANY),
                  pl.BlockSpec(memory_space=pl.ANY)],
        out_specs=blk(),
        scratch_shapes=[pltpu.VMEM((2, D_MODEL, D_EXPERT), F32),
                        pltpu.VMEM((2, D_MODEL, D_EXPERT), F32),
                        pltpu.VMEM((2, D_EXPERT, D_MODEL), F32),
                        pltpu.VMEM((D_MODEL, D_EXPERT), BF16),
                        pltpu.VMEM((D_MODEL, D_EXPERT), BF16),
                        pltpu.VMEM((D_EXPERT, D_MODEL), BF16),
                        pltpu.SemaphoreType.DMA((2, 3))],
    )
    return pl.pallas_call(
        _experts_kernel,
        grid_spec=grid_spec,
        out_shape=jax.ShapeDtypeStruct((n_blocks * MOE_BLOCK * SUBLANES, LANES), U32),
        compiler_params=_cparams(1),
        name="experts",
    )(*tables, xs, w_gate, w_up, w_down)


CMB_TM = 128


def _combine_kernel(d_cur_ref, d_next_ref, h_ref, route_ref, ys_hbm, o_ref, ybuf, sem):
    i = pl.program_id(0)
    n = pl.num_programs(0)
    slot = i % 2
    n_rows = TOP_K * CMB_TM

    def gather(idx_ref, s):
        def start(a, k):
            pltpu.make_async_copy(_packed_row(ys_hbm, idx_ref[0, 0, a]),
                                  _packed_row(ybuf.at[s], a), sem.at[s]).start(priority=k % 2)
        _grouped(n_rows, start)

    @pl.when(i == 0)
    def _():
        gather(d_cur_ref, 0)

    @pl.when(i + 1 < n)
    def _():
        gather(d_next_ref, 1 - slot)

    one_row = pltpu.make_async_copy(_packed_row(ys_hbm, 0), _packed_row(ybuf.at[slot], 0),
                                    sem.at[slot])
    _grouped(n_rows, lambda a, k: one_row.wait())
    route = route_ref[...]
    w0 = route[:, ROUTE_W0:ROUTE_W0 + 1]
    w1 = route[:, ROUTE_W1:ROUTE_W1 + 1]
    yb = ybuf.at[slot]
    for c in range(PACK_CHUNKS):
        lo0, hi0 = _unpack_pair(yb[pl.ds(c, CMB_TM, stride=SUBLANES), :])
        lo1, hi1 = _unpack_pair(yb[pl.ds(CMB_TM * SUBLANES + c, CMB_TM, stride=SUBLANES), :])
        lo_cols = slice(c * LANES, (c + 1) * LANES)
        hi_cols = slice(HALF + c * LANES, HALF + (c + 1) * LANES)
        o_ref[:, lo_cols] = h_ref[:, lo_cols] + (w0 * lo0 + w1 * lo1)
        o_ref[:, hi_cols] = h_ref[:, hi_cols] + (w0 * hi0 + w1 * hi1)


def _combine(dest3, h, route, ys):
    T = h.shape[0]
    tm = CMB_TM
    n = T // tm
    d_spec = lambda fn: pl.BlockSpec((1, 1, TOP_K * tm), fn, memory_space=pltpu.SMEM)
    return pl.pallas_call(
        _combine_kernel,
        grid=(n,),
        in_specs=[
            d_spec(lambda i: (i, 0, 0)),
            d_spec(lambda i: (jnp.minimum(i + 1, n - 1), 0, 0)),
            pl.BlockSpec((tm, D_MODEL), lambda i: (i, 0)),
            pl.BlockSpec((tm, LANES), lambda i: (i, 0)),
            pl.BlockSpec(memory_space=pl.ANY),
        ],
        out_specs=pl.BlockSpec((tm, D_MODEL), lambda i: (i, 0)),
        out_shape=jax.ShapeDtypeStruct((T, D_MODEL), F32),
        scratch_shapes=[pltpu.VMEM((2, TOP_K * tm * SUBLANES, LANES), U32),
                        pltpu.SemaphoreType.DMA((2,))],
        compiler_params=_cparams(1),
        name="combine",
    )(dest3, dest3, h, route, ys)


def _block_tables(counts_row, T):
    counts = counts_row[0, :N_EXPERTS].astype(jnp.int32)
    padded = (counts + MOE_BLOCK - 1) // MOE_BLOCK * MOE_BLOCK
    pends = jnp.cumsum(padded)
    pstarts = pends - padded
    n_blocks = -(-(T * TOP_K + N_EXPERTS * (MOE_BLOCK - 1)) // MOE_BLOCK)
    n_used = pends[-1] // MOE_BLOCK
    block_start = jnp.arange(n_blocks, dtype=jnp.int32) * MOE_BLOCK
    be = jnp.minimum(jnp.sum(block_start[:, None] >= pends[None, :], axis=1),
                     N_EXPERTS - 1).astype(jnp.int32)
    first =jnp.concatenate([jnp.ones((1,), jnp.int32), (be[1:] != be[:-1]).astype(jnp.int32)])
    par = (jnp.cumsum(first) - 1) % 2
    run_end_block = pends[be] // MOE_BLOCK
    nxt = jnp.where(run_end_block < n_used, be[jnp.minimum(run_end_block, n_blocks - 1)], -1)
    pstart_row = jnp.zeros((1, LANES), F32).at[0, :N_EXPERTS].set(pstarts.astype(F32))
    n_used = n_used.astype(jnp.int32).reshape(1)
    tables = tuple(t.astype(jnp.int32) for t in (be, first, par, nxt)) + (n_used,)
    pads = ((pstarts + counts).astype(jnp.int32), pends.astype(jnp.int32), n_used)
    return pstart_row, tables, pads, n_blocks


def _choice_major(dest, tile):
    T = dest.shape[0]
    return jnp.transpose(dest.reshape(T // tile, tile, TOP_K), (0, 2, 1)).reshape(
        T // tile, 1, TOP_K * tile)


def _rope_tables(S):
    inv = ROPE_THETA ** (-jnp.arange(0, HEAD_DIM, 2, dtype=F32) / HEAD_DIM)
    ang = jnp.arange(S, dtype=F32)[:, None] * inv[None, :]
    cos, sin = jnp.cos(ang), jnp.sin(ang)
    return jnp.concatenate([cos, cos], axis=-1), jnp.concatenate([-sin, sin], axis=-1)


def kernel(x, norm_mix_w, w_in, w_pool, pool_scale, q_norm_w, k_norm_w, sink_logits, w_out,
           norm_ffn_w, w_group_router, b_group_router, w_expert_router, b_expert_router,
           w_gate, w_up, w_down):
    B, S, D = x.shape
    T = B * S
    depth = w_in.shape[0]
    cos, sin = _rope_tables(S)
    coef_np, invc_np, bias_np = _band_constants(S)
    coef = jnp.asarray(coef_np, BF16)
    invc = jnp.asarray(invc_np, F32)
    bias = jnp.asarray(bias_np, F32)
    pad_lanes = LANES - N_GROUPS - N_EXPERTS

    h = x.reshape(T, D)
    for l in range(depth):
        u, q, k, v = _in_proj(h, norm_mix_w[l].reshape(1, D), w_in[l].astype(BF16),
                              q_norm_w[l].reshape(1, HEAD_DIM), k_norm_w[l].reshape(1, HEAD_DIM),
                              cos, sin, S)
        ab = _mixers(sink_logits[l], u.reshape(B, S, -1), q.reshape(B, S, -1),
                     k.reshape(B, S, -1), v.reshape(B, S, -1), coef, invc, bias,
                     w_pool[l].astype(BF16), pool_scale[l].reshape(1, POOL_WIDTH))
        w_router = jnp.concatenate(
            [w_group_router[l], w_expert_router[l], jnp.zeros((D, pad_lanes), F32)], axis=1)
        b_router = jnp.concatenate(
            [b_group_router[l], b_expert_router[l], jnp.zeros((pad_lanes,), F32)]).reshape(1, LANES)
        hmix, xn_packed, route, counts = _out_proj(
            ab.reshape(T, MIX_WIDTH), h, w_out[l].astype(BF16), norm_ffn_w[l].reshape(1, D),
            w_router.astype(BF16), b_router)
        pstart_row, tables, pads, n_blocks = _block_tables(counts, T)
        dest = _plan(route, pstart_row)[:, :TOP_K]
        xs = _dispatch(*pads, _choice_major(dest, DISP_TM), xn_packed, n_blocks)
        ys = _experts(tables, xs, w_gate[l], w_up[l], w_down[l])
        h = _combine(_choice_major(dest, CMB_TM), hmix, route, ys)
    return h.reshape(B, S, D)
```

```python
import functools

import numpy as np
import jax
import jax.numpy as jnp
from jax import lax
from jax.experimental import pallas as pl
from jax.experimental.pallas import tpu as pltpu

D_MODEL = 2048
POOL_WIDTH = 1024
POOL_WINDOWS = (2, 4, 8, 16)
N_POOL_GROUPS = 4
POOL_GROUP = 256
HEAD_DIM = 128
N_Q_HEADS = 8
N_KV_HEADS = 2
Q_PER_KV = 4
ATTN_WIDTH = 1024
KV_WIDTH = 256
IN_PROJ_WIDTH = 2560
MIX_WIDTH = 2048
WINDOW = 128
BLOCK = 128
BAND = 3 * BLOCK
ROPE_THETA = 10000.0
N_GROUPS = 8
EXPERTS_PER_GROUP = 8
N_EXPERTS = 64
TOP_K = 2
D_EXPERT = 512
MOE_BLOCK = 256
EPS = 1e-6

LANES = 128
SUBLANES = 8
HALF = D_MODEL // 2
PACK_CHUNKS = HALF // LANES
NEG_BIG = -1e30
VMEM_LIMIT = 56 * 1024 * 1024

BF16 = jnp.bfloat16
F32 = jnp.float32
U32 = jnp.uint32
HI_MASK = 0xFFFF0000


def _cparams(n_axes):
    return pltpu.CompilerParams(dimension_semantics=("arbitrary",) * n_axes,
                                vmem_limit_bytes=VMEM_LIMIT)


def _pack_pair(lo, hi):
    lo_bits = lax.bitcast_convert_type(lo.astype(BF16).astype(F32), U32) >> 16
    hi_bits = lax.bitcast_convert_type(hi.astype(BF16).astype(F32), U32) & jnp.uint32(HI_MASK)
    return lo_bits | hi_bits


def _unpack_pair(words):
    lo = lax.bitcast_convert_type(words << 16, F32)
    hi = lax.bitcast_convert_type(words & jnp.uint32(HI_MASK), F32)
    return lo, hi


IN_TM = 256


def _in_proj_kernel(x_ref, nw_ref, w_ref, qnw_ref, knw_ref, cos_ref, sin_ref,
                    u_ref, q_ref, k_ref, v_ref):
    x = x_ref[...]
    ms = jnp.mean(x * x, axis=-1, keepdims=True)
    xn = (x * lax.rsqrt(ms + EPS) * nw_ref[...]).astype(BF16)
    cos = cos_ref[...]
    sin = sin_ref[...]

    def head_norm_rope(t, w, scale):
        hms = jnp.mean(t * t, axis=-1, keepdims=True)
        t = t * lax.rsqrt(hms + EPS) * w
        t = t * cos + pltpu.roll(t, HEAD_DIM // 2, axis=1) * sin
        return t * scale

    u_ref[...] = jnp.dot(xn, w_ref[:, :POOL_WIDTH], preferred_element_type=F32).astype(BF16)
    zq = jnp.dot(xn, w_ref[:, POOL_WIDTH:POOL_WIDTH + ATTN_WIDTH], preferred_element_type=F32)
    qnw = qnw_ref[...]
    for h in range(N_Q_HEADS):
        sl = slice(h * HEAD_DIM, (h + 1) * HEAD_DIM)
        q_ref[:, sl] = head_norm_rope(zq[:, sl], qnw, HEAD_DIM ** -0.5).astype(BF16)
    o_k = POOL_WIDTH + ATTN_WIDTH
    zk = jnp.dot(xn, w_ref[:, o_k:o_k + KV_WIDTH], preferred_element_type=F32)
    knw = knw_ref[...]
    for h in range(N_KV_HEADS):
        sl = slice(h * HEAD_DIM, (h + 1) * HEAD_DIM)
        k_ref[:, sl] = head_norm_rope(zk[:, sl], knw, 1.0).astype(BF16)
    v_ref[...] = jnp.dot(xn, w_ref[:, o_k + KV_WIDTH:], preferred_element_type=F32).astype(BF16)


def _in_proj(x2, nw, w_in, qnw, knw, cos, sin, seq):
    T = x2.shape[0]
    tm = IN_TM
    pos_blocks = seq // tm
    full = lambda shape: pl.BlockSpec(shape, lambda i: (0,) * len(shape))
    return pl.pallas_call(
        _in_proj_kernel,
        grid=(T // tm,),
        in_specs=[
            pl.BlockSpec((tm, D_MODEL), lambda i: (i, 0)),
            full((1, D_MODEL)),
            full((D_MODEL, IN_PROJ_WIDTH)),
            full((1, HEAD_DIM)),
            full((1, HEAD_DIM)),
            pl.BlockSpec((tm, HEAD_DIM), lambda i: (i % pos_blocks, 0)),
            pl.BlockSpec((tm, HEAD_DIM), lambda i: (i % pos_blocks, 0)),
        ],
        out_specs=[
            pl.BlockSpec((tm, POOL_WIDTH), lambda i: (i, 0)),
            pl.BlockSpec((tm, ATTN_WIDTH), lambda i: (i, 0)),
            pl.BlockSpec((tm, KV_WIDTH), lambda i: (i, 0)),
            pl.BlockSpec((tm, KV_WIDTH), lambda i: (i, 0)),
        ],
        out_shape=[
            jax.ShapeDtypeStruct((T, POOL_WIDTH), BF16),
            jax.ShapeDtypeStruct((T, ATTN_WIDTH), BF16),
            jax.ShapeDtypeStruct((T, KV_WIDTH), BF16),
            jax.ShapeDtypeStruct((T, KV_WIDTH), BF16),
        ],
        compiler_params=_cparams(1),
        name="in_proj",
    )(x2, nw, w_in, qnw, knw, cos, sin)


MIX_TQ = 256


def _band_constants(seq):
    nb = seq // BLOCK
    coef = np.zeros((3, N_POOL_GROUPS, BLOCK, BAND), np.float32)
    inv_count = np.zeros((3, N_POOL_GROUPS, BLOCK, 1), np.float32)
    bias = np.zeros((3, BLOCK, BAND), np.float32)
    for kind, n in enumerate((0, 1, nb - 1)):
        start = min(max((n - 1) * BLOCK, 0), seq - BAND)
        t = n * BLOCK + np.arange(BLOCK)[:, None]
        s = start + np.arange(BAND)[None, :]
        bias[kind] = np.where(np.abs(s - t) <= WINDOW, 0.0, NEG_BIG)
        for g, win in enumerate(POOL_WINDOWS):
            half = win // 2
            lo = np.clip(t - half, 0, seq)
            hi = np.clip(t + half, 0, seq)
            count = (hi - lo).astype(np.float32)
            inside = ((s >= lo) & (s < hi)).astype(np.float32)
            coef[kind, g] = inside - count * (s == t)
            inv_count[kind, g] = 1.0 / count
    return coef, inv_count, bias


def _mixers_kernel(sink_ref, u_ref, q_ref, k_ref, v_ref, coef_ref, invc_ref, bias_ref,
                   wpool_ref, pscale_ref, ab_ref, *, seq):
    nb = seq // BLOCK
    j = pl.program_id(1)
    for r in range(MIX_TQ // BLOCK):
        n = j * (MIX_TQ // BLOCK) + r
        start = pl.multiple_of(jnp.clip((n - 1) * BLOCK, 0, seq - BAND), BLOCK)
        kind = jnp.where(n == 0, 0, jnp.where(n == nb - 1, 2, 1))
        rows = slice(r * BLOCK, (r + 1) * BLOCK)

        for g in range(N_POOL_GROUPS):
            cols = slice(g * POOL_GROUP, (g + 1) * POOL_GROUP)
            ub = u_ref[0, pl.ds(start, BAND), cols]
            d = jnp.dot(coef_ref[kind, g], ub, preferred_element_type=F32) * invc_ref[kind, g]
            y = jnp.dot(d.astype(BF16), wpool_ref[g], preferred_element_type=F32)
            ab_ref[0, rows, cols] = (y * pscale_ref[:, cols]).astype(BF16)

        bias = bias_ref[kind]
        for hk in range(N_KV_HEADS):
            kcols = slice(hk * HEAD_DIM, (hk + 1) * HEAD_DIM)
            kb = k_ref[0, pl.ds(start, BAND), kcols]
            vb = v_ref[0, pl.ds(start, BAND), kcols]
            heads = [hk * Q_PER_KV + g for g in range(Q_PER_KV)]
            qs = jnp.concatenate(
                [q_ref[0, rows, h * HEAD_DIM:(h + 1) * HEAD_DIM] for h in heads], axis=0)
            s = lax.dot_general(qs, kb, (((1,), (1,)), ((), ())), preferred_element_type=F32)
            for gi, h in enumerate(heads):
                sh = s[gi * BLOCK:(gi + 1) * BLOCK] + bias
                sink = sink_ref[h]
                m = jnp.maximum(jnp.max(sh, axis=-1, keepdims=True), sink)
                p = jnp.exp(sh - m)
                denom = jnp.sum(p, axis=-1, keepdims=True) + jnp.exp(sink - m)
                o = jnp.dot(p.astype(BF16), vb, preferred_element_type=F32) / denom
                ocols = slice(POOL_WIDTH + h * HEAD_DIM, POOL_WIDTH + (h + 1) * HEAD_DIM)
                ab_ref[0, rows, ocols] = o.astype(BF16)


def _mixers(sink, u, q, k, v, coef, invc, bias, wpool, pscale):
    B, S, _ = u.shape
    tq = MIX_TQ
    full = lambda shape: pl.BlockSpec(shape, lambda b, j: (0,) * len(shape))
    return pl.pallas_call(
        functools.partial(_mixers_kernel, seq=S),
        grid=(B, S // tq),
        in_specs=[
            pl.BlockSpec(memory_space=pltpu.SMEM),
            pl.BlockSpec((1, S, POOL_WIDTH), lambda b, j: (b, 0, 0)),
            pl.BlockSpec((1, tq, ATTN_WIDTH), lambda b, j: (b, j, 0)),
            pl.BlockSpec((1, S, KV_WIDTH), lambda b, j: (b, 0, 0)),
            pl.BlockSpec((1, S, KV_WIDTH), lambda b, j: (b, 0, 0)),
            full((3, N_POOL_GROUPS, BLOCK, BAND)),
            full((3, N_POOL_GROUPS, BLOCK, 1)),
            full((3, BLOCK, BAND)),
            full((N_POOL_GROUPS, POOL_GROUP, POOL_GROUP)),
            full((1, POOL_WIDTH)),
        ],
        out_specs=pl.BlockSpec((1, tq, MIX_WIDTH), lambda b, j: (b, j, 0)),
        out_shape=jax.ShapeDtypeStruct((B, S, MIX_WIDTH), BF16),
        compiler_params=_cparams(2),
        name="mixers",
    )(sink, u, q, k, v, coef, invc, bias, wpool, pscale)


OUT_TM = 256
ROUTE_E0, ROUTE_E1, ROUTE_W0, ROUTE_W1 = 0, 1, 2, 3


def _out_proj_kernel(ab_ref, x_ref, w_ref, nw_ref, wr_ref, br_ref,
                     h_ref, xn_ref, route_ref, counts_ref):
    h = x_ref[...] + jnp.dot(ab_ref[...], w_ref[...], preferred_element_type=F32)
    h_ref[...] = h
    ms = jnp.mean(h * h, axis=-1, keepdims=True)
    xn = h * lax.rsqrt(ms + EPS) * nw_ref[...]
    packed = _pack_pair(xn[:, :HALF], xn[:, HALF:])
    for c in range(PACK_CHUNKS):
        xn_ref[pl.ds(c, OUT_TM, stride=SUBLANES), :] = packed[:, c * LANES:(c + 1) * LANES]
    logits = jnp.dot(xn.astype(BF16), wr_ref[...], preferred_element_type=F32) + br_ref[...]

    lane = lax.broadcasted_iota(jnp.int32, logits.shape, 1)

    def first_argmax(vals):
        m = jnp.max(vals, axis=-1, keepdims=True)
        idx = jnp.min(jnp.where(vals == m, lane, LANES), axis=-1, keepdims=True)
        return m, idx

    gl = jnp.where(lane < N_GROUPS, logits, NEG_BIG)
    gmax, gidx = first_argmax(gl)
    gsum = jnp.sum(jnp.where(lane < N_GROUPS, jnp.exp(gl - gmax), 0.0), axis=-1, keepdims=True)
    g_w = 1.0 / gsum
    e_lo = N_GROUPS + gidx * EXPERTS_PER_GROUP
    el = jnp.where((lane >= e_lo) & (lane < e_lo + EXPERTS_PER_GROUP), logits, NEG_BIG)
    m1, i1 = first_argmax(el)
    m2, i2 = first_argmax(jnp.where(lane == i1, NEG_BIG, el))
    t = jnp.exp(m2 - m1)
    p1 = 1.0 / (1.0 + t)
    p2 = t * p1
    e1 = i1 - N_GROUPS
    e2 = i2 - N_GROUPS
    route_ref[...] = jnp.where(
        lane == ROUTE_E0, e1.astype(F32),
        jnp.where(lane == ROUTE_E1, e2.astype(F32),
                  jnp.where(lane == ROUTE_W0, g_w * p1,
                            jnp.where(lane == ROUTE_W1, g_w * p2, 0.0))))

    @pl.when(pl.program_id(0) == 0)
    def _():
        counts_ref[...] = jnp.zeros_like(counts_ref)

    chosen = ((lane == e1) | (lane == e2)).astype(F32)
    counts_ref[...] += jnp.sum(chosen, axis=0, keepdims=True)


def _out_proj(ab, x2, w_out, nw, w_router, b_router):
    T = x2.shape[0]
    tm = OUT_TM
    full = lambda shape: pl.BlockSpec(shape, lambda i: (0,) * len(shape))
    row = lambda width: pl.BlockSpec((tm, width), lambda i: (i, 0))
    return pl.pallas_call(
        _out_proj_kernel,
        grid=(T // tm,),
        in_specs=[row(MIX_WIDTH), row(D_MODEL), full((MIX_WIDTH, D_MODEL)), full((1, D_MODEL)),
                  full((D_MODEL, LANES)), full((1, LANES))],
        out_specs=[row(D_MODEL),
                   pl.BlockSpec((tm * SUBLANES, LANES), lambda i: (i, 0)),
                   row(LANES),
                   full((1, LANES))],
        out_shape=[jax.ShapeDtypeStruct((T, D_MODEL), F32),
                   jax.ShapeDtypeStruct((T * SUBLANES, LANES), U32),
                   jax.ShapeDtypeStruct((T, LANES), F32),
                   jax.ShapeDtypeStruct((1, LANES), F32)],
        compiler_params=_cparams(1),
        name="out_proj",
    )(ab, x2, w_out, nw, w_router, b_router)


PLAN_TM = 512


def _plan_kernel(route_ref, pstart_ref, tri_ref, dest_ref, carry_ref):
    @pl.when(pl.program_id(0) == 0)
    def _():
        carry_ref[...] = jnp.zeros_like(carry_ref)

    route = route_ref[...]
    lane = lax.broadcasted_iota(jnp.int32, route.shape, 1)
    e0 = route[:, ROUTE_E0:ROUTE_E0 + 1].astype(jnp.int32)
    e1 = route[:, ROUTE_E1:ROUTE_E1 + 1].astype(jnp.int32)
    oh0 = lane == e0
    oh1 = lane == e1
    both = (oh0 | oh1).astype(F32)
    earlier = jnp.dot(tri_ref[...], both.astype(BF16), preferred_element_type=F32)
    base = pstart_ref[...] + carry_ref[...] + earlier
    d0 = jnp.sum(jnp.where(oh0, base, 0.0), axis=-1, keepdims=True)
    d1 = jnp.sum(jnp.where(oh1, base, 0.0), axis=-1, keepdims=True)
    dest_ref[...] = jnp.where(lane == 0, d0, jnp.where(lane == 1, d1, 0.0)).astype(jnp.int32)
    carry_ref[...] += jnp.sum(both, axis=0, keepdims=True)


def _plan(route, pstart_row):
    T = route.shape[0]
    tm = PLAN_TM
    tri = jnp.asarray(np.tril(np.ones((tm, tm), np.float32), -1), BF16)
    return pl.pallas_call(
        _plan_kernel,
        grid=(T // tm,),
        in_specs=[pl.BlockSpec((tm, LANES), lambda i: (i, 0)),
                  pl.BlockSpec((1, LANES), lambda i: (0, 0)),
                  pl.BlockSpec((tm, tm), lambda i: (0, 0))],
        out_specs=pl.BlockSpec((tm, LANES), lambda i: (i, 0)),
        out_shape=jax.ShapeDtypeStruct((T, LANES), jnp.int32),
        scratch_shapes=[pltpu.VMEM((1, LANES), F32)],
        compiler_params=_cparams(1),
        name="plan",
    )(route, pstart_row, tri)


DISP_TM = 512
DMA_GROUP = 8
ROW_WORDS = SUBLANES


def _packed_row(ref, row):
    return ref.at[pl.ds(pl.multiple_of(row * ROW_WORDS, ROW_WORDS), ROW_WORDS)]


def _grouped(n, fn):
    def trip(t, carry):
        for k in range(DMA_GROUP):
            fn(t * DMA_GROUP + k, k)
        return carry
    lax.fori_loop(0, n // DMA_GROUP, trip, 0)


def _dispatch_kernel(pad_lo_ref, pad_hi_ref, nused_ref, dest_ref, xn_ref, xs_hbm,
                     zeros, sem, tail_sem, *, experts_per_step, tails_per_step, n_blocks):
    i = pl.program_id(0)
    n_rows = TOP_K * DISP_TM

    @pl.when(i == 0)
    def _():
        zeros[...] = jnp.zeros_like(zeros)

    def row_copy(a):
        src = _packed_row(xn_ref, jnp.bitwise_and(a, DISP_TM - 1))
        return pltpu.make_async_copy(src, _packed_row(xs_hbm, dest_ref[0, 0, a]), sem)

    _grouped(n_rows, lambda a, k: row_copy(a).start(priority=k % 2))

    def pad_fill(start):
        for j in range(experts_per_step):
            e = jnp.minimum(i * experts_per_step + j, N_EXPERTS - 1)
            live = i * experts_per_step + j < N_EXPERTS
            pos = pad_lo_ref[e]
            n_pad = jnp.where(live, pad_hi_ref[e] - pos, 0)
            for bit in reversed(range(MOE_BLOCK.bit_length() - 1)):
                size = 1 << bit
                take = jnp.bitwise_and(lax.shift_right_logical(n_pad, bit), 1)
                dst = xs_hbm.at[pl.ds(pl.multiple_of(pos * ROW_WORDS, ROW_WORDS), size * ROW_WORDS)]
                cp = pltpu.make_async_copy(zeros.at[pl.ds(0, size * ROW_WORDS)], dst, sem)

                @pl.when(take == 1)
                def _():
                    cp.start() if start else cp.wait()
                pos = pos + take * size

    def tail_copies(start):
        for j in range(tails_per_step):
            blk = nused_ref[0] + i * tails_per_step + j
            rows = pl.ds(pl.multiple_of(jnp.minimum(blk, n_blocks - 1) * (MOE_BLOCK * ROW_WORDS),
                                        MOE_BLOCK * ROW_WORDS), MOE_BLOCK * ROW_WORDS)
            cp = pltpu.make_async_copy(zeros, xs_hbm.at[rows], tail_sem)

            @pl.when(blk < n_blocks)
            def _():
                cp.start() if start else cp.wait()

    pad_fill(True)
    tail_copies(True)

    one_row = pltpu.make_async_copy(_packed_row(xn_ref, 0), _packed_row(xs_hbm, 0), sem)
    _grouped(n_rows, lambda a, k: one_row.wait())
    pad_fill(False)
    tail_copies(False)


def _dispatch(pad_lo, pad_hi, n_used, dest3, xn_packed, n_blocks):
    n = dest3.shape[0]
    min_used = (n * DISP_TM * TOP_K) // MOE_BLOCK
    tails_per_step = -(-(n_blocks - min_used) // n)
    grid_spec = pltpu.PrefetchScalarGridSpec(
        num_scalar_prefetch=3,
        grid=(n,),
        in_specs=[pl.BlockSpec((1, 1, TOP_K * DISP_TM), lambda i, *_: (i, 0, 0),
                               memory_space=pltpu.SMEM),
                  pl.BlockSpec((DISP_TM * ROW_WORDS, LANES), lambda i, *_: (i, 0))],
        out_specs=pl.BlockSpec(memory_space=pl.ANY),
        scratch_shapes=[pltpu.VMEM((MOE_BLOCK * ROW_WORDS, LANES), U32),
                        pltpu.SemaphoreType.DMA(()), pltpu.SemaphoreType.DMA(())],
    )
    return pl.pallas_call(
        functools.partial(_dispatch_kernel, experts_per_step=-(-N_EXPERTS // n),
                          tails_per_step=tails_per_step, n_blocks=n_blocks),
        grid_spec=grid_spec,
        out_shape=jax.ShapeDtypeStruct((n_blocks * MOE_BLOCK * ROW_WORDS, LANES), U32),
        compiler_params=_cparams(1),
        name="dispatch",
    )(pad_lo, pad_hi, n_used, dest3, xn_packed)


CAST_ROWS = 256


def _experts_kernel(be_ref, first_ref, par_ref, nxt2_ref, misc_ref,
                    xs_ref, wg_hbm, wu_hbm, wd_hbm, ys_ref,
                    fg, fu, fd, wg, wu, wd, sem):
    b = pl.program_id(0)
    n_used = misc_ref[0]

    def fetch(e, p):
        return (pltpu.make_async_copy(wg_hbm.at[e], fg.at[p], sem.at[p, 0]),
                pltpu.make_async_copy(wu_hbm.at[e], fu.at[p], sem.at[p, 1]),
                pltpu.make_async_copy(wd_hbm.at[e], fd.at[p], sem.at[p, 2]))

    @pl.when(b == 0)
    def _():
        for cp in fetch(be_ref[0], 0):
            cp.start()

        @pl.when(misc_ref[1] >= 0)
        def _():
            for cp in fetch(misc_ref[1], 1):
                cp.start()

    @pl.when((b < n_used) & (first_ref[b] == 1))
    def _():
        p = par_ref[b]
        for cp in fetch(be_ref[b], p):
            cp.wait()

        def cast(src, dst, n_rows):
            def body(i, carry):
                rows = pl.ds(pl.multiple_of(i * CAST_ROWS, CAST_ROWS), CAST_ROWS)
                dst[rows, :] = src[p, rows, :].astype(BF16)
                return carry
            lax.fori_loop(0, n_rows // CAST_ROWS, body, 0)

        cast(fg, wg, D_MODEL)
        cast(fu, wu, D_MODEL)
        cast(fd, wd, D_EXPERT)

        @pl.when(nxt2_ref[b] >= 0)
        def _():
            for cp in fetch(nxt2_ref[b], p):
                cp.start()

    @pl.when(b < n_used)
    def _():
        lo_parts, hi_parts = [], []
        for c in range(PACK_CHUNKS):
            lo, hi = _unpack_pair(xs_ref[pl.ds(c, MOE_BLOCK, stride=SUBLANES), :])
            lo_parts.append(lo.astype(BF16))
            hi_parts.append(hi.astype(BF16))
        xb = jnp.concatenate(lo_parts + hi_parts, axis=1)
        g = jnp.dot(xb, wg[...], preferred_element_type=F32)
        u = jnp.dot(xb, wu[...], preferred_element_type=F32)
        hmid = (g * jax.nn.sigmoid(g) * u).astype(BF16)
        y = jnp.dot(hmid, wd[...], preferred_element_type=F32)
        packed = _pack_pair(y[:, :HALF], y[:, HALF:])
        for c in range(PACK_CHUNKS):
            ys_ref[pl.ds(c, MOE_BLOCK, stride=SUBLANES), :] = packed[:, c * LANES:(c + 1) * LANES]

    @pl.when(b >= n_used)
    def _():
        ys_ref[...] = jnp.zeros_like(ys_ref)


def _experts(tables, xs, w_gate, w_up, w_down):
    n_blocks = tables[0].shape[0]
    blk = lambda: pl.BlockSpec((MOE_BLOCK * SUBLANES, LANES), lambda b, *_: (b, 0))
    grid_spec = pltpu.PrefetchScalarGridSpec(
        num_scalar_prefetch=len(tables),
        grid=(n_blocks,),
        in_specs=[blk(),
                  pl.BlockSpec(memory_space=pl.ANY),
                  pl.BlockSpec(memory_space=pl.ANY),
                  pl.BlockSpec(memory_space=pl.ANY)],
        out_specs=blk(),
        scratch_shapes=[pltpu.VMEM((2, D_MODEL, D_EXPERT), F32),
                        pltpu.VMEM((2, D_MODEL, D_EXPERT), F32),
                        pltpu.VMEM((2, D_EXPERT, D_MODEL), F32),
                        pltpu.VMEM((D_MODEL, D_EXPERT), BF16),
                        pltpu.VMEM((D_MODEL, D_EXPERT), BF16),
                        pltpu.VMEM((D_EXPERT, D_MODEL), BF16),
                        pltpu.SemaphoreType.DMA((2, 3))],
    )
    return pl.pallas_call(
        _experts_kernel,
        grid_spec=grid_spec,
        out_shape=jax.ShapeDtypeStruct((n_blocks * MOE_BLOCK * SUBLANES, LANES), U32),
        compiler_params=_cparams(1),
        name="experts",
    )(*tables, xs, w_gate, w_up, w_down)


CMB_TM = 128


def _combine_kernel(d_cur_ref, d_next_ref, h_ref, route_ref, ys_hbm, o_ref, ybuf, sem):
    i = pl.program_id(0)
    n = pl.num_programs(0)
    slot = i % 2
    n_rows = TOP_K * CMB_TM

    def gather(idx_ref, s):
        def start(a, k):
            pltpu.make_async_copy(_packed_row(ys_hbm, idx_ref[0, 0, a]),
                                  _packed_row(ybuf.at[s], a), sem.at[s]).start(priority=k % 2)
        _grouped(n_rows, start)

    @pl.when(i == 0)
    def _():
        gather(d_cur_ref, 0)

    @pl.when(i + 1 < n)
    def _():
        gather(d_next_ref, 1 - slot)

    one_row = pltpu.make_async_copy(_packed_row(ys_hbm, 0), _packed_row(ybuf.at[slot], 0),
                                    sem.at[slot])
    _grouped(n_rows, lambda a, k: one_row.wait())
    route = route_ref[...]
    w0 = route[:, ROUTE_W0:ROUTE_W0 + 1]
    w1 = route[:, ROUTE_W1:ROUTE_W1 + 1]
    yb = ybuf.at[slot]
    for c in range(PACK_CHUNKS):
        lo0, hi0 = _unpack_pair(yb[pl.ds(c, CMB_TM, stride=SUBLANES), :])
        lo1, hi1 = _unpack_pair(yb[pl.ds(CMB_TM * SUBLANES + c, CMB_TM, stride=SUBLANES), :])
        lo_cols = slice(c * LANES, (c + 1) * LANES)
        hi_cols = slice(HALF + c * LANES, HALF + (c + 1) * LANES)
        o_ref[:, lo_cols] = h_ref[:, lo_cols] + (w0 * lo0 + w1 * lo1)
        o_ref[:, hi_cols] = h_ref[:, hi_cols] + (w0 * hi0 + w1 * hi1)


def _combine(dest3, h, route, ys):
    T = h.shape[0]
    tm = CMB_TM
    n = T // tm
    d_spec = lambda fn: pl.BlockSpec((1, 1, TOP_K * tm), fn, memory_space=pltpu.SMEM)
    return pl.pallas_call(
        _combine_kernel,
        grid=(n,),
        in_specs=[
            d_spec(lambda i: (i, 0, 0)),
            d_spec(lambda i: (jnp.minimum(i + 1, n - 1), 0, 0)),
            pl.BlockSpec((tm, D_MODEL), lambda i: (i, 0)),
            pl.BlockSpec((tm, LANES), lambda i: (i, 0)),
            pl.BlockSpec(memory_space=pl.ANY),
        ],
        out_specs=pl.BlockSpec((tm, D_MODEL), lambda i: (i, 0)),
        out_shape=jax.ShapeDtypeStruct((T, D_MODEL), F32),
        scratch_shapes=[pltpu.VMEM((2, TOP_K * tm * SUBLANES, LANES), U32),
                        pltpu.SemaphoreType.DMA((2,))],
        compiler_params=_cparams(1),
        name="combine",
    )(dest3, dest3, h, route, ys)


def _block_tables(counts_row, T):
    counts = counts_row[0, :N_EXPERTS].astype(jnp.int32)
    padded = (counts + MOE_BLOCK - 1) // MOE_BLOCK * MOE_BLOCK
    pends = jnp.cumsum(padded)
    pstarts = pends - padded
    n_blocks = -(-(T * TOP_K + N_EXPERTS * (MOE_BLOCK - 1)) // MOE_BLOCK)
    n_used = pends[-1] // MOE_BLOCK
    block_start = jnp.arange(n_blocks, dtype=jnp.int32) * MOE_BLOCK
    be = jnp.minimum(jnp.sum(block_start[:, None] >= pends[None, :], axis=1),
                     N_EXPERTS - 1).astype(jnp.int32)
    first =jnp.concatenate([jnp.ones((1,), jnp.int32), (be[1:] != be[:-1]).astype(jnp.int32)])
    par = (jnp.cumsum(first) - 1) % 2
    def following(block):
        end = pends[be[jnp.minimum(block, n_blocks - 1)]] // MOE_BLOCK
        return jnp.where(block < n_used, end, n_used)
    nn_block = following(following(jnp.arange(n_blocks, dtype=jnp.int32)))
    nxt2 = jnp.where(nn_block < n_used, be[jnp.minimum(nn_block, n_blocks - 1)], -1)
    second_run = following(jnp.zeros((), jnp.int32))
    second_expert = jnp.where(second_run < n_used, be[jnp.minimum(second_run, n_blocks - 1)], -1)
    pstart_row = jnp.zeros((1, LANES), F32).at[0, :N_EXPERTS].set(pstarts.astype(F32))
    n_used = n_used.astype(jnp.int32).reshape(1)
    misc = jnp.stack([n_used[0], second_expert.astype(jnp.int32)])
    tables = tuple(t.astype(jnp.int32) for t in (be, first, par, nxt2, misc))
    pads = ((pstarts + counts).astype(jnp.int32), pends.astype(jnp.int32), n_used)
    return pstart_row, tables, pads, n_blocks


def _choice_major(dest, tile):
    T = dest.shape[0]
    return jnp.transpose(dest.reshape(T // tile, tile, TOP_K), (0, 2, 1)).reshape(
        T // tile, 1, TOP_K * tile)


def _rope_tables(S):
    inv = ROPE_THETA ** (-jnp.arange(0, HEAD_DIM, 2, dtype=F32) / HEAD_DIM)
    ang = jnp.arange(S, dtype=F32)[:, None] * inv[None, :]
    cos, sin = jnp.cos(ang), jnp.sin(ang)
    return jnp.concatenate([cos, cos], axis=-1), jnp.concatenate([-sin, sin], axis=-1)


def kernel(x, norm_mix_w, w_in, w_pool, pool_scale, q_norm_w, k_norm_w, sink_logits, w_out,
           norm_ffn_w, w_group_router, b_group_router, w_expert_router, b_expert_router,
           w_gate, w_up, w_down):
    B, S, D = x.shape
    T = B * S
    depth = w_in.shape[0]
    cos, sin = _rope_tables(S)
    coef_np, invc_np, bias_np = _band_constants(S)
    coef = jnp.asarray(coef_np, BF16)
    invc = jnp.asarray(invc_np, F32)
    bias = jnp.asarray(bias_np, F32)
    pad_lanes = LANES - N_GROUPS - N_EXPERTS

    h = x.reshape(T, D)
    for l in range(depth):
        u, q, k, v = _in_proj(h, norm_mix_w[l].reshape(1, D), w_in[l].astype(BF16),
                              q_norm_w[l].reshape(1, HEAD_DIM), k_norm_w[l].reshape(1, HEAD_DIM),
                              cos, sin, S)
        ab = _mixers(sink_logits[l], u.reshape(B, S, -1), q.reshape(B, S, -1),
                     k.reshape(B, S, -1), v.reshape(B, S, -1), coef, invc, bias,
                     w_pool[l].astype(BF16), pool_scale[l].reshape(1, POOL_WIDTH))
        w_router = jnp.concatenate(
            [w_group_router[l], w_expert_router[l], jnp.zeros((D, pad_lanes), F32)], axis=1)
        b_router = jnp.concatenate(
            [b_group_router[l], b_expert_router[l], jnp.zeros((pad_lanes,), F32)]).reshape(1, LANES)
        hmix, xn_packed, route, counts = _out_proj(
            ab.reshape(T, MIX_WIDTH), h, w_out[l].astype(BF16), norm_ffn_w[l].reshape(1, D),
            w_router.astype(BF16), b_router)
        pstart_row, tables, pads, n_blocks = _block_tables(counts, T)
        dest = _plan(route, pstart_row)[:, :TOP_K]
        xs = _dispatch(*pads, _choice_major(dest, DISP_TM), xn_packed, n_blocks)
        ys = _experts(tables, xs, w_gate[l], w_up[l], w_down[l])
        h = _combine(_choice_major(dest, CMB_TM), hmix, route, ys)
    return h.reshape(B, S, D)
```

```python
import functools

import numpy as np
import jax
import jax.numpy as jnp
from jax import lax
from jax.experimental import pallas as pl
from jax.experimental.pallas import tpu as pltpu

D_MODEL = 2048
POOL_WIDTH = 1024
POOL_WINDOWS = (2, 4, 8, 16)
N_POOL_GROUPS = 4
POOL_GROUP = 256
HEAD_DIM = 128
N_Q_HEADS = 8
N_KV_HEADS = 2
Q_PER_KV = 4
ATTN_WIDTH = 1024
KV_WIDTH = 256
IN_PROJ_WIDTH = 2560
MIX_WIDTH = 2048
WINDOW = 128
BLOCK = 128
BAND = 3 * BLOCK
ROPE_THETA = 10000.0
N_GROUPS = 8
EXPERTS_PER_GROUP = 8
N_EXPERTS = 64
TOP_K = 2
D_EXPERT = 512
MOE_BLOCK = 256
EPS = 1e-6

LANES = 128
SUBLANES = 8
HALF = D_MODEL // 2
PACK_CHUNKS = HALF // LANES
NEG_BIG = -1e30
VMEM_LIMIT = 56 * 1024 * 1024

BF16 = jnp.bfloat16
F32 = jnp.float32
U32 = jnp.uint32
HI_MASK = 0xFFFF0000


def _cparams(n_axes):
    return pltpu.CompilerParams(dimension_semantics=("arbitrary",) * n_axes,
                                vmem_limit_bytes=VMEM_LIMIT)


def _pack_pair(lo, hi):
    lo_bits = lax.bitcast_convert_type(lo.astype(BF16).astype(F32), U32) >> 16
    hi_bits = lax.bitcast_convert_type(hi.astype(BF16).astype(F32), U32) & jnp.uint32(HI_MASK)
    return lo_bits | hi_bits


def _unpack_pair(words):
    lo = lax.bitcast_convert_type(words << 16, F32)
    hi = lax.bitcast_convert_type(words & jnp.uint32(HI_MASK), F32)
    return lo, hi


IN_TM = 512
ROW_SPLIT = 2


def _in_proj_kernel(x_ref, nw_ref, w_ref, qnw_ref, knw_ref, cos_ref, sin_ref,
                    u_ref, q_ref, k_ref, v_ref):
    sub = IN_TM // ROW_SPLIT
    for part in range(ROW_SPLIT):
        rows = slice(part * sub, (part + 1) * sub)
        x = x_ref[rows, :]
        ms = jnp.mean(x * x, axis=-1, keepdims=True)
        xn = (x * lax.rsqrt(ms + EPS) * nw_ref[...]).astype(BF16)
        cos = cos_ref[rows, :]
        sin = sin_ref[rows, :]

        def head_norm_rope(t, w, scale):
            hms = jnp.mean(t * t, axis=-1, keepdims=True)
            t = t * lax.rsqrt(hms + EPS) * w
            t = t * cos + pltpu.roll(t, HEAD_DIM // 2, axis=1) * sin
            return t * scale

        u_ref[rows, :] = jnp.dot(xn, w_ref[:, :POOL_WIDTH],
                                 preferred_element_type=F32).astype(BF16)
        zq = jnp.dot(xn, w_ref[:, POOL_WIDTH:POOL_WIDTH + ATTN_WIDTH],
                     preferred_element_type=F32)
        qnw = qnw_ref[...]
        for h in range(N_Q_HEADS):
            sl = slice(h * HEAD_DIM, (h + 1) * HEAD_DIM)
            q_ref[rows, sl] = head_norm_rope(zq[:, sl], qnw, HEAD_DIM ** -0.5).astype(BF16)
        o_k = POOL_WIDTH + ATTN_WIDTH
        zk = jnp.dot(xn, w_ref[:, o_k:o_k + KV_WIDTH], preferred_element_type=F32)
        knw = knw_ref[...]
        for h in range(N_KV_HEADS):
            sl = slice(h * HEAD_DIM, (h + 1) * HEAD_DIM)
            k_ref[rows, sl] = head_norm_rope(zk[:, sl], knw, 1.0).astype(BF16)
        v_ref[rows, :] = jnp.dot(xn, w_ref[:, o_k + KV_WIDTH:],
                                 preferred_element_type=F32).astype(BF16)


def _in_proj(x2, nw, w_in, qnw, knw, cos, sin, seq):
    T = x2.shape[0]
    tm = IN_TM
    pos_blocks = seq // tm
    full = lambda shape: pl.BlockSpec(shape, lambda i: (0,) * len(shape))
    return pl.pallas_call(
        _in_proj_kernel,
        grid=(T // tm,),
        in_specs=[
            pl.BlockSpec((tm, D_MODEL), lambda i: (i, 0)),
            full((1, D_MODEL)),
            full((D_MODEL, IN_PROJ_WIDTH)),
            full((1, HEAD_DIM)),
            full((1, HEAD_DIM)),
            pl.BlockSpec((tm, HEAD_DIM), lambda i: (i % pos_blocks, 0)),
            pl.BlockSpec((tm, HEAD_DIM), lambda i: (i % pos_blocks, 0)),
        ],
        out_specs=[
            pl.BlockSpec((tm, POOL_WIDTH), lambda i: (i, 0)),
            pl.BlockSpec((tm, ATTN_WIDTH), lambda i: (i, 0)),
            pl.BlockSpec((tm, KV_WIDTH), lambda i: (i, 0)),
            pl.BlockSpec((tm, KV_WIDTH), lambda i: (i, 0)),
        ],
        out_shape=[
            jax.ShapeDtypeStruct((T, POOL_WIDTH), BF16),
            jax.ShapeDtypeStruct((T, ATTN_WIDTH), BF16),
            jax.ShapeDtypeStruct((T, KV_WIDTH), BF16),
            jax.ShapeDtypeStruct((T, KV_WIDTH), BF16),
        ],
        compiler_params=_cparams(1),
        name="in_proj",
    )(x2, nw, w_in, qnw, knw, cos, sin)


MIX_TQ = 256


def _band_constants(seq):
    nb = seq // BLOCK
    coef = np.zeros((3, N_POOL_GROUPS, BLOCK, BAND), np.float32)
    inv_count = np.zeros((3, N_POOL_GROUPS, BLOCK, 1), np.float32)
    bias = np.zeros((3, BLOCK, BAND), np.float32)
    for kind, n in enumerate((0, 1, nb - 1)):
        start = min(max((n - 1) * BLOCK, 0), seq - BAND)
        t = n * BLOCK + np.arange(BLOCK)[:, None]
        s = start + np.arange(BAND)[None, :]
        bias[kind] = np.where(np.abs(s - t) <= WINDOW, 0.0, NEG_BIG)
        for g, win in enumerate(POOL_WINDOWS):
            half = win // 2
            lo = np.clip(t - half, 0, seq)
            hi = np.clip(t + half, 0, seq)
            count = (hi - lo).astype(np.float32)
            inside = ((s >= lo) & (s < hi)).astype(np.float32)
            coef[kind, g] = inside - count * (s == t)
            inv_count[kind, g] = 1.0 / count
    return coef, inv_count, bias


def _mixers_kernel(sink_ref, u_ref, q_ref, k_ref, v_ref, coef_ref, invc_ref, bias_ref,
                   wpool_ref, pscale_ref, ab_ref, *, seq):
    nb = seq // BLOCK
    j = pl.program_id(1)
    n_sub = MIX_TQ // BLOCK
    subs = []
    for r in range(n_sub):
        n = j * n_sub + r
        start = pl.multiple_of(jnp.clip((n - 1) * BLOCK, 0, seq - BAND), BLOCK)
        kind = jnp.where(n == 0, 0, jnp.where(n == nb - 1, 2, 1))
        subs.append((start, kind, slice(r * BLOCK, (r + 1) * BLOCK)))

    scores = {}
    for r, (start, kind, rows) in enumerate(subs):
        for hk in range(N_KV_HEADS):
            kb = k_ref[0, pl.ds(start, BAND), hk * HEAD_DIM:(hk + 1) * HEAD_DIM]
            qs = jnp.concatenate(
                [q_ref[0, rows, h * HEAD_DIM:(h + 1) * HEAD_DIM]
                 for h in range(hk * Q_PER_KV, (hk + 1) * Q_PER_KV)], axis=0)
            scores[r, hk] = lax.dot_general(qs, kb, (((1,), (1,)), ((), ())),
                                            preferred_element_type=F32)
    diffs = {}
    for r, (start, kind, rows) in enumerate(subs):
        for g in range(N_POOL_GROUPS):
            ub = u_ref[0, pl.ds(start, BAND), g * POOL_GROUP:(g + 1) * POOL_GROUP]
            d = jnp.dot(coef_ref[kind, g], ub, preferred_element_type=F32) * invc_ref[kind, g]
            diffs[r, g] = d.astype(BF16)
    probs = {}
    for r, (start, kind, rows) in enumerate(subs):
        bias = bias_ref[kind]
        for hk in range(N_KV_HEADS):
            for gi in range(Q_PER_KV):
                h = hk * Q_PER_KV + gi
                sh = scores[r, hk][gi * BLOCK:(gi + 1) * BLOCK] + bias
                sink = sink_ref[h]
                m = jnp.maximum(jnp.max(sh, axis=-1, keepdims=True), sink)
                p = jnp.exp(sh - m)
                denom = jnp.sum(p, axis=-1, keepdims=True) + jnp.exp(sink - m)
                probs[r, h] = (p.astype(BF16), denom)
    for r, (start, kind, rows) in enumerate(subs):
        for g in range(N_POOL_GROUPS):
            cols = slice(g * POOL_GROUP, (g + 1) * POOL_GROUP)
            y = jnp.dot(diffs[r, g], wpool_ref[g], preferred_element_type=F32)
            ab_ref[0, rows, cols] = (y * pscale_ref[:, cols]).astype(BF16)
    for r, (start, kind, rows) in enumerate(subs):
        for hk in range(N_KV_HEADS):
            vb = v_ref[0, pl.ds(start, BAND), hk * HEAD_DIM:(hk + 1) * HEAD_DIM]
            for gi in range(Q_PER_KV):
                h = hk * Q_PER_KV + gi
                p, denom = probs[r, h]
                o = jnp.dot(p, vb, preferred_element_type=F32) / denom
                ocols = slice(POOL_WIDTH + h * HEAD_DIM, POOL_WIDTH + (h + 1) * HEAD_DIM)
                ab_ref[0, rows, ocols] = o.astype(BF16)


def _mixers(sink, u, q, k, v, coef, invc, bias, wpool, pscale):
    B, S, _ = u.shape
    tq = MIX_TQ
    full = lambda shape: pl.BlockSpec(shape, lambda b, j: (0,) * len(shape))
    return pl.pallas_call(
        functools.partial(_mixers_kernel, seq=S),
        grid=(B, S // tq),
        in_specs=[
            pl.BlockSpec(memory_space=pltpu.SMEM),
            pl.BlockSpec((1, S, POOL_WIDTH), lambda b, j: (b, 0, 0)),
            pl.BlockSpec((1, tq, ATTN_WIDTH), lambda b, j: (b, j, 0)),
            pl.BlockSpec((1, S, KV_WIDTH), lambda b, j: (b, 0, 0)),
            pl.BlockSpec((1, S, KV_WIDTH), lambda b, j: (b, 0, 0)),
            full((3, N_POOL_GROUPS, BLOCK, BAND)),
            full((3, N_POOL_GROUPS, BLOCK, 1)),
            full((3, BLOCK, BAND)),
            full((N_POOL_GROUPS, POOL_GROUP, POOL_GROUP)),
            full((1, POOL_WIDTH)),
        ],
        out_specs=pl.BlockSpec((1, tq, MIX_WIDTH), lambda b, j: (b, j, 0)),
        out_shape=jax.ShapeDtypeStruct((B, S, MIX_WIDTH), BF16),
        compiler_params=_cparams(2),
        name="mixers",
    )(sink, u, q, k, v, coef, invc, bias, wpool, pscale)


OUT_TM = 512
ROUTE_E0, ROUTE_E1, ROUTE_W0, ROUTE_W1 = 0, 1, 2, 3


def _out_proj_kernel(ab_ref, x_ref, w_ref, nw_ref, wr_ref, br_ref,
                     h_ref, xn_ref, route_ref, counts_ref):
    @pl.when(pl.program_id(0) == 0)
    def _():
        counts_ref[...] = jnp.zeros_like(counts_ref)

    sub = OUT_TM // ROW_SPLIT
    lane = lax.broadcasted_iota(jnp.int32, (sub, LANES), 1)

    def first_argmax(vals):
        m = jnp.max(vals, axis=-1, keepdims=True)
        idx = jnp.min(jnp.where(vals == m, lane, LANES), axis=-1, keepdims=True)
        return m, idx

    new_counts = jnp.zeros((1, LANES), F32)
    for part in range(ROW_SPLIT):
        rows = slice(part * sub, (part + 1) * sub)
        h = x_ref[rows, :] + jnp.dot(ab_ref[rows, :], w_ref[...], preferred_element_type=F32)
        h_ref[rows, :] = h
        ms = jnp.mean(h * h, axis=-1, keepdims=True)
        xn = h * lax.rsqrt(ms + EPS) * nw_ref[...]
        packed = _pack_pair(xn[:, :HALF], xn[:, HALF:])
        for c in range(PACK_CHUNKS):
            xn_ref[pl.ds(part * sub * SUBLANES + c, sub, stride=SUBLANES), :] = (
                packed[:, c * LANES:(c + 1) * LANES])
        logits = jnp.dot(xn.astype(BF16), wr_ref[...], preferred_element_type=F32) + br_ref[...]

        gl = jnp.where(lane < N_GROUPS, logits, NEG_BIG)
        gmax, gidx = first_argmax(gl)
        gsum = jnp.sum(jnp.where(lane < N_GROUPS, jnp.exp(gl - gmax), 0.0),
                       axis=-1, keepdims=True)
        g_w = 1.0 / gsum
        e_lo = N_GROUPS + gidx * EXPERTS_PER_GROUP
        el = jnp.where((lane >= e_lo) & (lane < e_lo + EXPERTS_PER_GROUP), logits, NEG_BIG)
        m1, i1 = first_argmax(el)
        m2, i2 = first_argmax(jnp.where(lane == i1, NEG_BIG, el))
        t = jnp.exp(m2 - m1)
        p1 = 1.0 / (1.0 + t)
        p2 = t * p1
        e1 = i1 - N_GROUPS
        e2 = i2 - N_GROUPS
        route_ref[rows, :] = jnp.where(
            lane == ROUTE_E0, e1.astype(F32),
            jnp.where(lane == ROUTE_E1, e2.astype(F32),
                      jnp.where(lane == ROUTE_W0, g_w * p1,
                                jnp.where(lane == ROUTE_W1, g_w * p2, 0.0))))
        chosen = ((lane == e1) | (lane == e2)).astype(F32)
        new_counts = new_counts + jnp.sum(chosen, axis=0, keepdims=True)
    counts_ref[...] += new_counts


def _out_proj(ab, x2, w_out, nw, w_router, b_router):
    T = x2.shape[0]
    tm = OUT_TM
    full = lambda shape: pl.BlockSpec(shape, lambda i: (0,) * len(shape))
    row = lambda width: pl.BlockSpec((tm, width), lambda i: (i, 0))
    return pl.pallas_call(
        _out_proj_kernel,
        grid=(T // tm,),
        in_specs=[row(MIX_WIDTH), row(D_MODEL), full((MIX_WIDTH, D_MODEL)), full((1, D_MODEL)),
                  full((D_MODEL, LANES)), full((1, LANES))],
        out_specs=[row(D_MODEL),
                   pl.BlockSpec((tm * SUBLANES, LANES), lambda i: (i, 0)),
                   row(LANES),
                   full((1, LANES))],
        out_shape=[jax.ShapeDtypeStruct((T, D_MODEL), F32),
                   jax.ShapeDtypeStruct((T * SUBLANES, LANES), U32),
                   jax.ShapeDtypeStruct((T, LANES), F32),
                   jax.ShapeDtypeStruct((1, LANES), F32)],
        compiler_params=_cparams(1),
        name="out_proj",
    )(ab, x2, w_out, nw, w_router, b_router)


PLAN_TM = 512


def _plan_kernel(route_ref, pstart_ref, tri_ref, dest_ref, carry_ref):
    @pl.when(pl.program_id(0) == 0)
    def _():
        carry_ref[...] = jnp.zeros_like(carry_ref)

    route = route_ref[...]
    lane = lax.broadcasted_iota(jnp.int32, route.shape, 1)
    e0 = route[:, ROUTE_E0:ROUTE_E0 + 1].astype(jnp.int32)
    e1 = route[:, ROUTE_E1:ROUTE_E1 + 1].astype(jnp.int32)
    oh0 = lane == e0
    oh1 = lane == e1
    both = (oh0 | oh1).astype(F32)
    earlier = jnp.dot(tri_ref[...], both.astype(BF16), preferred_element_type=F32)
    base = pstart_ref[...] + carry_ref[...] + earlier
    d0 = jnp.sum(jnp.where(oh0, base, 0.0), axis=-1, keepdims=True)
    d1 = jnp.sum(jnp.where(oh1, base, 0.0), axis=-1, keepdims=True)
    dest_ref[...] = jnp.where(lane == 0, d0, jnp.where(lane == 1, d1, 0.0)).astype(jnp.int32)
    carry_ref[...] += jnp.sum(both, axis=0, keepdims=True)


def _plan(route, pstart_row):
    T = route.shape[0]
    tm = PLAN_TM
    tri = jnp.asarray(np.tril(np.ones((tm, tm), np.float32), -1), BF16)
    return pl.pallas_call(
        _plan_kernel,
        grid=(T // tm,),
        in_specs=[pl.BlockSpec((tm, LANES), lambda i: (i, 0)),
                  pl.BlockSpec((1, LANES), lambda i: (0, 0)),
                  pl.BlockSpec((tm, tm), lambda i: (0, 0))],
        out_specs=pl.BlockSpec((tm, LANES), lambda i: (i, 0)),
        out_shape=jax.ShapeDtypeStruct((T, LANES), jnp.int32),
        scratch_shapes=[pltpu.VMEM((1, LANES), F32)],
        compiler_params=_cparams(1),
        name="plan",
    )(route, pstart_row, tri)


DISP_TM = 512
DMA_GROUP = 8
ROW_WORDS = SUBLANES


def _packed_row(ref, row):
    return ref.at[pl.ds(pl.multiple_of(row * ROW_WORDS, ROW_WORDS), ROW_WORDS)]


def _grouped(n, fn):
    def trip(t, carry):
        for k in range(DMA_GROUP):
            fn(t * DMA_GROUP + k, k)
        return carry
    lax.fori_loop(0, n // DMA_GROUP, trip, 0)


def _dispatch_kernel(pad_lo_ref, pad_hi_ref, nused_ref, dest_ref, xn_ref, xs_hbm,
                     zeros, sem, tail_sem, *, experts_per_step, tails_per_step, n_blocks):
    i = pl.program_id(0)
    n_rows = TOP_K * DISP_TM

    @pl.when(i == 0)
    def _():
        zeros[...] = jnp.zeros_like(zeros)

    def row_copy(a):
        src = _packed_row(xn_ref, jnp.bitwise_and(a, DISP_TM - 1))
        return pltpu.make_async_copy(src, _packed_row(xs_hbm, dest_ref[0, 0, a]), sem)

    _grouped(n_rows, lambda a, k: row_copy(a).start(priority=k % 2))

    def pad_fill(start):
        for j in range(experts_per_step):
            e = jnp.minimum(i * experts_per_step + j, N_EXPERTS - 1)
            live = i * experts_per_step + j < N_EXPERTS
            pos = pad_lo_ref[e]
            n_pad = jnp.where(live, pad_hi_ref[e] - pos, 0)
            for bit in reversed(range(MOE_BLOCK.bit_length() - 1)):
                size = 1 << bit
                take = jnp.bitwise_and(lax.shift_right_logical(n_pad, bit), 1)
                dst = xs_hbm.at[pl.ds(pl.multiple_of(pos * ROW_WORDS, ROW_WORDS), size * ROW_WORDS)]
                cp = pltpu.make_async_copy(zeros.at[pl.ds(0, size * ROW_WORDS)], dst, sem)

                @pl.when(take == 1)
                def _():
                    cp.start() if start else cp.wait()
                pos = pos + take * size

    def tail_copies(start):
        for j in range(tails_per_step):
            blk = nused_ref[0] + i * tails_per_step + j
            rows = pl.ds(pl.multiple_of(jnp.minimum(blk, n_blocks - 1) * (MOE_BLOCK * ROW_WORDS),
                                        MOE_BLOCK * ROW_WORDS), MOE_BLOCK * ROW_WORDS)
            cp = pltpu.make_async_copy(zeros, xs_hbm.at[rows], tail_sem)

            @pl.when(blk < n_blocks)
            def _():
                cp.start() if start else cp.wait()

    pad_fill(True)
    tail_copies(True)

    one_row = pltpu.make_async_copy(_packed_row(xn_ref, 0), _packed_row(xs_hbm, 0), sem)
    _grouped(n_rows, lambda a, k: one_row.wait())
    pad_fill(False)
    tail_copies(False)


def _dispatch(pad_lo, pad_hi, n_used, dest3, xn_packed, n_blocks):
    n = dest3.shape[0]
    min_used = (n * DISP_TM * TOP_K) // MOE_BLOCK
    tails_per_step = -(-(n_blocks - min_used) // n)
    grid_spec = pltpu.PrefetchScalarGridSpec(
        num_scalar_prefetch=3,
        grid=(n,),
        in_specs=[pl.BlockSpec((1, 1, TOP_K * DISP_TM), lambda i, *_: (i, 0, 0),
                               memory_space=pltpu.SMEM),
                  pl.BlockSpec((DISP_TM * ROW_WORDS, LANES), lambda i, *_: (i, 0))],
        out_specs=pl.BlockSpec(memory_space=pl.ANY),
        scratch_shapes=[pltpu.VMEM((MOE_BLOCK * ROW_WORDS, LANES), U32),
                        pltpu.SemaphoreType.DMA(()), pltpu.SemaphoreType.DMA(())],
    )
    return pl.pallas_call(
        functools.partial(_dispatch_kernel, experts_per_step=-(-N_EXPERTS // n),
                          tails_per_step=tails_per_step, n_blocks=n_blocks),
        grid_spec=grid_spec,
        out_shape=jax.ShapeDtypeStruct((n_blocks * MOE_BLOCK * ROW_WORDS, LANES), U32),
        compiler_params=_cparams(1),
        name="dispatch",
    )(pad_lo, pad_hi, n_used, dest3, xn_packed)


CAST_ROWS = 256
FETCH_SPLIT = 4


def _experts_kernel(be_ref, first_ref, par_ref, nxt2_ref, misc_ref,
                    xs_ref, wg_hbm, wu_hbm, wd_hbm, ys_ref,
                    fg, fu, fd, wg, wu, wd, sem):
    b = pl.program_id(0)
    n_used = misc_ref[0]

    def fetch(e, p):
        copies = []
        for k, (src, dst) in enumerate(((wg_hbm, fg), (wu_hbm, fu), (wd_hbm, fd))):
            rows = src.shape[1] // FETCH_SPLIT
            for c in range(FETCH_SPLIT):
                sl = pl.ds(c * rows, rows)
                copies.append(pltpu.make_async_copy(src.at[e, sl], dst.at[p, sl], sem.at[p, k]))
        return copies

    def start_fetch(e, p):
        for k, cp in enumerate(fetch(e, p)):
            cp.start(priority=k % 2)

    @pl.when(b == 0)
    def _():
        start_fetch(be_ref[0], 0)

        @pl.when(misc_ref[1] >= 0)
        def _():
            start_fetch(misc_ref[1], 1)

    @pl.when((b < n_used) & (first_ref[b] == 1))
    def _():
        p = par_ref[b]
        for cp in fetch(be_ref[b], p):
            cp.wait()

        def cast(src, dst, n_rows):
            def body(i, carry):
                rows = pl.ds(pl.multiple_of(i * CAST_ROWS, CAST_ROWS), CAST_ROWS)
                dst[rows, :] = src[p, rows, :].astype(BF16)
                return carry
            lax.fori_loop(0, n_rows // CAST_ROWS, body, 0)

        cast(fg, wg, D_MODEL)
        cast(fu, wu, D_MODEL)
        cast(fd, wd, D_EXPERT)

        @pl.when(nxt2_ref[b] >= 0)
        def _():
            start_fetch(nxt2_ref[b], p)

    @pl.when(b < n_used)
    def _():
        lo_parts, hi_parts = [], []
        for c in range(PACK_CHUNKS):
            lo, hi = _unpack_pair(xs_ref[pl.ds(c, MOE_BLOCK, stride=SUBLANES), :])
            lo_parts.append(lo.astype(BF16))
            hi_parts.append(hi.astype(BF16))
        xb = jnp.concatenate(lo_parts + hi_parts, axis=1)
        g = jnp.dot(xb, wg[...], preferred_element_type=F32)
        u = jnp.dot(xb, wu[...], preferred_element_type=F32)
        hmid = (g * jax.nn.sigmoid(g) * u).astype(BF16)
        y = jnp.dot(hmid, wd[...], preferred_element_type=F32)
        packed = _pack_pair(y[:, :HALF], y[:, HALF:])
        for c in range(PACK_CHUNKS):
            ys_ref[pl.ds(c, MOE_BLOCK, stride=SUBLANES), :] = packed[:, c * LANES:(c + 1) * LANES]

    @pl.when(b >= n_used)
    def _():
        ys_ref[...] = jnp.zeros_like(ys_ref)


def _experts(tables, xs, w_gate, w_up, w_down):
    n_blocks = tables[0].shape[0]
    blk = lambda: pl.BlockSpec((MOE_BLOCK * SUBLANES, LANES), lambda b, *_: (b, 0))
    grid_spec = pltpu.PrefetchScalarGridSpec(
        num_scalar_prefetch=len(tables),
        grid=(n_blocks,),
        in_specs=[blk(),
                  pl.BlockSpec(memory_space=pl.ANY),
                  pl.BlockSpec(memory_space=pl.ANY),
                  pl.BlockSpec(memory_space=pl.ANY)],
        out_specs=blk(),
        scratch_shapes=[pltpu.VMEM((2, D_MODEL, D_EXPERT), F32),
                        pltpu.VMEM((2, D_MODEL, D_EXPERT), F32),
                        pltpu.VMEM((2, D_EXPERT, D_MODEL), F32),
                        pltpu.VMEM((D_MODEL, D_EXPERT), BF16),
                        pltpu.VMEM((D_MODEL, D_EXPERT), BF16),
                        pltpu.VMEM((D_EXPERT, D_MODEL), BF16),
                        pltpu.SemaphoreType.DMA((2, 3))],
    )
    return pl.pallas_call(
        _experts_kernel,
        grid_spec=grid_spec,
        out_shape=jax.ShapeDtypeStruct((n_blocks * MOE_BLOCK * SUBLANES, LANES), U32),
        compiler_params=_cparams(1),
        name="experts",
    )(*tables, xs, w_gate, w_up, w_down)


CMB_TM = 128


def _combine_kernel(d_cur_ref, d_next_ref, h_ref, route_ref, ys_hbm, o_ref, ybuf, sem):
    i = pl.program_id(0)
    n = pl.num_programs(0)
    slot = i % 2
    n_rows = TOP_K * CMB_TM

    def gather(idx_ref, s):
        def start(a, k):
            pltpu.make_async_copy(_packed_row(ys_hbm, idx_ref[0, 0, a]),
                                  _packed_row(ybuf.at[s], a), sem.at[s]).start(priority=k % 2)
        _grouped(n_rows, start)

    @pl.when(i == 0)
    def _():
        gather(d_cur_ref, 0)

    @pl.when(i + 1 < n)
    def _():
        gather(d_next_ref, 1 - slot)

    one_row = pltpu.make_async_copy(_packed_row(ys_hbm, 0), _packed_row(ybuf.at[slot], 0),
                                    sem.at[slot])
    _grouped(n_rows, lambda a, k: one_row.wait())
    route = route_ref[...]
    w0 = route[:, ROUTE_W0:ROUTE_W0 + 1]
    w1 = route[:, ROUTE_W1:ROUTE_W1 + 1]
    yb = ybuf.at[slot]
    for c in range(PACK_CHUNKS):
        lo0, hi0 = _unpack_pair(yb[pl.ds(c, CMB_TM, stride=SUBLANES), :])
        lo1, hi1 = _unpack_pair(yb[pl.ds(CMB_TM * SUBLANES + c, CMB_TM, stride=SUBLANES), :])
        lo_cols = slice(c * LANES, (c + 1) * LANES)
        hi_cols = slice(HALF + c * LANES, HALF + (c + 1) * LANES)
        o_ref[:, lo_cols] = h_ref[:, lo_cols] + (w0 * lo0 + w1 * lo1)
        o_ref[:, hi_cols] = h_ref[:, hi_cols] + (w0 * hi0 + w1 * hi1)


def _combine(dest3, h, route, ys):
    T = h.shape[0]
    tm = CMB_TM
    n = T // tm
    d_spec = lambda fn: pl.BlockSpec((1, 1, TOP_K * tm), fn, memory_space=pltpu.SMEM)
    return pl.pallas_call(
        _combine_kernel,
        grid=(n,),
        in_specs=[
            d_spec(lambda i: (i, 0, 0)),
            d_spec(lambda i: (jnp.minimum(i + 1, n - 1), 0, 0)),
            pl.BlockSpec((tm, D_MODEL), lambda i: (i, 0)),
            pl.BlockSpec((tm, LANES), lambda i: (i, 0)),
            pl.BlockSpec(memory_space=pl.ANY),
        ],
        out_specs=pl.BlockSpec((tm, D_MODEL), lambda i: (i, 0)),
        out_shape=jax.ShapeDtypeStruct((T, D_MODEL), F32),
        scratch_shapes=[pltpu.VMEM((2, TOP_K * tm * SUBLANES, LANES), U32),
                        pltpu.SemaphoreType.DMA((2,))],
        compiler_params=_cparams(1),
        name="combine",
    )(dest3, dest3, h, route, ys)


def _block_tables(counts_row, T):
    counts = counts_row[0, :N_EXPERTS].astype(jnp.int32)
    padded = (counts + MOE_BLOCK - 1) // MOE_BLOCK * MOE_BLOCK
    pends = jnp.cumsum(padded)
    pstarts = pends - padded
    n_blocks = -(-(T * TOP_K + N_EXPERTS * (MOE_BLOCK - 1)) // MOE_BLOCK)
    n_used = pends[-1] // MOE_BLOCK
    block_start = jnp.arange(n_blocks, dtype=jnp.int32) * MOE_BLOCK
    be = jnp.minimum(jnp.sum(block_start[:, None] >= pends[None, :], axis=1),
                     N_EXPERTS - 1).astype(jnp.int32)
    first =jnp.concatenate([jnp.ones((1,), jnp.int32), (be[1:] != be[:-1]).astype(jnp.int32)])
    par = (jnp.cumsum(first) - 1) % 2
    def following(block):
        end = pends[be[jnp.minimum(block, n_blocks - 1)]] // MOE_BLOCK
        return jnp.where(block < n_used, end, n_used)
    nn_block = following(following(jnp.arange(n_blocks, dtype=jnp.int32)))
    nxt2 = jnp.where(nn_block < n_used, be[jnp.minimum(nn_block, n_blocks - 1)], -1)
    second_run = following(jnp.zeros((), jnp.int32))
    second_expert = jnp.where(second_run < n_used, be[jnp.minimum(second_run, n_blocks - 1)], -1)
    pstart_row = jnp.zeros((1, LANES), F32).at[0, :N_EXPERTS].set(pstarts.astype(F32))
    n_used = n_used.astype(jnp.int32).reshape(1)
    misc = jnp.stack([n_used[0], second_expert.astype(jnp.int32)])
    tables = tuple(t.astype(jnp.int32) for t in (be, first, par, nxt2, misc))
    pads = ((pstarts + counts).astype(jnp.int32), pends.astype(jnp.int32), n_used)
    return pstart_row, tables, pads, n_blocks


def _choice_major(dest, tile):
    T = dest.shape[0]
    return jnp.transpose(dest.reshape(T // tile, tile, TOP_K), (0, 2, 1)).reshape(
        T // tile, 1, TOP_K * tile)


def _rope_tables(S):
    inv = ROPE_THETA ** (-jnp.arange(0, HEAD_DIM, 2, dtype=F32) / HEAD_DIM)
    ang = jnp.arange(S, dtype=F32)[:, None] * inv[None, :]
    cos, sin = jnp.cos(ang), jnp.sin(ang)
    return jnp.concatenate([cos, cos], axis=-1), jnp.concatenate([-sin, sin], axis=-1)


def kernel(x, norm_mix_w, w_in, w_pool, pool_scale, q_norm_w, k_norm_w, sink_logits, w_out,
           norm_ffn_w, w_group_router, b_group_router, w_expert_router, b_expert_router,
           w_gate, w_up, w_down):
    B, S, D = x.shape
    T = B * S
    depth = w_in.shape[0]
    cos, sin = _rope_tables(S)
    coef_np, invc_np, bias_np = _band_constants(S)
    coef = jnp.asarray(coef_np, BF16)
    invc = jnp.asarray(invc_np, F32)
    bias = jnp.asarray(bias_np, F32)
    pad_lanes = LANES - N_GROUPS - N_EXPERTS

    h = x.reshape(T, D)
    for l in range(depth):
        u, q, k, v = _in_proj(h, norm_mix_w[l].reshape(1, D), w_in[l].astype(BF16),
                              q_norm_w[l].reshape(1, HEAD_DIM), k_norm_w[l].reshape(1, HEAD_DIM),
                              cos, sin, S)
        ab = _mixers(sink_logits[l], u.reshape(B, S, -1), q.reshape(B, S, -1),
                     k.reshape(B, S, -1), v.reshape(B, S, -1), coef, invc, bias,
                     w_pool[l].astype(BF16), pool_scale[l].reshape(1, POOL_WIDTH))
        w_router = jnp.concatenate(
            [w_group_router[l], w_expert_router[l], jnp.zeros((D, pad_lanes), F32)], axis=1)
        b_router = jnp.concatenate(
            [b_group_router[l], b_expert_router[l], jnp.zeros((pad_lanes,), F32)]).reshape(1, LANES)
        hmix, xn_packed, route, counts = _out_proj(
            ab.reshape(T, MIX_WIDTH), h, w_out[l].astype(BF16), norm_ffn_w[l].reshape(1, D),
            w_router.astype(BF16), b_router)
        pstart_row, tables, pads, n_blocks = _block_tables(counts, T)
        dest = _plan(route, pstart_row)[:, :TOP_K]
        xs = _dispatch(*pads, _choice_major(dest, DISP_TM), xn_packed, n_blocks)
        ys = _experts(tables, xs, w_gate[l], w_up[l], w_down[l])
        h = _combine(_choice_major(dest, CMB_TM), hmix, route, ys)
    return h.reshape(B, S, D)
```

```python
import functools

import numpy as np
import jax
import jax.numpy as jnp
from jax import lax
from jax.experimental import pallas as pl
from jax.experimental.pallas import tpu as pltpu

D_MODEL = 2048
POOL_WIDTH = 1024
POOL_WINDOWS = (2, 4, 8, 16)
N_POOL_GROUPS = 4
POOL_GROUP = 256
HEAD_DIM = 128
N_Q_HEADS = 8
N_KV_HEADS = 2
Q_PER_KV = 4
ATTN_WIDTH = 1024
KV_WIDTH = 256
IN_PROJ_WIDTH = 2560
MIX_WIDTH = 2048
WINDOW = 128
BLOCK = 128
BAND = 3 * BLOCK
ROPE_THETA = 10000.0
N_GROUPS = 8
EXPERTS_PER_GROUP = 8
N_EXPERTS = 64
TOP_K = 2
D_EXPERT = 512
MOE_BLOCK = 256
EPS = 1e-6

LANES = 128
SUBLANES = 8
HALF = D_MODEL // 2
PACK_CHUNKS = HALF // LANES
NEG_BIG = -1e30
VMEM_LIMIT = 56 * 1024 * 1024

BF16 = jnp.bfloat16
F32 = jnp.float32
U32 = jnp.uint32
HI_MASK = 0xFFFF0000


def _cparams(n_axes):
    return pltpu.CompilerParams(dimension_semantics=("arbitrary",) * n_axes,
                                vmem_limit_bytes=VMEM_LIMIT)


def _pack_pair(lo, hi):
    lo_bits = lax.bitcast_convert_type(lo.astype(BF16).astype(F32), U32) >> 16
    hi_bits = lax.bitcast_convert_type(hi.astype(BF16).astype(F32), U32) & jnp.uint32(HI_MASK)
    return lo_bits | hi_bits


def _unpack_pair(words):
    lo = lax.bitcast_convert_type(words << 16, F32)
    hi = lax.bitcast_convert_type(words & jnp.uint32(HI_MASK), F32)
    return lo, hi


IN_TM = 512
ROW_SPLIT = 2


def _in_proj_kernel(x_ref, nw_ref, w_ref, qnw_ref, knw_ref, cos_ref, sin_ref,
                    u_ref, q_ref, k_ref, v_ref):
    sub = IN_TM // ROW_SPLIT
    for part in range(ROW_SPLIT):
        rows = slice(part * sub, (part + 1) * sub)
        x = x_ref[rows, :]
        ms = jnp.mean(x * x, axis=-1, keepdims=True)
        xn = (x * lax.rsqrt(ms + EPS) * nw_ref[...]).astype(BF16)
        cos = cos_ref[rows, :]
        sin = sin_ref[rows, :]

        def head_norm_rope(t, w, scale):
            hms = jnp.mean(t * t, axis=-1, keepdims=True)
            t = t * lax.rsqrt(hms + EPS) * w
            t = t * cos + pltpu.roll(t, HEAD_DIM // 2, axis=1) * sin
            return t * scale

        u_ref[rows, :] = jnp.dot(xn, w_ref[:, :POOL_WIDTH],
                                 preferred_element_type=F32).astype(BF16)
        zq = jnp.dot(xn, w_ref[:, POOL_WIDTH:POOL_WIDTH + ATTN_WIDTH],
                     preferred_element_type=F32)
        qnw = qnw_ref[...]
        for h in range(N_Q_HEADS):
            sl = slice(h * HEAD_DIM, (h + 1) * HEAD_DIM)
            q_ref[rows, sl] = head_norm_rope(zq[:, sl], qnw, HEAD_DIM ** -0.5).astype(BF16)
        o_k = POOL_WIDTH + ATTN_WIDTH
        zk = jnp.dot(xn, w_ref[:, o_k:o_k + KV_WIDTH], preferred_element_type=F32)
        knw = knw_ref[...]
        for h in range(N_KV_HEADS):
            sl = slice(h * HEAD_DIM, (h + 1) * HEAD_DIM)
            k_ref[rows, sl] = head_norm_rope(zk[:, sl], knw, 1.0).astype(BF16)
        v_ref[rows, :] = jnp.dot(xn, w_ref[:, o_k + KV_WIDTH:],
                                 preferred_element_type=F32).astype(BF16)


def _in_proj(x2, nw, w_in, qnw, knw, cos, sin, seq):
    T = x2.shape[0]
    tm = IN_TM
    pos_blocks = seq // tm
    full = lambda shape: pl.BlockSpec(shape, lambda i: (0,) * len(shape))
    return pl.pallas_call(
        _in_proj_kernel,
        grid=(T // tm,),
        in_specs=[
            pl.BlockSpec((tm, D_MODEL), lambda i: (i, 0)),
            full((1, D_MODEL)),
            full((D_MODEL, IN_PROJ_WIDTH)),
            full((1, HEAD_DIM)),
            full((1, HEAD_DIM)),
            pl.BlockSpec((tm, HEAD_DIM), lambda i: (i % pos_blocks, 0)),
            pl.BlockSpec((tm, HEAD_DIM), lambda i: (i % pos_blocks, 0)),
        ],
        out_specs=[
            pl.BlockSpec((tm, POOL_WIDTH), lambda i: (i, 0)),
            pl.BlockSpec((tm, ATTN_WIDTH), lambda i: (i, 0)),
            pl.BlockSpec((tm, KV_WIDTH), lambda i: (i, 0)),
            pl.BlockSpec((tm, KV_WIDTH), lambda i: (i, 0)),
        ],
        out_shape=[
            jax.ShapeDtypeStruct((T, POOL_WIDTH), BF16),
            jax.ShapeDtypeStruct((T, ATTN_WIDTH), BF16),
            jax.ShapeDtypeStruct((T, KV_WIDTH), BF16),
            jax.ShapeDtypeStruct((T, KV_WIDTH), BF16),
        ],
        compiler_params=_cparams(1),
        name="in_proj",
    )(x2, nw, w_in, qnw, knw, cos, sin)


MIX_TQ = 256


def _band_constants(seq):
    nb = seq // BLOCK
    coef = np.zeros((3, N_POOL_GROUPS, BLOCK, BAND), np.float32)
    inv_count = np.zeros((3, N_POOL_GROUPS, BLOCK, 1), np.float32)
    bias = np.zeros((3, BLOCK, BAND), np.float32)
    for kind, n in enumerate((0, 1, nb - 1)):
        start = min(max((n - 1) * BLOCK, 0), seq - BAND)
        t = n * BLOCK + np.arange(BLOCK)[:, None]
        s = start + np.arange(BAND)[None, :]
        bias[kind] = np.where(np.abs(s - t) <= WINDOW, 0.0, NEG_BIG)
        for g, win in enumerate(POOL_WINDOWS):
            half = win // 2
            lo = np.clip(t - half, 0, seq)
            hi = np.clip(t + half, 0, seq)
            count = (hi - lo).astype(np.float32)
            inside = ((s >= lo) & (s < hi)).astype(np.float32)
            coef[kind, g] = inside - count * (s == t)
            inv_count[kind, g] = 1.0 / count
    return coef, inv_count, bias


def _mixers_kernel(sink_ref, u_ref, q_ref, k_ref, v_ref, coef_ref, invc_ref, bias_ref,
                   wpool_ref, pscale_ref, ab_ref, *, seq):
    nb = seq // BLOCK
    j = pl.program_id(1)
    n_sub = MIX_TQ // BLOCK
    subs = []
    for r in range(n_sub):
        n = j * n_sub + r
        start = pl.multiple_of(jnp.clip((n - 1) * BLOCK, 0, seq - BAND), BLOCK)
        kind = jnp.where(n == 0, 0, jnp.where(n == nb - 1, 2, 1))
        subs.append((start, kind, slice(r * BLOCK, (r + 1) * BLOCK)))

    scores = {}
    for r, (start, kind, rows) in enumerate(subs):
        for hk in range(N_KV_HEADS):
            kb = k_ref[0, pl.ds(start, BAND), hk * HEAD_DIM:(hk + 1) * HEAD_DIM]
            qs = jnp.concatenate(
                [q_ref[0, rows, h * HEAD_DIM:(h + 1) * HEAD_DIM]
                 for h in range(hk * Q_PER_KV, (hk + 1) * Q_PER_KV)], axis=0)
            scores[r, hk] = lax.dot_general(qs, kb, (((1,), (1,)), ((), ())),
                                            preferred_element_type=F32)
    diffs = {}
    for r, (start, kind, rows) in enumerate(subs):
        for g in range(N_POOL_GROUPS):
            ub = u_ref[0, pl.ds(start, BAND), g * POOL_GROUP:(g + 1) * POOL_GROUP]
            d = jnp.dot(coef_ref[kind, g], ub, preferred_element_type=F32) * invc_ref[kind, g]
            diffs[r, g] = d.astype(BF16)
    probs = {}
    for r, (start, kind, rows) in enumerate(subs):
        bias = bias_ref[kind]
        for hk in range(N_KV_HEADS):
            for gi in range(Q_PER_KV):
                h = hk * Q_PER_KV + gi
                sh = scores[r, hk][gi * BLOCK:(gi + 1) * BLOCK] + bias
                sink = sink_ref[h]
                m = jnp.maximum(jnp.max(sh, axis=-1, keepdims=True), sink)
                p = jnp.exp(sh - m)
                denom = jnp.sum(p, axis=-1, keepdims=True) + jnp.exp(sink - m)
                probs[r, h] = (p.astype(BF16), denom)
    for r, (start, kind, rows) in enumerate(subs):
        for g in range(N_POOL_GROUPS):
            cols = slice(g * POOL_GROUP, (g + 1) * POOL_GROUP)
            y = jnp.dot(diffs[r, g], wpool_ref[g], preferred_element_type=F32)
            ab_ref[0, rows, cols] = (y * pscale_ref[:, cols]).astype(BF16)
    for r, (start, kind, rows) in enumerate(subs):
        for hk in range(N_KV_HEADS):
            vb = v_ref[0, pl.ds(start, BAND), hk * HEAD_DIM:(hk + 1) * HEAD_DIM]
            for gi in range(Q_PER_KV):
                h = hk * Q_PER_KV + gi
                p, denom = probs[r, h]
                o = jnp.dot(p, vb, preferred_element_type=F32) / denom
                ocols = slice(POOL_WIDTH + h * HEAD_DIM, POOL_WIDTH + (h + 1) * HEAD_DIM)
                ab_ref[0, rows, ocols] = o.astype(BF16)


def _mixers(sink, u, q, k, v, coef, invc, bias, wpool, pscale):
    B, S, _ = u.shape
    tq = MIX_TQ
    full = lambda shape: pl.BlockSpec(shape, lambda b, j: (0,) * len(shape))
    return pl.pallas_call(
        functools.partial(_mixers_kernel, seq=S),
        grid=(B, S // tq),
        in_specs=[
            pl.BlockSpec(memory_space=pltpu.SMEM),
            pl.BlockSpec((1, S, POOL_WIDTH), lambda b, j: (b, 0, 0)),
            pl.BlockSpec((1, tq, ATTN_WIDTH), lambda b, j: (b, j, 0)),
            pl.BlockSpec((1, S, KV_WIDTH), lambda b, j: (b, 0, 0)),
            pl.BlockSpec((1, S, KV_WIDTH), lambda b, j: (b, 0, 0)),
            full((3, N_POOL_GROUPS, BLOCK, BAND)),
            full((3, N_POOL_GROUPS, BLOCK, 1)),
            full((3, BLOCK, BAND)),
            full((N_POOL_GROUPS, POOL_GROUP, POOL_GROUP)),
            full((1, POOL_WIDTH)),
        ],
        out_specs=pl.BlockSpec((1, tq, MIX_WIDTH), lambda b, j: (b, j, 0)),
        out_shape=jax.ShapeDtypeStruct((B, S, MIX_WIDTH), BF16),
        compiler_params=_cparams(2),
        name="mixers",
    )(sink, u, q, k, v, coef, invc, bias, wpool, pscale)


OUT_TM = 512
ROUTE_E0, ROUTE_E1, ROUTE_W0, ROUTE_W1 = 0, 1, 2, 3


def _out_proj_kernel(ab_ref, x_ref, w_ref, nw_ref, wr_ref, br_ref,
                     h_ref, xn_ref, route_ref, counts_ref):
    @pl.when(pl.program_id(0) == 0)
    def _():
        counts_ref[...] = jnp.zeros_like(counts_ref)

    sub = OUT_TM // ROW_SPLIT
    lane = lax.broadcasted_iota(jnp.int32, (sub, LANES), 1)

    def first_argmax(vals):
        m = jnp.max(vals, axis=-1, keepdims=True)
        idx = jnp.min(jnp.where(vals == m, lane, LANES), axis=-1, keepdims=True)
        return m, idx

    new_counts = jnp.zeros((1, LANES), F32)
    for part in range(ROW_SPLIT):
        rows = slice(part * sub, (part + 1) * sub)
        h = x_ref[rows, :] + jnp.dot(ab_ref[rows, :], w_ref[...], preferred_element_type=F32)
        h_ref[rows, :] = h
        ms = jnp.mean(h * h, axis=-1, keepdims=True)
        xn = h * lax.rsqrt(ms + EPS) * nw_ref[...]
        packed = _pack_pair(xn[:, :HALF], xn[:, HALF:])
        for c in range(PACK_CHUNKS):
            xn_ref[pl.ds(part * sub * SUBLANES + c, sub, stride=SUBLANES), :] = (
                packed[:, c * LANES:(c + 1) * LANES])
        logits = jnp.dot(xn.astype(BF16), wr_ref[...], preferred_element_type=F32) + br_ref[...]

        gl = jnp.where(lane < N_GROUPS, logits, NEG_BIG)
        gmax, gidx = first_argmax(gl)
        gsum = jnp.sum(jnp.where(lane < N_GROUPS, jnp.exp(gl - gmax), 0.0),
                       axis=-1, keepdims=True)
        g_w = 1.0 / gsum
        e_lo = N_GROUPS + gidx * EXPERTS_PER_GROUP
        el = jnp.where((lane >= e_lo) & (lane < e_lo + EXPERTS_PER_GROUP), logits, NEG_BIG)
        m1, i1 = first_argmax(el)
        m2, i2 = first_argmax(jnp.where(lane == i1, NEG_BIG, el))
        t = jnp.exp(m2 - m1)
        p1 = 1.0 / (1.0 + t)
        p2 = t * p1
        e1 = i1 - N_GROUPS
        e2 = i2 - N_GROUPS
        route_ref[rows, :] = jnp.where(
            lane == ROUTE_E0, e1.astype(F32),
            jnp.where(lane == ROUTE_E1, e2.astype(F32),
                      jnp.where(lane == ROUTE_W0, g_w * p1,
                                jnp.where(lane == ROUTE_W1, g_w * p2, 0.0))))
        chosen = ((lane == e1) | (lane == e2)).astype(F32)
        new_counts = new_counts + jnp.sum(chosen, axis=0, keepdims=True)
    counts_ref[...] += new_counts


def _out_proj(ab, x2, w_out, nw, w_router, b_router):
    T = x2.shape[0]
    tm = OUT_TM
    full = lambda shape: pl.BlockSpec(shape, lambda i: (0,) * len(shape))
    row = lambda width: pl.BlockSpec((tm, width), lambda i: (i, 0))
    return pl.pallas_call(
        _out_proj_kernel,
        grid=(T // tm,),
        in_specs=[row(MIX_WIDTH), row(D_MODEL), full((MIX_WIDTH, D_MODEL)), full((1, D_MODEL)),
                  full((D_MODEL, LANES)), full((1, LANES))],
        out_specs=[row(D_MODEL),
                   pl.BlockSpec((tm * SUBLANES, LANES), lambda i: (i, 0)),
                   row(LANES),
                   full((1, LANES))],
        out_shape=[jax.ShapeDtypeStruct((T, D_MODEL), F32),
                   jax.ShapeDtypeStruct((T * SUBLANES, LANES), U32),
                   jax.ShapeDtypeStruct((T, LANES), F32),
                   jax.ShapeDtypeStruct((1, LANES), F32)],
        compiler_params=_cparams(1),
        name="out_proj",
    )(ab, x2, w_out, nw, w_router, b_router)


PLAN_TM = 512


def _plan_kernel(route_ref, pstart_ref, tri_ref, dest_ref, carry_ref):
    @pl.when(pl.program_id(0) == 0)
    def _():
        carry_ref[...] = jnp.zeros_like(carry_ref)

    route = route_ref[...]
    lane = lax.broadcasted_iota(jnp.int32, route.shape, 1)
    e0 = route[:, ROUTE_E0:ROUTE_E0 + 1].astype(jnp.int32)
    e1 = route[:, ROUTE_E1:ROUTE_E1 + 1].astype(jnp.int32)
    oh0 = lane == e0
    oh1 = lane == e1
    both = (oh0 | oh1).astype(F32)
    earlier = jnp.dot(tri_ref[...], both.astype(BF16), preferred_element_type=F32)
    base = pstart_ref[...] + carry_ref[...] + earlier
    d0 = jnp.sum(jnp.where(oh0, base, 0.0), axis=-1, keepdims=True)
    d1 = jnp.sum(jnp.where(oh1, base, 0.0), axis=-1, keepdims=True)
    dest_ref[...] = jnp.where(lane == 0, d0, jnp.where(lane == 1, d1, 0.0)).astype(jnp.int32)
    carry_ref[...] += jnp.sum(both, axis=0, keepdims=True)


def _plan(route, pstart_row):
    T = route.shape[0]
    tm = PLAN_TM
    tri = jnp.asarray(np.tril(np.ones((tm, tm), np.float32), -1), BF16)
    return pl.pallas_call(
        _plan_kernel,
        grid=(T // tm,),
        in_specs=[pl.BlockSpec((tm, LANES), lambda i: (i, 0)),
                  pl.BlockSpec((1, LANES), lambda i: (0, 0)),
                  pl.BlockSpec((tm, tm), lambda i: (0, 0))],
        out_specs=pl.BlockSpec((tm, LANES), lambda i: (i, 0)),
        out_shape=jax.ShapeDtypeStruct((T, LANES), jnp.int32),
        scratch_shapes=[pltpu.VMEM((1, LANES), F32)],
        compiler_params=_cparams(1),
        name="plan",
    )(route, pstart_row, tri)


DISP_TM = 512
DMA_GROUP = 8
ROW_WORDS = SUBLANES


def _packed_row(ref, row):
    return ref.at[pl.ds(pl.multiple_of(row * ROW_WORDS, ROW_WORDS), ROW_WORDS)]


def _grouped(n, fn):
    def trip(t, carry):
        for k in range(DMA_GROUP):
            fn(t * DMA_GROUP + k, k)
        return carry
    lax.fori_loop(0, n // DMA_GROUP, trip, 0)


def _dispatch_kernel(pad_lo_ref, pad_hi_ref, nused_ref, dest_ref, xn_ref, xs_hbm,
                     zeros, sem, tail_sem, *, experts_per_step, tails_per_step, n_blocks):
    i = pl.program_id(0)
    n_rows = TOP_K * DISP_TM

    @pl.when(i == 0)
    def _():
        zeros[...] = jnp.zeros_like(zeros)

    def row_copy(a):
        src = _packed_row(xn_ref, jnp.bitwise_and(a, DISP_TM - 1))
        return pltpu.make_async_copy(src, _packed_row(xs_hbm, dest_ref[0, 0, a]), sem)

    _grouped(n_rows, lambda a, k: row_copy(a).start(priority=k % 2))

    def pad_fill(start):
        for j in range(experts_per_step):
            e = jnp.minimum(i * experts_per_step + j, N_EXPERTS - 1)
            live = i * experts_per_step + j < N_EXPERTS
            pos = pad_lo_ref[e]
            n_pad = jnp.where(live, pad_hi_ref[e] - pos, 0)
            for bit in reversed(range(MOE_BLOCK.bit_length() - 1)):
                size = 1 << bit
                take = jnp.bitwise_and(lax.shift_right_logical(n_pad, bit), 1)
                dst = xs_hbm.at[pl.ds(pl.multiple_of(pos * ROW_WORDS, ROW_WORDS), size * ROW_WORDS)]
                cp = pltpu.make_async_copy(zeros.at[pl.ds(0, size * ROW_WORDS)], dst, sem)

                @pl.when(take == 1)
                def _():
                    cp.start() if start else cp.wait()
                pos = pos + take * size

    def tail_copies(start):
        for j in range(tails_per_step):
            blk = nused_ref[0] + i * tails_per_step + j
            rows = pl.ds(pl.multiple_of(jnp.minimum(blk, n_blocks - 1) * (MOE_BLOCK * ROW_WORDS),
                                        MOE_BLOCK * ROW_WORDS), MOE_BLOCK * ROW_WORDS)
            cp = pltpu.make_async_copy(zeros, xs_hbm.at[rows], tail_sem)

            @pl.when(blk < n_blocks)
            def _():
                cp.start() if start else cp.wait()

    pad_fill(True)
    tail_copies(True)

    tile_rows = DISP_TM * ROW_WORDS
    for _ in range(TOP_K):
        pltpu.make_async_copy(xn_ref, xs_hbm.at[pl.ds(0, tile_rows)], sem).wait()
    pad_fill(False)
    tail_copies(False)


def _dispatch(pad_lo, pad_hi, n_used, dest3, xn_packed, n_blocks):
    n = dest3.shape[0]
    min_used = (n * DISP_TM * TOP_K) // MOE_BLOCK
    tails_per_step = -(-(n_blocks - min_used) // n)
    grid_spec = pltpu.PrefetchScalarGridSpec(
        num_scalar_prefetch=3,
        grid=(n,),
        in_specs=[pl.BlockSpec((1, 1, TOP_K * DISP_TM), lambda i, *_: (i, 0, 0),
                               memory_space=pltpu.SMEM),
                  pl.BlockSpec((DISP_TM * ROW_WORDS, LANES), lambda i, *_: (i, 0))],
        out_specs=pl.BlockSpec(memory_space=pl.ANY),
        scratch_shapes=[pltpu.VMEM((MOE_BLOCK * ROW_WORDS, LANES), U32),
                        pltpu.SemaphoreType.DMA(()), pltpu.SemaphoreType.DMA(())],
    )
    return pl.pallas_call(
        functools.partial(_dispatch_kernel, experts_per_step=-(-N_EXPERTS // n),
                          tails_per_step=tails_per_step, n_blocks=n_blocks),
        grid_spec=grid_spec,
        out_shape=jax.ShapeDtypeStruct((n_blocks * MOE_BLOCK * ROW_WORDS, LANES), U32),
        compiler_params=_cparams(1),
        name="dispatch",
    )(pad_lo, pad_hi, n_used, dest3, xn_packed)


CAST_ROWS = 256
FETCH_SPLIT = 4


def _experts_kernel(be_ref, first_ref, par_ref, nxt2_ref, misc_ref,
                    xs_ref, wg_hbm, wu_hbm, wd_hbm, ys_ref,
                    fg, fu, fd, wg, wu, wd, sem):
    b = pl.program_id(0)
    n_used = misc_ref[0]

    def fetch(e, p):
        copies = []
        for k, (src, dst) in enumerate(((wg_hbm, fg), (wu_hbm, fu), (wd_hbm, fd))):
            rows = src.shape[1] // FETCH_SPLIT
            for c in range(FETCH_SPLIT):
                sl = pl.ds(c * rows, rows)
                copies.append(pltpu.make_async_copy(src.at[e, sl], dst.at[p, sl], sem.at[p, k]))
        return copies

    def start_fetch(e, p):
        for k, cp in enumerate(fetch(e, p)):
            cp.start(priority=k % 2)

    @pl.when(b == 0)
    def _():
        start_fetch(be_ref[0], 0)

        @pl.when(misc_ref[1] >= 0)
        def _():
            start_fetch(misc_ref[1], 1)

    @pl.when((b < n_used) & (first_ref[b] == 1))
    def _():
        p = par_ref[b]
        for cp in fetch(be_ref[b], p):
            cp.wait()

        def cast(src, dst, n_rows):
            def body(i, carry):
                rows = pl.ds(pl.multiple_of(i * CAST_ROWS, CAST_ROWS), CAST_ROWS)
                dst[rows, :] = src[p, rows, :].astype(BF16)
                return carry
            lax.fori_loop(0, n_rows // CAST_ROWS, body, 0)

        cast(fg, wg, D_MODEL)
        cast(fu, wu, D_MODEL)
        cast(fd, wd, D_EXPERT)

        @pl.when(nxt2_ref[b] >= 0)
        def _():
            start_fetch(nxt2_ref[b], p)

    @pl.when(b < n_used)
    def _():
        lo_parts, hi_parts = [], []
        for c in range(PACK_CHUNKS):
            lo, hi = _unpack_pair(xs_ref[pl.ds(c, MOE_BLOCK, stride=SUBLANES), :])
            lo_parts.append(lo.astype(BF16))
            hi_parts.append(hi.astype(BF16))
        xb = jnp.concatenate(lo_parts + hi_parts, axis=1)
        g = jnp.dot(xb, wg[...], preferred_element_type=F32)
        u = jnp.dot(xb, wu[...], preferred_element_type=F32)
        hmid = (g * jax.nn.sigmoid(g) * u).astype(BF16)
        y = jnp.dot(hmid, wd[...], preferred_element_type=F32)
        packed = _pack_pair(y[:, :HALF], y[:, HALF:])
        for c in range(PACK_CHUNKS):
            ys_ref[pl.ds(c, MOE_BLOCK, stride=SUBLANES), :] = packed[:, c * LANES:(c + 1) * LANES]

    @pl.when(b >= n_used)
    def _():
        ys_ref[...] = jnp.zeros_like(ys_ref)


def _experts(tables, xs, w_gate, w_up, w_down):
    n_blocks = tables[0].shape[0]
    blk = lambda: pl.BlockSpec((MOE_BLOCK * SUBLANES, LANES), lambda b, *_: (b, 0))
    grid_spec = pltpu.PrefetchScalarGridSpec(
        num_scalar_prefetch=len(tables),
        grid=(n_blocks,),
        in_specs=[blk(),
                  pl.BlockSpec(memory_space=pl.ANY),
                  pl.BlockSpec(memory_space=pl.ANY),
                  pl.BlockSpec(memory_space=pl.ANY)],
        out_specs=blk(),
        scratch_shapes=[pltpu.VMEM((2, D_MODEL, D_EXPERT), F32),
                        pltpu.VMEM((2, D_MODEL, D_EXPERT), F32),
                        pltpu.VMEM((2, D_EXPERT, D_MODEL), F32),
                        pltpu.VMEM((D_MODEL, D_EXPERT), BF16),
                        pltpu.VMEM((D_MODEL, D_EXPERT), BF16),
                        pltpu.VMEM((D_EXPERT, D_MODEL), BF16),
                        pltpu.SemaphoreType.DMA((2, 3))],
    )
    return pl.pallas_call(
        _experts_kernel,
        grid_spec=grid_spec,
        out_shape=jax.ShapeDtypeStruct((n_blocks * MOE_BLOCK * SUBLANES, LANES), U32),
        compiler_params=_cparams(1),
        name="experts",
    )(*tables, xs, w_gate, w_up, w_down)


CMB_TM = 256


def _combine_kernel(d_cur_ref, d_next_ref, h_ref, route_ref, ys_hbm, o_ref, ybuf, sem):
    i = pl.program_id(0)
    n = pl.num_programs(0)
    slot = i % 2
    n_rows = TOP_K * CMB_TM

    def gather(idx_ref, s):
        def start(a, k):
            pltpu.make_async_copy(_packed_row(ys_hbm, idx_ref[0, 0, a]),
                                  _packed_row(ybuf.at[s], a), sem.at[s]).start(priority=k % 2)
        _grouped(n_rows, start)

    @pl.when(i == 0)
    def _():
        gather(d_cur_ref, 0)

    @pl.when(i + 1 < n)
    def _():
        gather(d_next_ref, 1 - slot)

    pltpu.make_async_copy(ys_hbm.at[pl.ds(0, n_rows * ROW_WORDS)], ybuf.at[slot],
                          sem.at[slot]).wait()
    route = route_ref[...]
    w0 = route[:, ROUTE_W0:ROUTE_W0 + 1]
    w1 = route[:, ROUTE_W1:ROUTE_W1 + 1]
    yb = ybuf.at[slot]
    for c in range(PACK_CHUNKS):
        lo0, hi0 = _unpack_pair(yb[pl.ds(c, CMB_TM, stride=SUBLANES), :])
        lo1, hi1 = _unpack_pair(yb[pl.ds(CMB_TM * SUBLANES + c, CMB_TM, stride=SUBLANES), :])
        lo_cols = slice(c * LANES, (c + 1) * LANES)
        hi_cols = slice(HALF + c * LANES, HALF + (c + 1) * LANES)
        o_ref[:, lo_cols] = h_ref[:, lo_cols] + (w0 * lo0 + w1 * lo1)
        o_ref[:, hi_cols] = h_ref[:, hi_cols] + (w0 * hi0 + w1 * hi1)


def _combine(dest3, h, route, ys):
    T = h.shape[0]
    tm = CMB_TM
    n = T // tm
    d_spec = lambda fn: pl.BlockSpec((1, 1, TOP_K * tm), fn, memory_space=pltpu.SMEM)
    return pl.pallas_call(
        _combine_kernel,
        grid=(n,),
        in_specs=[
            d_spec(lambda i: (i, 0, 0)),
            d_spec(lambda i: (jnp.minimum(i + 1, n - 1), 0, 0)),
            pl.BlockSpec((tm, D_MODEL), lambda i: (i, 0)),
            pl.BlockSpec((tm, LANES), lambda i: (i, 0)),
            pl.BlockSpec(memory_space=pl.ANY),
        ],
        out_specs=pl.BlockSpec((tm, D_MODEL), lambda i: (i, 0)),
        out_shape=jax.ShapeDtypeStruct((T, D_MODEL), F32),
        scratch_shapes=[pltpu.VMEM((2, TOP_K * tm * SUBLANES, LANES), U32),
                        pltpu.SemaphoreType.DMA((2,))],
        compiler_params=_cparams(1),
        name="combine",
    )(dest3, dest3, h, route, ys)


def _block_tables(counts_row, T):
    counts = counts_row[0, :N_EXPERTS].astype(jnp.int32)
    padded = (counts + MOE_BLOCK - 1) // MOE_BLOCK * MOE_BLOCK
    pends = jnp.cumsum(padded)
    pstarts = pends - padded
    n_blocks = -(-(T * TOP_K + N_EXPERTS * (MOE_BLOCK - 1)) // MOE_BLOCK)
    n_used = pends[-1] // MOE_BLOCK
    block_start = jnp.arange(n_blocks, dtype=jnp.int32) * MOE_BLOCK
    be = jnp.minimum(jnp.sum(block_start[:, None] >= pends[None, :], axis=1),
                     N_EXPERTS - 1).astype(jnp.int32)
    first =jnp.concatenate([jnp.ones((1,), jnp.int32), (be[1:] != be[:-1]).astype(jnp.int32)])
    par = (jnp.cumsum(first) - 1) % 2
    def following(block):
        end = pends[be[jnp.minimum(block, n_blocks - 1)]] // MOE_BLOCK
        return jnp.where(block < n_used, end, n_used)
    nn_block = following(following(jnp.arange(n_blocks, dtype=jnp.int32)))
    nxt2 = jnp.where(nn_block < n_used, be[jnp.minimum(nn_block, n_blocks - 1)], -1)
    second_run = following(jnp.zeros((), jnp.int32))
    second_expert = jnp.where(second_run < n_used, be[jnp.minimum(second_run, n_blocks - 1)], -1)
    pstart_row = jnp.zeros((1, LANES), F32).at[0, :N_EXPERTS].set(pstarts.astype(F32))
    n_used = n_used.astype(jnp.int32).reshape(1)
    misc = jnp.stack([n_used[0], second_expert.astype(jnp.int32)])
    tables = tuple(t.astype(jnp.int32) for t in (be, first, par, nxt2, misc))
    pads = ((pstarts + counts).astype(jnp.int32), pends.astype(jnp.int32), n_used)
    return pstart_row, tables, pads, n_blocks


def _choice_major(dest, tile):
    T = dest.shape[0]
    return jnp.transpose(dest.reshape(T // tile, tile, TOP_K), (0, 2, 1)).reshape(
        T // tile, 1, TOP_K * tile)


def _rope_tables(S):
    inv = ROPE_THETA ** (-jnp.arange(0, HEAD_DIM, 2, dtype=F32) / HEAD_DIM)
    ang = jnp.arange(S, dtype=F32)[:, None] * inv[None, :]
    cos, sin = jnp.cos(ang), jnp.sin(ang)
    return jnp.concatenate([cos, cos], axis=-1), jnp.concatenate([-sin, sin], axis=-1)


def kernel(x, norm_mix_w, w_in, w_pool, pool_scale, q_norm_w, k_norm_w, sink_logits, w_out,
           norm_ffn_w, w_group_router, b_group_router, w_expert_router, b_expert_router,
           w_gate, w_up, w_down):
    B, S, D = x.shape
    T = B * S
    depth = w_in.shape[0]
    cos, sin = _rope_tables(S)
    coef_np, invc_np, bias_np = _band_constants(S)
    coef = jnp.asarray(coef_np, BF16)
    invc = jnp.asarray(invc_np, F32)
    bias = jnp.asarray(bias_np, F32)
    pad_lanes = LANES - N_GROUPS - N_EXPERTS

    h = x.reshape(T, D)
    for l in range(depth):
        u, q, k, v = _in_proj(h, norm_mix_w[l].reshape(1, D), w_in[l].astype(BF16),
                              q_norm_w[l].reshape(1, HEAD_DIM), k_norm_w[l].reshape(1, HEAD_DIM),
                              cos, sin, S)
        ab = _mixers(sink_logits[l], u.reshape(B, S, -1), q.reshape(B, S, -1),
                     k.reshape(B, S, -1), v.reshape(B, S, -1), coef, invc, bias,
                     w_pool[l].astype(BF16), pool_scale[l].reshape(1, POOL_WIDTH))
        w_router = jnp.concatenate(
            [w_group_router[l], w_expert_router[l], jnp.zeros((D, pad_lanes), F32)], axis=1)
        b_router = jnp.concatenate(
            [b_group_router[l], b_expert_router[l], jnp.zeros((pad_lanes,), F32)]).reshape(1, LANES)
        hmix, xn_packed, route, counts = _out_proj(
            ab.reshape(T, MIX_WIDTH), h, w_out[l].astype(BF16), norm_ffn_w[l].reshape(1, D),
            w_router.astype(BF16), b_router)
        pstart_row, tables, pads, n_blocks = _block_tables(counts, T)
        dest = _plan(route, pstart_row)[:, :TOP_K]
        xs = _dispatch(*pads, _choice_major(dest, DISP_TM), xn_packed, n_blocks)
        ys = _experts(tables, xs, w_gate[l], w_up[l], w_down[l])
        h = _combine(_choice_major(dest, CMB_TM), hmix, route, ys)
    return h.reshape(B, S, D)
```

```python
import functools

import numpy as np
import jax
import jax.numpy as jnp
from jax import lax
from jax.experimental import pallas as pl
from jax.experimental.pallas import tpu as pltpu

D_MODEL = 2048
POOL_WIDTH = 1024
POOL_WINDOWS = (2, 4, 8, 16)
N_POOL_GROUPS = 4
POOL_GROUP = 256
HEAD_DIM = 128
N_Q_HEADS = 8
N_KV_HEADS = 2
Q_PER_KV = 4
ATTN_WIDTH = 1024
KV_WIDTH = 256
IN_PROJ_WIDTH = 2560
MIX_WIDTH = 2048
WINDOW = 128
BLOCK = 128
BAND = 3 * BLOCK
ROPE_THETA = 10000.0
N_GROUPS = 8
EXPERTS_PER_GROUP = 8
N_EXPERTS = 64
TOP_K = 2
D_EXPERT = 512
MOE_BLOCK = 256
EPS = 1e-6

LANES = 128
SUBLANES = 8
HALF = D_MODEL // 2
PACK_CHUNKS = HALF // LANES
NEG_BIG = -1e30
VMEM_LIMIT = 56 * 1024 * 1024

BF16 = jnp.bfloat16
F32 = jnp.float32
U32 = jnp.uint32
HI_MASK = 0xFFFF0000


def _cparams(n_axes):
    return pltpu.CompilerParams(dimension_semantics=("arbitrary",) * n_axes,
                                vmem_limit_bytes=VMEM_LIMIT)


def _pack_pair(lo, hi):
    lo_bits = lax.bitcast_convert_type(lo.astype(BF16).astype(F32), U32) >> 16
    hi_bits = lax.bitcast_convert_type(hi.astype(BF16).astype(F32), U32) & jnp.uint32(HI_MASK)
    return lo_bits | hi_bits


def _unpack_pair(words):
    lo = lax.bitcast_convert_type(words << 16, F32)
    hi = lax.bitcast_convert_type(words & jnp.uint32(HI_MASK), F32)
    return lo, hi


IN_TM = 512
ROW_SPLIT = 2


def _in_proj_kernel(x_ref, nw_ref, w_ref, qnw_ref, knw_ref, cos_ref, sin_ref,
                    u_ref, q_ref, k_ref, v_ref):
    sub = IN_TM // ROW_SPLIT
    for part in range(ROW_SPLIT):
        rows = slice(part * sub, (part + 1) * sub)
        x = x_ref[rows, :]
        ms = jnp.mean(x * x, axis=-1, keepdims=True)
        xn = (x * lax.rsqrt(ms + EPS) * nw_ref[...]).astype(BF16)
        cos = cos_ref[rows, :]
        sin = sin_ref[rows, :]

        def head_norm_rope(t, w, scale):
            hms = jnp.mean(t * t, axis=-1, keepdims=True)
            t = t * lax.rsqrt(hms + EPS) * w
            t = t * cos + pltpu.roll(t, HEAD_DIM // 2, axis=1) * sin
            return t * scale

        u_ref[rows, :] = jnp.dot(xn, w_ref[:, :POOL_WIDTH],
                                 preferred_element_type=F32).astype(BF16)
        zq = jnp.dot(xn, w_ref[:, POOL_WIDTH:POOL_WIDTH + ATTN_WIDTH],
                     preferred_element_type=F32)
        qnw = qnw_ref[...]
        for h in range(N_Q_HEADS):
            sl = slice(h * HEAD_DIM, (h + 1) * HEAD_DIM)
            q_ref[rows, sl] = head_norm_rope(zq[:, sl], qnw, HEAD_DIM ** -0.5).astype(BF16)
        o_k = POOL_WIDTH + ATTN_WIDTH
        zk = jnp.dot(xn, w_ref[:, o_k:o_k + KV_WIDTH], preferred_element_type=F32)
        knw = knw_ref[...]
        for h in range(N_KV_HEADS):
            sl = slice(h * HEAD_DIM, (h + 1) * HEAD_DIM)
            k_ref[rows, sl] = head_norm_rope(zk[:, sl], knw, 1.0).astype(BF16)
        v_ref[rows, :] = jnp.dot(xn, w_ref[:, o_k + KV_WIDTH:],
                                 preferred_element_type=F32).astype(BF16)


def _in_proj(x2, nw, w_in, qnw, knw, cos, sin, seq):
    T = x2.shape[0]
    tm = IN_TM
    pos_blocks = seq // tm
    full = lambda shape: pl.BlockSpec(shape, lambda i: (0,) * len(shape))
    return pl.pallas_call(
        _in_proj_kernel,
        grid=(T // tm,),
        in_specs=[
            pl.BlockSpec((tm, D_MODEL), lambda i: (i, 0)),
            full((1, D_MODEL)),
            full((D_MODEL, IN_PROJ_WIDTH)),
            full((1, HEAD_DIM)),
            full((1, HEAD_DIM)),
            pl.BlockSpec((tm, HEAD_DIM), lambda i: (i % pos_blocks, 0)),
            pl.BlockSpec((tm, HEAD_DIM), lambda i: (i % pos_blocks, 0)),
        ],
        out_specs=[
            pl.BlockSpec((tm, POOL_WIDTH), lambda i: (i, 0)),
            pl.BlockSpec((tm, ATTN_WIDTH), lambda i: (i, 0)),
            pl.BlockSpec((tm, KV_WIDTH), lambda i: (i, 0)),
            pl.BlockSpec((tm, KV_WIDTH), lambda i: (i, 0)),
        ],
        out_shape=[
            jax.ShapeDtypeStruct((T, POOL_WIDTH), BF16),
            jax.ShapeDtypeStruct((T, ATTN_WIDTH), BF16),
            jax.ShapeDtypeStruct((T, KV_WIDTH), BF16),
            jax.ShapeDtypeStruct((T, KV_WIDTH), BF16),
        ],
        compiler_params=_cparams(1),
        name="in_proj",
    )(x2, nw, w_in, qnw, knw, cos, sin)


MIX_TQ = 256


def _band_constants(seq):
    nb = seq // BLOCK
    coef = np.zeros((3, N_POOL_GROUPS, BLOCK, BAND), np.float32)
    inv_count = np.zeros((3, N_POOL_GROUPS, BLOCK, 1), np.float32)
    bias = np.zeros((3, BLOCK, BAND), np.float32)
    for kind, n in enumerate((0, 1, nb - 1)):
        start = min(max((n - 1) * BLOCK, 0), seq - BAND)
        t = n * BLOCK + np.arange(BLOCK)[:, None]
        s = start + np.arange(BAND)[None, :]
        bias[kind] = np.where(np.abs(s - t) <= WINDOW, 0.0, NEG_BIG)
        for g, win in enumerate(POOL_WINDOWS):
            half = win // 2
            lo = np.clip(t - half, 0, seq)
            hi = np.clip(t + half, 0, seq)
            count = (hi - lo).astype(np.float32)
            inside = ((s >= lo) & (s < hi)).astype(np.float32)
            coef[kind, g] = inside - count * (s == t)
            inv_count[kind, g] = 1.0 / count
    return coef, inv_count, bias


def _mixers_kernel(sink_ref, u_ref, q_ref, k_ref, v_ref, coef_ref, invc_ref, bias_ref,
                   wpool_ref, pscale_ref, ab_ref, *, seq):
    nb = seq // BLOCK
    j = pl.program_id(1)
    n_sub = MIX_TQ // BLOCK
    subs = []
    for r in range(n_sub):
        n = j * n_sub + r
        start = pl.multiple_of(jnp.clip((n - 1) * BLOCK, 0, seq - BAND), BLOCK)
        kind = jnp.where(n == 0, 0, jnp.where(n == nb - 1, 2, 1))
        subs.append((start, kind, slice(r * BLOCK, (r + 1) * BLOCK)))

    scores = {}
    for r, (start, kind, rows) in enumerate(subs):
        for hk in range(N_KV_HEADS):
            kb = k_ref[0, pl.ds(start, BAND), hk * HEAD_DIM:(hk + 1) * HEAD_DIM]
            qs = jnp.concatenate(
                [q_ref[0, rows, h * HEAD_DIM:(h + 1) * HEAD_DIM]
                 for h in range(hk * Q_PER_KV, (hk + 1) * Q_PER_KV)], axis=0)
            scores[r, hk] = lax.dot_general(qs, kb, (((1,), (1,)), ((), ())),
                                            preferred_element_type=F32)
    diffs = {}
    for r, (start, kind, rows) in enumerate(subs):
        for g in range(N_POOL_GROUPS):
            ub = u_ref[0, pl.ds(start, BAND), g * POOL_GROUP:(g + 1) * POOL_GROUP]
            d = jnp.dot(coef_ref[kind, g], ub, preferred_element_type=F32) * invc_ref[kind, g]
            diffs[r, g] = d.astype(BF16)
    probs = {}
    for r, (start, kind, rows) in enumerate(subs):
        bias = bias_ref[kind]
        for hk in range(N_KV_HEADS):
            for gi in range(Q_PER_KV):
                h = hk * Q_PER_KV + gi
                sh = scores[r, hk][gi * BLOCK:(gi + 1) * BLOCK] + bias
                sink = sink_ref[h]
                m = jnp.maximum(jnp.max(sh, axis=-1, keepdims=True), sink)
                p = jnp.exp(sh - m)
                denom = jnp.sum(p, axis=-1, keepdims=True) + jnp.exp(sink - m)
                probs[r, h] = (p.astype(BF16), denom)
    for r, (start, kind, rows) in enumerate(subs):
        for g in range(N_POOL_GROUPS):
            cols = slice(g * POOL_GROUP, (g + 1) * POOL_GROUP)
            y = jnp.dot(diffs[r, g], wpool_ref[g], preferred_element_type=F32)
            ab_ref[0, rows, cols] = (y * pscale_ref[:, cols]).astype(BF16)
    for r, (start, kind, rows) in enumerate(subs):
        for hk in range(N_KV_HEADS):
            vb = v_ref[0, pl.ds(start, BAND), hk * HEAD_DIM:(hk + 1) * HEAD_DIM]
            for gi in range(Q_PER_KV):
                h = hk * Q_PER_KV + gi
                p, denom = probs[r, h]
                o = jnp.dot(p, vb, preferred_element_type=F32) / denom
                ocols = slice(POOL_WIDTH + h * HEAD_DIM, POOL_WIDTH + (h + 1) * HEAD_DIM)
                ab_ref[0, rows, ocols] = o.astype(BF16)


def _mixers(sink, u, q, k, v, coef, invc, bias, wpool, pscale):
    B, S, _ = u.shape
    tq = MIX_TQ
    full = lambda shape: pl.BlockSpec(shape, lambda b, j: (0,) * len(shape))
    return pl.pallas_call(
        functools.partial(_mixers_kernel, seq=S),
        grid=(B, S // tq),
        in_specs=[
            pl.BlockSpec(memory_space=pltpu.SMEM),
            pl.BlockSpec((1, S, POOL_WIDTH), lambda b, j: (b, 0, 0)),
            pl.BlockSpec((1, tq, ATTN_WIDTH), lambda b, j: (b, j, 0)),
            pl.BlockSpec((1, S, KV_WIDTH), lambda b, j: (b, 0, 0)),
            pl.BlockSpec((1, S, KV_WIDTH), lambda b, j: (b, 0, 0)),
            full((3, N_POOL_GROUPS, BLOCK, BAND)),
            full((3, N_POOL_GROUPS, BLOCK, 1)),
            full((3, BLOCK, BAND)),
            full((N_POOL_GROUPS, POOL_GROUP, POOL_GROUP)),
            full((1, POOL_WIDTH)),
        ],
        out_specs=pl.BlockSpec((1, tq, MIX_WIDTH), lambda b, j: (b, j, 0)),
        out_shape=jax.ShapeDtypeStruct((B, S, MIX_WIDTH), BF16),
        compiler_params=_cparams(2),
        name="mixers",
    )(sink, u, q, k, v, coef, invc, bias, wpool, pscale)


OUT_TM = 512
ROUTE_E0, ROUTE_E1, ROUTE_W0, ROUTE_W1 = 0, 1, 2, 3


def _out_proj_kernel(ab_ref, x_ref, w_ref, nw_ref, wr_ref, br_ref,
                     h_ref, xn_ref, route_ref, counts_ref):
    @pl.when(pl.program_id(0) == 0)
    def _():
        counts_ref[...] = jnp.zeros_like(counts_ref)

    sub = OUT_TM // ROW_SPLIT
    lane = lax.broadcasted_iota(jnp.int32, (sub, LANES), 1)

    def first_argmax(vals):
        m = jnp.max(vals, axis=-1, keepdims=True)
        idx = jnp.min(jnp.where(vals == m, lane, LANES), axis=-1, keepdims=True)
        return m, idx

    new_counts = jnp.zeros((1, LANES), F32)
    for part in range(ROW_SPLIT):
        rows = slice(part * sub, (part + 1) * sub)
        h = x_ref[rows, :] + jnp.dot(ab_ref[rows, :], w_ref[...], preferred_element_type=F32)
        h_ref[rows, :] = h
        ms = jnp.mean(h * h, axis=-1, keepdims=True)
        xn = h * lax.rsqrt(ms + EPS) * nw_ref[...]
        packed = _pack_pair(xn[:, :HALF], xn[:, HALF:])
        for c in range(PACK_CHUNKS):
            xn_ref[pl.ds(part * sub * SUBLANES + c, sub, stride=SUBLANES), :] = (
                packed[:, c * LANES:(c + 1) * LANES])
        logits = jnp.dot(xn.astype(BF16), wr_ref[...], preferred_element_type=F32) + br_ref[...]

        gl = jnp.where(lane < N_GROUPS, logits, NEG_BIG)
        gmax, gidx = first_argmax(gl)
        gsum = jnp.sum(jnp.where(lane < N_GROUPS, jnp.exp(gl - gmax), 0.0),
                       axis=-1, keepdims=True)
        g_w = 1.0 / gsum
        e_lo = N_GROUPS + gidx * EXPERTS_PER_GROUP
        el = jnp.where((lane >= e_lo) & (lane < e_lo + EXPERTS_PER_GROUP), logits, NEG_BIG)
        m1, i1 = first_argmax(el)
        m2, i2 = first_argmax(jnp.where(lane == i1, NEG_BIG, el))
        t = jnp.exp(m2 - m1)
        p1 = 1.0 / (1.0 + t)
        p2 = t * p1
        e1 = i1 - N_GROUPS
        e2 = i2 - N_GROUPS
        route_ref[rows, :] = jnp.where(
            lane == ROUTE_E0, e1.astype(F32),
            jnp.where(lane == ROUTE_E1, e2.astype(F32),
                      jnp.where(lane == ROUTE_W0, g_w * p1,
                                jnp.where(lane == ROUTE_W1, g_w * p2, 0.0))))
        chosen = ((lane == e1) | (lane == e2)).astype(F32)
        new_counts = new_counts + jnp.sum(chosen, axis=0, keepdims=True)
    counts_ref[...] += new_counts


def _out_proj(ab, x2, w_out, nw, w_router, b_router):
    T = x2.shape[0]
    tm = OUT_TM
    full = lambda shape: pl.BlockSpec(shape, lambda i: (0,) * len(shape))
    row = lambda width: pl.BlockSpec((tm, width), lambda i: (i, 0))
    return pl.pallas_call(
        _out_proj_kernel,
        grid=(T // tm,),
        in_specs=[row(MIX_WIDTH), row(D_MODEL), full((MIX_WIDTH, D_MODEL)), full((1, D_MODEL)),
                  full((D_MODEL, LANES)), full((1, LANES))],
        out_specs=[row(D_MODEL),
                   pl.BlockSpec((tm * SUBLANES, LANES), lambda i: (i, 0)),
                   row(LANES),
                   full((1, LANES))],
        out_shape=[jax.ShapeDtypeStruct((T, D_MODEL), F32),
                   jax.ShapeDtypeStruct((T * SUBLANES, LANES), U32),
                   jax.ShapeDtypeStruct((T, LANES), F32),
                   jax.ShapeDtypeStruct((1, LANES), F32)],
        compiler_params=_cparams(1),
        name="out_proj",
    )(ab, x2, w_out, nw, w_router, b_router)


PLAN_TM = 512


def _plan_kernel(route_ref, pstart_ref, tri_ref, dest_ref, carry_ref):
    @pl.when(pl.program_id(0) == 0)
    def _():
        carry_ref[...] = jnp.zeros_like(carry_ref)

    route = route_ref[...]
    lane = lax.broadcasted_iota(jnp.int32, route.shape, 1)
    e0 = route[:, ROUTE_E0:ROUTE_E0 + 1].astype(jnp.int32)
    e1 = route[:, ROUTE_E1:ROUTE_E1 + 1].astype(jnp.int32)
    oh0 = lane == e0
    oh1 = lane == e1
    both = (oh0 | oh1).astype(F32)
    earlier = jnp.dot(tri_ref[...], both.astype(BF16), preferred_element_type=F32)
    base = pstart_ref[...] + carry_ref[...] + earlier
    d0 = jnp.sum(jnp.where(oh0, base, 0.0), axis=-1, keepdims=True)
    d1 = jnp.sum(jnp.where(oh1, base, 0.0), axis=-1, keepdims=True)
    dest_ref[...] = jnp.where(lane == 0, d0, jnp.where(lane == 1, d1, 0.0)).astype(jnp.int32)
    carry_ref[...] += jnp.sum(both, axis=0, keepdims=True)


def _plan(route, pstart_row):
    T = route.shape[0]
    tm = PLAN_TM
    tri = jnp.asarray(np.tril(np.ones((tm, tm), np.float32), -1), BF16)
    return pl.pallas_call(
        _plan_kernel,
        grid=(T // tm,),
        in_specs=[pl.BlockSpec((tm, LANES), lambda i: (i, 0)),
                  pl.BlockSpec((1, LANES), lambda i: (0, 0)),
                  pl.BlockSpec((tm, tm), lambda i: (0, 0))],
        out_specs=pl.BlockSpec((tm, LANES), lambda i: (i, 0)),
        out_shape=jax.ShapeDtypeStruct((T, LANES), jnp.int32),
        scratch_shapes=[pltpu.VMEM((1, LANES), F32)],
        compiler_params=_cparams(1),
        name="plan",
    )(route, pstart_row, tri)


DISP_TM = 512
DMA_GROUP = 8
ROW_WORDS = SUBLANES


def _packed_row(ref, row):
    return ref.at[pl.ds(pl.multiple_of(row * ROW_WORDS, ROW_WORDS), ROW_WORDS)]


def _grouped(n, fn):
    def trip(t, carry):
        for k in range(DMA_GROUP):
            fn(t * DMA_GROUP + k, k)
        return carry
    lax.fori_loop(0, n // DMA_GROUP, trip, 0)


def _dispatch_kernel(pad_lo_ref, pad_hi_ref, nused_ref, dest_ref, xn_ref, xs_hbm,
                     zeros, sem, tail_sem, *, experts_per_step, tails_per_step, n_blocks):
    i = pl.program_id(0)
    n_rows = TOP_K * DISP_TM

    @pl.when(i == 0)
    def _():
        zeros[...] = jnp.zeros_like(zeros)

    def scatter_group(t, carry):
        base = pl.multiple_of(t * DMA_GROUP, DMA_GROUP)
        window = xn_ref.at[pl.ds(base * ROW_WORDS, DMA_GROUP * ROW_WORDS)]
        for k in range(DMA_GROUP):
            src = window.at[pl.ds(k * ROW_WORDS, ROW_WORDS)]
            for choice in range(TOP_K):
                dst = _packed_row(xs_hbm, dest_ref[0, 0, choice * DISP_TM + base + k])
                pltpu.make_async_copy(src, dst, sem).start(priority=choice)
        return carry

    lax.fori_loop(0, DISP_TM // DMA_GROUP, scatter_group, 0)

    def pad_fill(start):
        for j in range(experts_per_step):
            e = jnp.minimum(i * experts_per_step + j, N_EXPERTS - 1)
            live = i * experts_per_step + j < N_EXPERTS
            pos = pad_lo_ref[e]
            n_pad = jnp.where(live, pad_hi_ref[e] - pos, 0)
            for bit in reversed(range(MOE_BLOCK.bit_length() - 1)):
                size = 1 << bit
                take = jnp.bitwise_and(lax.shift_right_logical(n_pad, bit), 1)
                dst = xs_hbm.at[pl.ds(pl.multiple_of(pos * ROW_WORDS, ROW_WORDS), size * ROW_WORDS)]
                cp = pltpu.make_async_copy(zeros.at[pl.ds(0, size * ROW_WORDS)], dst, sem)

                @pl.when(take == 1)
                def _():
                    cp.start() if start else cp.wait()
                pos = pos + take * size

    def tail_copies(start):
        for j in range(tails_per_step):
            blk = nused_ref[0] + i * tails_per_step + j
            rows = pl.ds(pl.multiple_of(jnp.minimum(blk, n_blocks - 1) * (MOE_BLOCK * ROW_WORDS),
                                        MOE_BLOCK * ROW_WORDS), MOE_BLOCK * ROW_WORDS)
            cp = pltpu.make_async_copy(zeros, xs_hbm.at[rows], tail_sem)

            @pl.when(blk < n_blocks)
            def _():
                cp.start() if start else cp.wait()

    pad_fill(True)
    tail_copies(True)

    tile_rows = DISP_TM * ROW_WORDS
    for _ in range(TOP_K):
        pltpu.make_async_copy(xn_ref, xs_hbm.at[pl.ds(0, tile_rows)], sem).wait()
    pad_fill(False)
    tail_copies(False)


def _dispatch(pad_lo, pad_hi, n_used, dest3, xn_packed, n_blocks):
    n = dest3.shape[0]
    min_used = (n * DISP_TM * TOP_K) // MOE_BLOCK
    tails_per_step = -(-(n_blocks - min_used) // n)
    grid_spec = pltpu.PrefetchScalarGridSpec(
        num_scalar_prefetch=3,
        grid=(n,),
        in_specs=[pl.BlockSpec((1, 1, TOP_K * DISP_TM), lambda i, *_: (i, 0, 0),
                               memory_space=pltpu.SMEM),
                  pl.BlockSpec((DISP_TM * ROW_WORDS, LANES), lambda i, *_: (i, 0))],
        out_specs=pl.BlockSpec(memory_space=pl.ANY),
        scratch_shapes=[pltpu.VMEM((MOE_BLOCK * ROW_WORDS, LANES), U32),
                        pltpu.SemaphoreType.DMA(()), pltpu.SemaphoreType.DMA(())],
    )
    return pl.pallas_call(
        functools.partial(_dispatch_kernel, experts_per_step=-(-N_EXPERTS // n),
                          tails_per_step=tails_per_step, n_blocks=n_blocks),
        grid_spec=grid_spec,
        out_shape=jax.ShapeDtypeStruct((n_blocks * MOE_BLOCK * ROW_WORDS, LANES), U32),
        compiler_params=_cparams(1),
        name="dispatch",
    )(pad_lo, pad_hi, n_used, dest3, xn_packed)


CAST_ROWS = 256
FETCH_SPLIT = 4


def _experts_kernel(be_ref, first_ref, par_ref, nxt2_ref, misc_ref,
                    xs_ref, wg_hbm, wu_hbm, wd_hbm, ys_ref,
                    fg, fu, fd, wg, wu, wd, sem):
    b = pl.program_id(0)
    n_used = misc_ref[0]

    def fetch(e, p):
        copies = []
        for k, (src, dst) in enumerate(((wg_hbm, fg), (wu_hbm, fu), (wd_hbm, fd))):
            rows = src.shape[1] // FETCH_SPLIT
            for c in range(FETCH_SPLIT):
                sl = pl.ds(c * rows, rows)
                copies.append(pltpu.make_async_copy(src.at[e, sl], dst.at[p, sl], sem.at[p, k]))
        return copies

    def start_fetch(e, p):
        for k, cp in enumerate(fetch(e, p)):
            cp.start(priority=k % 2)

    @pl.when(b == 0)
    def _():
        start_fetch(be_ref[0], 0)

        @pl.when(misc_ref[1] >= 0)
        def _():
            start_fetch(misc_ref[1], 1)

    @pl.when((b < n_used) & (first_ref[b] == 1))
    def _():
        p = par_ref[b]
        for cp in fetch(be_ref[b], p):
            cp.wait()

        def cast(src, dst, n_rows):
            def body(i, carry):
                rows = pl.ds(pl.multiple_of(i * CAST_ROWS, CAST_ROWS), CAST_ROWS)
                dst[rows, :] = src[p, rows, :].astype(BF16)
                return carry
            lax.fori_loop(0, n_rows // CAST_ROWS, body, 0)

        cast(fg, wg, D_MODEL)
        cast(fu, wu, D_MODEL)
        cast(fd, wd, D_EXPERT)

        @pl.when(nxt2_ref[b] >= 0)
        def _():
            start_fetch(nxt2_ref[b], p)

    @pl.when(b < n_used)
    def _():
        lo_parts, hi_parts = [], []
        for c in range(PACK_CHUNKS):
            lo, hi = _unpack_pair(xs_ref[pl.ds(c, MOE_BLOCK, stride=SUBLANES), :])
            lo_parts.append(lo.astype(BF16))
            hi_parts.append(hi.astype(BF16))
        xb = jnp.concatenate(lo_parts + hi_parts, axis=1)
        g = jnp.dot(xb, wg[...], preferred_element_type=F32)
        u = jnp.dot(xb, wu[...], preferred_element_type=F32)
        hmid = (g * jax.nn.sigmoid(g) * u).astype(BF16)
        y = jnp.dot(hmid, wd[...], preferred_element_type=F32)
        packed = _pack_pair(y[:, :HALF], y[:, HALF:])
        for c in range(PACK_CHUNKS):
            ys_ref[pl.ds(c, MOE_BLOCK, stride=SUBLANES), :] = packed[:, c * LANES:(c + 1) * LANES]

    @pl.when(b >= n_used)
    def _():
        ys_ref[...] = jnp.zeros_like(ys_ref)


def _experts(tables, xs, w_gate, w_up, w_down):
    n_blocks = tables[0].shape[0]
    blk = lambda: pl.BlockSpec((MOE_BLOCK * SUBLANES, LANES), lambda b, *_: (b, 0))
    grid_spec = pltpu.PrefetchScalarGridSpec(
        num_scalar_prefetch=len(tables),
        grid=(n_blocks,),
        in_specs=[blk(),
                  pl.BlockSpec(memory_space=pl.ANY),
                  pl.BlockSpec(memory_space=pl.ANY),
                  pl.BlockSpec(memory_space=pl.ANY)],
        out_specs=blk(),
        scratch_shapes=[pltpu.VMEM((2, D_MODEL, D_EXPERT), F32),
                        pltpu.VMEM((2, D_MODEL, D_EXPERT), F32),
                        pltpu.VMEM((2, D_EXPERT, D_MODEL), F32),
                        pltpu.VMEM((D_MODEL, D_EXPERT), BF16),
                        pltpu.VMEM((D_MODEL, D_EXPERT), BF16),
                        pltpu.VMEM((D_EXPERT, D_MODEL), BF16),
                        pltpu.SemaphoreType.DMA((2, 3))],
    )
    return pl.pallas_call(
        _experts_kernel,
        grid_spec=grid_spec,
        out_shape=jax.ShapeDtypeStruct((n_blocks * MOE_BLOCK * SUBLANES, LANES), U32),
        compiler_params=_cparams(1),
        name="experts",
    )(*tables, xs, w_gate, w_up, w_down)


CMB_TM = 256


def _combine_kernel(d_cur_ref, d_next_ref, h_ref, route_ref, ys_hbm, o_ref, ybuf, sem):
    i = pl.program_id(0)
    n = pl.num_programs(0)
    slot = i % 2
    n_rows = TOP_K * CMB_TM

    def gather(idx_ref, s):
        def gather_group(t, carry):
            base = pl.multiple_of(t * DMA_GROUP, DMA_GROUP)
            window = ybuf.at[s, pl.ds(base * ROW_WORDS, DMA_GROUP * ROW_WORDS)]
            for k in range(DMA_GROUP):
                pltpu.make_async_copy(_packed_row(ys_hbm, idx_ref[0, 0, base + k]),
                                      window.at[pl.ds(k * ROW_WORDS, ROW_WORDS)],
                                      sem.at[s]).start(priority=k % 2)
            return carry
        lax.fori_loop(0, n_rows // DMA_GROUP, gather_group, 0)

    @pl.when(i == 0)
    def _():
        gather(d_cur_ref, 0)

    @pl.when(i + 1 < n)
    def _():
        gather(d_next_ref, 1 - slot)

    pltpu.make_async_copy(ys_hbm.at[pl.ds(0, n_rows * ROW_WORDS)], ybuf.at[slot],
                          sem.at[slot]).wait()
    route = route_ref[...]
    w0 = route[:, ROUTE_W0:ROUTE_W0 + 1]
    w1 = route[:, ROUTE_W1:ROUTE_W1 + 1]
    yb = ybuf.at[slot]
    for c in range(PACK_CHUNKS):
        lo0, hi0 = _unpack_pair(yb[pl.ds(c, CMB_TM, stride=SUBLANES), :])
        lo1, hi1 = _unpack_pair(yb[pl.ds(CMB_TM * SUBLANES + c, CMB_TM, stride=SUBLANES), :])
        lo_cols = slice(c * LANES, (c + 1) * LANES)
        hi_cols = slice(HALF + c * LANES, HALF + (c + 1) * LANES)
        o_ref[:, lo_cols] = h_ref[:, lo_cols] + (w0 * lo0 + w1 * lo1)
        o_ref[:, hi_cols] = h_ref[:, hi_cols] + (w0 * hi0 + w1 * hi1)


def _combine(dest3, h, route, ys):
    T = h.shape[0]
    tm = CMB_TM
    n = T // tm
    d_spec = lambda fn: pl.BlockSpec((1, 1, TOP_K * tm), fn, memory_space=pltpu.SMEM)
    return pl.pallas_call(
        _combine_kernel,
        grid=(n,),
        in_specs=[
            d_spec(lambda i: (i, 0, 0)),
            d_spec(lambda i: (jnp.minimum(i + 1, n - 1), 0, 0)),
            pl.BlockSpec((tm, D_MODEL), lambda i: (i, 0)),
            pl.BlockSpec((tm, LANES), lambda i: (i, 0)),
            pl.BlockSpec(memory_space=pl.ANY),
        ],
        out_specs=pl.BlockSpec((tm, D_MODEL), lambda i: (i, 0)),
        out_shape=jax.ShapeDtypeStruct((T, D_MODEL), F32),
        scratch_shapes=[pltpu.VMEM((2, TOP_K * tm * SUBLANES, LANES), U32),
                        pltpu.SemaphoreType.DMA((2,))],
        compiler_params=_cparams(1),
        name="combine",
    )(dest3, dest3, h, route, ys)


def _block_tables(counts_row, T):
    counts = counts_row[0, :N_EXPERTS].astype(jnp.int32)
    padded = (counts + MOE_BLOCK - 1) // MOE_BLOCK * MOE_BLOCK
    pends = jnp.cumsum(padded)
    pstarts = pends - padded
    n_blocks = -(-(T * TOP_K + N_EXPERTS * (MOE_BLOCK - 1)) // MOE_BLOCK)
    n_used = pends[-1] // MOE_BLOCK
    block_start = jnp.arange(n_blocks, dtype=jnp.int32) * MOE_BLOCK
    be = jnp.minimum(jnp.sum(block_start[:, None] >= pends[None, :], axis=1),
                     N_EXPERTS - 1).astype(jnp.int32)
    first =jnp.concatenate([jnp.ones((1,), jnp.int32), (be[1:] != be[:-1]).astype(jnp.int32)])
    par = (jnp.cumsum(first) - 1) % 2
    def following(block):
        end = pends[be[jnp.minimum(block, n_blocks - 1)]] // MOE_BLOCK
        return jnp.where(block < n_used, end, n_used)
    nn_block = following(following(jnp.arange(n_blocks, dtype=jnp.int32)))
    nxt2 = jnp.where(nn_block < n_used, be[jnp.minimum(nn_block, n_blocks - 1)], -1)
    second_run = following(jnp.zeros((), jnp.int32))
    second_expert = jnp.where(second_run < n_used, be[jnp.minimum(second_run, n_blocks - 1)], -1)
    pstart_row = jnp.zeros((1, LANES), F32).at[0, :N_EXPERTS].set(pstarts.astype(F32))
    n_used = n_used.astype(jnp.int32).reshape(1)
    misc = jnp.stack([n_used[0], second_expert.astype(jnp.int32)])
    tables = tuple(t.astype(jnp.int32) for t in (be, first, par, nxt2, misc))
    pads = ((pstarts + counts).astype(jnp.int32), pends.astype(jnp.int32), n_used)
    return pstart_row, tables, pads, n_blocks


def _choice_major(dest, tile):
    T = dest.shape[0]
    return jnp.transpose(dest.reshape(T // tile, tile, TOP_K), (0, 2, 1)).reshape(
        T // tile, 1, TOP_K * tile)


def _rope_tables(S):
    inv = ROPE_THETA ** (-jnp.arange(0, HEAD_DIM, 2, dtype=F32) / HEAD_DIM)
    ang = jnp.arange(S, dtype=F32)[:, None] * inv[None, :]
    cos, sin = jnp.cos(ang), jnp.sin(ang)
    return jnp.concatenate([cos, cos], axis=-1), jnp.concatenate([-sin, sin], axis=-1)


def kernel(x, norm_mix_w, w_in, w_pool, pool_scale, q_norm_w, k_norm_w, sink_logits, w_out,
           norm_ffn_w, w_group_router, b_group_router, w_expert_router, b_expert_router,
           w_gate, w_up, w_down):
    B, S, D = x.shape
    T = B * S
    depth = w_in.shape[0]
    cos, sin = _rope_tables(S)
    coef_np, invc_np, bias_np = _band_constants(S)
    coef = jnp.asarray(coef_np, BF16)
    invc = jnp.asarray(invc_np, F32)
    bias = jnp.asarray(bias_np, F32)
    pad_lanes = LANES - N_GROUPS - N_EXPERTS

    h = x.reshape(T, D)
    for l in range(depth):
        u, q, k, v = _in_proj(h, norm_mix_w[l].reshape(1, D), w_in[l].astype(BF16),
                              q_norm_w[l].reshape(1, HEAD_DIM), k_norm_w[l].reshape(1, HEAD_DIM),
                              cos, sin, S)
        ab = _mixers(sink_logits[l], u.reshape(B, S, -1), q.reshape(B, S, -1),
                     k.reshape(B, S, -1), v.reshape(B, S, -1), coef, invc, bias,
                     w_pool[l].astype(BF16), pool_scale[l].reshape(1, POOL_WIDTH))
        w_router = jnp.concatenate(
            [w_group_router[l], w_expert_router[l], jnp.zeros((D, pad_lanes), F32)], axis=1)
        b_router = jnp.concatenate(
            [b_group_router[l], b_expert_router[l], jnp.zeros((pad_lanes,), F32)]).reshape(1, LANES)
        hmix, xn_packed, route, counts = _out_proj(
            ab.reshape(T, MIX_WIDTH), h, w_out[l].astype(BF16), norm_ffn_w[l].reshape(1, D),
            w_router.astype(BF16), b_router)
        pstart_row, tables, pads, n_blocks = _block_tables(counts, T)
        dest = _plan(route, pstart_row)[:, :TOP_K]
        xs = _dispatch(*pads, _choice_major(dest, DISP_TM), xn_packed, n_blocks)
        ys = _experts(tables, xs, w_gate[l], w_up[l], w_down[l])
        h = _combine(_choice_major(dest, CMB_TM), hmix, route, ys)
    return h.reshape(B, S, D)
```

```python
import functools

import numpy as np
import jax
import jax.numpy as jnp
from jax import lax
from jax.experimental import pallas as pl
from jax.experimental.pallas import tpu as pltpu

D_MODEL = 2048
POOL_WIDTH = 1024
POOL_WINDOWS = (2, 4, 8, 16)
N_POOL_GROUPS = 4
POOL_GROUP = 256
HEAD_DIM = 128
N_Q_HEADS = 8
N_KV_HEADS = 2
Q_PER_KV = 4
ATTN_WIDTH = 1024
KV_WIDTH = 256
IN_PROJ_WIDTH = 2560
MIX_WIDTH = 2048
WINDOW = 128
BLOCK = 128
BAND = 3 * BLOCK
ROPE_THETA = 10000.0
N_GROUPS = 8
EXPERTS_PER_GROUP = 8
N_EXPERTS = 64
TOP_K = 2
D_EXPERT = 512
MOE_BLOCK = 256
EPS = 1e-6

LANES = 128
SUBLANES = 8
HALF = D_MODEL // 2
PACK_CHUNKS = HALF // LANES
NEG_BIG = -1e30
VMEM_LIMIT = 56 * 1024 * 1024

BF16 = jnp.bfloat16
F32 = jnp.float32
U32 = jnp.uint32
HI_MASK = 0xFFFF0000


def _cparams(n_axes):
    return pltpu.CompilerParams(dimension_semantics=("arbitrary",) * n_axes,
                                vmem_limit_bytes=VMEM_LIMIT)


def _pack_pair(lo, hi):
    lo_bits = lax.bitcast_convert_type(lo.astype(BF16).astype(F32), U32) >> 16
    hi_bits = lax.bitcast_convert_type(hi.astype(BF16).astype(F32), U32) & jnp.uint32(HI_MASK)
    return lo_bits | hi_bits


def _unpack_pair(words):
    lo = lax.bitcast_convert_type(words << 16, F32)
    hi = lax.bitcast_convert_type(words & jnp.uint32(HI_MASK), F32)
    return lo, hi


IN_TM = 512
ROW_SPLIT = 2


def _in_proj_kernel(x_ref, nw_ref, w_ref, qnw_ref, knw_ref, cos_ref, sin_ref,
                    u_ref, q_ref, k_ref, v_ref):
    sub = IN_TM // ROW_SPLIT
    for part in range(ROW_SPLIT):
        rows = slice(part * sub, (part + 1) * sub)
        x = x_ref[rows, :]
        ms = jnp.mean(x * x, axis=-1, keepdims=True)
        xn = (x * lax.rsqrt(ms + EPS) * nw_ref[...]).astype(BF16)
        cos = cos_ref[rows, :]
        sin = sin_ref[rows, :]

        def head_norm_rope(t, w, scale):
            hms = jnp.mean(t * t, axis=-1, keepdims=True)
            t = t * lax.rsqrt(hms + EPS) * w
            t = t * cos + pltpu.roll(t, HEAD_DIM // 2, axis=1) * sin
            return t * scale

        u_ref[rows, :] = jnp.dot(xn, w_ref[:, :POOL_WIDTH],
                                 preferred_element_type=F32).astype(BF16)
        zq = jnp.dot(xn, w_ref[:, POOL_WIDTH:POOL_WIDTH + ATTN_WIDTH],
                     preferred_element_type=F32)
        qnw = qnw_ref[...]
        for h in range(N_Q_HEADS):
            sl = slice(h * HEAD_DIM, (h + 1) * HEAD_DIM)
            q_ref[rows, sl] = head_norm_rope(zq[:, sl], qnw, HEAD_DIM ** -0.5).astype(BF16)
        o_k = POOL_WIDTH + ATTN_WIDTH
        zk = jnp.dot(xn, w_ref[:, o_k:o_k + KV_WIDTH], preferred_element_type=F32)
        knw = knw_ref[...]
        for h in range(N_KV_HEADS):
            sl = slice(h * HEAD_DIM, (h + 1) * HEAD_DIM)
            k_ref[rows, sl] = head_norm_rope(zk[:, sl], knw, 1.0).astype(BF16)
        v_ref[rows, :] = jnp.dot(xn, w_ref[:, o_k + KV_WIDTH:],
                                 preferred_element_type=F32).astype(BF16)


def _in_proj(x2, nw, w_in, qnw, knw, cos, sin, seq):
    T = x2.shape[0]
    tm = IN_TM
    pos_blocks = seq // tm
    full = lambda shape: pl.BlockSpec(shape, lambda i: (0,) * len(shape))
    return pl.pallas_call(
        _in_proj_kernel,
        grid=(T // tm,),
        in_specs=[
            pl.BlockSpec((tm, D_MODEL), lambda i: (i, 0)),
            full((1, D_MODEL)),
            full((D_MODEL, IN_PROJ_WIDTH)),
            full((1, HEAD_DIM)),
            full((1, HEAD_DIM)),
            pl.BlockSpec((tm, HEAD_DIM), lambda i: (i % pos_blocks, 0)),
            pl.BlockSpec((tm, HEAD_DIM), lambda i: (i % pos_blocks, 0)),
        ],
        out_specs=[
            pl.BlockSpec((tm, POOL_WIDTH), lambda i: (i, 0)),
            pl.BlockSpec((tm, ATTN_WIDTH), lambda i: (i, 0)),
            pl.BlockSpec((tm, KV_WIDTH), lambda i: (i, 0)),
            pl.BlockSpec((tm, KV_WIDTH), lambda i: (i, 0)),
        ],
        out_shape=[
            jax.ShapeDtypeStruct((T, POOL_WIDTH), BF16),
            jax.ShapeDtypeStruct((T, ATTN_WIDTH), BF16),
            jax.ShapeDtypeStruct((T, KV_WIDTH), BF16),
            jax.ShapeDtypeStruct((T, KV_WIDTH), BF16),
        ],
        compiler_params=_cparams(1),
        name="in_proj",
    )(x2, nw, w_in, qnw, knw, cos, sin)


MIX_TQ = 256


def _band_constants(seq):
    nb = seq // BLOCK
    coef = np.zeros((3, N_POOL_GROUPS, BLOCK, BAND), np.float32)
    inv_count = np.zeros((3, N_POOL_GROUPS, BLOCK, 1), np.float32)
    bias = np.zeros((3, BLOCK, BAND), np.float32)
    for kind, n in enumerate((0, 1, nb - 1)):
        start = min(max((n - 1) * BLOCK, 0), seq - BAND)
        t = n * BLOCK + np.arange(BLOCK)[:, None]
        s = start + np.arange(BAND)[None, :]
        bias[kind] = np.where(np.abs(s - t) <= WINDOW, 0.0, NEG_BIG)
        for g, win in enumerate(POOL_WINDOWS):
            half = win // 2
            lo = np.clip(t - half, 0, seq)
            hi = np.clip(t + half, 0, seq)
            count = (hi - lo).astype(np.float32)
            inside = ((s >= lo) & (s < hi)).astype(np.float32)
            coef[kind, g] = inside - count * (s == t)
            inv_count[kind, g] = 1.0 / count
    return coef, inv_count, bias


def _mixers_kernel(sink_ref, u_ref, q_ref, k_ref, v_ref, coef_ref, invc_ref, bias_ref,
                   wpool_ref, pscale_ref, ab_ref, *, seq):
    nb = seq // BLOCK
    j = pl.program_id(1)
    n_sub = MIX_TQ // BLOCK
    subs = []
    for r in range(n_sub):
        n = j * n_sub + r
        start = pl.multiple_of(jnp.clip((n - 1) * BLOCK, 0, seq - BAND), BLOCK)
        kind = jnp.where(n == 0, 0, jnp.where(n == nb - 1, 2, 1))
        subs.append((start, kind, slice(r * BLOCK, (r + 1) * BLOCK)))

    scores = {}
    for r, (start, kind, rows) in enumerate(subs):
        for hk in range(N_KV_HEADS):
            kb = k_ref[0, pl.ds(start, BAND), hk * HEAD_DIM:(hk + 1) * HEAD_DIM]
            qs = jnp.concatenate(
                [q_ref[0, rows, h * HEAD_DIM:(h + 1) * HEAD_DIM]
                 for h in range(hk * Q_PER_KV, (hk + 1) * Q_PER_KV)], axis=0)
            scores[r, hk] = lax.dot_general(qs, kb, (((1,), (1,)), ((), ())),
                                            preferred_element_type=F32)
    diffs = {}
    for r, (start, kind, rows) in enumerate(subs):
        for g in range(N_POOL_GROUPS):
            ub = u_ref[0, pl.ds(start, BAND), g * POOL_GROUP:(g + 1) * POOL_GROUP]
            d = jnp.dot(coef_ref[kind, g], ub, preferred_element_type=F32) * invc_ref[kind, g]
            diffs[r, g] = d.astype(BF16)
    probs = {}
    for r, (start, kind, rows) in enumerate(subs):
        bias = bias_ref[kind]
        for hk in range(N_KV_HEADS):
            for gi in range(Q_PER_KV):
                h = hk * Q_PER_KV + gi
                sh = scores[r, hk][gi * BLOCK:(gi + 1) * BLOCK] + bias
                sink = sink_ref[h]
                m = jnp.maximum(jnp.max(sh, axis=-1, keepdims=True), sink)
                p = jnp.exp(sh - m)
                denom = jnp.sum(p, axis=-1, keepdims=True) + jnp.exp(sink - m)
                probs[r, h] = (p.astype(BF16), denom)
    for r, (start, kind, rows) in enumerate(subs):
        for g in range(N_POOL_GROUPS):
            cols = slice(g * POOL_GROUP, (g + 1) * POOL_GROUP)
            y = jnp.dot(diffs[r, g], wpool_ref[g], preferred_element_type=F32)
            ab_ref[0, rows, cols] = (y * pscale_ref[:, cols]).astype(BF16)
    for r, (start, kind, rows) in enumerate(subs):
        for hk in range(N_KV_HEADS):
            vb = v_ref[0, pl.ds(start, BAND), hk * HEAD_DIM:(hk + 1) * HEAD_DIM]
            for gi in range(Q_PER_KV):
                h = hk * Q_PER_KV + gi
                p, denom = probs[r, h]
                o = jnp.dot(p, vb, preferred_element_type=F32) / denom
                ocols = slice(POOL_WIDTH + h * HEAD_DIM, POOL_WIDTH + (h + 1) * HEAD_DIM)
                ab_ref[0, rows, ocols] = o.astype(BF16)


def _mixers(sink, u, q, k, v, coef, invc, bias, wpool, pscale):
    B, S, _ = u.shape
    tq = MIX_TQ
    full = lambda shape: pl.BlockSpec(shape, lambda b, j: (0,) * len(shape))
    return pl.pallas_call(
        functools.partial(_mixers_kernel, seq=S),
        grid=(B, S // tq),
        in_specs=[
            pl.BlockSpec(memory_space=pltpu.SMEM),
            pl.BlockSpec((1, S, POOL_WIDTH), lambda b, j: (b, 0, 0)),
            pl.BlockSpec((1, tq, ATTN_WIDTH), lambda b, j: (b, j, 0)),
            pl.BlockSpec((1, S, KV_WIDTH), lambda b, j: (b, 0, 0)),
            pl.BlockSpec((1, S, KV_WIDTH), lambda b, j: (b, 0, 0)),
            full((3, N_POOL_GROUPS, BLOCK, BAND)),
            full((3, N_POOL_GROUPS, BLOCK, 1)),
            full((3, BLOCK, BAND)),
            full((N_POOL_GROUPS, POOL_GROUP, POOL_GROUP)),
            full((1, POOL_WIDTH)),
        ],
        out_specs=pl.BlockSpec((1, tq, MIX_WIDTH), lambda b, j: (b, j, 0)),
        out_shape=jax.ShapeDtypeStruct((B, S, MIX_WIDTH), BF16),
        compiler_params=_cparams(2),
        name="mixers",
    )(sink, u, q, k, v, coef, invc, bias, wpool, pscale)


OUT_TM = 512
ROUTE_E0, ROUTE_E1, ROUTE_W0, ROUTE_W1 = 0, 1, 2, 3


def _out_proj_kernel(ab_ref, x_ref, w_ref, nw_ref, wr_ref, br_ref,
                     h_ref, xn_ref, route_ref, counts_ref):
    @pl.when(pl.program_id(0) == 0)
    def _():
        counts_ref[...] = jnp.zeros_like(counts_ref)

    sub = OUT_TM // ROW_SPLIT
    lane = lax.broadcasted_iota(jnp.int32, (sub, LANES), 1)

    def first_argmax(vals):
        m = jnp.max(vals, axis=-1, keepdims=True)
        idx = jnp.min(jnp.where(vals == m, lane, LANES), axis=-1, keepdims=True)
        return m, idx

    new_counts = jnp.zeros((1, LANES), F32)
    for part in range(ROW_SPLIT):
        rows = slice(part * sub, (part + 1) * sub)
        h = x_ref[rows, :] + jnp.dot(ab_ref[rows, :], w_ref[...], preferred_element_type=F32)
        h_ref[rows, :] = h
        ms = jnp.mean(h * h, axis=-1, keepdims=True)
        xn = h * lax.rsqrt(ms + EPS) * nw_ref[...]
        packed = _pack_pair(xn[:, :HALF], xn[:, HALF:])
        for c in range(PACK_CHUNKS):
            xn_ref[pl.ds(part * sub * SUBLANES + c, sub, stride=SUBLANES), :] = (
                packed[:, c * LANES:(c + 1) * LANES])
        logits = jnp.dot(xn.astype(BF16), wr_ref[...], preferred_element_type=F32) + br_ref[...]

        gl = jnp.where(lane < N_GROUPS, logits, NEG_BIG)
        gmax, gidx = first_argmax(gl)
        gsum = jnp.sum(jnp.where(lane < N_GROUPS, jnp.exp(gl - gmax), 0.0),
                       axis=-1, keepdims=True)
        g_w = 1.0 / gsum
        e_lo = N_GROUPS + gidx * EXPERTS_PER_GROUP
        el = jnp.where((lane >= e_lo) & (lane < e_lo + EXPERTS_PER_GROUP), logits, NEG_BIG)
        m1, i1 = first_argmax(el)
        m2, i2 = first_argmax(jnp.where(lane == i1, NEG_BIG, el))
        t = jnp.exp(m2 - m1)
        p1 = 1.0 / (1.0 + t)
        p2 = t * p1
        e1 = i1 - N_GROUPS
        e2 = i2 - N_GROUPS
        route_ref[rows, :] = jnp.where(
            lane == ROUTE_E0, e1.astype(F32),
            jnp.where(lane == ROUTE_E1, e2.astype(F32),
                      jnp.where(lane == ROUTE_W0, g_w * p1,
                                jnp.where(lane == ROUTE_W1, g_w * p2, 0.0))))
        chosen = ((lane == e1) | (lane == e2)).astype(F32)
        new_counts = new_counts + jnp.sum(chosen, axis=0, keepdims=True)
    counts_ref[...] += new_counts


def _out_proj(ab, x2, w_out, nw, w_router, b_router):
    T = x2.shape[0]
    tm = OUT_TM
    full = lambda shape: pl.BlockSpec(shape, lambda i: (0,) * len(shape))
    row = lambda width: pl.BlockSpec((tm, width), lambda i: (i, 0))
    return pl.pallas_call(
        _out_proj_kernel,
        grid=(T // tm,),
        in_specs=[row(MIX_WIDTH), row(D_MODEL), full((MIX_WIDTH, D_MODEL)), full((1, D_MODEL)),
                  full((D_MODEL, LANES)), full((1, LANES))],
        out_specs=[row(D_MODEL),
                   pl.BlockSpec((tm * SUBLANES, LANES), lambda i: (i, 0)),
                   row(LANES),
                   full((1, LANES))],
        out_shape=[jax.ShapeDtypeStruct((T, D_MODEL), F32),
                   jax.ShapeDtypeStruct((T * SUBLANES, LANES), U32),
                   jax.ShapeDtypeStruct((T, LANES), F32),
                   jax.ShapeDtypeStruct((1, LANES), F32)],
        compiler_params=_cparams(1),
        name="out_proj",
    )(ab, x2, w_out, nw, w_router, b_router)


PLAN_TM = 512


TAB_EXPERT, TAB_FIRST, TAB_PARITY, TAB_NEXT2 = 0, 1, 2, 3
TAB_ROWS = 4
ETAB_PAD_LO, ETAB_PAD_HI, ETAB_MISC = 0, 1, 2
ETAB_ROWS = 3
MISC_N_USED, MISC_SECOND_EXPERT = 0, 1
TABLE_ROWS = SUBLANES


def _slot_tables(counts_ref, tab_ref, etab_ref, run_expert, run_of_block, pstart_s, n_blocks):
    log_block = MOE_BLOCK.bit_length() - 1

    def clear(r, carry):
        run_expert[r] = -1
        return carry
    lax.fori_loop(0, run_expert.shape[0], clear, 0)

    def per_expert(e, carry):
        slot, blk, run = carry
        c = counts_ref[e]
        n_blk = lax.shift_right_logical(c + (MOE_BLOCK - 1), log_block)
        pstart_s[e] = slot
        etab_ref[ETAB_PAD_LO, e] = slot + c
        etab_ref[ETAB_PAD_HI, e] = slot + n_blk * MOE_BLOCK

        def per_block(b, inner):
            tab_ref[TAB_EXPERT, b] = e
            tab_ref[TAB_FIRST, b] = (b == blk).astype(jnp.int32)
            tab_ref[TAB_PARITY, b] = jnp.bitwise_and(run, 1)
            run_of_block[b] = run
            return inner
        lax.fori_loop(blk, blk + n_blk, per_block, 0)

        @pl.when(n_blk > 0)
        def _():
            run_expert[run] = e
        return slot + n_blk * MOE_BLOCK, blk + n_blk, run + (n_blk > 0).astype(jnp.int32)

    _, n_used, _ = lax.fori_loop(0, N_EXPERTS, per_expert,
                                 (jnp.int32(0), jnp.int32(0), jnp.int32(0)))

    def per_used_block(b, carry):
        tab_ref[TAB_NEXT2, b] = run_expert[run_of_block[b] + 2]
        return carry
    lax.fori_loop(0, n_used, per_used_block, 0)

    def per_unused_block(b, carry):
        for row in (TAB_EXPERT, TAB_FIRST, TAB_PARITY):
            tab_ref[row, b] = 0
        tab_ref[TAB_NEXT2, b] = -1
        return carry
    lax.fori_loop(n_used, n_blocks, per_unused_block, 0)

    def clear_misc(e, carry):
        etab_ref[ETAB_MISC, e] = 0
        return carry
    lax.fori_loop(0, N_EXPERTS, clear_misc, 0)
    etab_ref[ETAB_MISC, MISC_N_USED] = n_used
    etab_ref[ETAB_MISC, MISC_SECOND_EXPERT] = run_expert[1]


def _plan_kernel(counts_ref, route_ref, tri_ref, dest_ref, tab_ref, etab_ref,
                 carry_ref, pstart_ref, run_expert, run_of_block, pstart_s, *, n_blocks):
    @pl.when(pl.program_id(0) == 0)
    def _():
        carry_ref[...] = jnp.zeros_like(carry_ref)
        _slot_tables(counts_ref, tab_ref, etab_ref, run_expert, run_of_block, pstart_s, n_blocks)
        lane_row = lax.broadcasted_iota(jnp.int32, (1, LANES), 1)

        def place(e, row):
            return jnp.where(lane_row == e, pstart_s[e].astype(F32), row)
        pstart_ref[...] = lax.fori_loop(0, N_EXPERTS, place, jnp.zeros((1, LANES), F32))

    route = route_ref[...]
    lane = lax.broadcasted_iota(jnp.int32, route.shape, 1)
    e0 = route[:, ROUTE_E0:ROUTE_E0 + 1].astype(jnp.int32)
    e1 = route[:, ROUTE_E1:ROUTE_E1 + 1].astype(jnp.int32)
    oh0 = lane == e0
    oh1 = lane == e1
    both = (oh0 | oh1).astype(F32)
    earlier = jnp.dot(tri_ref[...], both.astype(BF16), preferred_element_type=F32)
    base = pstart_ref[...] + carry_ref[...] + earlier
    d0 = jnp.sum(jnp.where(oh0, base, 0.0), axis=-1, keepdims=True)
    d1 = jnp.sum(jnp.where(oh1, base, 0.0), axis=-1, keepdims=True)
    by_token = jnp.where(lane == 0, d0, jnp.where(lane == 1, d1, 0.0))
    dest_ref[...] = by_token.T[:TABLE_ROWS].astype(jnp.int32)
    carry_ref[...] += jnp.sum(both, axis=0, keepdims=True)


def _plan(counts, route, n_blocks):
    T = route.shape[0]
    tm = PLAN_TM
    tri = jnp.asarray(np.tril(np.ones((tm, tm), np.float32), -1), BF16)
    smem_out = pl.BlockSpec(memory_space=pltpu.SMEM)
    grid_spec = pltpu.PrefetchScalarGridSpec(
        num_scalar_prefetch=1,
        grid=(T // tm,),
        in_specs=[pl.BlockSpec((tm, LANES), lambda i, c: (i, 0)),
                  pl.BlockSpec((tm, tm), lambda i, c: (0, 0))],
        out_specs=[pl.BlockSpec((TABLE_ROWS, tm), lambda i, c: (0, i)), smem_out, smem_out],
        scratch_shapes=[pltpu.VMEM((1, LANES), F32), pltpu.VMEM((1, LANES), F32),
                        pltpu.SMEM((N_EXPERTS + 2,), jnp.int32),
                        pltpu.SMEM((n_blocks,), jnp.int32),
                        pltpu.SMEM((N_EXPERTS,), jnp.int32)],
    )
    return pl.pallas_call(
        functools.partial(_plan_kernel, n_blocks=n_blocks),
        grid_spec=grid_spec,
        out_shape=[jax.ShapeDtypeStruct((TABLE_ROWS, T), jnp.int32),
                   jax.ShapeDtypeStruct((TAB_ROWS, n_blocks), jnp.int32),
                   jax.ShapeDtypeStruct((ETAB_ROWS, N_EXPERTS), jnp.int32)],
        compiler_params=_cparams(1),
        name="plan",
    )(counts, route, tri)


DISP_TM = 512
DMA_GROUP = 8
ROW_WORDS = SUBLANES


def _packed_row(ref, row):
    return ref.at[pl.ds(pl.multiple_of(row * ROW_WORDS, ROW_WORDS), ROW_WORDS)]


def _grouped(n, fn):
    def trip(t, carry):
        for k in range(DMA_GROUP):
            fn(t * DMA_GROUP + k, k)
        return carry
    lax.fori_loop(0, n // DMA_GROUP, trip, 0)


def _dispatch_kernel(etab_ref, dest0_ref, dest1_ref, xn_ref, xs_hbm,
                     zeros, sem, tail_sem, *, experts_per_step, tails_per_step, n_blocks):
    i = pl.program_id(0)
    n_rows = TOP_K * DISP_TM

    @pl.when(i == 0)
    def _():
        zeros[...] = jnp.zeros_like(zeros)

    def scatter_group(t, carry):
        base = pl.multiple_of(t * DMA_GROUP, DMA_GROUP)
        window = xn_ref.at[pl.ds(base * ROW_WORDS, DMA_GROUP * ROW_WORDS)]
        for k in range(DMA_GROUP):
            src = window.at[pl.ds(k * ROW_WORDS, ROW_WORDS)]
            for choice, dest_ref in enumerate((dest0_ref, dest1_ref)):
                dst = _packed_row(xs_hbm, dest_ref[0, 0, 0, base + k])
                pltpu.make_async_copy(src, dst, sem).start(priority=choice)
        return carry

    lax.fori_loop(0, DISP_TM // DMA_GROUP, scatter_group, 0)

    def pad_fill(start):
        for j in range(experts_per_step):
            e = jnp.minimum(i * experts_per_step + j, N_EXPERTS - 1)
            live = i * experts_per_step + j < N_EXPERTS
            pos = etab_ref[ETAB_PAD_LO, e]
            n_pad = jnp.where(live, etab_ref[ETAB_PAD_HI, e] - pos, 0)
            for bit in reversed(range(MOE_BLOCK.bit_length() - 1)):
                size = 1 << bit
                take = jnp.bitwise_and(lax.shift_right_logical(n_pad, bit), 1)
                dst = xs_hbm.at[pl.ds(pl.multiple_of(pos * ROW_WORDS, ROW_WORDS), size * ROW_WORDS)]
                cp = pltpu.make_async_copy(zeros.at[pl.ds(0, size * ROW_WORDS)], dst, sem)

                @pl.when(take == 1)
                def _():
                    cp.start() if start else cp.wait()
                pos = pos + take * size

    def tail_copies(start):
        for j in range(tails_per_step):
            blk = etab_ref[ETAB_MISC, MISC_N_USED] + i * tails_per_step + j
            rows = pl.ds(pl.multiple_of(jnp.minimum(blk, n_blocks - 1) * (MOE_BLOCK * ROW_WORDS),
                                        MOE_BLOCK * ROW_WORDS), MOE_BLOCK * ROW_WORDS)
            cp = pltpu.make_async_copy(zeros, xs_hbm.at[rows], tail_sem)

            @pl.when(blk < n_blocks)
            def _():
                cp.start() if start else cp.wait()

    pad_fill(True)
    tail_copies(True)

    tile_rows = DISP_TM * ROW_WORDS
    for _ in range(TOP_K):
        pltpu.make_async_copy(xn_ref, xs_hbm.at[pl.ds(0, tile_rows)], sem).wait()
    pad_fill(False)
    tail_copies(False)


def _slot_tiles(dest, tile):
    return dest.reshape(dest.shape[0], dest.shape[1] // tile, 1, tile)


def _slot_spec(tile, choice, index):
    return pl.BlockSpec((1, 1, 1, tile), lambda i, *_: (choice, index(i), 0, 0),
                        memory_space=pltpu.SMEM)


def _dispatch(etab, dest, xn_packed, n_blocks):
    dest4 = _slot_tiles(dest, DISP_TM)
    n = dest4.shape[1]
    min_used = (n * DISP_TM * TOP_K) // MOE_BLOCK
    tails_per_step = -(-(n_blocks - min_used) // n)
    grid_spec = pltpu.PrefetchScalarGridSpec(
        num_scalar_prefetch=1,
        grid=(n,),
        in_specs=[_slot_spec(DISP_TM, 0, lambda i: i), _slot_spec(DISP_TM, 1, lambda i: i),
                  pl.BlockSpec((DISP_TM * ROW_WORDS, LANES), lambda i, *_: (i, 0))],
        out_specs=pl.BlockSpec(memory_space=pl.ANY),
        scratch_shapes=[pltpu.VMEM((MOE_BLOCK * ROW_WORDS, LANES), U32),
                        pltpu.SemaphoreType.DMA(()), pltpu.SemaphoreType.DMA(())],
    )
    return pl.pallas_call(
        functools.partial(_dispatch_kernel, experts_per_step=-(-N_EXPERTS // n),
                          tails_per_step=tails_per_step, n_blocks=n_blocks),
        grid_spec=grid_spec,
        out_shape=jax.ShapeDtypeStruct((n_blocks * MOE_BLOCK * ROW_WORDS, LANES), U32),
        compiler_params=_cparams(1),
        name="dispatch",
    )(etab, dest4, dest4, xn_packed)


CAST_ROWS = 256
FETCH_SPLIT = 4


def _experts_kernel(tab_ref, etab_ref,
                    xs_ref, wg_hbm, wu_hbm, wd_hbm, ys_ref,
                    fg, fu, fd, wg, wu, wd, sem):
    b = pl.program_id(0)
    n_used = etab_ref[ETAB_MISC, MISC_N_USED]
    second_expert = etab_ref[ETAB_MISC, MISC_SECOND_EXPERT]

    def fetch(e, p):
        copies = []
        for k, (src, dst) in enumerate(((wg_hbm, fg), (wu_hbm, fu), (wd_hbm, fd))):
            rows = src.shape[1] // FETCH_SPLIT
            for c in range(FETCH_SPLIT):
                sl = pl.ds(c * rows, rows)
                copies.append(pltpu.make_async_copy(src.at[e, sl], dst.at[p, sl], sem.at[p, k]))
        return copies

    def start_fetch(e, p):
        for k, cp in enumerate(fetch(e, p)):
            cp.start(priority=k % 2)

    @pl.when(b == 0)
    def _():
        start_fetch(tab_ref[TAB_EXPERT, 0], 0)

        @pl.when(second_expert >= 0)
        def _():
            start_fetch(second_expert, 1)

    @pl.when((b < n_used) & (tab_ref[TAB_FIRST, b] == 1))
    def _():
        p = tab_ref[TAB_PARITY, b]
        for cp in fetch(tab_ref[TAB_EXPERT, b], p):
            cp.wait()

        def cast(src, dst, n_rows):
            def body(i, carry):
                rows = pl.ds(pl.multiple_of(i * CAST_ROWS, CAST_ROWS), CAST_ROWS)
                dst[rows, :] = src[p, rows, :].astype(BF16)
                return carry
            lax.fori_loop(0, n_rows // CAST_ROWS, body, 0)

        cast(fg, wg, D_MODEL)
        cast(fu, wu, D_MODEL)
        cast(fd, wd, D_EXPERT)

        @pl.when(tab_ref[TAB_NEXT2, b] >= 0)
        def _():
            start_fetch(tab_ref[TAB_NEXT2, b], p)

    @pl.when(b < n_used)
    def _():
        lo_parts, hi_parts = [], []
        for c in range(PACK_CHUNKS):
            lo, hi = _unpack_pair(xs_ref[pl.ds(c, MOE_BLOCK, stride=SUBLANES), :])
            lo_parts.append(lo.astype(BF16))
            hi_parts.append(hi.astype(BF16))
        xb = jnp.concatenate(lo_parts + hi_parts, axis=1)
        g = jnp.dot(xb, wg[...], preferred_element_type=F32)
        u = jnp.dot(xb, wu[...], preferred_element_type=F32)
        hmid = (g * jax.nn.sigmoid(g) * u).astype(BF16)
        y = jnp.dot(hmid, wd[...], preferred_element_type=F32)
        packed = _pack_pair(y[:, :HALF], y[:, HALF:])
        for c in range(PACK_CHUNKS):
            ys_ref[pl.ds(c, MOE_BLOCK, stride=SUBLANES), :] = packed[:, c * LANES:(c + 1) * LANES]

    @pl.when(b >= n_used)
    def _():
        ys_ref[...] = jnp.zeros_like(ys_ref)


def _experts(tab, etab, xs, w_gate, w_up, w_down):
    n_blocks = tab.shape[1]
    blk = lambda: pl.BlockSpec((MOE_BLOCK * SUBLANES, LANES), lambda b, *_: (b, 0))
    grid_spec = pltpu.PrefetchScalarGridSpec(
        num_scalar_prefetch=2,
        grid=(n_blocks,),
        in_specs=[blk(),
                  pl.BlockSpec(memory_space=pl.ANY),
                  pl.BlockSpec(memory_space=pl.ANY),
                  pl.BlockSpec(memory_space=pl.ANY)],
        out_specs=blk(),
        scratch_shapes=[pltpu.VMEM((2, D_MODEL, D_EXPERT), F32),
                        pltpu.VMEM((2, D_MODEL, D_EXPERT), F32),
                        pltpu.VMEM((2, D_EXPERT, D_MODEL), F32),
                        pltpu.VMEM((D_MODEL, D_EXPERT), BF16),
                        pltpu.VMEM((D_MODEL, D_EXPERT), BF16),
                        pltpu.VMEM((D_EXPERT, D_MODEL), BF16),
                        pltpu.SemaphoreType.DMA((2, 3))],
    )
    return pl.pallas_call(
        _experts_kernel,
        grid_spec=grid_spec,
        out_shape=jax.ShapeDtypeStruct((n_blocks * MOE_BLOCK * SUBLANES, LANES), U32),
        compiler_params=_cparams(1),
        name="experts",
    )(tab, etab, xs, w_gate, w_up, w_down)


CMB_TM = 256


def _combine_kernel(d0_cur_ref, d1_cur_ref, d0_next_ref, d1_next_ref, h_ref, route_ref, ys_hbm,
                    o_ref, ybuf, sem):
    i = pl.program_id(0)
    n = pl.num_programs(0)
    slot = i % 2
    n_rows = TOP_K * CMB_TM

    def gather(idx_refs, s):
        def gather_group(t, carry):
            base = pl.multiple_of(t * DMA_GROUP, DMA_GROUP)
            for choice, idx_ref in enumerate(idx_refs):
                first = (choice * CMB_TM + base) * ROW_WORDS
                window = ybuf.at[s, pl.ds(first, DMA_GROUP * ROW_WORDS)]
                for k in range(DMA_GROUP):
                    pltpu.make_async_copy(_packed_row(ys_hbm, idx_ref[0, 0, 0, base + k]),
                                          window.at[pl.ds(k * ROW_WORDS, ROW_WORDS)],
                                          sem.at[s]).start(priority=k % 2)
            return carry
        lax.fori_loop(0, CMB_TM // DMA_GROUP, gather_group, 0)

    @pl.when(i == 0)
    def _():
        gather((d0_cur_ref, d1_cur_ref), 0)

    @pl.when(i + 1 < n)
    def _():
        gather((d0_next_ref, d1_next_ref), 1 - slot)

    pltpu.make_async_copy(ys_hbm.at[pl.ds(0, n_rows * ROW_WORDS)], ybuf.at[slot],
                          sem.at[slot]).wait()
    route = route_ref[...]
    w0 = route[:, ROUTE_W0:ROUTE_W0 + 1]
    w1 = route[:, ROUTE_W1:ROUTE_W1 + 1]
    yb = ybuf.at[slot]
    for c in range(PACK_CHUNKS):
        lo0, hi0 = _unpack_pair(yb[pl.ds(c, CMB_TM, stride=SUBLANES), :])
        lo1, hi1 = _unpack_pair(yb[pl.ds(CMB_TM * SUBLANES + c, CMB_TM, stride=SUBLANES), :])
        lo_cols = slice(c * LANES, (c + 1) * LANES)
        hi_cols = slice(HALF + c * LANES, HALF + (c + 1) * LANES)
        o_ref[:, lo_cols] = h_ref[:, lo_cols] + (w0 * lo0 + w1 * lo1)
        o_ref[:, hi_cols] = h_ref[:, hi_cols] + (w0 * hi0 + w1 * hi1)


def _combine(dest, h, route, ys):
    T = h.shape[0]
    tm = CMB_TM
    n = T // tm
    dest4 = _slot_tiles(dest, tm)
    cur = lambda i: i
    nxt = lambda i: jnp.minimum(i + 1, n - 1)
    return pl.pallas_call(
        _combine_kernel,
        grid=(n,),
        in_specs=[
            _slot_spec(tm, 0, cur), _slot_spec(tm, 1, cur),
            _slot_spec(tm, 0, nxt), _slot_spec(tm, 1, nxt),
            pl.BlockSpec((tm, D_MODEL), lambda i: (i, 0)),
            pl.BlockSpec((tm, LANES), lambda i: (i, 0)),
            pl.BlockSpec(memory_space=pl.ANY),
        ],
        out_specs=pl.BlockSpec((tm, D_MODEL), lambda i: (i, 0)),
        out_shape=jax.ShapeDtypeStruct((T, D_MODEL), F32),
        scratch_shapes=[pltpu.VMEM((2, TOP_K * tm * SUBLANES, LANES), U32),
                        pltpu.SemaphoreType.DMA((2,))],
        compiler_params=_cparams(1),
        name="combine",
    )(dest4, dest4, dest4, dest4, h, route, ys)


def _rope_tables(S):
    inv = ROPE_THETA ** (-jnp.arange(0, HEAD_DIM, 2, dtype=F32) / HEAD_DIM)
    ang = jnp.arange(S, dtype=F32)[:, None] * inv[None, :]
    cos, sin = jnp.cos(ang), jnp.sin(ang)
    return jnp.concatenate([cos, cos], axis=-1), jnp.concatenate([-sin, sin], axis=-1)


def kernel(x, norm_mix_w, w_in, w_pool, pool_scale, q_norm_w, k_norm_w, sink_logits, w_out,
           norm_ffn_w, w_group_router, b_group_router, w_expert_router, b_expert_router,
           w_gate, w_up, w_down):
    B, S, D = x.shape
    T = B * S
    depth = w_in.shape[0]
    cos, sin = _rope_tables(S)
    coef_np, invc_np, bias_np = _band_constants(S)
    coef = jnp.asarray(coef_np, BF16)
    invc = jnp.asarray(invc_np, F32)
    bias = jnp.asarray(bias_np, F32)
    pad_lanes = LANES - N_GROUPS - N_EXPERTS

    h = x.reshape(T, D)
    for l in range(depth):
        u, q, k, v = _in_proj(h, norm_mix_w[l].reshape(1, D), w_in[l].astype(BF16),
                              q_norm_w[l].reshape(1, HEAD_DIM), k_norm_w[l].reshape(1, HEAD_DIM),
                              cos, sin, S)
        ab = _mixers(sink_logits[l], u.reshape(B, S, -1), q.reshape(B, S, -1),
                     k.reshape(B, S, -1), v.reshape(B, S, -1), coef, invc, bias,
                     w_pool[l].astype(BF16), pool_scale[l].reshape(1, POOL_WIDTH))
        w_router = jnp.concatenate(
            [w_group_router[l], w_expert_router[l], jnp.zeros((D, pad_lanes), F32)], axis=1)
        b_router = jnp.concatenate(
            [b_group_router[l], b_expert_router[l], jnp.zeros((pad_lanes,), F32)]).reshape(1, LANES)
        hmix, xn_packed, route, counts = _out_proj(
            ab.reshape(T, MIX_WIDTH), h, w_out[l].astype(BF16), norm_ffn_w[l].reshape(1, D),
            w_router.astype(BF16), b_router)
        n_blocks = -(-(T * TOP_K + N_EXPERTS * (MOE_BLOCK - 1)) // MOE_BLOCK)
        dest, tab, etab = _plan(counts[0].astype(jnp.int32), route, n_blocks)
        xs = _dispatch(etab, dest, xn_packed, n_blocks)
        ys = _experts(tab, etab, xs, w_gate[l], w_up[l], w_down[l])
        h = _combine(dest, hmix, route, ys)
    return h.reshape(B, S, D)
```

```python
import functools

import numpy as np
import jax
import jax.numpy as jnp
from jax import lax
from jax.experimental import pallas as pl
from jax.experimental.pallas import tpu as pltpu

D_MODEL = 2048
POOL_WIDTH = 1024
POOL_WINDOWS = (2, 4, 8, 16)
N_POOL_GROUPS = 4
POOL_GROUP = 256
HEAD_DIM = 128
N_Q_HEADS = 8
N_KV_HEADS = 2
Q_PER_KV = 4
ATTN_WIDTH = 1024
KV_WIDTH = 256
IN_PROJ_WIDTH = 2560
MIX_WIDTH = 2048
WINDOW = 128
BLOCK = 128
BAND = 3 * BLOCK
ROPE_THETA = 10000.0
N_GROUPS = 8
EXPERTS_PER_GROUP = 8
N_EXPERTS = 64
TOP_K = 2
D_EXPERT = 512
MOE_BLOCK = 256
EPS = 1e-6

LANES = 128
SUBLANES = 8
HALF = D_MODEL // 2
PACK_CHUNKS = HALF // LANES
NEG_BIG = -1e30
VMEM_LIMIT = 56 * 1024 * 1024

BF16 = jnp.bfloat16
F32 = jnp.float32
U32 = jnp.uint32
HI_MASK = 0xFFFF0000


def _cparams(n_axes):
    return pltpu.CompilerParams(dimension_semantics=("arbitrary",) * n_axes,
                                vmem_limit_bytes=VMEM_LIMIT)


def _pack_pair(lo, hi):
    lo_bits = lax.bitcast_convert_type(lo.astype(BF16).astype(F32), U32) >> 16
    hi_bits = lax.bitcast_convert_type(hi.astype(BF16).astype(F32), U32) & jnp.uint32(HI_MASK)
    return lo_bits | hi_bits


def _unpack_pair(words):
    lo = lax.bitcast_convert_type(words << 16, F32)
    hi = lax.bitcast_convert_type(words & jnp.uint32(HI_MASK), F32)
    return lo, hi


IN_TM = 512
ROW_SPLIT = 2


def _in_proj_kernel(x_ref, nw_ref, w_ref, qnw_ref, knw_ref, cos_ref, sin_ref,
                    u_ref, q_ref, k_ref, v_ref):
    sub = IN_TM // ROW_SPLIT
    for part in range(ROW_SPLIT):
        rows = slice(part * sub, (part + 1) * sub)
        x = x_ref[rows, :]
        ms = jnp.mean(x * x, axis=-1, keepdims=True)
        xn = (x * lax.rsqrt(ms + EPS) * nw_ref[...]).astype(BF16)
        cos = cos_ref[rows, :]
        sin = sin_ref[rows, :]

        def head_norm_rope(t, w, scale):
            hms = jnp.mean(t * t, axis=-1, keepdims=True)
            t = t * lax.rsqrt(hms + EPS) * w
            t = t * cos + pltpu.roll(t, HEAD_DIM // 2, axis=1) * sin
            return t * scale

        u_ref[rows, :] = jnp.dot(xn, w_ref[:, :POOL_WIDTH],
                                 preferred_element_type=F32).astype(BF16)
        zq = jnp.dot(xn, w_ref[:, POOL_WIDTH:POOL_WIDTH + ATTN_WIDTH],
                     preferred_element_type=F32)
        qnw = qnw_ref[...]
        for h in range(N_Q_HEADS):
            sl = slice(h * HEAD_DIM, (h + 1) * HEAD_DIM)
            q_ref[rows, sl] = head_norm_rope(zq[:, sl], qnw, HEAD_DIM ** -0.5).astype(BF16)
        o_k = POOL_WIDTH + ATTN_WIDTH
        zk = jnp.dot(xn, w_ref[:, o_k:o_k + KV_WIDTH], preferred_element_type=F32)
        knw = knw_ref[...]
        for h in range(N_KV_HEADS):
            sl = slice(h * HEAD_DIM, (h + 1) * HEAD_DIM)
            k_ref[rows, sl] = head_norm_rope(zk[:, sl], knw, 1.0).astype(BF16)
        v_ref[rows, :] = jnp.dot(xn, w_ref[:, o_k + KV_WIDTH:],
                                 preferred_element_type=F32).astype(BF16)


def _in_proj(x2, nw, w_in, qnw, knw, cos, sin, seq):
    T = x2.shape[0]
    tm = IN_TM
    pos_blocks = seq // tm
    full = lambda shape: pl.BlockSpec(shape, lambda i: (0,) * len(shape))
    return pl.pallas_call(
        _in_proj_kernel,
        grid=(T // tm,),
        in_specs=[
            pl.BlockSpec((tm, D_MODEL), lambda i: (i, 0)),
            full((1, D_MODEL)),
            full((D_MODEL, IN_PROJ_WIDTH)),
            full((1, HEAD_DIM)),
            full((1, HEAD_DIM)),
            pl.BlockSpec((tm, HEAD_DIM), lambda i: (i % pos_blocks, 0)),
            pl.BlockSpec((tm, HEAD_DIM), lambda i: (i % pos_blocks, 0)),
        ],
        out_specs=[
            pl.BlockSpec((tm, POOL_WIDTH), lambda i: (i, 0)),
            pl.BlockSpec((tm, ATTN_WIDTH), lambda i: (i, 0)),
            pl.BlockSpec((tm, KV_WIDTH), lambda i: (i, 0)),
            pl.BlockSpec((tm, KV_WIDTH), lambda i: (i, 0)),
        ],
        out_shape=[
            jax.ShapeDtypeStruct((T, POOL_WIDTH), BF16),
            jax.ShapeDtypeStruct((T, ATTN_WIDTH), BF16),
            jax.ShapeDtypeStruct((T, KV_WIDTH), BF16),
            jax.ShapeDtypeStruct((T, KV_WIDTH), BF16),
        ],
        compiler_params=_cparams(1),
        name="in_proj",
    )(x2, nw, w_in, qnw, knw, cos, sin)


MIX_TQ = 256


def _band_constants(seq):
    nb = seq // BLOCK
    coef = np.zeros((3, N_POOL_GROUPS, BLOCK, BAND), np.float32)
    inv_count = np.zeros((3, N_POOL_GROUPS, BLOCK, 1), np.float32)
    bias = np.zeros((3, BLOCK, BAND), np.float32)
    for kind, n in enumerate((0, 1, nb - 1)):
        start = min(max((n - 1) * BLOCK, 0), seq - BAND)
        t = n * BLOCK + np.arange(BLOCK)[:, None]
        s = start + np.arange(BAND)[None, :]
        bias[kind] = np.where(np.abs(s - t) <= WINDOW, 0.0, NEG_BIG)
        for g, win in enumerate(POOL_WINDOWS):
            half = win // 2
            lo = np.clip(t - half, 0, seq)
            hi = np.clip(t + half, 0, seq)
            count = (hi - lo).astype(np.float32)
            inside = ((s >= lo) & (s < hi)).astype(np.float32)
            coef[kind, g] = inside - count * (s == t)
            inv_count[kind, g] = 1.0 / count
    return coef, inv_count, bias


def _mixers_kernel(sink_ref, u_ref, q_ref, k_ref, v_ref, coef_ref, invc_ref, bias_ref,
                   wpool_ref, pscale_ref, ab_ref, *, seq):
    nb = seq // BLOCK
    j = pl.program_id(1)
    n_sub = MIX_TQ // BLOCK
    subs = []
    for r in range(n_sub):
        n = j * n_sub + r
        start = pl.multiple_of(jnp.clip((n - 1) * BLOCK, 0, seq - BAND), BLOCK)
        kind = jnp.where(n == 0, 0, jnp.where(n == nb - 1, 2, 1))
        subs.append((start, kind, slice(r * BLOCK, (r + 1) * BLOCK)))

    scores = {}
    for r, (start, kind, rows) in enumerate(subs):
        for hk in range(N_KV_HEADS):
            kb = k_ref[0, pl.ds(start, BAND), hk * HEAD_DIM:(hk + 1) * HEAD_DIM]
            qs = jnp.concatenate(
                [q_ref[0, rows, h * HEAD_DIM:(h + 1) * HEAD_DIM]
                 for h in range(hk * Q_PER_KV, (hk + 1) * Q_PER_KV)], axis=0)
            scores[r, hk] = lax.dot_general(qs, kb, (((1,), (1,)), ((), ())),
                                            preferred_element_type=F32)
    diffs = {}
    for r, (start, kind, rows) in enumerate(subs):
        for g in range(N_POOL_GROUPS):
            ub = u_ref[0, pl.ds(start, BAND), g * POOL_GROUP:(g + 1) * POOL_GROUP]
            d = jnp.dot(coef_ref[kind, g], ub, preferred_element_type=F32) * invc_ref[kind, g]
            diffs[r, g] = d.astype(BF16)
    probs = {}
    for r, (start, kind, rows) in enumerate(subs):
        bias = bias_ref[kind]
        for hk in range(N_KV_HEADS):
            for gi in range(Q_PER_KV):
                h = hk * Q_PER_KV + gi
                sh = scores[r, hk][gi * BLOCK:(gi + 1) * BLOCK] + bias
                sink = sink_ref[h]
                m = jnp.maximum(jnp.max(sh, axis=-1, keepdims=True), sink)
                p = jnp.exp(sh - m)
                denom = jnp.sum(p, axis=-1, keepdims=True) + jnp.exp(sink - m)
                probs[r, h] = (p.astype(BF16), denom)
    for r, (start, kind, rows) in enumerate(subs):
        for g in range(N_POOL_GROUPS):
            cols = slice(g * POOL_GROUP, (g + 1) * POOL_GROUP)
            y = jnp.dot(diffs[r, g], wpool_ref[g], preferred_element_type=F32)
            ab_ref[0, rows, cols] = (y * pscale_ref[:, cols]).astype(BF16)
    for r, (start, kind, rows) in enumerate(subs):
        for hk in range(N_KV_HEADS):
            vb = v_ref[0, pl.ds(start, BAND), hk * HEAD_DIM:(hk + 1) * HEAD_DIM]
            for gi in range(Q_PER_KV):
                h = hk * Q_PER_KV + gi
                p, denom = probs[r, h]
                o = jnp.dot(p, vb, preferred_element_type=F32) / denom
                ocols = slice(POOL_WIDTH + h * HEAD_DIM, POOL_WIDTH + (h + 1) * HEAD_DIM)
                ab_ref[0, rows, ocols] = o.astype(BF16)


def _mixers(sink, u, q, k, v, coef, invc, bias, wpool, pscale):
    B, S, _ = u.shape
    tq = MIX_TQ
    full = lambda shape: pl.BlockSpec(shape, lambda b, j: (0,) * len(shape))
    return pl.pallas_call(
        functools.partial(_mixers_kernel, seq=S),
        grid=(B, S // tq),
        in_specs=[
            pl.BlockSpec(memory_space=pltpu.SMEM),
            pl.BlockSpec((1, S, POOL_WIDTH), lambda b, j: (b, 0, 0)),
            pl.BlockSpec((1, tq, ATTN_WIDTH), lambda b, j: (b, j, 0)),
            pl.BlockSpec((1, S, KV_WIDTH), lambda b, j: (b, 0, 0)),
            pl.BlockSpec((1, S, KV_WIDTH), lambda b, j: (b, 0, 0)),
            full((3, N_POOL_GROUPS, BLOCK, BAND)),
            full((3, N_POOL_GROUPS, BLOCK, 1)),
            full((3, BLOCK, BAND)),
            full((N_POOL_GROUPS, POOL_GROUP, POOL_GROUP)),
            full((1, POOL_WIDTH)),
        ],
        out_specs=pl.BlockSpec((1, tq, MIX_WIDTH), lambda b, j: (b, j, 0)),
        out_shape=jax.ShapeDtypeStruct((B, S, MIX_WIDTH), BF16),
        compiler_params=_cparams(2),
        name="mixers",
    )(sink, u, q, k, v, coef, invc, bias, wpool, pscale)


OUT_TM = 512
ROUTE_E0, ROUTE_E1, ROUTE_W0, ROUTE_W1 = 0, 1, 2, 3


def _out_proj_kernel(ab_ref, x_ref, w_ref, nw_ref, wr_ref, br_ref,
                     h_ref, xn_ref, route_ref, counts_ref):
    @pl.when(pl.program_id(0) == 0)
    def _():
        counts_ref[...] = jnp.zeros_like(counts_ref)

    sub = OUT_TM // ROW_SPLIT
    lane = lax.broadcasted_iota(jnp.int32, (sub, LANES), 1)

    def first_argmax(vals):
        m = jnp.max(vals, axis=-1, keepdims=True)
        idx = jnp.min(jnp.where(vals == m, lane, LANES), axis=-1, keepdims=True)
        return m, idx

    new_counts = jnp.zeros((1, LANES), F32)
    for part in range(ROW_SPLIT):
        rows = slice(part * sub, (part + 1) * sub)
        h = x_ref[rows, :] + jnp.dot(ab_ref[rows, :], w_ref[...], preferred_element_type=F32)
        h_ref[rows, :] = h
        ms = jnp.mean(h * h, axis=-1, keepdims=True)
        xn = h * lax.rsqrt(ms + EPS) * nw_ref[...]
        packed = _pack_pair(xn[:, :HALF], xn[:, HALF:])
        for c in range(PACK_CHUNKS):
            xn_ref[pl.ds(part * sub * SUBLANES + c, sub, stride=SUBLANES), :] = (
                packed[:, c * LANES:(c + 1) * LANES])
        logits = jnp.dot(xn.astype(BF16), wr_ref[...], preferred_element_type=F32) + br_ref[...]

        gl = jnp.where(lane < N_GROUPS, logits, NEG_BIG)
        gmax, gidx = first_argmax(gl)
        gsum = jnp.sum(jnp.where(lane < N_GROUPS, jnp.exp(gl - gmax), 0.0),
                       axis=-1, keepdims=True)
        g_w = 1.0 / gsum
        e_lo = N_GROUPS + gidx * EXPERTS_PER_GROUP
        el = jnp.where((lane >= e_lo) & (lane < e_lo + EXPERTS_PER_GROUP), logits, NEG_BIG)
        m1, i1 = first_argmax(el)
        m2, i2 = first_argmax(jnp.where(lane == i1, NEG_BIG, el))
        t = jnp.exp(m2 - m1)
        p1 = 1.0 / (1.0 + t)
        p2 = t * p1
        e1 = i1 - N_GROUPS
        e2 = i2 - N_GROUPS
        route_ref[rows, :] = jnp.where(
            lane == ROUTE_E0, e1.astype(F32),
            jnp.where(lane == ROUTE_E1, e2.astype(F32),
                      jnp.where(lane == ROUTE_W0, g_w * p1,
                                jnp.where(lane == ROUTE_W1, g_w * p2, 0.0))))
        chosen = ((lane == e1) | (lane == e2)).astype(F32)
        new_counts = new_counts + jnp.sum(chosen, axis=0, keepdims=True)
    counts_ref[...] += new_counts


def _out_proj(ab, x2, w_out, nw, w_router, b_router):
    T = x2.shape[0]
    tm = OUT_TM
    full = lambda shape: pl.BlockSpec(shape, lambda i: (0,) * len(shape))
    row = lambda width: pl.BlockSpec((tm, width), lambda i: (i, 0))
    return pl.pallas_call(
        _out_proj_kernel,
        grid=(T // tm,),
        in_specs=[row(MIX_WIDTH), row(D_MODEL), full((MIX_WIDTH, D_MODEL)), full((1, D_MODEL)),
                  full((D_MODEL, LANES)), full((1, LANES))],
        out_specs=[row(D_MODEL),
                   pl.BlockSpec((tm * SUBLANES, LANES), lambda i: (i, 0)),
                   row(LANES),
                   full((1, LANES))],
        out_shape=[jax.ShapeDtypeStruct((T, D_MODEL), F32),
                   jax.ShapeDtypeStruct((T * SUBLANES, LANES), U32),
                   jax.ShapeDtypeStruct((T, LANES), F32),
                   jax.ShapeDtypeStruct((1, LANES), F32)],
        compiler_params=_cparams(1),
        name="out_proj",
    )(ab, x2, w_out, nw, w_router, b_router)


PLAN_TM = 512


TAB_EXPERT, TAB_FIRST, TAB_BUFFER, TAB_NEXT = 0, 1, 2, 3
TAB_ROWS = 4
ETAB_PAD_LO, ETAB_PAD_HI, ETAB_MISC = 0, 1, 2
ETAB_ROWS = 3
MISC_N_USED, MISC_RUN_EXPERT = 0, 1
FETCH_AHEAD = 3
TABLE_ROWS = SUBLANES


def _slot_tables(counts_ref, tab_ref, etab_ref, run_expert, run_of_block, pstart_s, n_blocks):
    log_block = MOE_BLOCK.bit_length() - 1

    def clear(r, carry):
        run_expert[r] = -1
        return carry
    lax.fori_loop(0, run_expert.shape[0], clear, 0)

    def per_expert(e, carry):
        slot, blk, run, buf = carry
        c = counts_ref[e]
        n_blk = lax.shift_right_logical(c + (MOE_BLOCK - 1), log_block)
        pstart_s[e] = slot
        etab_ref[ETAB_PAD_LO, e] = slot + c
        etab_ref[ETAB_PAD_HI, e] = slot + n_blk * MOE_BLOCK

        def per_block(b, inner):
            tab_ref[TAB_EXPERT, b] = e
            tab_ref[TAB_FIRST, b] = (b == blk).astype(jnp.int32)
            tab_ref[TAB_BUFFER, b] = buf
            run_of_block[b] = run
            return inner
        lax.fori_loop(blk, blk + n_blk, per_block, 0)

        @pl.when(n_blk > 0)
        def _():
            run_expert[run] = e
        has_run = (n_blk > 0).astype(jnp.int32)
        next_buf = jnp.where(buf + has_run == FETCH_AHEAD, 0, buf + has_run)
        return slot + n_blk * MOE_BLOCK, blk + n_blk, run + has_run, next_buf

    zero = jnp.int32(0)
    _, n_used, _, _ = lax.fori_loop(0, N_EXPERTS, per_expert, (zero, zero, zero, zero))

    def per_used_block(b, carry):
        tab_ref[TAB_NEXT, b] = run_expert[run_of_block[b] + FETCH_AHEAD]
        return carry
    lax.fori_loop(0, n_used, per_used_block, 0)

    def per_unused_block(b, carry):
        for row in (TAB_EXPERT, TAB_FIRST, TAB_BUFFER):
            tab_ref[row, b] = 0
        tab_ref[TAB_NEXT, b] = -1
        return carry
    lax.fori_loop(n_used, n_blocks, per_unused_block, 0)

    def clear_misc(e, carry):
        etab_ref[ETAB_MISC, e] = 0
        return carry
    lax.fori_loop(0, N_EXPERTS, clear_misc, 0)
    etab_ref[ETAB_MISC, MISC_N_USED] = n_used
    for r in range(FETCH_AHEAD):
        etab_ref[ETAB_MISC, MISC_RUN_EXPERT + r] = run_expert[r]


def _plan_kernel(counts_ref, route_ref, tri_ref, dest_ref, tab_ref, etab_ref,
                 carry_ref, pstart_ref, run_expert, run_of_block, pstart_s, *, n_blocks):
    @pl.when(pl.program_id(0) == 0)
    def _():
        carry_ref[...] = jnp.zeros_like(carry_ref)
        _slot_tables(counts_ref, tab_ref, etab_ref, run_expert, run_of_block, pstart_s, n_blocks)
        lane_row = lax.broadcasted_iota(jnp.int32, (1, LANES), 1)

        def place(e, row):
            return jnp.where(lane_row == e, pstart_s[e].astype(F32), row)
        pstart_ref[...] = lax.fori_loop(0, N_EXPERTS, place, jnp.zeros((1, LANES), F32))

    route = route_ref[...]
    lane = lax.broadcasted_iota(jnp.int32, route.shape, 1)
    e0 = route[:, ROUTE_E0:ROUTE_E0 + 1].astype(jnp.int32)
    e1 = route[:, ROUTE_E1:ROUTE_E1 + 1].astype(jnp.int32)
    oh0 = lane == e0
    oh1 = lane == e1
    both = (oh0 | oh1).astype(F32)
    earlier = jnp.dot(tri_ref[...], both.astype(BF16), preferred_element_type=F32)
    base = pstart_ref[...] + carry_ref[...] + earlier
    d0 = jnp.sum(jnp.where(oh0, base, 0.0), axis=-1, keepdims=True)
    d1 = jnp.sum(jnp.where(oh1, base, 0.0), axis=-1, keepdims=True)
    by_token = jnp.where(lane == 0, d0, jnp.where(lane == 1, d1, 0.0))
    dest_ref[...] = by_token.T[:TABLE_ROWS].astype(jnp.int32)
    carry_ref[...] += jnp.sum(both, axis=0, keepdims=True)


def _plan(counts, route, n_blocks):
    T = route.shape[0]
    tm = PLAN_TM
    tri = jnp.asarray(np.tril(np.ones((tm, tm), np.float32), -1), BF16)
    smem_out = pl.BlockSpec(memory_space=pltpu.SMEM)
    grid_spec = pltpu.PrefetchScalarGridSpec(
        num_scalar_prefetch=1,
        grid=(T // tm,),
        in_specs=[pl.BlockSpec((tm, LANES), lambda i, c: (i, 0)),
                  pl.BlockSpec((tm, tm), lambda i, c: (0, 0))],
        out_specs=[pl.BlockSpec((TABLE_ROWS, tm), lambda i, c: (0, i)), smem_out, smem_out],
        scratch_shapes=[pltpu.VMEM((1, LANES), F32), pltpu.VMEM((1, LANES), F32),
                        pltpu.SMEM((N_EXPERTS + FETCH_AHEAD,), jnp.int32),
                        pltpu.SMEM((n_blocks,), jnp.int32),
                        pltpu.SMEM((N_EXPERTS,), jnp.int32)],
    )
    return pl.pallas_call(
        functools.partial(_plan_kernel, n_blocks=n_blocks),
        grid_spec=grid_spec,
        out_shape=[jax.ShapeDtypeStruct((TABLE_ROWS, T), jnp.int32),
                   jax.ShapeDtypeStruct((TAB_ROWS, n_blocks), jnp.int32),
                   jax.ShapeDtypeStruct((ETAB_ROWS, N_EXPERTS), jnp.int32)],
        compiler_params=_cparams(1),
        name="plan",
    )(counts, route, tri)


DISP_TM = 512
DMA_GROUP = 8
ROW_WORDS = SUBLANES


def _packed_row(ref, row):
    return ref.at[pl.ds(pl.multiple_of(row * ROW_WORDS, ROW_WORDS), ROW_WORDS)]


def _grouped(n, fn):
    def trip(t, carry):
        for k in range(DMA_GROUP):
            fn(t * DMA_GROUP + k, k)
        return carry
    lax.fori_loop(0, n // DMA_GROUP, trip, 0)


def _dispatch_kernel(etab_ref, dest0_ref, dest1_ref, xn_ref, xs_hbm,
                     zeros, sem, tail_sem, *, experts_per_step, tails_per_step, n_blocks):
    i = pl.program_id(0)
    n_rows = TOP_K * DISP_TM

    @pl.when(i == 0)
    def _():
        zeros[...] = jnp.zeros_like(zeros)

    def scatter_group(t, carry):
        base = pl.multiple_of(t * DMA_GROUP, DMA_GROUP)
        window = xn_ref.at[pl.ds(base * ROW_WORDS, DMA_GROUP * ROW_WORDS)]
        for k in range(DMA_GROUP):
            src = window.at[pl.ds(k * ROW_WORDS, ROW_WORDS)]
            for choice, dest_ref in enumerate((dest0_ref, dest1_ref)):
                dst = _packed_row(xs_hbm, dest_ref[0, 0, 0, base + k])
                pltpu.make_async_copy(src, dst, sem).start(priority=choice)
        return carry

    lax.fori_loop(0, DISP_TM // DMA_GROUP, scatter_group, 0)

    def pad_fill(start):
        for j in range(experts_per_step):
            e = jnp.minimum(i * experts_per_step + j, N_EXPERTS - 1)
            live = i * experts_per_step + j < N_EXPERTS
            pos = etab_ref[ETAB_PAD_LO, e]
            n_pad = jnp.where(live, etab_ref[ETAB_PAD_HI, e] - pos, 0)
            for bit in reversed(range(MOE_BLOCK.bit_length() - 1)):
                size = 1 << bit
                take = jnp.bitwise_and(lax.shift_right_logical(n_pad, bit), 1)
                dst = xs_hbm.at[pl.ds(pl.multiple_of(pos * ROW_WORDS, ROW_WORDS), size * ROW_WORDS)]
                cp = pltpu.make_async_copy(zeros.at[pl.ds(0, size * ROW_WORDS)], dst, sem)

                @pl.when(take == 1)
                def _():
                    cp.start() if start else cp.wait()
                pos = pos + take * size

    def tail_copies(start):
        for j in range(tails_per_step):
            blk = etab_ref[ETAB_MISC, MISC_N_USED] + i * tails_per_step + j
            rows = pl.ds(pl.multiple_of(jnp.minimum(blk, n_blocks - 1) * (MOE_BLOCK * ROW_WORDS),
                                        MOE_BLOCK * ROW_WORDS), MOE_BLOCK * ROW_WORDS)
            cp = pltpu.make_async_copy(zeros, xs_hbm.at[rows], tail_sem)

            @pl.when(blk < n_blocks)
            def _():
                cp.start() if start else cp.wait()

    pad_fill(True)
    tail_copies(True)

    tile_rows = DISP_TM * ROW_WORDS
    for _ in range(TOP_K):
        pltpu.make_async_copy(xn_ref, xs_hbm.at[pl.ds(0, tile_rows)], sem).wait()
    pad_fill(False)
    tail_copies(False)


def _slot_tiles(dest, tile):
    return dest.reshape(dest.shape[0], dest.shape[1] // tile, 1, tile)


def _slot_spec(tile, choice, index):
    return pl.BlockSpec((1, 1, 1, tile), lambda i, *_: (choice, index(i), 0, 0),
                        memory_space=pltpu.SMEM)


def _dispatch(etab, dest, xn_packed, n_blocks):
    dest4 = _slot_tiles(dest, DISP_TM)
    n = dest4.shape[1]
    min_used = (n * DISP_TM * TOP_K) // MOE_BLOCK
    tails_per_step = -(-(n_blocks - min_used) // n)
    grid_spec = pltpu.PrefetchScalarGridSpec(
        num_scalar_prefetch=1,
        grid=(n,),
        in_specs=[_slot_spec(DISP_TM, 0, lambda i: i), _slot_spec(DISP_TM, 1, lambda i: i),
                  pl.BlockSpec((DISP_TM * ROW_WORDS, LANES), lambda i, *_: (i, 0))],
        out_specs=pl.BlockSpec(memory_space=pl.ANY),
        scratch_shapes=[pltpu.VMEM((MOE_BLOCK * ROW_WORDS, LANES), U32),
                        pltpu.SemaphoreType.DMA(()), pltpu.SemaphoreType.DMA(())],
    )
    return pl.pallas_call(
        functools.partial(_dispatch_kernel, experts_per_step=-(-N_EXPERTS // n),
                          tails_per_step=tails_per_step, n_blocks=n_blocks),
        grid_spec=grid_spec,
        out_shape=jax.ShapeDtypeStruct((n_blocks * MOE_BLOCK * ROW_WORDS, LANES), U32),
        compiler_params=_cparams(1),
        name="dispatch",
    )(etab, dest4, dest4, xn_packed)


CAST_ROWS = 256
FETCH_SPLIT = 4


def _experts_kernel(tab_ref, etab_ref,
                    xs_ref, wg_hbm, wu_hbm, wd_hbm, ys_ref,
                    fg, fu, fd, wg, wu, wd, sem):
    b = pl.program_id(0)
    n_used = etab_ref[ETAB_MISC, MISC_N_USED]

    def fetch(e, p):
        copies = []
        for k, (src, dst) in enumerate(((wg_hbm, fg), (wu_hbm, fu), (wd_hbm, fd))):
            rows = src.shape[1] // FETCH_SPLIT
            for c in range(FETCH_SPLIT):
                sl = pl.ds(c * rows, rows)
                copies.append(pltpu.make_async_copy(src.at[e, sl], dst.at[p, sl], sem.at[p, k]))
        return copies

    def start_fetch(e, p):
        for k, cp in enumerate(fetch(e, p)):
            cp.start(priority=k % 2)

    @pl.when(b == 0)
    def _():
        for r in range(FETCH_AHEAD):
            e = etab_ref[ETAB_MISC, MISC_RUN_EXPERT + r]

            @pl.when(e >= 0)
            def _():
                start_fetch(e, r)

    @pl.when((b < n_used) & (tab_ref[TAB_FIRST, b] == 1))
    def _():
        p = tab_ref[TAB_BUFFER, b]
        for cp in fetch(tab_ref[TAB_EXPERT, b], p):
            cp.wait()

        def cast(src, dst, n_rows):
            def body(i, carry):
                rows = pl.ds(pl.multiple_of(i * CAST_ROWS, CAST_ROWS), CAST_ROWS)
                dst[rows, :] = src[p, rows, :].astype(BF16)
                return carry
            lax.fori_loop(0, n_rows // CAST_ROWS, body, 0)

        cast(fg, wg, D_MODEL)
        cast(fu, wu, D_MODEL)
        cast(fd, wd, D_EXPERT)

        @pl.when(tab_ref[TAB_NEXT, b] >= 0)
        def _():
            start_fetch(tab_ref[TAB_NEXT, b], p)

    @pl.when(b < n_used)
    def _():
        lo_parts, hi_parts = [], []
        for c in range(PACK_CHUNKS):
            lo, hi = _unpack_pair(xs_ref[pl.ds(c, MOE_BLOCK, stride=SUBLANES), :])
            lo_parts.append(lo.astype(BF16))
            hi_parts.append(hi.astype(BF16))
        xb = jnp.concatenate(lo_parts + hi_parts, axis=1)
        g = jnp.dot(xb, wg[...], preferred_element_type=F32)
        u = jnp.dot(xb, wu[...], preferred_element_type=F32)
        hmid = (g * jax.nn.sigmoid(g) * u).astype(BF16)
        y = jnp.dot(hmid, wd[...], preferred_element_type=F32)
        packed = _pack_pair(y[:, :HALF], y[:, HALF:])
        for c in range(PACK_CHUNKS):
            ys_ref[pl.ds(c, MOE_BLOCK, stride=SUBLANES), :] = packed[:, c * LANES:(c + 1) * LANES]

    @pl.when(b >= n_used)
    def _():
        ys_ref[...] = jnp.zeros_like(ys_ref)


def _experts(tab, etab, xs, w_gate, w_up, w_down):
    n_blocks = tab.shape[1]
    blk = lambda: pl.BlockSpec((MOE_BLOCK * SUBLANES, LANES), lambda b, *_: (b, 0))
    grid_spec = pltpu.PrefetchScalarGridSpec(
        num_scalar_prefetch=2,
        grid=(n_blocks,),
        in_specs=[blk(),
                  pl.BlockSpec(memory_space=pl.ANY),
                  pl.BlockSpec(memory_space=pl.ANY),
                  pl.BlockSpec(memory_space=pl.ANY)],
        out_specs=blk(),
        scratch_shapes=[pltpu.VMEM((FETCH_AHEAD, D_MODEL, D_EXPERT), F32),
                        pltpu.VMEM((FETCH_AHEAD, D_MODEL, D_EXPERT), F32),
                        pltpu.VMEM((FETCH_AHEAD, D_EXPERT, D_MODEL), F32),
                        pltpu.VMEM((D_MODEL, D_EXPERT), BF16),
                        pltpu.VMEM((D_MODEL, D_EXPERT), BF16),
                        pltpu.VMEM((D_EXPERT, D_MODEL), BF16),
                        pltpu.SemaphoreType.DMA((FETCH_AHEAD, 3))],
    )
    return pl.pallas_call(
        _experts_kernel,
        grid_spec=grid_spec,
        out_shape=jax.ShapeDtypeStruct((n_blocks * MOE_BLOCK * SUBLANES, LANES), U32),
        compiler_params=_cparams(1),
        name="experts",
    )(tab, etab, xs, w_gate, w_up, w_down)


CMB_TM = 256


def _combine_kernel(d0_cur_ref, d1_cur_ref, d0_next_ref, d1_next_ref, h_ref, route_ref, ys_hbm,
                    o_ref, ybuf, sem):
    i = pl.program_id(0)
    n = pl.num_programs(0)
    slot = i % 2
    n_rows = TOP_K * CMB_TM

    def gather(idx_refs, s):
        def gather_group(t, carry):
            base = pl.multiple_of(t * DMA_GROUP, DMA_GROUP)
            for choice, idx_ref in enumerate(idx_refs):
                first = (choice * CMB_TM + base) * ROW_WORDS
                window = ybuf.at[s, pl.ds(first, DMA_GROUP * ROW_WORDS)]
                for k in range(DMA_GROUP):
                    pltpu.make_async_copy(_packed_row(ys_hbm, idx_ref[0, 0, 0, base + k]),
                                          window.at[pl.ds(k * ROW_WORDS, ROW_WORDS)],
                                          sem.at[s]).start(priority=k % 2)
            return carry
        lax.fori_loop(0, CMB_TM // DMA_GROUP, gather_group, 0)

    @pl.when(i == 0)
    def _():
        gather((d0_cur_ref, d1_cur_ref), 0)

    @pl.when(i + 1 < n)
    def _():
        gather((d0_next_ref, d1_next_ref), 1 - slot)

    pltpu.make_async_copy(ys_hbm.at[pl.ds(0, n_rows * ROW_WORDS)], ybuf.at[slot],
                          sem.at[slot]).wait()
    route = route_ref[...]
    w0 = route[:, ROUTE_W0:ROUTE_W0 + 1]
    w1 = route[:, ROUTE_W1:ROUTE_W1 + 1]
    yb = ybuf.at[slot]
    for c in range(PACK_CHUNKS):
        lo0, hi0 = _unpack_pair(yb[pl.ds(c, CMB_TM, stride=SUBLANES), :])
        lo1, hi1 = _unpack_pair(yb[pl.ds(CMB_TM * SUBLANES + c, CMB_TM, stride=SUBLANES), :])
        lo_cols = slice(c * LANES, (c + 1) * LANES)
        hi_cols = slice(HALF + c * LANES, HALF + (c + 1) * LANES)
        o_ref[:, lo_cols] = h_ref[:, lo_cols] + (w0 * lo0 + w1 * lo1)
        o_ref[:, hi_cols] = h_ref[:, hi_cols] + (w0 * hi0 + w1 * hi1)


def _combine(dest, h, route, ys):
    T = h.shape[0]
    tm = CMB_TM
    n = T // tm
    dest4 = _slot_tiles(dest, tm)
    cur = lambda i: i
    nxt = lambda i: jnp.minimum(i + 1, n - 1)
    return pl.pallas_call(
        _combine_kernel,
        grid=(n,),
        in_specs=[
            _slot_spec(tm, 0, cur), _slot_spec(tm, 1, cur),
            _slot_spec(tm, 0, nxt), _slot_spec(tm, 1, nxt),
            pl.BlockSpec((tm, D_MODEL), lambda i: (i, 0)),
            pl.BlockSpec((tm, LANES), lambda i: (i, 0)),
            pl.BlockSpec(memory_space=pl.ANY),
        ],
        out_specs=pl.BlockSpec((tm, D_MODEL), lambda i: (i, 0)),
        out_shape=jax.ShapeDtypeStruct((T, D_MODEL), F32),
        scratch_shapes=[pltpu.VMEM((2, TOP_K * tm * SUBLANES, LANES), U32),
                        pltpu.SemaphoreType.DMA((2,))],
        compiler_params=_cparams(1),
        name="combine",
    )(dest4, dest4, dest4, dest4, h, route, ys)


def _rope_tables(S):
    inv = ROPE_THETA ** (-jnp.arange(0, HEAD_DIM, 2, dtype=F32) / HEAD_DIM)
    ang = jnp.arange(S, dtype=F32)[:, None] * inv[None, :]
    cos, sin = jnp.cos(ang), jnp.sin(ang)
    return jnp.concatenate([cos, cos], axis=-1), jnp.concatenate([-sin, sin], axis=-1)


def kernel(x, norm_mix_w, w_in, w_pool, pool_scale, q_norm_w, k_norm_w, sink_logits, w_out,
           norm_ffn_w, w_group_router, b_group_router, w_expert_router, b_expert_router,
           w_gate, w_up, w_down):
    B, S, D = x.shape
    T = B * S
    depth = w_in.shape[0]
    cos, sin = _rope_tables(S)
    coef_np, invc_np, bias_np = _band_constants(S)
    coef = jnp.asarray(coef_np, BF16)
    invc = jnp.asarray(invc_np, F32)
    bias = jnp.asarray(bias_np, F32)
    pad_lanes = LANES - N_GROUPS - N_EXPERTS

    h = x.reshape(T, D)
    for l in range(depth):
        u, q, k, v = _in_proj(h, norm_mix_w[l].reshape(1, D), w_in[l].astype(BF16),
                              q_norm_w[l].reshape(1, HEAD_DIM), k_norm_w[l].reshape(1, HEAD_DIM),
                              cos, sin, S)
        ab = _mixers(sink_logits[l], u.reshape(B, S, -1), q.reshape(B, S, -1),
                     k.reshape(B, S, -1), v.reshape(B, S, -1), coef, invc, bias,
                     w_pool[l].astype(BF16), pool_scale[l].reshape(1, POOL_WIDTH))
        w_router = jnp.concatenate(
            [w_group_router[l], w_expert_router[l], jnp.zeros((D, pad_lanes), F32)], axis=1)
        b_router = jnp.concatenate(
            [b_group_router[l], b_expert_router[l], jnp.zeros((pad_lanes,), F32)]).reshape(1, LANES)
        hmix, xn_packed, route, counts = _out_proj(
            ab.reshape(T, MIX_WIDTH), h, w_out[l].astype(BF16), norm_ffn_w[l].reshape(1, D),
            w_router.astype(BF16), b_router)
        n_blocks = -(-(T * TOP_K + N_EXPERTS * (MOE_BLOCK - 1)) // MOE_BLOCK)
        dest, tab, etab = _plan(counts[0].astype(jnp.int32), route, n_blocks)
        xs = _dispatch(etab, dest, xn_packed, n_blocks)
        ys = _experts(tab, etab, xs, w_gate[l], w_up[l], w_down[l])
        h = _combine(dest, hmix, route, ys)
    return h.reshape(B, S, D)
```

```python
import functools

import numpy as np
import jax
import jax.numpy as jnp
from jax import lax
from jax.experimental import pallas as pl
from jax.experimental.pallas import tpu as pltpu

D_MODEL = 2048
POOL_WIDTH = 1024
POOL_WINDOWS = (2, 4, 8, 16)
N_POOL_GROUPS = 4
POOL_GROUP = 256
HEAD_DIM = 128
N_Q_HEADS = 8
N_KV_HEADS = 2
Q_PER_KV = 4
ATTN_WIDTH = 1024
KV_WIDTH = 256
IN_PROJ_WIDTH = 2560
MIX_WIDTH = 2048
WINDOW = 128
BLOCK = 128
BAND = 3 * BLOCK
ROPE_THETA = 10000.0
N_GROUPS = 8
EXPERTS_PER_GROUP = 8
N_EXPERTS = 64
TOP_K = 2
D_EXPERT = 512
MOE_BLOCK = 256
EPS = 1e-6

LANES = 128
SUBLANES = 8
HALF = D_MODEL // 2
PACK_CHUNKS = HALF // LANES
NEG_BIG = -1e30
VMEM_LIMIT = 56 * 1024 * 1024

BF16 = jnp.bfloat16
F32 = jnp.float32
U32 = jnp.uint32
HI_MASK = 0xFFFF0000


def _cparams(n_axes):
    return pltpu.CompilerParams(dimension_semantics=("arbitrary",) * n_axes,
                                vmem_limit_bytes=VMEM_LIMIT)


def _pack_pair(lo, hi):
    lo_bits = lax.bitcast_convert_type(lo.astype(BF16).astype(F32), U32) >> 16
    hi_bits = lax.bitcast_convert_type(hi.astype(BF16).astype(F32), U32) & jnp.uint32(HI_MASK)
    return lo_bits | hi_bits


def _unpack_pair(words):
    lo = lax.bitcast_convert_type(words << 16, F32)
    hi = lax.bitcast_convert_type(words & jnp.uint32(HI_MASK), F32)
    return lo, hi


IN_TM = 512
ROW_SPLIT = 2


def _in_proj_kernel(x_ref, nw_ref, w_ref, qnw_ref, knw_ref, cos_ref, sin_ref,
                    u_ref, q_ref, k_ref, v_ref):
    sub = IN_TM // ROW_SPLIT
    for part in range(ROW_SPLIT):
        rows = slice(part * sub, (part + 1) * sub)
        x = x_ref[rows, :]
        ms = jnp.mean(x * x, axis=-1, keepdims=True)
        xn = (x * lax.rsqrt(ms + EPS) * nw_ref[...]).astype(BF16)
        cos = cos_ref[rows, :]
        sin = sin_ref[rows, :]

        def head_norm_rope(t, w, scale):
            hms = jnp.mean(t * t, axis=-1, keepdims=True)
            t = t * lax.rsqrt(hms + EPS) * w
            t = t * cos + pltpu.roll(t, HEAD_DIM // 2, axis=1) * sin
            return t * scale

        u_ref[rows, :] = jnp.dot(xn, w_ref[:, :POOL_WIDTH],
                                 preferred_element_type=F32).astype(BF16)
        zq = jnp.dot(xn, w_ref[:, POOL_WIDTH:POOL_WIDTH + ATTN_WIDTH],
                     preferred_element_type=F32)
        qnw = qnw_ref[...]
        for h in range(N_Q_HEADS):
            sl = slice(h * HEAD_DIM, (h + 1) * HEAD_DIM)
            q_ref[rows, sl] = head_norm_rope(zq[:, sl], qnw, HEAD_DIM ** -0.5).astype(BF16)
        o_k = POOL_WIDTH + ATTN_WIDTH
        zk = jnp.dot(xn, w_ref[:, o_k:o_k + KV_WIDTH], preferred_element_type=F32)
        knw = knw_ref[...]
        for h in range(N_KV_HEADS):
            sl = slice(h * HEAD_DIM, (h + 1) * HEAD_DIM)
            k_ref[rows, sl] = head_norm_rope(zk[:, sl], knw, 1.0).astype(BF16)
        v_ref[rows, :] = jnp.dot(xn, w_ref[:, o_k + KV_WIDTH:],
                                 preferred_element_type=F32).astype(BF16)


def _in_proj(x2, nw, w_in, qnw, knw, cos, sin, seq):
    T = x2.shape[0]
    tm = IN_TM
    pos_blocks = seq // tm
    full = lambda shape: pl.BlockSpec(shape, lambda i: (0,) * len(shape))
    return pl.pallas_call(
        _in_proj_kernel,
        grid=(T // tm,),
        in_specs=[
            pl.BlockSpec((tm, D_MODEL), lambda i: (i, 0)),
            full((1, D_MODEL)),
            full((D_MODEL, IN_PROJ_WIDTH)),
            full((1, HEAD_DIM)),
            full((1, HEAD_DIM)),
            pl.BlockSpec((tm, HEAD_DIM), lambda i: (i % pos_blocks, 0)),
            pl.BlockSpec((tm, HEAD_DIM), lambda i: (i % pos_blocks, 0)),
        ],
        out_specs=[
            pl.BlockSpec((tm, POOL_WIDTH), lambda i: (i, 0)),
            pl.BlockSpec((tm, ATTN_WIDTH), lambda i: (i, 0)),
            pl.BlockSpec((tm, KV_WIDTH), lambda i: (i, 0)),
            pl.BlockSpec((tm, KV_WIDTH), lambda i: (i, 0)),
        ],
        out_shape=[
            jax.ShapeDtypeStruct((T, POOL_WIDTH), BF16),
            jax.ShapeDtypeStruct((T, ATTN_WIDTH), BF16),
            jax.ShapeDtypeStruct((T, KV_WIDTH), BF16),
            jax.ShapeDtypeStruct((T, KV_WIDTH), BF16),
        ],
        compiler_params=_cparams(1),
        name="in_proj",
    )(x2, nw, w_in, qnw, knw, cos, sin)


MIX_TQ = 512


def _band_constants(seq):
    nb = seq // BLOCK
    coef = np.zeros((3, N_POOL_GROUPS, BLOCK, BAND), np.float32)
    inv_count = np.zeros((3, N_POOL_GROUPS, BLOCK, 1), np.float32)
    bias = np.zeros((3, BLOCK, BAND), np.float32)
    for kind, n in enumerate((0, 1, nb - 1)):
        start = min(max((n - 1) * BLOCK, 0), seq - BAND)
        t = n * BLOCK + np.arange(BLOCK)[:, None]
        s = start + np.arange(BAND)[None, :]
        bias[kind] = np.where(np.abs(s - t) <= WINDOW, 0.0, NEG_BIG)
        for g, win in enumerate(POOL_WINDOWS):
            half = win // 2
            lo = np.clip(t - half, 0, seq)
            hi = np.clip(t + half, 0, seq)
            count = (hi - lo).astype(np.float32)
            inside = ((s >= lo) & (s < hi)).astype(np.float32)
            coef[kind, g] = inside - count * (s == t)
            inv_count[kind, g] = 1.0 / count
    return coef, inv_count, bias


def _mixers_kernel(sink_ref, u_ref, q_ref, k_ref, v_ref, coef_ref, invc_ref, bias_ref,
                   wpool_ref, pscale_ref, ab_ref, *, seq):
    nb = seq // BLOCK
    j = pl.program_id(1)
    n_sub = MIX_TQ // BLOCK
    subs = []
    for r in range(n_sub):
        n = j * n_sub + r
        start = pl.multiple_of(jnp.clip((n - 1) * BLOCK, 0, seq - BAND), BLOCK)
        kind = jnp.where(n == 0, 0, jnp.where(n == nb - 1, 2, 1))
        subs.append((start, kind, slice(r * BLOCK, (r + 1) * BLOCK)))

    scores = {}
    for r, (start, kind, rows) in enumerate(subs):
        for hk in range(N_KV_HEADS):
            kb = k_ref[0, pl.ds(start, BAND), hk * HEAD_DIM:(hk + 1) * HEAD_DIM]
            qs = jnp.concatenate(
                [q_ref[0, rows, h * HEAD_DIM:(h + 1) * HEAD_DIM]
                 for h in range(hk * Q_PER_KV, (hk + 1) * Q_PER_KV)], axis=0)
            scores[r, hk] = lax.dot_general(qs, kb, (((1,), (1,)), ((), ())),
                                            preferred_element_type=F32)
    diffs = {}
    for r, (start, kind, rows) in enumerate(subs):
        for g in range(N_POOL_GROUPS):
            ub = u_ref[0, pl.ds(start, BAND), g * POOL_GROUP:(g + 1) * POOL_GROUP]
            d = jnp.dot(coef_ref[kind, g], ub, preferred_element_type=F32) * invc_ref[kind, g]
            diffs[r, g] = d.astype(BF16)
    probs = {}
    for r, (start, kind, rows) in enumerate(subs):
        bias = bias_ref[kind]
        for hk in range(N_KV_HEADS):
            for gi in range(Q_PER_KV):
                h = hk * Q_PER_KV + gi
                sh = scores[r, hk][gi * BLOCK:(gi + 1) * BLOCK] + bias
                sink = sink_ref[h]
                m = jnp.maximum(jnp.max(sh, axis=-1, keepdims=True), sink)
                p = jnp.exp(sh - m)
                denom = jnp.sum(p, axis=-1, keepdims=True) + jnp.exp(sink - m)
                probs[r, h] = (p.astype(BF16), denom)
    for r, (start, kind, rows) in enumerate(subs):
        for g in range(N_POOL_GROUPS):
            cols = slice(g * POOL_GROUP, (g + 1) * POOL_GROUP)
            y = jnp.dot(diffs[r, g], wpool_ref[g], preferred_element_type=F32)
            ab_ref[0, rows, cols] = (y * pscale_ref[:, cols]).astype(BF16)
    for r, (start, kind, rows) in enumerate(subs):
        for hk in range(N_KV_HEADS):
            vb = v_ref[0, pl.ds(start, BAND), hk * HEAD_DIM:(hk + 1) * HEAD_DIM]
            for gi in range(Q_PER_KV):
                h = hk * Q_PER_KV + gi
                p, denom = probs[r, h]
                o = jnp.dot(p, vb, preferred_element_type=F32) / denom
                ocols = slice(POOL_WIDTH + h * HEAD_DIM, POOL_WIDTH + (h + 1) * HEAD_DIM)
                ab_ref[0, rows, ocols] = o.astype(BF16)


def _mixers(sink, u, q, k, v, coef, invc, bias, wpool, pscale):
    B, S, _ = u.shape
    tq = MIX_TQ
    full = lambda shape: pl.BlockSpec(shape, lambda b, j: (0,) * len(shape))
    return pl.pallas_call(
        functools.partial(_mixers_kernel, seq=S),
        grid=(B, S // tq),
        in_specs=[
            pl.BlockSpec(memory_space=pltpu.SMEM),
            pl.BlockSpec((1, S, POOL_WIDTH), lambda b, j: (b, 0, 0)),
            pl.BlockSpec((1, tq, ATTN_WIDTH), lambda b, j: (b, j, 0)),
            pl.BlockSpec((1, S, KV_WIDTH), lambda b, j: (b, 0, 0)),
            pl.BlockSpec((1, S, KV_WIDTH), lambda b, j: (b, 0, 0)),
            full((3, N_POOL_GROUPS, BLOCK, BAND)),
            full((3, N_POOL_GROUPS, BLOCK, 1)),
            full((3, BLOCK, BAND)),
            full((N_POOL_GROUPS, POOL_GROUP, POOL_GROUP)),
            full((1, POOL_WIDTH)),
        ],
        out_specs=pl.BlockSpec((1, tq, MIX_WIDTH), lambda b, j: (b, j, 0)),
        out_shape=jax.ShapeDtypeStruct((B, S, MIX_WIDTH), BF16),
        compiler_params=_cparams(2),
        name="mixers",
    )(sink, u, q, k, v, coef, invc, bias, wpool, pscale)


OUT_TM = 512
ROUTE_E0, ROUTE_E1, ROUTE_W0, ROUTE_W1 = 0, 1, 2, 3


def _out_proj_kernel(ab_ref, x_ref, w_ref, nw_ref, wr_ref, br_ref,
                     h_ref, xn_ref, route_ref, counts_ref):
    @pl.when(pl.program_id(0) == 0)
    def _():
        counts_ref[...] = jnp.zeros_like(counts_ref)

    sub = OUT_TM // ROW_SPLIT
    lane = lax.broadcasted_iota(jnp.int32, (sub, LANES), 1)

    def first_argmax(vals):
        m = jnp.max(vals, axis=-1, keepdims=True)
        idx = jnp.min(jnp.where(vals == m, lane, LANES), axis=-1, keepdims=True)
        return m, idx

    new_counts = jnp.zeros((1, LANES), F32)
    for part in range(ROW_SPLIT):
        rows = slice(part * sub, (part + 1) * sub)
        h = x_ref[rows, :] + jnp.dot(ab_ref[rows, :], w_ref[...], preferred_element_type=F32)
        h_ref[rows, :] = h
        ms = jnp.mean(h * h, axis=-1, keepdims=True)
        xn = h * lax.rsqrt(ms + EPS) * nw_ref[...]
        packed = _pack_pair(xn[:, :HALF], xn[:, HALF:])
        for c in range(PACK_CHUNKS):
            xn_ref[pl.ds(part * sub * SUBLANES + c, sub, stride=SUBLANES), :] = (
                packed[:, c * LANES:(c + 1) * LANES])
        logits = jnp.dot(xn.astype(BF16), wr_ref[...], preferred_element_type=F32) + br_ref[...]

        gl = jnp.where(lane < N_GROUPS, logits, NEG_BIG)
        gmax, gidx = first_argmax(gl)
        gsum = jnp.sum(jnp.where(lane < N_GROUPS, jnp.exp(gl - gmax), 0.0),
                       axis=-1, keepdims=True)
        g_w = 1.0 / gsum
        e_lo = N_GROUPS + gidx * EXPERTS_PER_GROUP
        el = jnp.where((lane >= e_lo) & (lane < e_lo + EXPERTS_PER_GROUP), logits, NEG_BIG)
        m1, i1 = first_argmax(el)
        m2, i2 = first_argmax(jnp.where(lane == i1, NEG_BIG, el))
        t = jnp.exp(m2 - m1)
        p1 = 1.0 / (1.0 + t)
        p2 = t * p1
        e1 = i1 - N_GROUPS
        e2 = i2 - N_GROUPS
        route_ref[rows, :] = jnp.where(
            lane == ROUTE_E0, e1.astype(F32),
            jnp.where(lane == ROUTE_E1, e2.astype(F32),
                      jnp.where(lane == ROUTE_W0, g_w * p1,
                                jnp.where(lane == ROUTE_W1, g_w * p2, 0.0))))
        chosen = ((lane == e1) | (lane == e2)).astype(F32)
        new_counts = new_counts + jnp.sum(chosen, axis=0, keepdims=True)
    counts_ref[...] += new_counts


def _out_proj(ab, x2, w_out, nw, w_router, b_router):
    T = x2.shape[0]
    tm = OUT_TM
    full = lambda shape: pl.BlockSpec(shape, lambda i: (0,) * len(shape))
    row = lambda width: pl.BlockSpec((tm, width), lambda i: (i, 0))
    return pl.pallas_call(
        _out_proj_kernel,
        grid=(T // tm,),
        in_specs=[row(MIX_WIDTH), row(D_MODEL), full((MIX_WIDTH, D_MODEL)), full((1, D_MODEL)),
                  full((D_MODEL, LANES)), full((1, LANES))],
        out_specs=[row(D_MODEL),
                   pl.BlockSpec((tm * SUBLANES, LANES), lambda i: (i, 0)),
                   row(LANES),
                   full((1, LANES))],
        out_shape=[jax.ShapeDtypeStruct((T, D_MODEL), F32),
                   jax.ShapeDtypeStruct((T * SUBLANES, LANES), U32),
                   jax.ShapeDtypeStruct((T, LANES), F32),
                   jax.ShapeDtypeStruct((1, LANES), F32)],
        compiler_params=_cparams(1),
        name="out_proj",
    )(ab, x2, w_out, nw, w_router, b_router)


PLAN_TM = 512


TAB_EXPERT, TAB_FIRST, TAB_BUFFER, TAB_NEXT = 0, 1, 2, 3
TAB_ROWS = 4
ETAB_PAD_LO, ETAB_PAD_HI, ETAB_MISC = 0, 1, 2
ETAB_ROWS = 3
MISC_N_USED, MISC_RUN_EXPERT = 0, 1
FETCH_AHEAD = 2
TABLE_ROWS = SUBLANES


def _slot_tables(counts_ref, tab_ref, etab_ref, run_expert, run_of_block, pstart_s, n_blocks):
    log_block = MOE_BLOCK.bit_length() - 1

    def clear(r, carry):
        run_expert[r] = -1
        return carry
    lax.fori_loop(0, run_expert.shape[0], clear, 0)

    def per_expert(e, carry):
        slot, blk, run, buf = carry
        c = counts_ref[e]
        n_blk = lax.shift_right_logical(c + (MOE_BLOCK - 1), log_block)
        pstart_s[e] = slot
        etab_ref[ETAB_PAD_LO, e] = slot + c
        etab_ref[ETAB_PAD_HI, e] = slot + n_blk * MOE_BLOCK

        def per_block(b, inner):
            tab_ref[TAB_EXPERT, b] = e
            tab_ref[TAB_FIRST, b] = (b == blk).astype(jnp.int32)
            tab_ref[TAB_BUFFER, b] = buf
            run_of_block[b] = run
            return inner
        lax.fori_loop(blk, blk + n_blk, per_block, 0)

        @pl.when(n_blk > 0)
        def _():
            run_expert[run] = e
        has_run = (n_blk > 0).astype(jnp.int32)
        next_buf = jnp.where(buf + has_run == FETCH_AHEAD, 0, buf + has_run)
        return slot + n_blk * MOE_BLOCK, blk + n_blk, run + has_run, next_buf

    zero = jnp.int32(0)
    _, n_used, _, _ = lax.fori_loop(0, N_EXPERTS, per_expert, (zero, zero, zero, zero))

    def per_used_block(b, carry):
        tab_ref[TAB_NEXT, b] = run_expert[run_of_block[b] + FETCH_AHEAD]
        return carry
    lax.fori_loop(0, n_used, per_used_block, 0)

    def per_unused_block(b, carry):
        for row in (TAB_EXPERT, TAB_FIRST, TAB_BUFFER):
            tab_ref[row, b] = 0
        tab_ref[TAB_NEXT, b] = -1
        return carry
    lax.fori_loop(n_used, n_blocks, per_unused_block, 0)

    def clear_misc(e, carry):
        etab_ref[ETAB_MISC, e] = 0
        return carry
    lax.fori_loop(0, N_EXPERTS, clear_misc, 0)
    etab_ref[ETAB_MISC, MISC_N_USED] = n_used
    for r in range(FETCH_AHEAD):
        etab_ref[ETAB_MISC, MISC_RUN_EXPERT + r] = run_expert[r]


def _plan_kernel(counts_ref, route_ref, tri_ref, dest_ref, tab_ref, etab_ref,
                 carry_ref, pstart_ref, run_expert, run_of_block, pstart_s, *, n_blocks):
    @pl.when(pl.program_id(0) == 0)
    def _():
        carry_ref[...] = jnp.zeros_like(carry_ref)
        _slot_tables(counts_ref, tab_ref, etab_ref, run_expert, run_of_block, pstart_s, n_blocks)
        lane_row = lax.broadcasted_iota(jnp.int32, (1, LANES), 1)

        def place(e, row):
            return jnp.where(lane_row == e, pstart_s[e].astype(F32), row)
        pstart_ref[...] = lax.fori_loop(0, N_EXPERTS, place, jnp.zeros((1, LANES), F32))

    route = route_ref[...]
    lane = lax.broadcasted_iota(jnp.int32, route.shape, 1)
    e0 = route[:, ROUTE_E0:ROUTE_E0 + 1].astype(jnp.int32)
    e1 = route[:, ROUTE_E1:ROUTE_E1 + 1].astype(jnp.int32)
    oh0 = lane == e0
    oh1 = lane == e1
    both = (oh0 | oh1).astype(F32)
    earlier = jnp.dot(tri_ref[...], both.astype(BF16), preferred_element_type=F32)
    base = pstart_ref[...] + carry_ref[...] + earlier
    d0 = jnp.sum(jnp.where(oh0, base, 0.0), axis=-1, keepdims=True)
    d1 = jnp.sum(jnp.where(oh1, base, 0.0), axis=-1, keepdims=True)
    by_token = jnp.where(lane == 0, d0, jnp.where(lane == 1, d1, 0.0))
    dest_ref[...] = by_token.T[:TABLE_ROWS].astype(jnp.int32)
    carry_ref[...] += jnp.sum(both, axis=0, keepdims=True)


def _plan(counts, route, n_blocks):
    T = route.shape[0]
    tm = PLAN_TM
    tri = jnp.asarray(np.tril(np.ones((tm, tm), np.float32), -1), BF16)
    smem_out = pl.BlockSpec(memory_space=pltpu.SMEM)
    grid_spec = pltpu.PrefetchScalarGridSpec(
        num_scalar_prefetch=1,
        grid=(T // tm,),
        in_specs=[pl.BlockSpec((tm, LANES), lambda i, c: (i, 0)),
                  pl.BlockSpec((tm, tm), lambda i, c: (0, 0))],
        out_specs=[pl.BlockSpec((TABLE_ROWS, tm), lambda i, c: (0, i)), smem_out, smem_out],
        scratch_shapes=[pltpu.VMEM((1, LANES), F32), pltpu.VMEM((1, LANES), F32),
                        pltpu.SMEM((N_EXPERTS + FETCH_AHEAD,), jnp.int32),
                        pltpu.SMEM((n_blocks,), jnp.int32),
                        pltpu.SMEM((N_EXPERTS,), jnp.int32)],
    )
    return pl.pallas_call(
        functools.partial(_plan_kernel, n_blocks=n_blocks),
        grid_spec=grid_spec,
        out_shape=[jax.ShapeDtypeStruct((TABLE_ROWS, T), jnp.int32),
                   jax.ShapeDtypeStruct((TAB_ROWS, n_blocks), jnp.int32),
                   jax.ShapeDtypeStruct((ETAB_ROWS, N_EXPERTS), jnp.int32)],
        compiler_params=_cparams(1),
        name="plan",
    )(counts, route, tri)


DISP_TM = 512
DMA_GROUP = 8
ROW_WORDS = SUBLANES


def _packed_row(ref, row):
    return ref.at[pl.ds(pl.multiple_of(row * ROW_WORDS, ROW_WORDS), ROW_WORDS)]


def _grouped(n, fn):
    def trip(t, carry):
        for k in range(DMA_GROUP):
            fn(t * DMA_GROUP + k, k)
        return carry
    lax.fori_loop(0, n // DMA_GROUP, trip, 0)


def _dispatch_kernel(etab_ref, dest0_ref, dest1_ref, xn_ref, xs_hbm,
                     zeros, sem, tail_sem, *, experts_per_step, tails_per_step, n_blocks):
    i = pl.program_id(0)
    n_rows = TOP_K * DISP_TM

    @pl.when(i == 0)
    def _():
        zeros[...] = jnp.zeros_like(zeros)

    def scatter_group(t, carry):
        base = pl.multiple_of(t * DMA_GROUP, DMA_GROUP)
        window = xn_ref.at[pl.ds(base * ROW_WORDS, DMA_GROUP * ROW_WORDS)]
        for k in range(DMA_GROUP):
            src = window.at[pl.ds(k * ROW_WORDS, ROW_WORDS)]
            for choice, dest_ref in enumerate((dest0_ref, dest1_ref)):
                dst = _packed_row(xs_hbm, dest_ref[0, 0, 0, base + k])
                pltpu.make_async_copy(src, dst, sem).start(priority=choice)
        return carry

    lax.fori_loop(0, DISP_TM // DMA_GROUP, scatter_group, 0)

    def pad_fill(start):
        for j in range(experts_per_step):
            e = jnp.minimum(i * experts_per_step + j, N_EXPERTS - 1)
            live = i * experts_per_step + j < N_EXPERTS
            pos = etab_ref[ETAB_PAD_LO, e]
            n_pad = jnp.where(live, etab_ref[ETAB_PAD_HI, e] - pos, 0)
            for bit in reversed(range(MOE_BLOCK.bit_length() - 1)):
                size = 1 << bit
                take = jnp.bitwise_and(lax.shift_right_logical(n_pad, bit), 1)
                dst = xs_hbm.at[pl.ds(pl.multiple_of(pos * ROW_WORDS, ROW_WORDS), size * ROW_WORDS)]
                cp = pltpu.make_async_copy(zeros.at[pl.ds(0, size * ROW_WORDS)], dst, sem)

                @pl.when(take == 1)
                def _():
                    cp.start() if start else cp.wait()
                pos = pos + take * size

    def tail_copies(start):
        for j in range(tails_per_step):
            blk = etab_ref[ETAB_MISC, MISC_N_USED] + i * tails_per_step + j
            rows = pl.ds(pl.multiple_of(jnp.minimum(blk, n_blocks - 1) * (MOE_BLOCK * ROW_WORDS),
                                        MOE_BLOCK * ROW_WORDS), MOE_BLOCK * ROW_WORDS)
            cp = pltpu.make_async_copy(zeros, xs_hbm.at[rows], tail_sem)

            @pl.when(blk < n_blocks)
            def _():
                cp.start() if start else cp.wait()

    pad_fill(True)
    tail_copies(True)

    tile_rows = DISP_TM * ROW_WORDS
    for _ in range(TOP_K):
        pltpu.make_async_copy(xn_ref, xs_hbm.at[pl.ds(0, tile_rows)], sem).wait()
    pad_fill(False)
    tail_copies(False)


def _slot_tiles(dest, tile):
    return dest.reshape(dest.shape[0], dest.shape[1] // tile, 1, tile)


def _slot_spec(tile, choice, index):
    return pl.BlockSpec((1, 1, 1, tile), lambda i, *_: (choice, index(i), 0, 0),
                        memory_space=pltpu.SMEM)


def _dispatch(etab, dest, xn_packed, n_blocks):
    dest4 = _slot_tiles(dest, DISP_TM)
    n = dest4.shape[1]
    min_used = (n * DISP_TM * TOP_K) // MOE_BLOCK
    tails_per_step = -(-(n_blocks - min_used) // n)
    grid_spec = pltpu.PrefetchScalarGridSpec(
        num_scalar_prefetch=1,
        grid=(n,),
        in_specs=[_slot_spec(DISP_TM, 0, lambda i: i), _slot_spec(DISP_TM, 1, lambda i: i),
                  pl.BlockSpec((DISP_TM * ROW_WORDS, LANES), lambda i, *_: (i, 0))],
        out_specs=pl.BlockSpec(memory_space=pl.ANY),
        scratch_shapes=[pltpu.VMEM((MOE_BLOCK * ROW_WORDS, LANES), U32),
                        pltpu.SemaphoreType.DMA(()), pltpu.SemaphoreType.DMA(())],
    )
    return pl.pallas_call(
        functools.partial(_dispatch_kernel, experts_per_step=-(-N_EXPERTS // n),
                          tails_per_step=tails_per_step, n_blocks=n_blocks),
        grid_spec=grid_spec,
        out_shape=jax.ShapeDtypeStruct((n_blocks * MOE_BLOCK * ROW_WORDS, LANES), U32),
        compiler_params=_cparams(1),
        name="dispatch",
    )(etab, dest4, dest4, xn_packed)


CAST_ROWS = 256
FETCH_SPLIT = 4


def _experts_kernel(tab_ref, etab_ref,
                    xs_ref, wg_hbm, wu_hbm, wd_hbm, ys_ref,
                    fg, fu, fd, wg, wu, wd, sem):
    b = pl.program_id(0)
    n_used = etab_ref[ETAB_MISC, MISC_N_USED]

    def fetch(e, p):
        copies = []
        for k, (src, dst) in enumerate(((wg_hbm, fg), (wu_hbm, fu), (wd_hbm, fd))):
            rows = src.shape[1] // FETCH_SPLIT
            for c in range(FETCH_SPLIT):
                sl = pl.ds(c * rows, rows)
                copies.append(pltpu.make_async_copy(src.at[e, sl], dst.at[p, sl], sem.at[p, k]))
        return copies

    def start_fetch(e, p):
        for k, cp in enumerate(fetch(e, p)):
            cp.start(priority=k % 2)

    @pl.when(b == 0)
    def _():
        for r in range(FETCH_AHEAD):
            e = etab_ref[ETAB_MISC, MISC_RUN_EXPERT + r]

            @pl.when(e >= 0)
            def _():
                start_fetch(e, r)

    @pl.when((b < n_used) & (tab_ref[TAB_FIRST, b] == 1))
    def _():
        p = tab_ref[TAB_BUFFER, b]
        for cp in fetch(tab_ref[TAB_EXPERT, b], p):
            cp.wait()

        def cast(src, dst, n_rows):
            def body(i, carry):
                rows = pl.ds(pl.multiple_of(i * CAST_ROWS, CAST_ROWS), CAST_ROWS)
                dst[rows, :] = src[p, rows, :].astype(BF16)
                return carry
            lax.fori_loop(0, n_rows // CAST_ROWS, body, 0)

        cast(fg, wg, D_MODEL)
        cast(fu, wu, D_MODEL)
        cast(fd, wd, D_EXPERT)

        @pl.when(tab_ref[TAB_NEXT, b] >= 0)
        def _():
            start_fetch(tab_ref[TAB_NEXT, b], p)

    @pl.when(b < n_used)
    def _():
        lo_parts, hi_parts = [], []
        for c in range(PACK_CHUNKS):
            lo, hi = _unpack_pair(xs_ref[pl.ds(c, MOE_BLOCK, stride=SUBLANES), :])
            lo_parts.append(lo.astype(BF16))
            hi_parts.append(hi.astype(BF16))
        xb = jnp.concatenate(lo_parts + hi_parts, axis=1)
        g = jnp.dot(xb, wg[...], preferred_element_type=F32)
        u = jnp.dot(xb, wu[...], preferred_element_type=F32)
        hmid = (g * jax.nn.sigmoid(g) * u).astype(BF16)
        y = jnp.dot(hmid, wd[...], preferred_element_type=F32)
        packed = _pack_pair(y[:, :HALF], y[:, HALF:])
        for c in range(PACK_CHUNKS):
            ys_ref[pl.ds(c, MOE_BLOCK, stride=SUBLANES), :] = packed[:, c * LANES:(c + 1) * LANES]

    @pl.when(b >= n_used)
    def _():
        ys_ref[...] = jnp.zeros_like(ys_ref)


def _experts(tab, etab, xs, w_gate, w_up, w_down):
    n_blocks = tab.shape[1]
    blk = lambda: pl.BlockSpec((MOE_BLOCK * SUBLANES, LANES), lambda b, *_: (b, 0))
    grid_spec = pltpu.PrefetchScalarGridSpec(
        num_scalar_prefetch=2,
        grid=(n_blocks,),
        in_specs=[blk(),
                  pl.BlockSpec(memory_space=pl.ANY),
                  pl.BlockSpec(memory_space=pl.ANY),
                  pl.BlockSpec(memory_space=pl.ANY)],
        out_specs=blk(),
        scratch_shapes=[pltpu.VMEM((FETCH_AHEAD, D_MODEL, D_EXPERT), F32),
                        pltpu.VMEM((FETCH_AHEAD, D_MODEL, D_EXPERT), F32),
                        pltpu.VMEM((FETCH_AHEAD, D_EXPERT, D_MODEL), F32),
                        pltpu.VMEM((D_MODEL, D_EXPERT), BF16),
                        pltpu.VMEM((D_MODEL, D_EXPERT), BF16),
                        pltpu.VMEM((D_EXPERT, D_MODEL), BF16),
                        pltpu.SemaphoreType.DMA((FETCH_AHEAD, 3))],
    )
    return pl.pallas_call(
        _experts_kernel,
        grid_spec=grid_spec,
        out_shape=jax.ShapeDtypeStruct((n_blocks * MOE_BLOCK * SUBLANES, LANES), U32),
        compiler_params=_cparams(1),
        name="experts",
    )(tab, etab, xs, w_gate, w_up, w_down)


CMB_TM = 512


def _combine_kernel(d0_cur_ref, d1_cur_ref, d0_next_ref, d1_next_ref, h_ref, route_ref, ys_hbm,
                    o_ref, ybuf, sem):
    i = pl.program_id(0)
    n = pl.num_programs(0)
    slot = i % 2
    n_rows = TOP_K * CMB_TM

    def gather(idx_refs, s):
        def gather_group(t, carry):
            base = pl.multiple_of(t * DMA_GROUP, DMA_GROUP)
            for choice, idx_ref in enumerate(idx_refs):
                first = (choice * CMB_TM + base) * ROW_WORDS
                window = ybuf.at[s, pl.ds(first, DMA_GROUP * ROW_WORDS)]
                for k in range(DMA_GROUP):
                    pltpu.make_async_copy(_packed_row(ys_hbm, idx_ref[0, 0, 0, base + k]),
                                          window.at[pl.ds(k * ROW_WORDS, ROW_WORDS)],
                                          sem.at[s]).start(priority=k % 2)
            return carry
        lax.fori_loop(0, CMB_TM // DMA_GROUP, gather_group, 0)

    @pl.when(i == 0)
    def _():
        gather((d0_cur_ref, d1_cur_ref), 0)

    @pl.when(i + 1 < n)
    def _():
        gather((d0_next_ref, d1_next_ref), 1 - slot)

    pltpu.make_async_copy(ys_hbm.at[pl.ds(0, n_rows * ROW_WORDS)], ybuf.at[slot],
                          sem.at[slot]).wait()
    route = route_ref[...]
    w0 = route[:, ROUTE_W0:ROUTE_W0 + 1]
    w1 = route[:, ROUTE_W1:ROUTE_W1 + 1]
    yb = ybuf.at[slot]
    for c in range(PACK_CHUNKS):
        lo0, hi0 = _unpack_pair(yb[pl.ds(c, CMB_TM, stride=SUBLANES), :])
        lo1, hi1 = _unpack_pair(yb[pl.ds(CMB_TM * SUBLANES + c, CMB_TM, stride=SUBLANES), :])
        lo_cols = slice(c * LANES, (c + 1) * LANES)
        hi_cols = slice(HALF + c * LANES, HALF + (c + 1) * LANES)
        o_ref[:, lo_cols] = h_ref[:, lo_cols] + (w0 * lo0 + w1 * lo1)
        o_ref[:, hi_cols] = h_ref[:, hi_cols] + (w0 * hi0 + w1 * hi1)


def _combine(dest, h, route, ys):
    T = h.shape[0]
    tm = CMB_TM
    n = T // tm
    dest4 = _slot_tiles(dest, tm)
    cur = lambda i: i
    nxt = lambda i: jnp.minimum(i + 1, n - 1)
    return pl.pallas_call(
        _combine_kernel,
        grid=(n,),
        in_specs=[
            _slot_spec(tm, 0, cur), _slot_spec(tm, 1, cur),
            _slot_spec(tm, 0, nxt), _slot_spec(tm, 1, nxt),
            pl.BlockSpec((tm, D_MODEL), lambda i: (i, 0)),
            pl.BlockSpec((tm, LANES), lambda i: (i, 0)),
            pl.BlockSpec(memory_space=pl.ANY),
        ],
        out_specs=pl.BlockSpec((tm, D_MODEL), lambda i: (i, 0)),
        out_shape=jax.ShapeDtypeStruct((T, D_MODEL), F32),
        scratch_shapes=[pltpu.VMEM((2, TOP_K * tm * SUBLANES, LANES), U32),
                        pltpu.SemaphoreType.DMA((2,))],
        compiler_params=_cparams(1),
        name="combine",
    )(dest4, dest4, dest4, dest4, h, route, ys)


def _rope_tables(S):
    inv = ROPE_THETA ** (-jnp.arange(0, HEAD_DIM, 2, dtype=F32) / HEAD_DIM)
    ang = jnp.arange(S, dtype=F32)[:, None] * inv[None, :]
    cos, sin = jnp.cos(ang), jnp.sin(ang)
    return jnp.concatenate([cos, cos], axis=-1), jnp.concatenate([-sin, sin], axis=-1)


def kernel(x, norm_mix_w, w_in, w_pool, pool_scale, q_norm_w, k_norm_w, sink_logits, w_out,
           norm_ffn_w, w_group_router, b_group_router, w_expert_router, b_expert_router,
           w_gate, w_up, w_down):
    B, S, D = x.shape
    T = B * S
    depth = w_in.shape[0]
    cos, sin = _rope_tables(S)
    coef_np, invc_np, bias_np = _band_constants(S)
    coef = jnp.asarray(coef_np, BF16)
    invc = jnp.asarray(invc_np, F32)
    bias = jnp.asarray(bias_np, F32)
    pad_lanes = LANES - N_GROUPS - N_EXPERTS

    h = x.reshape(T, D)
    for l in range(depth):
        u, q, k, v = _in_proj(h, norm_mix_w[l].reshape(1, D), w_in[l].astype(BF16),
                              q_norm_w[l].reshape(1, HEAD_DIM), k_norm_w[l].reshape(1, HEAD_DIM),
                              cos, sin, S)
        ab = _mixers(sink_logits[l], u.reshape(B, S, -1), q.reshape(B, S, -1),
                     k.reshape(B, S, -1), v.reshape(B, S, -1), coef, invc, bias,
                     w_pool[l].astype(BF16), pool_scale[l].reshape(1, POOL_WIDTH))
        w_router = jnp.concatenate(
            [w_group_router[l], w_expert_router[l], jnp.zeros((D, pad_lanes), F32)], axis=1)
        b_router = jnp.concatenate(
            [b_group_router[l], b_expert_router[l], jnp.zeros((pad_lanes,), F32)]).reshape(1, LANES)
        hmix, xn_packed, route, counts = _out_proj(
            ab.reshape(T, MIX_WIDTH), h, w_out[l].astype(BF16), norm_ffn_w[l].reshape(1, D),
            w_router.astype(BF16), b_router)
        n_blocks = -(-(T * TOP_K + N_EXPERTS * (MOE_BLOCK - 1)) // MOE_BLOCK)
        dest, tab, etab = _plan(counts[0].astype(jnp.int32), route, n_blocks)
        xs = _dispatch(etab, dest, xn_packed, n_blocks)
        ys = _experts(tab, etab, xs, w_gate[l], w_up[l], w_down[l])
        h = _combine(dest, hmix, route, ys)
    return h.reshape(B, S, D)
```

```python
import functools

import numpy as np
import jax
import jax.numpy as jnp
from jax import lax
from jax.experimental import pallas as pl
from jax.experimental.pallas import tpu as pltpu

D_MODEL = 2048
POOL_WIDTH = 1024
POOL_WINDOWS = (2, 4, 8, 16)
N_POOL_GROUPS = 4
POOL_GROUP = 256
HEAD_DIM = 128
N_Q_HEADS = 8
N_KV_HEADS = 2
Q_PER_KV = 4
ATTN_WIDTH = 1024
KV_WIDTH = 256
IN_PROJ_WIDTH = 2560
MIX_WIDTH = 2048
WINDOW = 128
BLOCK = 128
BAND = 3 * BLOCK
ROPE_THETA = 10000.0
N_GROUPS = 8
EXPERTS_PER_GROUP = 8
N_EXPERTS = 64
TOP_K = 2
D_EXPERT = 512
MOE_BLOCK = 256
EPS = 1e-6

LANES = 128
SUBLANES = 8
HALF = D_MODEL // 2
PACK_CHUNKS = HALF // LANES
NEG_BIG = -1e30
VMEM_LIMIT = 56 * 1024 * 1024

BF16 = jnp.bfloat16
F32 = jnp.float32
U32 = jnp.uint32
HI_MASK = 0xFFFF0000


def _cparams(n_axes):
    return pltpu.CompilerParams(dimension_semantics=("arbitrary",) * n_axes,
                                vmem_limit_bytes=VMEM_LIMIT)


def _pack_pair(lo, hi):
    lo_bits = lax.bitcast_convert_type(lo.astype(BF16).astype(F32), U32) >> 16
    hi_bits = lax.bitcast_convert_type(hi.astype(BF16).astype(F32), U32) & jnp.uint32(HI_MASK)
    return lo_bits | hi_bits


def _unpack_pair(words):
    lo = lax.bitcast_convert_type(words << 16, F32)
    hi = lax.bitcast_convert_type(words & jnp.uint32(HI_MASK), F32)
    return lo, hi


IN_TM = 512
ROW_SPLIT = 2


def _in_proj_kernel(x_ref, nw_ref, w_ref, qnw_ref, knw_ref, cos_ref, sin_ref,
                    u_ref, q_ref, k_ref, v_ref):
    sub = IN_TM // ROW_SPLIT
    for part in range(ROW_SPLIT):
        rows = slice(part * sub, (part + 1) * sub)
        x = x_ref[rows, :]
        ms = jnp.mean(x * x, axis=-1, keepdims=True)
        xn = (x * lax.rsqrt(ms + EPS) * nw_ref[...]).astype(BF16)
        cos = cos_ref[rows, :]
        sin = sin_ref[rows, :]

        def head_norm_rope(t, w, scale):
            hms = jnp.mean(t * t, axis=-1, keepdims=True)
            t = t * lax.rsqrt(hms + EPS) * w
            t = t * cos + pltpu.roll(t, HEAD_DIM // 2, axis=1) * sin
            return t * scale

        u_ref[rows, :] = jnp.dot(xn, w_ref[:, :POOL_WIDTH],
                                 preferred_element_type=F32).astype(BF16)
        zq = jnp.dot(xn, w_ref[:, POOL_WIDTH:POOL_WIDTH + ATTN_WIDTH],
                     preferred_element_type=F32)
        qnw = qnw_ref[...]
        for h in range(N_Q_HEADS):
            sl = slice(h * HEAD_DIM, (h + 1) * HEAD_DIM)
            q_ref[rows, sl] = head_norm_rope(zq[:, sl], qnw, HEAD_DIM ** -0.5).astype(BF16)
        o_k = POOL_WIDTH + ATTN_WIDTH
        zk = jnp.dot(xn, w_ref[:, o_k:o_k + KV_WIDTH], preferred_element_type=F32)
        knw = knw_ref[...]
        for h in range(N_KV_HEADS):
            sl = slice(h * HEAD_DIM, (h + 1) * HEAD_DIM)
            k_ref[rows, sl] = head_norm_rope(zk[:, sl], knw, 1.0).astype(BF16)
        v_ref[rows, :] = jnp.dot(xn, w_ref[:, o_k + KV_WIDTH:],
                                 preferred_element_type=F32).astype(BF16)


def _in_proj(x2, nw, w_in, qnw, knw, cos, sin, seq):
    T = x2.shape[0]
    tm = IN_TM
    pos_blocks = seq // tm
    full = lambda shape: pl.BlockSpec(shape, lambda i: (0,) * len(shape))
    return pl.pallas_call(
        _in_proj_kernel,
        grid=(T // tm,),
        in_specs=[
            pl.BlockSpec((tm, D_MODEL), lambda i: (i, 0)),
            full((1, D_MODEL)),
            full((D_MODEL, IN_PROJ_WIDTH)),
            full((1, HEAD_DIM)),
            full((1, HEAD_DIM)),
            pl.BlockSpec((tm, HEAD_DIM), lambda i: (i % pos_blocks, 0)),
            pl.BlockSpec((tm, HEAD_DIM), lambda i: (i % pos_blocks, 0)),
        ],
        out_specs=[
            pl.BlockSpec((tm, POOL_WIDTH), lambda i: (i, 0)),
            pl.BlockSpec((tm, ATTN_WIDTH), lambda i: (i, 0)),
            pl.BlockSpec((tm, KV_WIDTH), lambda i: (i, 0)),
            pl.BlockSpec((tm, KV_WIDTH), lambda i: (i, 0)),
        ],
        out_shape=[
            jax.ShapeDtypeStruct((T, POOL_WIDTH), BF16),
            jax.ShapeDtypeStruct((T, ATTN_WIDTH), BF16),
            jax.ShapeDtypeStruct((T, KV_WIDTH), BF16),
            jax.ShapeDtypeStruct((T, KV_WIDTH), BF16),
        ],
        compiler_params=_cparams(1),
        name="in_proj",
    )(x2, nw, w_in, qnw, knw, cos, sin)


MIX_TQ = 512


def _band_constants(seq):
    nb = seq // BLOCK
    coef = np.zeros((3, N_POOL_GROUPS, BLOCK, BAND), np.float32)
    inv_count = np.zeros((3, N_POOL_GROUPS, BLOCK, 1), np.float32)
    bias = np.zeros((3, BLOCK, BAND), np.float32)
    for kind, n in enumerate((0, 1, nb - 1)):
        start = min(max((n - 1) * BLOCK, 0), seq - BAND)
        t = n * BLOCK + np.arange(BLOCK)[:, None]
        s = start + np.arange(BAND)[None, :]
        bias[kind] = np.where(np.abs(s - t) <= WINDOW, 0.0, NEG_BIG)
        for g, win in enumerate(POOL_WINDOWS):
            half = win // 2
            lo = np.clip(t - half, 0, seq)
            hi = np.clip(t + half, 0, seq)
            count = (hi - lo).astype(np.float32)
            inside = ((s >= lo) & (s < hi)).astype(np.float32)
            coef[kind, g] = inside - count * (s == t)
            inv_count[kind, g] = 1.0 / count
    return coef, inv_count, bias


def _mixers_kernel(sink_ref, u_ref, q_ref, k_ref, v_ref, coef_ref, invc_ref, bias_ref,
                   wpool_ref, pscale_ref, ab_ref, *, seq):
    nb = seq // BLOCK
    j = pl.program_id(1)
    n_sub = MIX_TQ // BLOCK
    subs = []
    for r in range(n_sub):
        n = j * n_sub + r
        start = pl.multiple_of(jnp.clip((n - 1) * BLOCK, 0, seq - BAND), BLOCK)
        kind = jnp.where(n == 0, 0, jnp.where(n == nb - 1, 2, 1))
        subs.append((start, kind, slice(r * BLOCK, (r + 1) * BLOCK)))

    scores = {}
    for r, (start, kind, rows) in enumerate(subs):
        for hk in range(N_KV_HEADS):
            kb = k_ref[0, pl.ds(start, BAND), hk * HEAD_DIM:(hk + 1) * HEAD_DIM]
            qs = jnp.concatenate(
                [q_ref[0, rows, h * HEAD_DIM:(h + 1) * HEAD_DIM]
                 for h in range(hk * Q_PER_KV, (hk + 1) * Q_PER_KV)], axis=0)
            scores[r, hk] = lax.dot_general(qs, kb, (((1,), (1,)), ((), ())),
                                            preferred_element_type=F32)
    diffs = {}
    for r, (start, kind, rows) in enumerate(subs):
        for g in range(N_POOL_GROUPS):
            ub = u_ref[0, pl.ds(start, BAND), g * POOL_GROUP:(g + 1) * POOL_GROUP]
            d = jnp.dot(coef_ref[kind, g], ub, preferred_element_type=F32) * invc_ref[kind, g]
            diffs[r, g] = d.astype(BF16)
    probs = {}
    for r, (start, kind, rows) in enumerate(subs):
        bias = bias_ref[kind]
        for hk in range(N_KV_HEADS):
            for gi in range(Q_PER_KV):
                h = hk * Q_PER_KV + gi
                sh = scores[r, hk][gi * BLOCK:(gi + 1) * BLOCK] + bias
                sink = sink_ref[h]
                m = jnp.maximum(jnp.max(sh, axis=-1, keepdims=True), sink)
                p = jnp.exp(sh - m)
                denom = jnp.sum(p, axis=-1, keepdims=True) + jnp.exp(sink - m)
                probs[r, h] = (p.astype(BF16), denom)
    for r, (start, kind, rows) in enumerate(subs):
        for g in range(N_POOL_GROUPS):
            cols = slice(g * POOL_GROUP, (g + 1) * POOL_GROUP)
            y = jnp.dot(diffs[r, g], wpool_ref[g], preferred_element_type=F32)
            ab_ref[0, rows, cols] = (y * pscale_ref[:, cols]).astype(BF16)
    for r, (start, kind, rows) in enumerate(subs):
        for hk in range(N_KV_HEADS):
            vb = v_ref[0, pl.ds(start, BAND), hk * HEAD_DIM:(hk + 1) * HEAD_DIM]
            for gi in range(Q_PER_KV):
                h = hk * Q_PER_KV + gi
                p, denom = probs[r, h]
                o = jnp.dot(p, vb, preferred_element_type=F32) / denom
                ocols = slice(POOL_WIDTH + h * HEAD_DIM, POOL_WIDTH + (h + 1) * HEAD_DIM)
                ab_ref[0, rows, ocols] = o.astype(BF16)


def _mixers(sink, u, q, k, v, coef, invc, bias, wpool, pscale):
    B, S, _ = u.shape
    tq = MIX_TQ
    full = lambda shape: pl.BlockSpec(shape, lambda b, j: (0,) * len(shape))
    return pl.pallas_call(
        functools.partial(_mixers_kernel, seq=S),
        grid=(B, S // tq),
        in_specs=[
            pl.BlockSpec(memory_space=pltpu.SMEM),
            pl.BlockSpec((1, S, POOL_WIDTH), lambda b, j: (b, 0, 0)),
            pl.BlockSpec((1, tq, ATTN_WIDTH), lambda b, j: (b, j, 0)),
            pl.BlockSpec((1, S, KV_WIDTH), lambda b, j: (b, 0, 0)),
            pl.BlockSpec((1, S, KV_WIDTH), lambda b, j: (b, 0, 0)),
            full((3, N_POOL_GROUPS, BLOCK, BAND)),
            full((3, N_POOL_GROUPS, BLOCK, 1)),
            full((3, BLOCK, BAND)),
            full((N_POOL_GROUPS, POOL_GROUP, POOL_GROUP)),
            full((1, POOL_WIDTH)),
        ],
        out_specs=pl.BlockSpec((1, tq, MIX_WIDTH), lambda b, j: (b, j, 0)),
        out_shape=jax.ShapeDtypeStruct((B, S, MIX_WIDTH), BF16),
        compiler_params=_cparams(2),
        name="mixers",
    )(sink, u, q, k, v, coef, invc, bias, wpool, pscale)


OUT_TM = 512
ROUTE_E0, ROUTE_E1, ROUTE_W0, ROUTE_W1 = 0, 1, 2, 3


def _out_proj_kernel(ab_ref, x_ref, w_ref, nw_ref, wr_ref, br_ref,
                     h_ref, xn_ref, route_ref, counts_ref):
    @pl.when(pl.program_id(0) == 0)
    def _():
        counts_ref[...] = jnp.zeros_like(counts_ref)

    sub = OUT_TM // ROW_SPLIT
    lane = lax.broadcasted_iota(jnp.int32, (sub, LANES), 1)

    def first_argmax(vals):
        m = jnp.max(vals, axis=-1, keepdims=True)
        idx = jnp.min(jnp.where(vals == m, lane, LANES), axis=-1, keepdims=True)
        return m, idx

    new_counts = jnp.zeros((1, LANES), F32)
    for part in range(ROW_SPLIT):
        rows = slice(part * sub, (part + 1) * sub)
        h = x_ref[rows, :] + jnp.dot(ab_ref[rows, :], w_ref[...], preferred_element_type=F32)
        h_ref[rows, :] = h
        ms = jnp.mean(h * h, axis=-1, keepdims=True)
        xn = h * lax.rsqrt(ms + EPS) * nw_ref[...]
        packed = _pack_pair(xn[:, :HALF], xn[:, HALF:])
        for c in range(PACK_CHUNKS):
            xn_ref[pl.ds(part * sub * SUBLANES + c, sub, stride=SUBLANES), :] = (
                packed[:, c * LANES:(c + 1) * LANES])
        logits = jnp.dot(xn.astype(BF16), wr_ref[...], preferred_element_type=F32) + br_ref[...]

        gl = jnp.where(lane < N_GROUPS, logits, NEG_BIG)
        gmax, gidx = first_argmax(gl)
        gsum = jnp.sum(jnp.where(lane < N_GROUPS, jnp.exp(gl - gmax), 0.0),
                       axis=-1, keepdims=True)
        g_w = 1.0 / gsum
        e_lo = N_GROUPS + gidx * EXPERTS_PER_GROUP
        el = jnp.where((lane >= e_lo) & (lane < e_lo + EXPERTS_PER_GROUP), logits, NEG_BIG)
        m1, i1 = first_argmax(el)
        m2, i2 = first_argmax(jnp.where(lane == i1, NEG_BIG, el))
        t = jnp.exp(m2 - m1)
        p1 = 1.0 / (1.0 + t)
        p2 = t * p1
        e1 = i1 - N_GROUPS
        e2 = i2 - N_GROUPS
        route_ref[rows, :] = jnp.where(
            lane == ROUTE_E0, e1.astype(F32),
            jnp.where(lane == ROUTE_E1, e2.astype(F32),
                      jnp.where(lane == ROUTE_W0, g_w * p1,
                                jnp.where(lane == ROUTE_W1, g_w * p2, 0.0))))
        chosen = ((lane == e1) | (lane == e2)).astype(F32)
        new_counts = new_counts + jnp.sum(chosen, axis=0, keepdims=True)
    counts_ref[...] += new_counts


def _out_proj(ab, x2, w_out, nw, w_router, b_router):
    T = x2.shape[0]
    tm = OUT_TM
    full = lambda shape: pl.BlockSpec(shape, lambda i: (0,) * len(shape))
    row = lambda width: pl.BlockSpec((tm, width), lambda i: (i, 0))
    return pl.pallas_call(
        _out_proj_kernel,
        grid=(T // tm,),
        in_specs=[row(MIX_WIDTH), row(D_MODEL), full((MIX_WIDTH, D_MODEL)), full((1, D_MODEL)),
                  full((D_MODEL, LANES)), full((1, LANES))],
        out_specs=[row(D_MODEL),
                   pl.BlockSpec((tm * SUBLANES, LANES), lambda i: (i, 0)),
                   row(LANES),
                   full((1, LANES))],
        out_shape=[jax.ShapeDtypeStruct((T, D_MODEL), F32),
                   jax.ShapeDtypeStruct((T * SUBLANES, LANES), U32),
                   jax.ShapeDtypeStruct((T, LANES), F32),
                   jax.ShapeDtypeStruct((1, LANES), F32)],
        compiler_params=_cparams(1),
        name="out_proj",
    )(ab, x2, w_out, nw, w_router, b_router)


PLAN_TM = 512


TAB_EXPERT, TAB_FIRST, TAB_BUFFER, TAB_NEXT = 0, 1, 2, 3
TAB_ROWS = 4
ETAB_PAD_LO, ETAB_PAD_HI, ETAB_MISC = 0, 1, 2
ETAB_ROWS = 3
MISC_N_USED, MISC_RUN_EXPERT = 0, 1
FETCH_AHEAD = 2
TABLE_ROWS = SUBLANES


def _slot_tables(counts_ref, tab_ref, etab_ref, run_expert, run_of_block, pstart_s, n_blocks):
    log_block = MOE_BLOCK.bit_length() - 1

    def clear(r, carry):
        run_expert[r] = -1
        return carry
    lax.fori_loop(0, run_expert.shape[0], clear, 0)

    def per_expert(e, carry):
        slot, blk, run, buf = carry
        c = counts_ref[e]
        n_blk = lax.shift_right_logical(c + (MOE_BLOCK - 1), log_block)
        pstart_s[e] = slot
        etab_ref[ETAB_PAD_LO, e] = slot + c
        etab_ref[ETAB_PAD_HI, e] = slot + n_blk * MOE_BLOCK

        def per_block(b, inner):
            tab_ref[TAB_EXPERT, b] = e
            tab_ref[TAB_FIRST, b] = (b == blk).astype(jnp.int32)
            tab_ref[TAB_BUFFER, b] = buf
            run_of_block[b] = run
            return inner
        lax.fori_loop(blk, blk + n_blk, per_block, 0)

        @pl.when(n_blk > 0)
        def _():
            run_expert[run] = e
        has_run = (n_blk > 0).astype(jnp.int32)
        next_buf = jnp.where(buf + has_run == FETCH_AHEAD, 0, buf + has_run)
        return slot + n_blk * MOE_BLOCK, blk + n_blk, run + has_run, next_buf

    zero = jnp.int32(0)
    _, n_used, _, _ = lax.fori_loop(0, N_EXPERTS, per_expert, (zero, zero, zero, zero))

    def per_used_block(b, carry):
        tab_ref[TAB_NEXT, b] = run_expert[run_of_block[b] + FETCH_AHEAD]
        return carry
    lax.fori_loop(0, n_used, per_used_block, 0)

    def per_unused_block(b, carry):
        for row in (TAB_EXPERT, TAB_FIRST, TAB_BUFFER):
            tab_ref[row, b] = 0
        tab_ref[TAB_NEXT, b] = -1
        return carry
    lax.fori_loop(n_used, n_blocks, per_unused_block, 0)

    def clear_misc(e, carry):
        etab_ref[ETAB_MISC, e] = 0
        return carry
    lax.fori_loop(0, N_EXPERTS, clear_misc, 0)
    etab_ref[ETAB_MISC, MISC_N_USED] = n_used
    for r in range(FETCH_AHEAD):
        etab_ref[ETAB_MISC, MISC_RUN_EXPERT + r] = run_expert[r]


def _plan_kernel(counts_ref, route_ref, tri_ref, dest_ref, tab_ref, etab_ref,
                 carry_ref, pstart_ref, run_expert, run_of_block, pstart_s, *, n_blocks):
    @pl.when(pl.program_id(0) == 0)
    def _():
        carry_ref[...] = jnp.zeros_like(carry_ref)
        _slot_tables(counts_ref, tab_ref, etab_ref, run_expert, run_of_block, pstart_s, n_blocks)
        lane_row = lax.broadcasted_iota(jnp.int32, (1, LANES), 1)

        def place(e, row):
            return jnp.where(lane_row == e, pstart_s[e].astype(F32), row)
        pstart_ref[...] = lax.fori_loop(0, N_EXPERTS, place, jnp.zeros((1, LANES), F32))

    route = route_ref[...]
    lane = lax.broadcasted_iota(jnp.int32, route.shape, 1)
    e0 = route[:, ROUTE_E0:ROUTE_E0 + 1].astype(jnp.int32)
    e1 = route[:, ROUTE_E1:ROUTE_E1 + 1].astype(jnp.int32)
    oh0 = lane == e0
    oh1 = lane == e1
    both = (oh0 | oh1).astype(F32)
    earlier = jnp.dot(tri_ref[...], both.astype(BF16), preferred_element_type=F32)
    base = pstart_ref[...] + carry_ref[...] + earlier
    d0 = jnp.sum(jnp.where(oh0, base, 0.0), axis=-1, keepdims=True)
    d1 = jnp.sum(jnp.where(oh1, base, 0.0), axis=-1, keepdims=True)
    by_token = jnp.where(lane == 0, d0, jnp.where(lane == 1, d1, 0.0))
    dest_ref[...] = by_token.T[:TABLE_ROWS].astype(jnp.int32)
    carry_ref[...] += jnp.sum(both, axis=0, keepdims=True)


def _plan(counts, route, n_blocks):
    T = route.shape[0]
    tm = PLAN_TM
    tri = jnp.asarray(np.tril(np.ones((tm, tm), np.float32), -1), BF16)
    smem_out = pl.BlockSpec(memory_space=pltpu.SMEM)
    grid_spec = pltpu.PrefetchScalarGridSpec(
        num_scalar_prefetch=1,
        grid=(T // tm,),
        in_specs=[pl.BlockSpec((tm, LANES), lambda i, c: (i, 0)),
                  pl.BlockSpec((tm, tm), lambda i, c: (0, 0))],
        out_specs=[pl.BlockSpec((TABLE_ROWS, tm), lambda i, c: (0, i)), smem_out, smem_out],
        scratch_shapes=[pltpu.VMEM((1, LANES), F32), pltpu.VMEM((1, LANES), F32),
                        pltpu.SMEM((N_EXPERTS + FETCH_AHEAD,), jnp.int32),
                        pltpu.SMEM((n_blocks,), jnp.int32),
                        pltpu.SMEM((N_EXPERTS,), jnp.int32)],
    )
    return pl.pallas_call(
        functools.partial(_plan_kernel, n_blocks=n_blocks),
        grid_spec=grid_spec,
        out_shape=[jax.ShapeDtypeStruct((TABLE_ROWS, T), jnp.int32),
                   jax.ShapeDtypeStruct((TAB_ROWS, n_blocks), jnp.int32),
                   jax.ShapeDtypeStruct((ETAB_ROWS, N_EXPERTS), jnp.int32)],
        compiler_params=_cparams(1),
        name="plan",
    )(counts, route, tri)


DISP_TM = 512
DMA_GROUP = 8
ROW_WORDS = SUBLANES


def _packed_row(ref, row):
    return ref.at[pl.ds(pl.multiple_of(row * ROW_WORDS, ROW_WORDS), ROW_WORDS)]


def _grouped(n, fn):
    def trip(t, carry):
        for k in range(DMA_GROUP):
            fn(t * DMA_GROUP + k, k)
        return carry
    lax.fori_loop(0, n // DMA_GROUP, trip, 0)


def _dispatch_kernel(etab_ref, dest0_ref, dest1_ref, xn_ref, xs_hbm,
                     zeros, sem, tail_sem, *, experts_per_step, tails_per_step, n_blocks):
    i = pl.program_id(0)
    n_rows = TOP_K * DISP_TM

    @pl.when(i == 0)
    def _():
        zeros[...] = jnp.zeros_like(zeros)

    def scatter_group(t, carry):
        base = pl.multiple_of(t * DMA_GROUP, DMA_GROUP)
        window = xn_ref.at[pl.ds(base * ROW_WORDS, DMA_GROUP * ROW_WORDS)]
        for k in range(DMA_GROUP):
            src = window.at[pl.ds(k * ROW_WORDS, ROW_WORDS)]
            for choice, dest_ref in enumerate((dest0_ref, dest1_ref)):
                dst = _packed_row(xs_hbm, dest_ref[0, 0, 0, base + k])
                pltpu.make_async_copy(src, dst, sem).start(priority=choice)
        return carry

    lax.fori_loop(0, DISP_TM // DMA_GROUP, scatter_group, 0)

    def pad_fill(start):
        for j in range(experts_per_step):
            e = jnp.minimum(i * experts_per_step + j, N_EXPERTS - 1)
            live = i * experts_per_step + j < N_EXPERTS
            pos = etab_ref[ETAB_PAD_LO, e]
            n_pad = jnp.where(live, etab_ref[ETAB_PAD_HI, e] - pos, 0)
            for bit in reversed(range(MOE_BLOCK.bit_length() - 1)):
                size = 1 << bit
                take = jnp.bitwise_and(lax.shift_right_logical(n_pad, bit), 1)
                dst = xs_hbm.at[pl.ds(pl.multiple_of(pos * ROW_WORDS, ROW_WORDS), size * ROW_WORDS)]
                cp = pltpu.make_async_copy(zeros.at[pl.ds(0, size * ROW_WORDS)], dst, sem)

                @pl.when(take == 1)
                def _():
                    cp.start() if start else cp.wait()
                pos = pos + take * size

    def tail_copies(start):
        for j in range(tails_per_step):
            blk = etab_ref[ETAB_MISC, MISC_N_USED] + i * tails_per_step + j
            rows = pl.ds(pl.multiple_of(jnp.minimum(blk, n_blocks - 1) * (MOE_BLOCK * ROW_WORDS),
                                        MOE_BLOCK * ROW_WORDS), MOE_BLOCK * ROW_WORDS)
            cp = pltpu.make_async_copy(zeros, xs_hbm.at[rows], tail_sem)

            @pl.when(blk < n_blocks)
            def _():
                cp.start() if start else cp.wait()

    pad_fill(True)
    tail_copies(True)

    tile_rows = DISP_TM * ROW_WORDS
    for _ in range(TOP_K):
        pltpu.make_async_copy(xn_ref, xs_hbm.at[pl.ds(0, tile_rows)], sem).wait()
    pad_fill(False)
    tail_copies(False)


def _slot_tiles(dest, tile):
    return dest.reshape(dest.shape[0], dest.shape[1] // tile, 1, tile)


def _slot_spec(tile, choice, index):
    return pl.BlockSpec((1, 1, 1, tile), lambda i, *_: (choice, index(i), 0, 0),
                        memory_space=pltpu.SMEM)


def _dispatch(etab, dest, xn_packed, n_blocks):
    dest4 = _slot_tiles(dest, DISP_TM)
    n = dest4.shape[1]
    min_used = (n * DISP_TM * TOP_K) // MOE_BLOCK
    tails_per_step = -(-(n_blocks - min_used) // n)
    grid_spec = pltpu.PrefetchScalarGridSpec(
        num_scalar_prefetch=1,
        grid=(n,),
        in_specs=[_slot_spec(DISP_TM, 0, lambda i: i), _slot_spec(DISP_TM, 1, lambda i: i),
                  pl.BlockSpec((DISP_TM * ROW_WORDS, LANES), lambda i, *_: (i, 0))],
        out_specs=pl.BlockSpec(memory_space=pl.ANY),
        scratch_shapes=[pltpu.VMEM((MOE_BLOCK * ROW_WORDS, LANES), U32),
                        pltpu.SemaphoreType.DMA(()), pltpu.SemaphoreType.DMA(())],
    )
    return pl.pallas_call(
        functools.partial(_dispatch_kernel, experts_per_step=-(-N_EXPERTS // n),
                          tails_per_step=tails_per_step, n_blocks=n_blocks),
        grid_spec=grid_spec,
        out_shape=jax.ShapeDtypeStruct((n_blocks * MOE_BLOCK * ROW_WORDS, LANES), U32),
        compiler_params=_cparams(1),
        name="dispatch",
    )(etab, dest4, dest4, xn_packed)


CAST_ROWS = 256
FETCH_SPLIT = 4
BLOCKS_PER_STEP = 4


def _experts_kernel(tab_ref, etab_ref,
                    xs_ref, wg_hbm, wu_hbm, wd_hbm, ys_ref,
                    fg, fu, fd, wg, wu, wd, sem):
    n_used = etab_ref[ETAB_MISC, MISC_N_USED]
    for j in range(BLOCKS_PER_STEP):
        rows = pl.ds(j * MOE_BLOCK * SUBLANES, MOE_BLOCK * SUBLANES)
        _expert_block(pl.program_id(0) * BLOCKS_PER_STEP + j, n_used, tab_ref, etab_ref,
                      xs_ref.at[rows], wg_hbm, wu_hbm, wd_hbm, ys_ref.at[rows],
                      fg, fu, fd, wg, wu, wd, sem)


def _expert_block(b, n_used, tab_ref, etab_ref, xs_ref, wg_hbm, wu_hbm, wd_hbm, ys_ref,
                  fg, fu, fd, wg, wu, wd, sem):

    def fetch(e, p):
        copies = []
        for k, (src, dst) in enumerate(((wg_hbm, fg), (wu_hbm, fu), (wd_hbm, fd))):
            rows = src.shape[1] // FETCH_SPLIT
            for c in range(FETCH_SPLIT):
                sl = pl.ds(c * rows, rows)
                copies.append(pltpu.make_async_copy(src.at[e, sl], dst.at[p, sl], sem.at[p, k]))
        return copies

    def start_fetch(e, p):
        for k, cp in enumerate(fetch(e, p)):
            cp.start(priority=k % 2)

    @pl.when(b == 0)
    def _():
        for r in range(FETCH_AHEAD):
            e = etab_ref[ETAB_MISC, MISC_RUN_EXPERT + r]

            @pl.when(e >= 0)
            def _():
                start_fetch(e, r)

    @pl.when((b < n_used) & (tab_ref[TAB_FIRST, b] == 1))
    def _():
        p = tab_ref[TAB_BUFFER, b]
        for cp in fetch(tab_ref[TAB_EXPERT, b], p):
            cp.wait()

        def cast(src, dst, n_rows):
            def body(i, carry):
                rows = pl.ds(pl.multiple_of(i * CAST_ROWS, CAST_ROWS), CAST_ROWS)
                dst[rows, :] = src[p, rows, :].astype(BF16)
                return carry
            lax.fori_loop(0, n_rows // CAST_ROWS, body, 0)

        cast(fg, wg, D_MODEL)
        cast(fu, wu, D_MODEL)
        cast(fd, wd, D_EXPERT)

        @pl.when(tab_ref[TAB_NEXT, b] >= 0)
        def _():
            start_fetch(tab_ref[TAB_NEXT, b], p)

    @pl.when(b < n_used)
    def _():
        lo_parts, hi_parts = [], []
        for c in range(PACK_CHUNKS):
            lo, hi = _unpack_pair(xs_ref[pl.ds(c, MOE_BLOCK, stride=SUBLANES), :])
            lo_parts.append(lo.astype(BF16))
            hi_parts.append(hi.astype(BF16))
        xb = jnp.concatenate(lo_parts + hi_parts, axis=1)
        g = jnp.dot(xb, wg[...], preferred_element_type=F32)
        u = jnp.dot(xb, wu[...], preferred_element_type=F32)
        hmid = (g * jax.nn.sigmoid(g) * u).astype(BF16)
        y = jnp.dot(hmid, wd[...], preferred_element_type=F32)
        packed = _pack_pair(y[:, :HALF], y[:, HALF:])
        for c in range(PACK_CHUNKS):
            ys_ref[pl.ds(c, MOE_BLOCK, stride=SUBLANES), :] = packed[:, c * LANES:(c + 1) * LANES]

    @pl.when(b >= n_used)
    def _():
        ys_ref[...] = jnp.zeros(ys_ref.shape, ys_ref.dtype)


def _experts(tab, etab, xs, w_gate, w_up, w_down):
    n_blocks = tab.shape[1]
    assert n_blocks % BLOCKS_PER_STEP == 0
    blk = lambda: pl.BlockSpec((BLOCKS_PER_STEP * MOE_BLOCK * SUBLANES, LANES),
                               lambda s, *_: (s, 0))
    grid_spec = pltpu.PrefetchScalarGridSpec(
        num_scalar_prefetch=2,
        grid=(n_blocks // BLOCKS_PER_STEP,),
        in_specs=[blk(),
                  pl.BlockSpec(memory_space=pl.ANY),
                  pl.BlockSpec(memory_space=pl.ANY),
                  pl.BlockSpec(memory_space=pl.ANY)],
        out_specs=blk(),
        scratch_shapes=[pltpu.VMEM((FETCH_AHEAD, D_MODEL, D_EXPERT), F32),
                        pltpu.VMEM((FETCH_AHEAD, D_MODEL, D_EXPERT), F32),
                        pltpu.VMEM((FETCH_AHEAD, D_EXPERT, D_MODEL), F32),
                        pltpu.VMEM((D_MODEL, D_EXPERT), BF16),
                        pltpu.VMEM((D_MODEL, D_EXPERT), BF16),
                        pltpu.VMEM((D_EXPERT, D_MODEL), BF16),
                        pltpu.SemaphoreType.DMA((FETCH_AHEAD, 3))],
    )
    return pl.pallas_call(
        _experts_kernel,
        grid_spec=grid_spec,
        out_shape=jax.ShapeDtypeStruct((n_blocks * MOE_BLOCK * SUBLANES, LANES), U32),
        compiler_params=_cparams(1),
        name="experts",
    )(tab, etab, xs, w_gate, w_up, w_down)


CMB_TM = 256


def _combine_kernel(d0_cur_ref, d1_cur_ref, d0_next_ref, d1_next_ref, h_ref, route_ref, ys_hbm,
                    o_ref, ybuf, sem):
    i = pl.program_id(0)
    n = pl.num_programs(0)
    slot = i % 2
    n_rows = TOP_K * CMB_TM

    def gather(idx_refs, s):
        def gather_group(t, carry):
            base = pl.multiple_of(t * DMA_GROUP, DMA_GROUP)
            for choice, idx_ref in enumerate(idx_refs):
                first = (choice * CMB_TM + base) * ROW_WORDS
                window = ybuf.at[s, pl.ds(first, DMA_GROUP * ROW_WORDS)]
                for k in range(DMA_GROUP):
                    pltpu.make_async_copy(_packed_row(ys_hbm, idx_ref[0, 0, 0, base + k]),
                                          window.at[pl.ds(k * ROW_WORDS, ROW_WORDS)],
                                          sem.at[s]).start(priority=k % 2)
            return carry
        lax.fori_loop(0, CMB_TM // DMA_GROUP, gather_group, 0)

    @pl.when(i == 0)
    def _():
        gather((d0_cur_ref, d1_cur_ref), 0)

    @pl.when(i + 1 < n)
    def _():
        gather((d0_next_ref, d1_next_ref), 1 - slot)

    pltpu.make_async_copy(ys_hbm.at[pl.ds(0, n_rows * ROW_WORDS)], ybuf.at[slot],
                          sem.at[slot]).wait()
    route = route_ref[...]
    w0 = route[:, ROUTE_W0:ROUTE_W0 + 1]
    w1 = route[:, ROUTE_W1:ROUTE_W1 + 1]
    yb = ybuf.at[slot]
    for c in range(PACK_CHUNKS):
        lo0, hi0 = _unpack_pair(yb[pl.ds(c, CMB_TM, stride=SUBLANES), :])
        lo1, hi1 = _unpack_pair(yb[pl.ds(CMB_TM * SUBLANES + c, CMB_TM, stride=SUBLANES), :])
        lo_cols = slice(c * LANES, (c + 1) * LANES)
        hi_cols = slice(HALF + c * LANES, HALF + (c + 1) * LANES)
        o_ref[:, lo_cols] = h_ref[:, lo_cols] + (w0 * lo0 + w1 * lo1)
        o_ref[:, hi_cols] = h_ref[:, hi_cols] + (w0 * hi0 + w1 * hi1)


def _combine(dest, h, route, ys):
    T = h.shape[0]
    tm = CMB_TM
    n = T // tm
    dest4 = _slot_tiles(dest, tm)
    cur = lambda i: i
    nxt = lambda i: jnp.minimum(i + 1, n - 1)
    return pl.pallas_call(
        _combine_kernel,
        grid=(n,),
        in_specs=[
            _slot_spec(tm, 0, cur), _slot_spec(tm, 1, cur),
            _slot_spec(tm, 0, nxt), _slot_spec(tm, 1, nxt),
            pl.BlockSpec((tm, D_MODEL), lambda i: (i, 0)),
            pl.BlockSpec((tm, LANES), lambda i: (i, 0)),
            pl.BlockSpec(memory_space=pl.ANY),
        ],
        out_specs=pl.BlockSpec((tm, D_MODEL), lambda i: (i, 0)),
        out_shape=jax.ShapeDtypeStruct((T, D_MODEL), F32),
        scratch_shapes=[pltpu.VMEM((2, TOP_K * tm * SUBLANES, LANES), U32),
                        pltpu.SemaphoreType.DMA((2,))],
        compiler_params=_cparams(1),
        name="combine",
    )(dest4, dest4, dest4, dest4, h, route, ys)


def _rope_tables(S):
    inv = ROPE_THETA ** (-jnp.arange(0, HEAD_DIM, 2, dtype=F32) / HEAD_DIM)
    ang = jnp.arange(S, dtype=F32)[:, None] * inv[None, :]
    cos, sin = jnp.cos(ang), jnp.sin(ang)
    return jnp.concatenate([cos, cos], axis=-1), jnp.concatenate([-sin, sin], axis=-1)


def kernel(x, norm_mix_w, w_in, w_pool, pool_scale, q_norm_w, k_norm_w, sink_logits, w_out,
           norm_ffn_w, w_group_router, b_group_router, w_expert_router, b_expert_router,
           w_gate, w_up, w_down):
    B, S, D = x.shape
    T = B * S
    depth = w_in.shape[0]
    cos, sin = _rope_tables(S)
    coef_np, invc_np, bias_np = _band_constants(S)
    coef = jnp.asarray(coef_np, BF16)
    invc = jnp.asarray(invc_np, F32)
    bias = jnp.asarray(bias_np, F32)
    pad_lanes = LANES - N_GROUPS - N_EXPERTS

    h = x.reshape(T, D)
    for l in range(depth):
        u, q, k, v = _in_proj(h, norm_mix_w[l].reshape(1, D), w_in[l].astype(BF16),
                              q_norm_w[l].reshape(1, HEAD_DIM), k_norm_w[l].reshape(1, HEAD_DIM),
                              cos, sin, S)
        ab = _mixers(sink_logits[l], u.reshape(B, S, -1), q.reshape(B, S, -1),
                     k.reshape(B, S, -1), v.reshape(B, S, -1), coef, invc, bias,
                     w_pool[l].astype(BF16), pool_scale[l].reshape(1, POOL_WIDTH))
        w_router = jnp.concatenate(
            [w_group_router[l], w_expert_router[l], jnp.zeros((D, pad_lanes), F32)], axis=1)
        b_router = jnp.concatenate(
            [b_group_router[l], b_expert_router[l], jnp.zeros((pad_lanes,), F32)]).reshape(1, LANES)
        hmix, xn_packed, route, counts = _out_proj(
            ab.reshape(T, MIX_WIDTH), h, w_out[l].astype(BF16), norm_ffn_w[l].reshape(1, D),
            w_router.astype(BF16), b_router)
        n_blocks = -(-(T * TOP_K + N_EXPERTS * (MOE_BLOCK - 1)) // MOE_BLOCK)
        dest, tab, etab = _plan(counts[0].astype(jnp.int32), route, n_blocks)
        xs = _dispatch(etab, dest, xn_packed, n_blocks)
        ys = _experts(tab, etab, xs, w_gate[l], w_up[l], w_down[l])
        h = _combine(dest, hmix, route, ys)
    return h.reshape(B, S, D)
```

```python
import functools

import numpy as np
import jax
import jax.numpy as jnp
from jax import lax
from jax.experimental import pallas as pl
from jax.experimental.pallas import tpu as pltpu

D_MODEL = 2048
POOL_WIDTH = 1024
POOL_WINDOWS = (2, 4, 8, 16)
N_POOL_GROUPS = 4
POOL_GROUP = 256
HEAD_DIM = 128
N_Q_HEADS = 8
N_KV_HEADS = 2
Q_PER_KV = 4
ATTN_WIDTH = 1024
KV_WIDTH = 256
IN_PROJ_WIDTH = 2560
MIX_WIDTH = 2048
WINDOW = 128
BLOCK = 128
BAND = 3 * BLOCK
ROPE_THETA = 10000.0
N_GROUPS = 8
EXPERTS_PER_GROUP = 8
N_EXPERTS = 64
TOP_K = 2
D_EXPERT = 512
MOE_BLOCK = 256
EPS = 1e-6

LANES = 128
SUBLANES = 8
HALF = D_MODEL // 2
PACK_CHUNKS = HALF // LANES
NEG_BIG = -1e30
VMEM_LIMIT = 56 * 1024 * 1024

BF16 = jnp.bfloat16
F32 = jnp.float32
U32 = jnp.uint32
HI_MASK = 0xFFFF0000


def _cparams(n_axes):
    return pltpu.CompilerParams(dimension_semantics=("arbitrary",) * n_axes,
                                vmem_limit_bytes=VMEM_LIMIT)


def _pack_pair(lo, hi):
    lo_bits = lax.bitcast_convert_type(lo.astype(BF16).astype(F32), U32) >> 16
    hi_bits = lax.bitcast_convert_type(hi.astype(BF16).astype(F32), U32) & jnp.uint32(HI_MASK)
    return lo_bits | hi_bits


def _unpack_pair(words):
    lo = lax.bitcast_convert_type(words << 16, F32)
    hi = lax.bitcast_convert_type(words & jnp.uint32(HI_MASK), F32)
    return lo, hi


IN_TM = 512
ROW_SPLIT = 2


def _in_proj_kernel(x_ref, nw_ref, w_ref, qnw_ref, knw_ref, cos_ref, sin_ref,
                    u_ref, q_ref, k_ref, v_ref):
    sub = IN_TM // ROW_SPLIT
    for part in range(ROW_SPLIT):
        rows = slice(part * sub, (part + 1) * sub)
        x = x_ref[rows, :]
        ms = jnp.mean(x * x, axis=-1, keepdims=True)
        xn = (x * lax.rsqrt(ms + EPS) * nw_ref[...]).astype(BF16)
        cos = cos_ref[rows, :]
        sin = sin_ref[rows, :]

        def head_norm_rope(t, w, scale):
            hms = jnp.mean(t * t, axis=-1, keepdims=True)
            t = t * lax.rsqrt(hms + EPS) * w
            t = t * cos + pltpu.roll(t, HEAD_DIM // 2, axis=1) * sin
            return t * scale

        u_ref[rows, :] = jnp.dot(xn, w_ref[:, :POOL_WIDTH],
                                 preferred_element_type=F32).astype(BF16)
        zq = jnp.dot(xn, w_ref[:, POOL_WIDTH:POOL_WIDTH + ATTN_WIDTH],
                     preferred_element_type=F32)
        qnw = qnw_ref[...]
        for h in range(N_Q_HEADS):
            sl = slice(h * HEAD_DIM, (h + 1) * HEAD_DIM)
            q_ref[rows, sl] = head_norm_rope(zq[:, sl], qnw, HEAD_DIM ** -0.5).astype(BF16)
        o_k = POOL_WIDTH + ATTN_WIDTH
        zk = jnp.dot(xn, w_ref[:, o_k:o_k + KV_WIDTH], preferred_element_type=F32)
        knw = knw_ref[...]
        for h in range(N_KV_HEADS):
            sl = slice(h * HEAD_DIM, (h + 1) * HEAD_DIM)
            k_ref[rows, sl] = head_norm_rope(zk[:, sl], knw, 1.0).astype(BF16)
        v_ref[rows, :] = jnp.dot(xn, w_ref[:, o_k + KV_WIDTH:],
                                 preferred_element_type=F32).astype(BF16)


def _in_proj(x2, nw, w_in, qnw, knw, cos, sin, seq):
    T = x2.shape[0]
    tm = IN_TM
    pos_blocks = seq // tm
    full = lambda shape: pl.BlockSpec(shape, lambda i: (0,) * len(shape))
    return pl.pallas_call(
        _in_proj_kernel,
        grid=(T // tm,),
        in_specs=[
            pl.BlockSpec((tm, D_MODEL), lambda i: (i, 0)),
            full((1, D_MODEL)),
            full((D_MODEL, IN_PROJ_WIDTH)),
            full((1, HEAD_DIM)),
            full((1, HEAD_DIM)),
            pl.BlockSpec((tm, HEAD_DIM), lambda i: (i % pos_blocks, 0)),
            pl.BlockSpec((tm, HEAD_DIM), lambda i: (i % pos_blocks, 0)),
        ],
        out_specs=[
            pl.BlockSpec((tm, POOL_WIDTH), lambda i: (i, 0)),
            pl.BlockSpec((tm, ATTN_WIDTH), lambda i: (i, 0)),
            pl.BlockSpec((tm, KV_WIDTH), lambda i: (i, 0)),
            pl.BlockSpec((tm, KV_WIDTH), lambda i: (i, 0)),
        ],
        out_shape=[
            jax.ShapeDtypeStruct((T, POOL_WIDTH), BF16),
            jax.ShapeDtypeStruct((T, ATTN_WIDTH), BF16),
            jax.ShapeDtypeStruct((T, KV_WIDTH), BF16),
            jax.ShapeDtypeStruct((T, KV_WIDTH), BF16),
        ],
        compiler_params=_cparams(1),
        name="in_proj",
    )(x2, nw, w_in, qnw, knw, cos, sin)


MIX_TQ = 512


def _band_constants(seq):
    nb = seq // BLOCK
    coef = np.zeros((3, N_POOL_GROUPS, BLOCK, BAND), np.float32)
    inv_count = np.zeros((3, N_POOL_GROUPS, BLOCK, 1), np.float32)
    bias = np.zeros((3, BLOCK, BAND), np.float32)
    for kind, n in enumerate((0, 1, nb - 1)):
        start = min(max((n - 1) * BLOCK, 0), seq - BAND)
        t = n * BLOCK + np.arange(BLOCK)[:, None]
        s = start + np.arange(BAND)[None, :]
        bias[kind] = np.where(np.abs(s - t) <= WINDOW, 0.0, NEG_BIG)
        for g, win in enumerate(POOL_WINDOWS):
            half = win // 2
            lo = np.clip(t - half, 0, seq)
            hi = np.clip(t + half, 0, seq)
            count = (hi - lo).astype(np.float32)
            inside = ((s >= lo) & (s < hi)).astype(np.float32)
            coef[kind, g] = inside - count * (s == t)
            inv_count[kind, g] = 1.0 / count
    return coef, inv_count, bias


def _mixers_kernel(sink_ref, u_ref, q_ref, k_ref, v_ref, coef_ref, invc_ref, bias_ref,
                   wpool_ref, pscale_ref, ab_ref, *, seq):
    nb = seq // BLOCK
    j = pl.program_id(1)
    n_sub = MIX_TQ // BLOCK
    subs = []
    for r in range(n_sub):
        n = j * n_sub + r
        start = pl.multiple_of(jnp.clip((n - 1) * BLOCK, 0, seq - BAND), BLOCK)
        kind = jnp.where(n == 0, 0, jnp.where(n == nb - 1, 2, 1))
        subs.append((start, kind, slice(r * BLOCK, (r + 1) * BLOCK)))

    scores = {}
    for r, (start, kind, rows) in enumerate(subs):
        for hk in range(N_KV_HEADS):
            kb = k_ref[0, pl.ds(start, BAND), hk * HEAD_DIM:(hk + 1) * HEAD_DIM]
            qs = jnp.concatenate(
                [q_ref[0, rows, h * HEAD_DIM:(h + 1) * HEAD_DIM]
                 for h in range(hk * Q_PER_KV, (hk + 1) * Q_PER_KV)], axis=0)
            scores[r, hk] = lax.dot_general(qs, kb, (((1,), (1,)), ((), ())),
                                            preferred_element_type=F32)
    diffs = {}
    for r, (start, kind, rows) in enumerate(subs):
        for g in range(N_POOL_GROUPS):
            ub = u_ref[0, pl.ds(start, BAND), g * POOL_GROUP:(g + 1) * POOL_GROUP]
            d = jnp.dot(coef_ref[kind, g], ub, preferred_element_type=F32) * invc_ref[kind, g]
            diffs[r, g] = d.astype(BF16)
    probs = {}
    for r, (start, kind, rows) in enumerate(subs):
        bias = bias_ref[kind]
        for hk in range(N_KV_HEADS):
            for gi in range(Q_PER_KV):
                h = hk * Q_PER_KV + gi
                sh = scores[r, hk][gi * BLOCK:(gi + 1) * BLOCK] + bias
                sink = sink_ref[h]
                m = jnp.maximum(jnp.max(sh, axis=-1, keepdims=True), sink)
                p = jnp.exp(sh - m)
                denom = jnp.sum(p, axis=-1, keepdims=True) + jnp.exp(sink - m)
                probs[r, h] = (p.astype(BF16), denom)
    for r, (start, kind, rows) in enumerate(subs):
        for g in range(N_POOL_GROUPS):
            cols = slice(g * POOL_GROUP, (g + 1) * POOL_GROUP)
            y = jnp.dot(diffs[r, g], wpool_ref[g], preferred_element_type=F32)
            ab_ref[0, rows, cols] = (y * pscale_ref[:, cols]).astype(BF16)
    for r, (start, kind, rows) in enumerate(subs):
        for hk in range(N_KV_HEADS):
            vb = v_ref[0, pl.ds(start, BAND), hk * HEAD_DIM:(hk + 1) * HEAD_DIM]
            for gi in range(Q_PER_KV):
                h = hk * Q_PER_KV + gi
                p, denom = probs[r, h]
                o = jnp.dot(p, vb, preferred_element_type=F32) / denom
                ocols = slice(POOL_WIDTH + h * HEAD_DIM, POOL_WIDTH + (h + 1) * HEAD_DIM)
                ab_ref[0, rows, ocols] = o.astype(BF16)


def _mixers(sink, u, q, k, v, coef, invc, bias, wpool, pscale):
    B, S, _ = u.shape
    tq = MIX_TQ
    full = lambda shape: pl.BlockSpec(shape, lambda b, j: (0,) * len(shape))
    return pl.pallas_call(
        functools.partial(_mixers_kernel, seq=S),
        grid=(B, S // tq),
        in_specs=[
            pl.BlockSpec(memory_space=pltpu.SMEM),
            pl.BlockSpec((1, S, POOL_WIDTH), lambda b, j: (b, 0, 0)),
            pl.BlockSpec((1, tq, ATTN_WIDTH), lambda b, j: (b, j, 0)),
            pl.BlockSpec((1, S, KV_WIDTH), lambda b, j: (b, 0, 0)),
            pl.BlockSpec((1, S, KV_WIDTH), lambda b, j: (b, 0, 0)),
            full((3, N_POOL_GROUPS, BLOCK, BAND)),
            full((3, N_POOL_GROUPS, BLOCK, 1)),
            full((3, BLOCK, BAND)),
            full((N_POOL_GROUPS, POOL_GROUP, POOL_GROUP)),
            full((1, POOL_WIDTH)),
        ],
        out_specs=pl.BlockSpec((1, tq, MIX_WIDTH), lambda b, j: (b, j, 0)),
        out_shape=jax.ShapeDtypeStruct((B, S, MIX_WIDTH), BF16),
        compiler_params=_cparams(2),
        name="mixers",
    )(sink, u, q, k, v, coef, invc, bias, wpool, pscale)


OUT_TM = 512
ROUTE_E0, ROUTE_E1, ROUTE_W0, ROUTE_W1 = 0, 1, 2, 3


def _out_proj_kernel(ab_ref, x_ref, w_ref, nw_ref, wr_ref, br_ref,
                     h_ref, xn_ref, route_ref, counts_ref, h_keep):
    i = pl.program_id(0)

    @pl.when(i == 0)
    def _():
        counts_ref[...] = jnp.zeros_like(counts_ref)
        h_keep[...] = jnp.zeros_like(h_keep)

    sub = OUT_TM // ROW_SPLIT
    lane = lax.broadcasted_iota(jnp.int32, (sub, LANES), 1)
    prev_slot = (i + 1) % 2
    cur_slot = i % 2

    def first_argmax(vals):
        m = jnp.max(vals, axis=-1, keepdims=True)
        idx = jnp.min(jnp.where(vals == m, lane, LANES), axis=-1, keepdims=True)
        return m, idx

    def project(part):
        rows = slice(part * sub, (part + 1) * sub)
        h = x_ref[rows, :] + jnp.dot(ab_ref[rows, :], w_ref[...], preferred_element_type=F32)
        h_ref[rows, :] = h
        h_keep[cur_slot, rows, :] = h

    project(0)
    logits_of = []
    for part in range(ROW_SPLIT):
        rows = slice(part * sub, (part + 1) * sub)
        h = h_keep[prev_slot, rows, :]
        ms = jnp.mean(h * h, axis=-1, keepdims=True)
        xn = h * lax.rsqrt(ms + EPS) * nw_ref[...]
        packed = _pack_pair(xn[:, :HALF], xn[:, HALF:])
        for c in range(PACK_CHUNKS):
            xn_ref[pl.ds(part * sub * SUBLANES + c, sub, stride=SUBLANES), :] = (
                packed[:, c * LANES:(c + 1) * LANES])
        logits_of.append(
            jnp.dot(xn.astype(BF16), wr_ref[...], preferred_element_type=F32) + br_ref[...])
    for part in range(1, ROW_SPLIT):
        project(part)

    new_counts = jnp.zeros((1, LANES), F32)
    for part, logits in enumerate(logits_of):
        rows = slice(part * sub, (part + 1) * sub)
        gl = jnp.where(lane < N_GROUPS, logits, NEG_BIG)
        gmax, gidx = first_argmax(gl)
        gsum = jnp.sum(jnp.where(lane < N_GROUPS, jnp.exp(gl - gmax), 0.0),
                       axis=-1, keepdims=True)
        g_w = 1.0 / gsum
        e_lo = N_GROUPS + gidx * EXPERTS_PER_GROUP
        el = jnp.where((lane >= e_lo) & (lane < e_lo + EXPERTS_PER_GROUP), logits, NEG_BIG)
        m1, i1 = first_argmax(el)
        m2, i2 = first_argmax(jnp.where(lane == i1, NEG_BIG, el))
        t = jnp.exp(m2 - m1)
        p1 = 1.0 / (1.0 + t)
        p2 = t * p1
        e1 = i1 - N_GROUPS
        e2 = i2 - N_GROUPS
        route_ref[rows, :] = jnp.where(
            lane == ROUTE_E0, e1.astype(F32),
            jnp.where(lane == ROUTE_E1, e2.astype(F32),
                      jnp.where(lane == ROUTE_W0, g_w * p1,
                                jnp.where(lane == ROUTE_W1, g_w * p2, 0.0))))
        chosen = ((lane == e1) | (lane == e2)).astype(F32)
        new_counts = new_counts + jnp.sum(chosen, axis=0, keepdims=True)
    counts_ref[...] += jnp.where(i > 0, new_counts, 0.0)


def _out_proj(ab, x2, w_out, nw, w_router, b_router):
    T = x2.shape[0]
    tm = OUT_TM
    n_tiles = T // tm
    full = lambda shape: pl.BlockSpec(shape, lambda i: (0,) * len(shape))
    cur = lambda width: pl.BlockSpec((tm, width), lambda i: (jnp.minimum(i, n_tiles - 1), 0))
    prev = lambda rows, width: pl.BlockSpec((rows, width), lambda i: (jnp.maximum(i - 1, 0), 0))
    return pl.pallas_call(
        _out_proj_kernel,
        grid=(n_tiles + 1,),
        in_specs=[cur(MIX_WIDTH), cur(D_MODEL), full((MIX_WIDTH, D_MODEL)), full((1, D_MODEL)),
                  full((D_MODEL, LANES)), full((1, LANES))],
        out_specs=[cur(D_MODEL),
                   prev(tm * SUBLANES, LANES),
                   prev(tm, LANES),
                   full((1, LANES))],
        scratch_shapes=[pltpu.VMEM((2, tm, D_MODEL), F32)],
        out_shape=[jax.ShapeDtypeStruct((T, D_MODEL), F32),
                   jax.ShapeDtypeStruct((T * SUBLANES, LANES), U32),
                   jax.ShapeDtypeStruct((T, LANES), F32),
                   jax.ShapeDtypeStruct((1, LANES), F32)],
        compiler_params=_cparams(1),
        name="out_proj",
    )(ab, x2, w_out, nw, w_router, b_router)


FUSE_TM = 512


def _mix_proj_kernel(sink_ref, u_ref, q_ref, k_ref, v_ref, coef_ref, invc_ref, bias_ref,
                     wpool_ref, pscale_ref, x_ref, w_ref, nw_ref, wr_ref, br_ref,
                     h_ref, xn_ref, route_ref, counts_ref, ab_keep, *, seq, n_tiles):
    i = pl.program_id(0)

    @pl.when(i == 0)
    def _():
        counts_ref[...] = jnp.zeros_like(counts_ref)
        ab_keep[...] = jnp.zeros_like(ab_keep)

    cur_slot = i % 2
    prev_slot = (i + 1) % 2
    nb = seq // BLOCK
    n_sub = FUSE_TM // BLOCK
    j = lax.rem(jnp.minimum(i, n_tiles - 1), seq // FUSE_TM)
    subs = []
    for r in range(n_sub):
        n = j * n_sub + r
        start = pl.multiple_of(jnp.clip((n - 1) * BLOCK, 0, seq - BAND), BLOCK)
        kind = jnp.where(n == 0, 0, jnp.where(n == nb - 1, 2, 1))
        subs.append((start, kind, slice(r * BLOCK, (r + 1) * BLOCK)))

    half = FUSE_TM // ROW_SPLIT
    lane = lax.broadcasted_iota(jnp.int32, (half, LANES), 1)

    def project(part):
        rows = slice(part * half, (part + 1) * half)
        h = x_ref[rows, :] + jnp.dot(ab_keep[prev_slot, rows, :], w_ref[...],
                                     preferred_element_type=F32)
        h_ref[rows, :] = h
        return h

    def first_argmax(vals):
        m = jnp.max(vals, axis=-1, keepdims=True)
        idx = jnp.min(jnp.where(vals == m, lane, LANES), axis=-1, keepdims=True)
        return m, idx

    def norm_pack_route(part, h):
        rows = slice(part * half, (part + 1) * half)
        ms = jnp.mean(h * h, axis=-1, keepdims=True)
        xn = h * lax.rsqrt(ms + EPS) * nw_ref[...]
        packed = _pack_pair(xn[:, :HALF], xn[:, HALF:])
        for c in range(PACK_CHUNKS):
            xn_ref[pl.ds(part * half * SUBLANES + c, half, stride=SUBLANES), :] = (
                packed[:, c * LANES:(c + 1) * LANES])
        logits = jnp.dot(xn.astype(BF16), wr_ref[...], preferred_element_type=F32) + br_ref[...]
        gl = jnp.where(lane < N_GROUPS, logits, NEG_BIG)
        gmax, gidx = first_argmax(gl)
        gsum = jnp.sum(jnp.where(lane < N_GROUPS, jnp.exp(gl - gmax), 0.0),
                       axis=-1, keepdims=True)
        g_w = 1.0 / gsum
        e_lo = N_GROUPS + gidx * EXPERTS_PER_GROUP
        el = jnp.where((lane >= e_lo) & (lane < e_lo + EXPERTS_PER_GROUP), logits, NEG_BIG)
        m1, i1 = first_argmax(el)
        m2, i2 = first_argmax(jnp.where(lane == i1, NEG_BIG, el))
        t = jnp.exp(m2 - m1)
        p1 = 1.0 / (1.0 + t)
        p2 = t * p1
        e1 = i1 - N_GROUPS
        e2 = i2 - N_GROUPS
        route_ref[rows, :] = jnp.where(
            lane == ROUTE_E0, e1.astype(F32),
            jnp.where(lane == ROUTE_E1, e2.astype(F32),
                      jnp.where(lane == ROUTE_W0, g_w * p1,
                                jnp.where(lane == ROUTE_W1, g_w * p2, 0.0))))
        chosen = ((lane == e1) | (lane == e2)).astype(F32)
        return jnp.sum(chosen, axis=0, keepdims=True)

    scores = {}
    for r, (start, kind, rows) in enumerate(subs):
        for hk in range(N_KV_HEADS):
            kb = k_ref[0, pl.ds(start, BAND), hk * HEAD_DIM:(hk + 1) * HEAD_DIM]
            qs = jnp.concatenate(
                [q_ref[0, rows, h * HEAD_DIM:(h + 1) * HEAD_DIM]
                 for h in range(hk * Q_PER_KV, (hk + 1) * Q_PER_KV)], axis=0)
            scores[r, hk] = lax.dot_general(qs, kb, (((1,), (1,)), ((), ())),
                                            preferred_element_type=F32)
    diffs = {}
    for r, (start, kind, rows) in enumerate(subs):
        for g in range(N_POOL_GROUPS):
            ub = u_ref[0, pl.ds(start, BAND), g * POOL_GROUP:(g + 1) * POOL_GROUP]
            d = jnp.dot(coef_ref[kind, g], ub, preferred_element_type=F32) * invc_ref[kind, g]
            diffs[r, g] = d.astype(BF16)

    h_parts = [project(0)]

    probs = {}
    for r, (start, kind, rows) in enumerate(subs):
        bias = bias_ref[kind]
        for hk in range(N_KV_HEADS):
            for gi in range(Q_PER_KV):
                h = hk * Q_PER_KV + gi
                sh = scores[r, hk][gi * BLOCK:(gi + 1) * BLOCK] + bias
                sink = sink_ref[h]
                m = jnp.maximum(jnp.max(sh, axis=-1, keepdims=True), sink)
                p = jnp.exp(sh - m)
                denom = jnp.sum(p, axis=-1, keepdims=True) + jnp.exp(sink - m)
                probs[r, h] = (p.astype(BF16), denom)
    for r, (start, kind, rows) in enumerate(subs):
        for g in range(N_POOL_GROUPS):
            cols = slice(g * POOL_GROUP, (g + 1) * POOL_GROUP)
            y = jnp.dot(diffs[r, g], wpool_ref[g], preferred_element_type=F32)
            ab_keep[cur_slot, rows, cols] = (y * pscale_ref[:, cols]).astype(BF16)
    for r, (start, kind, rows) in enumerate(subs):
        for hk in range(N_KV_HEADS):
            vb = v_ref[0, pl.ds(start, BAND), hk * HEAD_DIM:(hk + 1) * HEAD_DIM]
            for gi in range(Q_PER_KV):
                h = hk * Q_PER_KV + gi
                p, denom = probs[r, h]
                o = jnp.dot(p, vb, preferred_element_type=F32) / denom
                ocols = slice(POOL_WIDTH + h * HEAD_DIM, POOL_WIDTH + (h + 1) * HEAD_DIM)
                ab_keep[cur_slot, rows, ocols] = o.astype(BF16)

    for part in range(1, ROW_SPLIT):
        h_parts.append(project(part))
    new_counts = jnp.zeros((1, LANES), F32)
    for part, h in enumerate(h_parts):
        new_counts = new_counts + norm_pack_route(part, h)
    counts_ref[...] += jnp.where(i > 0, new_counts, 0.0)


def _mix_proj(sink, u, q, k, v, coef, invc, bias, wpool, pscale, x2, w_out, nw, w_router,
              b_router):
    B, S, _ = u.shape
    T = x2.shape[0]
    tm = FUSE_TM
    per_seq = S // tm
    n_tiles = T // tm
    once = pl.Buffered(1)
    full = lambda shape: pl.BlockSpec(shape, lambda i: (0,) * len(shape), pipeline_mode=once)
    cur = lambda i: jnp.minimum(i, n_tiles - 1)
    prev = lambda i: jnp.maximum(i - 1, 0)
    per_batch = lambda width: pl.BlockSpec((1, S, width), lambda i: (cur(i) // per_seq, 0, 0),
                                           pipeline_mode=once)
    prev_rows = lambda rows, width: pl.BlockSpec((rows, width), lambda i: (prev(i), 0))
    return pl.pallas_call(
        functools.partial(_mix_proj_kernel, seq=S, n_tiles=n_tiles),
        grid=(n_tiles + 1,),
        in_specs=[
            pl.BlockSpec(memory_space=pltpu.SMEM),
            per_batch(POOL_WIDTH),
            pl.BlockSpec((1, tm, ATTN_WIDTH), lambda i: (cur(i) // per_seq, cur(i) % per_seq, 0)),
            per_batch(KV_WIDTH),
            per_batch(KV_WIDTH),
            full((3, N_POOL_GROUPS, BLOCK, BAND)),
            full((3, N_POOL_GROUPS, BLOCK, 1)),
            full((3, BLOCK, BAND)),
            full((N_POOL_GROUPS, POOL_GROUP, POOL_GROUP)),
            full((1, POOL_WIDTH)),
            prev_rows(tm, D_MODEL),
            full((MIX_WIDTH, D_MODEL)),
            full((1, D_MODEL)),
            full((D_MODEL, LANES)),
            full((1, LANES)),
        ],
        out_specs=[prev_rows(tm, D_MODEL),
                   prev_rows(tm * SUBLANES, LANES),
                   prev_rows(tm, LANES),
                   pl.BlockSpec((1, LANES), lambda i: (0, 0))],
        out_shape=[jax.ShapeDtypeStruct((T, D_MODEL), F32),
                   jax.ShapeDtypeStruct((T * SUBLANES, LANES), U32),
                   jax.ShapeDtypeStruct((T, LANES), F32),
                   jax.ShapeDtypeStruct((1, LANES), F32)],
        scratch_shapes=[pltpu.VMEM((2, tm, MIX_WIDTH), BF16)],
        compiler_params=_cparams(1),
        name="mix_proj",
    )(sink, u, q, k, v, coef, invc, bias, wpool, pscale, x2, w_out, nw, w_router, b_router)


PLAN_TM = 512


TAB_EXPERT, TAB_FIRST, TAB_BUFFER, TAB_NEXT = 0, 1, 2, 3
TAB_ROWS = 4
ETAB_PAD_LO, ETAB_PAD_HI, ETAB_MISC = 0, 1, 2
ETAB_ROWS = 3
MISC_N_USED, MISC_RUN_EXPERT = 0, 1
FETCH_AHEAD = 2
TABLE_ROWS = SUBLANES


def _slot_tables(counts_ref, tab_ref, etab_ref, run_expert, run_of_block, pstart_s, n_blocks):
    log_block = MOE_BLOCK.bit_length() - 1

    def clear(r, carry):
        run_expert[r] = -1
        return carry
    lax.fori_loop(0, run_expert.shape[0], clear, 0)

    def per_expert(e, carry):
        slot, blk, run, buf = carry
        c = counts_ref[e]
        n_blk = lax.shift_right_logical(c + (MOE_BLOCK - 1), log_block)
        pstart_s[e] = slot
        etab_ref[ETAB_PAD_LO, e] = slot + c
        etab_ref[ETAB_PAD_HI, e] = slot + n_blk * MOE_BLOCK

        def per_block(b, inner):
            tab_ref[TAB_EXPERT, b] = e
            tab_ref[TAB_FIRST, b] = (b == blk).astype(jnp.int32)
            tab_ref[TAB_BUFFER, b] = buf
            run_of_block[b] = run
            return inner
        lax.fori_loop(blk, blk + n_blk, per_block, 0)

        @pl.when(n_blk > 0)
        def _():
            run_expert[run] = e
        has_run = (n_blk > 0).astype(jnp.int32)
        next_buf = jnp.where(buf + has_run == FETCH_AHEAD, 0, buf + has_run)
        return slot + n_blk * MOE_BLOCK, blk + n_blk, run + has_run, next_buf

    zero = jnp.int32(0)
    _, n_used, _, _ = lax.fori_loop(0, N_EXPERTS, per_expert, (zero, zero, zero, zero))

    def per_used_block(b, carry):
        tab_ref[TAB_NEXT, b] = run_expert[run_of_block[b] + FETCH_AHEAD]
        return carry
    lax.fori_loop(0, n_used, per_used_block, 0)

    def per_unused_block(b, carry):
        for row in (TAB_EXPERT, TAB_FIRST, TAB_BUFFER):
            tab_ref[row, b] = 0
        tab_ref[TAB_NEXT, b] = -1
        return carry
    lax.fori_loop(n_used, n_blocks, per_unused_block, 0)

    def clear_misc(e, carry):
        etab_ref[ETAB_MISC, e] = 0
        return carry
    lax.fori_loop(0, N_EXPERTS, clear_misc, 0)
    etab_ref[ETAB_MISC, MISC_N_USED] = n_used
    for r in range(FETCH_AHEAD):
        etab_ref[ETAB_MISC, MISC_RUN_EXPERT + r] = run_expert[r]


def _plan_kernel(counts_ref, route_ref, tri_ref, dest_ref, tab_ref, etab_ref,
                 carry_ref, pstart_ref, run_expert, run_of_block, pstart_s, *, n_blocks):
    @pl.when(pl.program_id(0) == 0)
    def _():
        carry_ref[...] = jnp.zeros_like(carry_ref)
        _slot_tables(counts_ref, tab_ref, etab_ref, run_expert, run_of_block, pstart_s, n_blocks)
        lane_row = lax.broadcasted_iota(jnp.int32, (1, LANES), 1)

        def place(e, row):
            return jnp.where(lane_row == e, pstart_s[e].astype(F32), row)
        pstart_ref[...] = lax.fori_loop(0, N_EXPERTS, place, jnp.zeros((1, LANES), F32))

    route = route_ref[...]
    lane = lax.broadcasted_iota(jnp.int32, route.shape, 1)
    e0 = route[:, ROUTE_E0:ROUTE_E0 + 1].astype(jnp.int32)
    e1 = route[:, ROUTE_E1:ROUTE_E1 + 1].astype(jnp.int32)
    oh0 = lane == e0
    oh1 = lane == e1
    both = (oh0 | oh1).astype(F32)
    earlier = jnp.dot(tri_ref[...], both.astype(BF16), preferred_element_type=F32)
    base = pstart_ref[...] + carry_ref[...] + earlier
    d0 = jnp.sum(jnp.where(oh0, base, 0.0), axis=-1, keepdims=True)
    d1 = jnp.sum(jnp.where(oh1, base, 0.0), axis=-1, keepdims=True)
    by_token = jnp.where(lane == 0, d0, jnp.where(lane == 1, d1, 0.0))
    dest_ref[...] = by_token.T[:TABLE_ROWS].astype(jnp.int32)
    carry_ref[...] += jnp.sum(both, axis=0, keepdims=True)


def _plan(counts, route, n_blocks):
    T = route.shape[0]
    tm = PLAN_TM
    tri = jnp.asarray(np.tril(np.ones((tm, tm), np.float32), -1), BF16)
    smem_out = pl.BlockSpec(memory_space=pltpu.SMEM)
    grid_spec = pltpu.PrefetchScalarGridSpec(
        num_scalar_prefetch=1,
        grid=(T // tm,),
        in_specs=[pl.BlockSpec((tm, LANES), lambda i, c: (i, 0)),
                  pl.BlockSpec((tm, tm), lambda i, c: (0, 0))],
        out_specs=[pl.BlockSpec((TABLE_ROWS, tm), lambda i, c: (0, i)), smem_out, smem_out],
        scratch_shapes=[pltpu.VMEM((1, LANES), F32), pltpu.VMEM((1, LANES), F32),
                        pltpu.SMEM((N_EXPERTS + FETCH_AHEAD,), jnp.int32),
                        pltpu.SMEM((n_blocks,), jnp.int32),
                        pltpu.SMEM((N_EXPERTS,), jnp.int32)],
    )
    return pl.pallas_call(
        functools.partial(_plan_kernel, n_blocks=n_blocks),
        grid_spec=grid_spec,
        out_shape=[jax.ShapeDtypeStruct((TABLE_ROWS, T), jnp.int32),
                   jax.ShapeDtypeStruct((TAB_ROWS, n_blocks), jnp.int32),
                   jax.ShapeDtypeStruct((ETAB_ROWS, N_EXPERTS), jnp.int32)],
        compiler_params=_cparams(1),
        name="plan",
    )(counts, route, tri)


DISP_TM = 512
DMA_GROUP = 8
ROW_WORDS = SUBLANES


def _packed_row(ref, row):
    return ref.at[pl.ds(pl.multiple_of(row * ROW_WORDS, ROW_WORDS), ROW_WORDS)]


def _grouped(n, fn):
    def trip(t, carry):
        for k in range(DMA_GROUP):
            fn(t * DMA_GROUP + k, k)
        return carry
    lax.fori_loop(0, n // DMA_GROUP, trip, 0)


def _dispatch_kernel(etab_ref, dest0_ref, dest1_ref, xn_ref, xs_hbm,
                     zeros, sem, tail_sem, *, experts_per_step, tails_per_step, n_blocks):
    i = pl.program_id(0)
    n_rows = TOP_K * DISP_TM

    @pl.when(i == 0)
    def _():
        zeros[...] = jnp.zeros_like(zeros)

    def scatter_group(t, carry):
        base = pl.multiple_of(t * DMA_GROUP, DMA_GROUP)
        window = xn_ref.at[pl.ds(base * ROW_WORDS, DMA_GROUP * ROW_WORDS)]
        for k in range(DMA_GROUP):
            src = window.at[pl.ds(k * ROW_WORDS, ROW_WORDS)]
            for choice, dest_ref in enumerate((dest0_ref, dest1_ref)):
                dst = _packed_row(xs_hbm, dest_ref[0, 0, 0, base + k])
                pltpu.make_async_copy(src, dst, sem).start(priority=choice)
        return carry

    lax.fori_loop(0, DISP_TM // DMA_GROUP, scatter_group, 0)

    def pad_fill(start):
        for j in range(experts_per_step):
            e = jnp.minimum(i * experts_per_step + j, N_EXPERTS - 1)
            live = i * experts_per_step + j < N_EXPERTS
            pos = etab_ref[ETAB_PAD_LO, e]
            n_pad = jnp.where(live, etab_ref[ETAB_PAD_HI, e] - pos, 0)
            for bit in reversed(range(MOE_BLOCK.bit_length() - 1)):
                size = 1 << bit
                take = jnp.bitwise_and(lax.shift_right_logical(n_pad, bit), 1)
                dst = xs_hbm.at[pl.ds(pl.multiple_of(pos * ROW_WORDS, ROW_WORDS), size * ROW_WORDS)]
                cp = pltpu.make_async_copy(zeros.at[pl.ds(0, size * ROW_WORDS)], dst, sem)

                @pl.when(take == 1)
                def _():
                    cp.start() if start else cp.wait()
                pos = pos + take * size

    def tail_copies(start):
        for j in range(tails_per_step):
            blk = etab_ref[ETAB_MISC, MISC_N_USED] + i * tails_per_step + j
            rows = pl.ds(pl.multiple_of(jnp.minimum(blk, n_blocks - 1) * (MOE_BLOCK * ROW_WORDS),
                                        MOE_BLOCK * ROW_WORDS), MOE_BLOCK * ROW_WORDS)
            cp = pltpu.make_async_copy(zeros, xs_hbm.at[rows], tail_sem)

            @pl.when(blk < n_blocks)
            def _():
                cp.start() if start else cp.wait()

    pad_fill(True)
    tail_copies(True)

    tile_rows = DISP_TM * ROW_WORDS
    for _ in range(TOP_K):
        pltpu.make_async_copy(xn_ref, xs_hbm.at[pl.ds(0, tile_rows)], sem).wait()
    pad_fill(False)
    tail_copies(False)


def _slot_tiles(dest, tile):
    return dest.reshape(dest.shape[0], dest.shape[1] // tile, 1, tile)


def _slot_spec(tile, choice, index):
    return pl.BlockSpec((1, 1, 1, tile), lambda i, *_: (choice, index(i), 0, 0),
                        memory_space=pltpu.SMEM)


def _dispatch(etab, dest, xn_packed, n_blocks):
    dest4 = _slot_tiles(dest, DISP_TM)
    n = dest4.shape[1]
    min_used = (n * DISP_TM * TOP_K) // MOE_BLOCK
    tails_per_step = -(-(n_blocks - min_used) // n)
    grid_spec = pltpu.PrefetchScalarGridSpec(
        num_scalar_prefetch=1,
        grid=(n,),
        in_specs=[_slot_spec(DISP_TM, 0, lambda i: i), _slot_spec(DISP_TM, 1, lambda i: i),
                  pl.BlockSpec((DISP_TM * ROW_WORDS, LANES), lambda i, *_: (i, 0))],
        out_specs=pl.BlockSpec(memory_space=pl.ANY),
        scratch_shapes=[pltpu.VMEM((MOE_BLOCK * ROW_WORDS, LANES), U32),
                        pltpu.SemaphoreType.DMA(()), pltpu.SemaphoreType.DMA(())],
    )
    return pl.pallas_call(
        functools.partial(_dispatch_kernel, experts_per_step=-(-N_EXPERTS // n),
                          tails_per_step=tails_per_step, n_blocks=n_blocks),
        grid_spec=grid_spec,
        out_shape=jax.ShapeDtypeStruct((n_blocks * MOE_BLOCK * ROW_WORDS, LANES), U32),
        compiler_params=_cparams(1),
        name="dispatch",
    )(etab, dest4, dest4, xn_packed)


CAST_ROWS = 256
FETCH_SPLIT = 4
BLOCKS_PER_STEP = 4


def _experts_kernel(tab_ref, etab_ref,
                    xs_ref, wg_hbm, wu_hbm, wd_hbm, ys_ref,
                    fg, fu, fd, wg, wu, wd, sem):
    n_used = etab_ref[ETAB_MISC, MISC_N_USED]
    for j in range(BLOCKS_PER_STEP):
        rows = pl.ds(j * MOE_BLOCK * SUBLANES, MOE_BLOCK * SUBLANES)
        _expert_block(pl.program_id(0) * BLOCKS_PER_STEP + j, n_used, tab_ref, etab_ref,
                      xs_ref.at[rows], wg_hbm, wu_hbm, wd_hbm, ys_ref.at[rows],
                      fg, fu, fd, wg, wu, wd, sem)


def _expert_block(b, n_used, tab_ref, etab_ref, xs_ref, wg_hbm, wu_hbm, wd_hbm, ys_ref,
                  fg, fu, fd, wg, wu, wd, sem):

    def fetch(e, p):
        copies = []
        for k, (src, dst) in enumerate(((wg_hbm, fg), (wu_hbm, fu), (wd_hbm, fd))):
            rows = src.shape[1] // FETCH_SPLIT
            for c in range(FETCH_SPLIT):
                sl = pl.ds(c * rows, rows)
                copies.append(pltpu.make_async_copy(src.at[e, sl], dst.at[p, sl], sem.at[p, k]))
        return copies

    def start_fetch(e, p):
        for k, cp in enumerate(fetch(e, p)):
            cp.start(priority=k % 2)

    @pl.when(b == 0)
    def _():
        for r in range(FETCH_AHEAD):
            e = etab_ref[ETAB_MISC, MISC_RUN_EXPERT + r]

            @pl.when(e >= 0)
            def _():
                start_fetch(e, r)

    @pl.when((b < n_used) & (tab_ref[TAB_FIRST, b] == 1))
    def _():
        p = tab_ref[TAB_BUFFER, b]
        for cp in fetch(tab_ref[TAB_EXPERT, b], p):
            cp.wait()

        def cast(src, dst, n_rows):
            def body(i, carry):
                rows = pl.ds(pl.multiple_of(i * CAST_ROWS, CAST_ROWS), CAST_ROWS)
                dst[rows, :] = src[p, rows, :].astype(BF16)
                return carry
            lax.fori_loop(0, n_rows // CAST_ROWS, body, 0)

        cast(fg, wg, D_MODEL)
        cast(fu, wu, D_MODEL)
        cast(fd, wd, D_EXPERT)

        @pl.when(tab_ref[TAB_NEXT, b] >= 0)
        def _():
            start_fetch(tab_ref[TAB_NEXT, b], p)

    @pl.when(b < n_used)
    def _():
        lo_parts, hi_parts = [], []
        for c in range(PACK_CHUNKS):
            lo, hi = _unpack_pair(xs_ref[pl.ds(c, MOE_BLOCK, stride=SUBLANES), :])
            lo_parts.append(lo.astype(BF16))
            hi_parts.append(hi.astype(BF16))
        xb = jnp.concatenate(lo_parts + hi_parts, axis=1)
        g = jnp.dot(xb, wg[...], preferred_element_type=F32)
        u = jnp.dot(xb, wu[...], preferred_element_type=F32)
        hmid = (g * jax.nn.sigmoid(g) * u).astype(BF16)
        y = jnp.dot(hmid, wd[...], preferred_element_type=F32)
        packed = _pack_pair(y[:, :HALF], y[:, HALF:])
        for c in range(PACK_CHUNKS):
            ys_ref[pl.ds(c, MOE_BLOCK, stride=SUBLANES), :] = packed[:, c * LANES:(c + 1) * LANES]

    @pl.when(b >= n_used)
    def _():
        ys_ref[...] = jnp.zeros(ys_ref.shape, ys_ref.dtype)


def _experts(tab, etab, xs, w_gate, w_up, w_down):
    n_blocks = tab.shape[1]
    assert n_blocks % BLOCKS_PER_STEP == 0
    blk = lambda: pl.BlockSpec((BLOCKS_PER_STEP * MOE_BLOCK * SUBLANES, LANES),
                               lambda s, *_: (s, 0))
    grid_spec = pltpu.PrefetchScalarGridSpec(
        num_scalar_prefetch=2,
        grid=(n_blocks // BLOCKS_PER_STEP,),
        in_specs=[blk(),
                  pl.BlockSpec(memory_space=pl.ANY),
                  pl.BlockSpec(memory_space=pl.ANY),
                  pl.BlockSpec(memory_space=pl.ANY)],
        out_specs=blk(),
        scratch_shapes=[pltpu.VMEM((FETCH_AHEAD, D_MODEL, D_EXPERT), F32),
                        pltpu.VMEM((FETCH_AHEAD, D_MODEL, D_EXPERT), F32),
                        pltpu.VMEM((FETCH_AHEAD, D_EXPERT, D_MODEL), F32),
                        pltpu.VMEM((D_MODEL, D_EXPERT), BF16),
                        pltpu.VMEM((D_MODEL, D_EXPERT), BF16),
                        pltpu.VMEM((D_EXPERT, D_MODEL), BF16),
                        pltpu.SemaphoreType.DMA((FETCH_AHEAD, 3))],
    )
    return pl.pallas_call(
        _experts_kernel,
        grid_spec=grid_spec,
        out_shape=jax.ShapeDtypeStruct((n_blocks * MOE_BLOCK * SUBLANES, LANES), U32),
        compiler_params=_cparams(1),
        name="experts",
    )(tab, etab, xs, w_gate, w_up, w_down)


CMB_TM = 256


def _combine_kernel(d0_cur_ref, d1_cur_ref, d0_next_ref, d1_next_ref, h_ref, route_ref, ys_hbm,
                    o_ref, ybuf, sem):
    i = pl.program_id(0)
    n = pl.num_programs(0)
    slot = i % 2
    n_rows = TOP_K * CMB_TM

    def gather(idx_refs, s):
        def gather_group(t, carry):
            base = pl.multiple_of(t * DMA_GROUP, DMA_GROUP)
            for choice, idx_ref in enumerate(idx_refs):
                first = (choice * CMB_TM + base) * ROW_WORDS
                window = ybuf.at[s, pl.ds(first, DMA_GROUP * ROW_WORDS)]
                for k in range(DMA_GROUP):
                    pltpu.make_async_copy(_packed_row(ys_hbm, idx_ref[0, 0, 0, base + k]),
                                          window.at[pl.ds(k * ROW_WORDS, ROW_WORDS)],
                                          sem.at[s]).start(priority=k % 2)
            return carry
        lax.fori_loop(0, CMB_TM // DMA_GROUP, gather_group, 0)

    @pl.when(i == 0)
    def _():
        gather((d0_cur_ref, d1_cur_ref), 0)

    @pl.when(i + 1 < n)
    def _():
        gather((d0_next_ref, d1_next_ref), 1 - slot)

    pltpu.make_async_copy(ys_hbm.at[pl.ds(0, n_rows * ROW_WORDS)], ybuf.at[slot],
                          sem.at[slot]).wait()
    route = route_ref[...]
    w0 = route[:, ROUTE_W0:ROUTE_W0 + 1]
    w1 = route[:, ROUTE_W1:ROUTE_W1 + 1]
    yb = ybuf.at[slot]
    for c in range(PACK_CHUNKS):
        lo0, hi0 = _unpack_pair(yb[pl.ds(c, CMB_TM, stride=SUBLANES), :])
        lo1, hi1 = _unpack_pair(yb[pl.ds(CMB_TM * SUBLANES + c, CMB_TM, stride=SUBLANES), :])
        lo_cols = slice(c * LANES, (c + 1) * LANES)
        hi_cols = slice(HALF + c * LANES, HALF + (c + 1) * LANES)
        o_ref[:, lo_cols] = h_ref[:, lo_cols] + (w0 * lo0 + w1 * lo1)
        o_ref[:, hi_cols] = h_ref[:, hi_cols] + (w0 * hi0 + w1 * hi1)


def _combine(dest, h, route, ys):
    T = h.shape[0]
    tm = CMB_TM
    n = T // tm
    dest4 = _slot_tiles(dest, tm)
    cur = lambda i: i
    nxt = lambda i: jnp.minimum(i + 1, n - 1)
    return pl.pallas_call(
        _combine_kernel,
        grid=(n,),
        in_specs=[
            _slot_spec(tm, 0, cur), _slot_spec(tm, 1, cur),
            _slot_spec(tm, 0, nxt), _slot_spec(tm, 1, nxt),
            pl.BlockSpec((tm, D_MODEL), lambda i: (i, 0)),
            pl.BlockSpec((tm, LANES), lambda i: (i, 0)),
            pl.BlockSpec(memory_space=pl.ANY),
        ],
        out_specs=pl.BlockSpec((tm, D_MODEL), lambda i: (i, 0)),
        out_shape=jax.ShapeDtypeStruct((T, D_MODEL), F32),
        scratch_shapes=[pltpu.VMEM((2, TOP_K * tm * SUBLANES, LANES), U32),
                        pltpu.SemaphoreType.DMA((2,))],
        compiler_params=_cparams(1),
        name="combine",
    )(dest4, dest4, dest4, dest4, h, route, ys)


def _rope_tables(S):
    inv = ROPE_THETA ** (-jnp.arange(0, HEAD_DIM, 2, dtype=F32) / HEAD_DIM)
    ang = jnp.arange(S, dtype=F32)[:, None] * inv[None, :]
    cos, sin = jnp.cos(ang), jnp.sin(ang)
    return jnp.concatenate([cos, cos], axis=-1), jnp.concatenate([-sin, sin], axis=-1)


def kernel(x, norm_mix_w, w_in, w_pool, pool_scale, q_norm_w, k_norm_w, sink_logits, w_out,
           norm_ffn_w, w_group_router, b_group_router, w_expert_router, b_expert_router,
           w_gate, w_up, w_down):
    B, S, D = x.shape
    T = B * S
    depth = w_in.shape[0]
    cos, sin = _rope_tables(S)
    coef_np, invc_np, bias_np = _band_constants(S)
    coef = jnp.asarray(coef_np, BF16)
    invc = jnp.asarray(invc_np, F32)
    bias = jnp.asarray(bias_np, F32)
    pad_lanes = LANES - N_GROUPS - N_EXPERTS

    h = x.reshape(T, D)
    for l in range(depth):
        u, q, k, v = _in_proj(h, norm_mix_w[l].reshape(1, D), w_in[l].astype(BF16),
                              q_norm_w[l].reshape(1, HEAD_DIM), k_norm_w[l].reshape(1, HEAD_DIM),
                              cos, sin, S)
        w_router = jnp.concatenate(
            [w_group_router[l], w_expert_router[l], jnp.zeros((D, pad_lanes), F32)], axis=1)
        b_router = jnp.concatenate(
            [b_group_router[l], b_expert_router[l], jnp.zeros((pad_lanes,), F32)]).reshape(1, LANES)
        hmix, xn_packed, route, counts = _mix_proj(
            sink_logits[l], u.reshape(B, S, -1), q.reshape(B, S, -1), k.reshape(B, S, -1),
            v.reshape(B, S, -1), coef, invc, bias, w_pool[l].astype(BF16),
            pool_scale[l].reshape(1, POOL_WIDTH), h, w_out[l].astype(BF16),
            norm_ffn_w[l].reshape(1, D), w_router.astype(BF16), b_router)
        n_blocks = -(-(T * TOP_K + N_EXPERTS * (MOE_BLOCK - 1)) // MOE_BLOCK)
        dest, tab, etab = _plan(counts[0].astype(jnp.int32), route, n_blocks)
        xs = _dispatch(etab, dest, xn_packed, n_blocks)
        ys = _experts(tab, etab, xs, w_gate[l], w_up[l], w_down[l])
        h = _combine(dest, hmix, route, ys)
    return h.reshape(B, S, D)
```

```python
import functools

import numpy as np
import jax
import jax.numpy as jnp
from jax import lax
from jax.experimental import pallas as pl
from jax.experimental.pallas import tpu as pltpu

D_MODEL = 2048
POOL_WIDTH = 1024
POOL_WINDOWS = (2, 4, 8, 16)
N_POOL_GROUPS = 4
POOL_GROUP = 256
HEAD_DIM = 128
N_Q_HEADS = 8
N_KV_HEADS = 2
Q_PER_KV = 4
ATTN_WIDTH = 1024
KV_WIDTH = 256
IN_PROJ_WIDTH = 2560
MIX_WIDTH = 2048
WINDOW = 128
BLOCK = 128
BAND = 3 * BLOCK
ROPE_THETA = 10000.0
N_GROUPS = 8
EXPERTS_PER_GROUP = 8
N_EXPERTS = 64
TOP_K = 2
D_EXPERT = 512
MOE_BLOCK = 256
EPS = 1e-6

LANES = 128
SUBLANES = 8
HALF = D_MODEL // 2
PACK_CHUNKS = HALF // LANES
NEG_BIG = -1e30
VMEM_LIMIT = 56 * 1024 * 1024

BF16 = jnp.bfloat16
F32 = jnp.float32
U32 = jnp.uint32
HI_MASK = 0xFFFF0000


def _cparams(n_axes):
    return pltpu.CompilerParams(dimension_semantics=("arbitrary",) * n_axes,
                                vmem_limit_bytes=VMEM_LIMIT)


def _pack_pair(lo, hi):
    lo_bits = lax.bitcast_convert_type(lo.astype(BF16).astype(F32), U32) >> 16
    hi_bits = lax.bitcast_convert_type(hi.astype(BF16).astype(F32), U32) & jnp.uint32(HI_MASK)
    return lo_bits | hi_bits


def _unpack_pair(words):
    lo = lax.bitcast_convert_type(words << 16, F32)
    hi = lax.bitcast_convert_type(words & jnp.uint32(HI_MASK), F32)
    return lo, hi


IN_TM = 512
ROW_SPLIT = 2


def _in_proj_kernel(x_ref, nw_ref, w_ref, qnw_ref, knw_ref, cos_ref, sin_ref,
                    u_ref, q_ref, k_ref, v_ref):
    sub = IN_TM // ROW_SPLIT
    for part in range(ROW_SPLIT):
        rows = slice(part * sub, (part + 1) * sub)
        x = x_ref[rows, :]
        ms = jnp.mean(x * x, axis=-1, keepdims=True)
        xn = (x * lax.rsqrt(ms + EPS) * nw_ref[...]).astype(BF16)
        cos = cos_ref[rows, :]
        sin = sin_ref[rows, :]

        def head_norm_rope(t, w, scale):
            hms = jnp.mean(t * t, axis=-1, keepdims=True)
            t = t * lax.rsqrt(hms + EPS) * w
            t = t * cos + pltpu.roll(t, HEAD_DIM // 2, axis=1) * sin
            return t * scale

        u_ref[rows, :] = jnp.dot(xn, w_ref[:, :POOL_WIDTH],
                                 preferred_element_type=F32).astype(BF16)
        zq = jnp.dot(xn, w_ref[:, POOL_WIDTH:POOL_WIDTH + ATTN_WIDTH],
                     preferred_element_type=F32)
        qnw = qnw_ref[...]
        for h in range(N_Q_HEADS):
            sl = slice(h * HEAD_DIM, (h + 1) * HEAD_DIM)
            q_ref[rows, sl] = head_norm_rope(zq[:, sl], qnw, HEAD_DIM ** -0.5).astype(BF16)
        o_k = POOL_WIDTH + ATTN_WIDTH
        zk = jnp.dot(xn, w_ref[:, o_k:o_k + KV_WIDTH], preferred_element_type=F32)
        knw = knw_ref[...]
        for h in range(N_KV_HEADS):
            sl = slice(h * HEAD_DIM, (h + 1) * HEAD_DIM)
            k_ref[rows, sl] = head_norm_rope(zk[:, sl], knw, 1.0).astype(BF16)
        v_ref[rows, :] = jnp.dot(xn, w_ref[:, o_k + KV_WIDTH:],
                                 preferred_element_type=F32).astype(BF16)


def _in_proj(x2, nw, w_in, qnw, knw, cos, sin, seq):
    T = x2.shape[0]
    tm = IN_TM
    pos_blocks = seq // tm
    full = lambda shape: pl.BlockSpec(shape, lambda i: (0,) * len(shape))
    return pl.pallas_call(
        _in_proj_kernel,
        grid=(T // tm,),
        in_specs=[
            pl.BlockSpec((tm, D_MODEL), lambda i: (i, 0)),
            full((1, D_MODEL)),
            full((D_MODEL, IN_PROJ_WIDTH)),
            full((1, HEAD_DIM)),
            full((1, HEAD_DIM)),
            pl.BlockSpec((tm, HEAD_DIM), lambda i: (i % pos_blocks, 0)),
            pl.BlockSpec((tm, HEAD_DIM), lambda i: (i % pos_blocks, 0)),
        ],
        out_specs=[
            pl.BlockSpec((tm, POOL_WIDTH), lambda i: (i, 0)),
            pl.BlockSpec((tm, ATTN_WIDTH), lambda i: (i, 0)),
            pl.BlockSpec((tm, KV_WIDTH), lambda i: (i, 0)),
            pl.BlockSpec((tm, KV_WIDTH), lambda i: (i, 0)),
        ],
        out_shape=[
            jax.ShapeDtypeStruct((T, POOL_WIDTH), BF16),
            jax.ShapeDtypeStruct((T, ATTN_WIDTH), BF16),
            jax.ShapeDtypeStruct((T, KV_WIDTH), BF16),
            jax.ShapeDtypeStruct((T, KV_WIDTH), BF16),
        ],
        compiler_params=_cparams(1),
        name="in_proj",
    )(x2, nw, w_in, qnw, knw, cos, sin)


MIX_TQ = 512


def _band_constants(seq):
    nb = seq // BLOCK
    coef = np.zeros((3, N_POOL_GROUPS, BLOCK, BAND), np.float32)
    inv_count = np.zeros((3, N_POOL_GROUPS, BLOCK, 1), np.float32)
    bias = np.zeros((3, BLOCK, BAND), np.float32)
    for kind, n in enumerate((0, 1, nb - 1)):
        start = min(max((n - 1) * BLOCK, 0), seq - BAND)
        t = n * BLOCK + np.arange(BLOCK)[:, None]
        s = start + np.arange(BAND)[None, :]
        bias[kind] = np.where(np.abs(s - t) <= WINDOW, 0.0, NEG_BIG)
        for g, win in enumerate(POOL_WINDOWS):
            half = win // 2
            lo = np.clip(t - half, 0, seq)
            hi = np.clip(t + half, 0, seq)
            count = (hi - lo).astype(np.float32)
            inside = ((s >= lo) & (s < hi)).astype(np.float32)
            coef[kind, g] = inside - count * (s == t)
            inv_count[kind, g] = 1.0 / count
    return coef, inv_count, bias


def _mixers_kernel(sink_ref, u_ref, q_ref, k_ref, v_ref, coef_ref, invc_ref, bias_ref,
                   wpool_ref, pscale_ref, ab_ref, *, seq):
    nb = seq // BLOCK
    j = pl.program_id(1)
    n_sub = MIX_TQ // BLOCK
    subs = []
    for r in range(n_sub):
        n = j * n_sub + r
        start = pl.multiple_of(jnp.clip((n - 1) * BLOCK, 0, seq - BAND), BLOCK)
        kind = jnp.where(n == 0, 0, jnp.where(n == nb - 1, 2, 1))
        subs.append((start, kind, slice(r * BLOCK, (r + 1) * BLOCK)))

    scores = {}
    for r, (start, kind, rows) in enumerate(subs):
        for hk in range(N_KV_HEADS):
            kb = k_ref[0, pl.ds(start, BAND), hk * HEAD_DIM:(hk + 1) * HEAD_DIM]
            qs = jnp.concatenate(
                [q_ref[0, rows, h * HEAD_DIM:(h + 1) * HEAD_DIM]
                 for h in range(hk * Q_PER_KV, (hk + 1) * Q_PER_KV)], axis=0)
            scores[r, hk] = lax.dot_general(qs, kb, (((1,), (1,)), ((), ())),
                                            preferred_element_type=F32)
    diffs = {}
    for r, (start, kind, rows) in enumerate(subs):
        for g in range(N_POOL_GROUPS):
            ub = u_ref[0, pl.ds(start, BAND), g * POOL_GROUP:(g + 1) * POOL_GROUP]
            d = jnp.dot(coef_ref[kind, g], ub, preferred_element_type=F32) * invc_ref[kind, g]
            diffs[r, g] = d.astype(BF16)
    probs = {}
    for r, (start, kind, rows) in enumerate(subs):
        bias = bias_ref[kind]
        for hk in range(N_KV_HEADS):
            for gi in range(Q_PER_KV):
                h = hk * Q_PER_KV + gi
                sh = scores[r, hk][gi * BLOCK:(gi + 1) * BLOCK] + bias
                sink = sink_ref[h]
                m = jnp.maximum(jnp.max(sh, axis=-1, keepdims=True), sink)
                p = jnp.exp(sh - m)
                denom = jnp.sum(p, axis=-1, keepdims=True) + jnp.exp(sink - m)
                probs[r, h] = (p.astype(BF16), denom)
    for r, (start, kind, rows) in enumerate(subs):
        for g in range(N_POOL_GROUPS):
            cols = slice(g * POOL_GROUP, (g + 1) * POOL_GROUP)
            y = jnp.dot(diffs[r, g], wpool_ref[g], preferred_element_type=F32)
            ab_ref[0, rows, cols] = (y * pscale_ref[:, cols]).astype(BF16)
    for r, (start, kind, rows) in enumerate(subs):
        for hk in range(N_KV_HEADS):
            vb = v_ref[0, pl.ds(start, BAND), hk * HEAD_DIM:(hk + 1) * HEAD_DIM]
            for gi in range(Q_PER_KV):
                h = hk * Q_PER_KV + gi
                p, denom = probs[r, h]
                o = jnp.dot(p, vb, preferred_element_type=F32) / denom
                ocols = slice(POOL_WIDTH + h * HEAD_DIM, POOL_WIDTH + (h + 1) * HEAD_DIM)
                ab_ref[0, rows, ocols] = o.astype(BF16)


def _mixers(sink, u, q, k, v, coef, invc, bias, wpool, pscale):
    B, S, _ = u.shape
    tq = MIX_TQ
    full = lambda shape: pl.BlockSpec(shape, lambda b, j: (0,) * len(shape))
    return pl.pallas_call(
        functools.partial(_mixers_kernel, seq=S),
        grid=(B, S // tq),
        in_specs=[
            pl.BlockSpec(memory_space=pltpu.SMEM),
            pl.BlockSpec((1, S, POOL_WIDTH), lambda b, j: (b, 0, 0)),
            pl.BlockSpec((1, tq, ATTN_WIDTH), lambda b, j: (b, j, 0)),
            pl.BlockSpec((1, S, KV_WIDTH), lambda b, j: (b, 0, 0)),
            pl.BlockSpec((1, S, KV_WIDTH), lambda b, j: (b, 0, 0)),
            full((3, N_POOL_GROUPS, BLOCK, BAND)),
            full((3, N_POOL_GROUPS, BLOCK, 1)),
            full((3, BLOCK, BAND)),
            full((N_POOL_GROUPS, POOL_GROUP, POOL_GROUP)),
            full((1, POOL_WIDTH)),
        ],
        out_specs=pl.BlockSpec((1, tq, MIX_WIDTH), lambda b, j: (b, j, 0)),
        out_shape=jax.ShapeDtypeStruct((B, S, MIX_WIDTH), BF16),
        compiler_params=_cparams(2),
        name="mixers",
    )(sink, u, q, k, v, coef, invc, bias, wpool, pscale)


OUT_TM = 512
ROUTE_E0, ROUTE_E1, ROUTE_W0, ROUTE_W1 = 0, 1, 2, 3


def _out_proj_kernel(ab_ref, x_ref, w_ref, nw_ref, wr_ref, br_ref,
                     h_ref, xn_ref, route_ref, counts_ref):
    @pl.when(pl.program_id(0) == 0)
    def _():
        counts_ref[...] = jnp.zeros_like(counts_ref)

    sub = OUT_TM // ROW_SPLIT
    lane = lax.broadcasted_iota(jnp.int32, (sub, LANES), 1)

    def first_argmax(vals):
        m = jnp.max(vals, axis=-1, keepdims=True)
        idx = jnp.min(jnp.where(vals == m, lane, LANES), axis=-1, keepdims=True)
        return m, idx

    new_counts = jnp.zeros((1, LANES), F32)
    for part in range(ROW_SPLIT):
        rows = slice(part * sub, (part + 1) * sub)
        h = x_ref[rows, :] + jnp.dot(ab_ref[rows, :], w_ref[...], preferred_element_type=F32)
        h_ref[rows, :] = h
        ms = jnp.mean(h * h, axis=-1, keepdims=True)
        xn = h * lax.rsqrt(ms + EPS) * nw_ref[...]
        packed = _pack_pair(xn[:, :HALF], xn[:, HALF:])
        for c in range(PACK_CHUNKS):
            xn_ref[pl.ds(part * sub * SUBLANES + c, sub, stride=SUBLANES), :] = (
                packed[:, c * LANES:(c + 1) * LANES])
        logits = jnp.dot(xn.astype(BF16), wr_ref[...], preferred_element_type=F32) + br_ref[...]

        gl = jnp.where(lane < N_GROUPS, logits, NEG_BIG)
        gmax, gidx = first_argmax(gl)
        gsum = jnp.sum(jnp.where(lane < N_GROUPS, jnp.exp(gl - gmax), 0.0),
                       axis=-1, keepdims=True)
        g_w = 1.0 / gsum
        e_lo = N_GROUPS + gidx * EXPERTS_PER_GROUP
        el = jnp.where((lane >= e_lo) & (lane < e_lo + EXPERTS_PER_GROUP), logits, NEG_BIG)
        m1, i1 = first_argmax(el)
        m2, i2 = first_argmax(jnp.where(lane == i1, NEG_BIG, el))
        t = jnp.exp(m2 - m1)
        p1 = 1.0 / (1.0 + t)
        p2 = t * p1
        e1 = i1 - N_GROUPS
        e2 = i2 - N_GROUPS
        route_ref[rows, :] = jnp.where(
            lane == ROUTE_E0, e1.astype(F32),
            jnp.where(lane == ROUTE_E1, e2.astype(F32),
                      jnp.where(lane == ROUTE_W0, g_w * p1,
                                jnp.where(lane == ROUTE_W1, g_w * p2, 0.0))))
        chosen = ((lane == e1) | (lane == e2)).astype(F32)
        new_counts = new_counts + jnp.sum(chosen, axis=0, keepdims=True)
    counts_ref[...] += new_counts


def _out_proj(ab, x2, w_out, nw, w_router, b_router):
    T = x2.shape[0]
    tm = OUT_TM
    full = lambda shape: pl.BlockSpec(shape, lambda i: (0,) * len(shape))
    row = lambda width: pl.BlockSpec((tm, width), lambda i: (i, 0))
    return pl.pallas_call(
        _out_proj_kernel,
        grid=(T // tm,),
        in_specs=[row(MIX_WIDTH), row(D_MODEL), full((MIX_WIDTH, D_MODEL)), full((1, D_MODEL)),
                  full((D_MODEL, LANES)), full((1, LANES))],
        out_specs=[row(D_MODEL),
                   pl.BlockSpec((tm * SUBLANES, LANES), lambda i: (i, 0)),
                   row(LANES),
                   full((1, LANES))],
        out_shape=[jax.ShapeDtypeStruct((T, D_MODEL), F32),
                   jax.ShapeDtypeStruct((T * SUBLANES, LANES), U32),
                   jax.ShapeDtypeStruct((T, LANES), F32),
                   jax.ShapeDtypeStruct((1, LANES), F32)],
        compiler_params=_cparams(1),
        name="out_proj",
    )(ab, x2, w_out, nw, w_router, b_router)


PLAN_TM = 1024


TAB_EXPERT, TAB_FIRST, TAB_BUFFER, TAB_NEXT = 0, 1, 2, 3
TAB_ROWS = 4
ETAB_PAD_LO, ETAB_PAD_HI, ETAB_MISC = 0, 1, 2
ETAB_ROWS = 3
MISC_N_USED, MISC_RUN_EXPERT = 0, 1
FETCH_AHEAD = 2
TABLE_ROWS = SUBLANES


def _slot_tables(counts_ref, tab_ref, etab_ref, run_expert, run_of_block, pstart_s, n_blocks):
    log_block = MOE_BLOCK.bit_length() - 1

    def clear(r, carry):
        run_expert[r] = -1
        return carry
    lax.fori_loop(0, run_expert.shape[0], clear, 0)

    def per_expert(e, carry):
        slot, blk, run, buf = carry
        c = counts_ref[e]
        n_blk = lax.shift_right_logical(c + (MOE_BLOCK - 1), log_block)
        pstart_s[e] = slot
        etab_ref[ETAB_PAD_LO, e] = slot + c
        etab_ref[ETAB_PAD_HI, e] = slot + n_blk * MOE_BLOCK

        def per_block(b, inner):
            tab_ref[TAB_EXPERT, b] = e
            tab_ref[TAB_FIRST, b] = (b == blk).astype(jnp.int32)
            tab_ref[TAB_BUFFER, b] = buf
            run_of_block[b] = run
            return inner
        lax.fori_loop(blk, blk + n_blk, per_block, 0)

        @pl.when(n_blk > 0)
        def _():
            run_expert[run] = e
        has_run = (n_blk > 0).astype(jnp.int32)
        next_buf = jnp.where(buf + has_run == FETCH_AHEAD, 0, buf + has_run)
        return slot + n_blk * MOE_BLOCK, blk + n_blk, run + has_run, next_buf

    zero = jnp.int32(0)
    _, n_used, _, _ = lax.fori_loop(0, N_EXPERTS, per_expert, (zero, zero, zero, zero))

    def per_used_block(b, carry):
        tab_ref[TAB_NEXT, b] = run_expert[run_of_block[b] + FETCH_AHEAD]
        return carry
    lax.fori_loop(0, n_used, per_used_block, 0)

    def per_unused_block(b, carry):
        for row in (TAB_EXPERT, TAB_FIRST, TAB_BUFFER):
            tab_ref[row, b] = 0
        tab_ref[TAB_NEXT, b] = -1
        return carry
    lax.fori_loop(n_used, n_blocks, per_unused_block, 0)

    def clear_misc(e, carry):
        etab_ref[ETAB_MISC, e] = 0
        return carry
    lax.fori_loop(0, N_EXPERTS, clear_misc, 0)
    etab_ref[ETAB_MISC, MISC_N_USED] = n_used
    for r in range(FETCH_AHEAD):
        etab_ref[ETAB_MISC, MISC_RUN_EXPERT + r] = run_expert[r]


def _plan_kernel(counts_ref, route_ref, tri_ref, dest_ref, tab_ref, etab_ref,
                 carry_ref, pstart_ref, run_expert, run_of_block, pstart_s, *, n_blocks):
    @pl.when(pl.program_id(0) == 0)
    def _():
        carry_ref[...] = jnp.zeros_like(carry_ref)
        _slot_tables(counts_ref, tab_ref, etab_ref, run_expert, run_of_block, pstart_s, n_blocks)
        lane_row = lax.broadcasted_iota(jnp.int32, (1, LANES), 1)

        def place(e, row):
            return jnp.where(lane_row == e, pstart_s[e].astype(F32), row)
        pstart_ref[...] = lax.fori_loop(0, N_EXPERTS, place, jnp.zeros((1, LANES), F32))

    route = route_ref[...]
    lane = lax.broadcasted_iota(jnp.int32, route.shape, 1)
    e0 = route[:, ROUTE_E0:ROUTE_E0 + 1].astype(jnp.int32)
    e1 = route[:, ROUTE_E1:ROUTE_E1 + 1].astype(jnp.int32)
    oh0 = lane == e0
    oh1 = lane == e1
    both = (oh0 | oh1).astype(F32)
    earlier = jnp.dot(tri_ref[...], both.astype(BF16), preferred_element_type=F32)
    base = pstart_ref[...] + carry_ref[...] + earlier
    d0 = jnp.sum(jnp.where(oh0, base, 0.0), axis=-1, keepdims=True)
    d1 = jnp.sum(jnp.where(oh1, base, 0.0), axis=-1, keepdims=True)
    by_token = jnp.where(lane == 0, d0, jnp.where(lane == 1, d1, 0.0))
    dest_ref[...] = by_token.T[:TABLE_ROWS].astype(jnp.int32)
    carry_ref[...] += jnp.sum(both, axis=0, keepdims=True)


def _plan(counts, route, n_blocks):
    T = route.shape[0]
    tm = PLAN_TM
    tri = jnp.asarray(np.tril(np.ones((tm, tm), np.float32), -1), BF16)
    smem_out = pl.BlockSpec(memory_space=pltpu.SMEM)
    grid_spec = pltpu.PrefetchScalarGridSpec(
        num_scalar_prefetch=1,
        grid=(T // tm,),
        in_specs=[pl.BlockSpec((tm, LANES), lambda i, c: (i, 0)),
                  pl.BlockSpec((tm, tm), lambda i, c: (0, 0))],
        out_specs=[pl.BlockSpec((TABLE_ROWS, tm), lambda i, c: (0, i)), smem_out, smem_out],
        scratch_shapes=[pltpu.VMEM((1, LANES), F32), pltpu.VMEM((1, LANES), F32),
                        pltpu.SMEM((N_EXPERTS + FETCH_AHEAD,), jnp.int32),
                        pltpu.SMEM((n_blocks,), jnp.int32),
                        pltpu.SMEM((N_EXPERTS,), jnp.int32)],
    )
    return pl.pallas_call(
        functools.partial(_plan_kernel, n_blocks=n_blocks),
        grid_spec=grid_spec,
        out_shape=[jax.ShapeDtypeStruct((TABLE_ROWS, T), jnp.int32),
                   jax.ShapeDtypeStruct((TAB_ROWS, n_blocks), jnp.int32),
                   jax.ShapeDtypeStruct((ETAB_ROWS, N_EXPERTS), jnp.int32)],
        compiler_params=_cparams(1),
        name="plan",
    )(counts, route, tri)


DISP_TM = 1024
DMA_GROUP = 8
ROW_WORDS = SUBLANES


def _packed_row(ref, row):
    return ref.at[pl.ds(pl.multiple_of(row * ROW_WORDS, ROW_WORDS), ROW_WORDS)]


def _dispatch_kernel(etab_ref, dest0_ref, dest1_ref, xn_ref, xs_hbm,
                     zeros, sem, tail_sem, *, experts_per_step, tails_per_step, n_blocks):
    i = pl.program_id(0)

    @pl.when(i == 0)
    def _():
        zeros[...] = jnp.zeros_like(zeros)

    def scatter_group(t, carry):
        base = pl.multiple_of(t * DMA_GROUP, DMA_GROUP)
        window = xn_ref.at[pl.ds(base * ROW_WORDS, DMA_GROUP * ROW_WORDS)]
        for k in range(DMA_GROUP):
            src = window.at[pl.ds(k * ROW_WORDS, ROW_WORDS)]
            for choice, dest_ref in enumerate((dest0_ref, dest1_ref)):
                dst = _packed_row(xs_hbm, dest_ref[0, 0, 0, base + k])
                pltpu.make_async_copy(src, dst, sem).start(priority=choice)
        return carry

    lax.fori_loop(0, DISP_TM // DMA_GROUP, scatter_group, 0)

    def pad_fill(start):
        for j in range(experts_per_step):
            e = jnp.minimum(i * experts_per_step + j, N_EXPERTS - 1)
            live = i * experts_per_step + j < N_EXPERTS
            pos = etab_ref[ETAB_PAD_LO, e]
            n_pad = jnp.where(live, etab_ref[ETAB_PAD_HI, e] - pos, 0)
            for bit in reversed(range(MOE_BLOCK.bit_length() - 1)):
                size = 1 << bit
                take = jnp.bitwise_and(lax.shift_right_logical(n_pad, bit), 1)
                dst = xs_hbm.at[pl.ds(pl.multiple_of(pos * ROW_WORDS, ROW_WORDS), size * ROW_WORDS)]
                cp = pltpu.make_async_copy(zeros.at[pl.ds(0, size * ROW_WORDS)], dst, sem)

                @pl.when(take == 1)
                def _():
                    cp.start() if start else cp.wait()
                pos = pos + take * size

    def tail_copies(start):
        for j in range(tails_per_step):
            blk = etab_ref[ETAB_MISC, MISC_N_USED] + i * tails_per_step + j
            rows = pl.ds(pl.multiple_of(jnp.minimum(blk, n_blocks - 1) * (MOE_BLOCK * ROW_WORDS),
                                        MOE_BLOCK * ROW_WORDS), MOE_BLOCK * ROW_WORDS)
            cp = pltpu.make_async_copy(zeros, xs_hbm.at[rows], tail_sem)

            @pl.when(blk < n_blocks)
            def _():
                cp.start() if start else cp.wait()

    pad_fill(True)
    tail_copies(True)

    tile_rows = DISP_TM * ROW_WORDS
    for _ in range(TOP_K):
        pltpu.make_async_copy(xn_ref, xs_hbm.at[pl.ds(0, tile_rows)], sem).wait()
    pad_fill(False)
    tail_copies(False)


def _slot_tiles(dest, tile):
    return dest.reshape(dest.shape[0], dest.shape[1] // tile, 1, tile)


def _slot_spec(tile, choice, index):
    return pl.BlockSpec((1, 1, 1, tile), lambda i, *_: (choice, index(i), 0, 0),
                        memory_space=pltpu.SMEM)


def _dispatch(etab, dest, xn_packed, n_blocks):
    dest4 = _slot_tiles(dest, DISP_TM)
    n = dest4.shape[1]
    min_used = (n * DISP_TM * TOP_K) // MOE_BLOCK
    tails_per_step = -(-(n_blocks - min_used) // n)
    grid_spec = pltpu.PrefetchScalarGridSpec(
        num_scalar_prefetch=1,
        grid=(n,),
        in_specs=[_slot_spec(DISP_TM, 0, lambda i: i), _slot_spec(DISP_TM, 1, lambda i: i),
                  pl.BlockSpec((DISP_TM * ROW_WORDS, LANES), lambda i, *_: (i, 0))],
        out_specs=pl.BlockSpec(memory_space=pl.ANY),
        scratch_shapes=[pltpu.VMEM((MOE_BLOCK * ROW_WORDS, LANES), U32),
                        pltpu.SemaphoreType.DMA(()), pltpu.SemaphoreType.DMA(())],
    )
    return pl.pallas_call(
        functools.partial(_dispatch_kernel, experts_per_step=-(-N_EXPERTS // n),
                          tails_per_step=tails_per_step, n_blocks=n_blocks),
        grid_spec=grid_spec,
        out_shape=jax.ShapeDtypeStruct((n_blocks * MOE_BLOCK * ROW_WORDS, LANES), U32),
        compiler_params=_cparams(1),
        name="dispatch",
    )(etab, dest4, dest4, xn_packed)


CAST_ROWS = 256
FETCH_SPLIT = 4
BLOCKS_PER_STEP = 4


def _experts_kernel(tab_ref, etab_ref,
                    xs_ref, wg_hbm, wu_hbm, wd_hbm, ys_ref,
                    fg, fu, fd, wg, wu, wd, sem):
    n_used = etab_ref[ETAB_MISC, MISC_N_USED]
    for j in range(BLOCKS_PER_STEP):
        rows = pl.ds(j * MOE_BLOCK * SUBLANES, MOE_BLOCK * SUBLANES)
        _expert_block(pl.program_id(0) * BLOCKS_PER_STEP + j, n_used, tab_ref, etab_ref,
                      xs_ref.at[rows], wg_hbm, wu_hbm, wd_hbm, ys_ref.at[rows],
                      fg, fu, fd, wg, wu, wd, sem)


def _expert_block(b, n_used, tab_ref, etab_ref, xs_ref, wg_hbm, wu_hbm, wd_hbm, ys_ref,
                  fg, fu, fd, wg, wu, wd, sem):

    def fetch(e, p):
        copies = []
        for k, (src, dst) in enumerate(((wg_hbm, fg), (wu_hbm, fu), (wd_hbm, fd))):
            rows = src.shape[1] // FETCH_SPLIT
            for c in range(FETCH_SPLIT):
                sl = pl.ds(c * rows, rows)
                copies.append(pltpu.make_async_copy(src.at[e, sl], dst.at[p, sl], sem.at[p, k]))
        return copies

    def start_fetch(e, p):
        for k, cp in enumerate(fetch(e, p)):
            cp.start(priority=k % 2)

    @pl.when(b == 0)
    def _():
        for r in range(FETCH_AHEAD):
            e = etab_ref[ETAB_MISC, MISC_RUN_EXPERT + r]

            @pl.when(e >= 0)
            def _():
                start_fetch(e, r)

    @pl.when((b < n_used) & (tab_ref[TAB_FIRST, b] == 1))
    def _():
        p = tab_ref[TAB_BUFFER, b]
        for cp in fetch(tab_ref[TAB_EXPERT, b], p):
            cp.wait()

        def cast(src, dst, n_rows):
            def body(i, carry):
                rows = pl.ds(pl.multiple_of(i * CAST_ROWS, CAST_ROWS), CAST_ROWS)
                dst[rows, :] = src[p, rows, :].astype(BF16)
                return carry
            lax.fori_loop(0, n_rows // CAST_ROWS, body, 0)

        cast(fg, wg, D_MODEL)
        cast(fu, wu, D_MODEL)
        cast(fd, wd, D_EXPERT)

        @pl.when(tab_ref[TAB_NEXT, b] >= 0)
        def _():
            start_fetch(tab_ref[TAB_NEXT, b], p)

    @pl.when(b < n_used)
    def _():
        lo_parts, hi_parts = [], []
        for c in range(PACK_CHUNKS):
            lo, hi = _unpack_pair(xs_ref[pl.ds(c, MOE_BLOCK, stride=SUBLANES), :])
            lo_parts.append(lo.astype(BF16))
            hi_parts.append(hi.astype(BF16))
        xb = jnp.concatenate(lo_parts + hi_parts, axis=1)
        g = jnp.dot(xb, wg[...], preferred_element_type=F32)
        u = jnp.dot(xb, wu[...], preferred_element_type=F32)
        hmid = (g * jax.nn.sigmoid(g) * u).astype(BF16)
        y = jnp.dot(hmid, wd[...], preferred_element_type=F32)
        packed = _pack_pair(y[:, :HALF], y[:, HALF:])
        for c in range(PACK_CHUNKS):
            ys_ref[pl.ds(c, MOE_BLOCK, stride=SUBLANES), :] = packed[:, c * LANES:(c + 1) * LANES]

    @pl.when(b >= n_used)
    def _():
        ys_ref[...] = jnp.zeros(ys_ref.shape, ys_ref.dtype)


def _experts(tab, etab, xs, w_gate, w_up, w_down):
    n_blocks = tab.shape[1]
    assert n_blocks % BLOCKS_PER_STEP == 0
    blk = lambda: pl.BlockSpec((BLOCKS_PER_STEP * MOE_BLOCK * SUBLANES, LANES),
                               lambda s, *_: (s, 0))
    grid_spec = pltpu.PrefetchScalarGridSpec(
        num_scalar_prefetch=2,
        grid=(n_blocks // BLOCKS_PER_STEP,),
        in_specs=[blk(),
                  pl.BlockSpec(memory_space=pl.ANY),
                  pl.BlockSpec(memory_space=pl.ANY),
                  pl.BlockSpec(memory_space=pl.ANY)],
        out_specs=blk(),
        scratch_shapes=[pltpu.VMEM((FETCH_AHEAD, D_MODEL, D_EXPERT), F32),
                        pltpu.VMEM((FETCH_AHEAD, D_MODEL, D_EXPERT), F32),
                        pltpu.VMEM((FETCH_AHEAD, D_EXPERT, D_MODEL), F32),
                        pltpu.VMEM((D_MODEL, D_EXPERT), BF16),
                        pltpu.VMEM((D_MODEL, D_EXPERT), BF16),
                        pltpu.VMEM((D_EXPERT, D_MODEL), BF16),
                        pltpu.SemaphoreType.DMA((FETCH_AHEAD, 3))],
    )
    return pl.pallas_call(
        _experts_kernel,
        grid_spec=grid_spec,
        out_shape=jax.ShapeDtypeStruct((n_blocks * MOE_BLOCK * SUBLANES, LANES), U32),
        compiler_params=_cparams(1),
        name="experts",
    )(tab, etab, xs, w_gate, w_up, w_down)


CMB_TM = 256


def _combine_kernel(d0_cur_ref, d1_cur_ref, d0_next_ref, d1_next_ref, h_ref, route_ref, ys_hbm,
                    o_ref, ybuf, sem):
    i = pl.program_id(0)
    n = pl.num_programs(0)
    slot = i % 2
    n_rows = TOP_K * CMB_TM

    def gather(idx_refs, s):
        def gather_group(t, carry):
            base = pl.multiple_of(t * DMA_GROUP, DMA_GROUP)
            for choice, idx_ref in enumerate(idx_refs):
                first = (choice * CMB_TM + base) * ROW_WORDS
                window = ybuf.at[s, pl.ds(first, DMA_GROUP * ROW_WORDS)]
                for k in range(DMA_GROUP):
                    pltpu.make_async_copy(_packed_row(ys_hbm, idx_ref[0, 0, 0, base + k]),
                                          window.at[pl.ds(k * ROW_WORDS, ROW_WORDS)],
                                          sem.at[s]).start(priority=k % 2)
            return carry
        lax.fori_loop(0, CMB_TM // DMA_GROUP, gather_group, 0)

    @pl.when(i == 0)
    def _():
        gather((d0_cur_ref, d1_cur_ref), 0)

    @pl.when(i + 1 < n)
    def _():
        gather((d0_next_ref, d1_next_ref), 1 - slot)

    pltpu.make_async_copy(ys_hbm.at[pl.ds(0, n_rows * ROW_WORDS)], ybuf.at[slot],
                          sem.at[slot]).wait()
    route = route_ref[...]
    w0 = route[:, ROUTE_W0:ROUTE_W0 + 1]
    w1 = route[:, ROUTE_W1:ROUTE_W1 + 1]
    yb = ybuf.at[slot]
    for c in range(PACK_CHUNKS):
        lo0, hi0 = _unpack_pair(yb[pl.ds(c, CMB_TM, stride=SUBLANES), :])
        lo1, hi1 = _unpack_pair(yb[pl.ds(CMB_TM * SUBLANES + c, CMB_TM, stride=SUBLANES), :])
        lo_cols = slice(c * LANES, (c + 1) * LANES)
        hi_cols = slice(HALF + c * LANES, HALF + (c + 1) * LANES)
        o_ref[:, lo_cols] = h_ref[:, lo_cols] + (w0 * lo0 + w1 * lo1)
        o_ref[:, hi_cols] = h_ref[:, hi_cols] + (w0 * hi0 + w1 * hi1)


def _combine(dest, h, route, ys):
    T = h.shape[0]
    tm = CMB_TM
    n = T // tm
    dest4 = _slot_tiles(dest, tm)
    cur = lambda i: i
    nxt = lambda i: jnp.minimum(i + 1, n - 1)
    return pl.pallas_call(
        _combine_kernel,
        grid=(n,),
        in_specs=[
            _slot_spec(tm, 0, cur), _slot_spec(tm, 1, cur),
            _slot_spec(tm, 0, nxt), _slot_spec(tm, 1, nxt),
            pl.BlockSpec((tm, D_MODEL), lambda i: (i, 0)),
            pl.BlockSpec((tm, LANES), lambda i: (i, 0)),
            pl.BlockSpec(memory_space=pl.ANY),
        ],
        out_specs=pl.BlockSpec((tm, D_MODEL), lambda i: (i, 0)),
        out_shape=jax.ShapeDtypeStruct((T, D_MODEL), F32),
        scratch_shapes=[pltpu.VMEM((2, TOP_K * tm * SUBLANES, LANES), U32),
                        pltpu.SemaphoreType.DMA((2,))],
        compiler_params=_cparams(1),
        name="combine",
    )(dest4, dest4, dest4, dest4, h, route, ys)


def _rope_tables(S):
    inv = ROPE_THETA ** (-jnp.arange(0, HEAD_DIM, 2, dtype=F32) / HEAD_DIM)
    ang = jnp.arange(S, dtype=F32)[:, None] * inv[None, :]
    cos, sin = jnp.cos(ang), jnp.sin(ang)
    return jnp.concatenate([cos, cos], axis=-1), jnp.concatenate([-sin, sin], axis=-1)


def kernel(x, norm_mix_w, w_in, w_pool, pool_scale, q_norm_w, k_norm_w, sink_logits, w_out,
           norm_ffn_w, w_group_router, b_group_router, w_expert_router, b_expert_router,
           w_gate, w_up, w_down):
    B, S, D = x.shape
    T = B * S
    depth = w_in.shape[0]
    cos, sin = _rope_tables(S)
    coef_np, invc_np, bias_np = _band_constants(S)
    coef = jnp.asarray(coef_np, BF16)
    invc = jnp.asarray(invc_np, F32)
    bias = jnp.asarray(bias_np, F32)
    pad_lanes = LANES - N_GROUPS - N_EXPERTS

    h = x.reshape(T, D)
    for l in range(depth):
        u, q, k, v = _in_proj(h, norm_mix_w[l].reshape(1, D), w_in[l].astype(BF16),
                              q_norm_w[l].reshape(1, HEAD_DIM), k_norm_w[l].reshape(1, HEAD_DIM),
                              cos, sin, S)
        ab = _mixers(sink_logits[l], u.reshape(B, S, -1), q.reshape(B, S, -1),
                     k.reshape(B, S, -1), v.reshape(B, S, -1), coef, invc, bias,
                     w_pool[l].astype(BF16), pool_scale[l].reshape(1, POOL_WIDTH))
        w_router = jnp.concatenate(
            [w_group_router[l], w_expert_router[l], jnp.zeros((D, pad_lanes), F32)], axis=1)
        b_router = jnp.concatenate(
            [b_group_router[l], b_expert_router[l], jnp.zeros((pad_lanes,), F32)]).reshape(1, LANES)
        hmix, xn_packed, route, counts = _out_proj(
            ab.reshape(T, MIX_WIDTH), h, w_out[l].astype(BF16), norm_ffn_w[l].reshape(1, D),
            w_router.astype(BF16), b_router)
        n_blocks = -(-(T * TOP_K + N_EXPERTS * (MOE_BLOCK - 1)) // MOE_BLOCK)
        dest, tab, etab = _plan(counts[0].astype(jnp.int32), route, n_blocks)
        xs = _dispatch(etab, dest, xn_packed, n_blocks)
        ys = _experts(tab, etab, xs, w_gate[l], w_up[l], w_down[l])
        h = _combine(dest, hmix, route, ys)
    return h.reshape(B, S, D)
```

```python
import functools

import numpy as np
import jax
import jax.numpy as jnp
from jax import lax
from jax.experimental import pallas as pl
from jax.experimental.pallas import tpu as pltpu

D_MODEL = 2048
POOL_WIDTH = 1024
POOL_WINDOWS = (2, 4, 8, 16)
N_POOL_GROUPS = 4
POOL_GROUP = 256
HEAD_DIM = 128
N_Q_HEADS = 8
N_KV_HEADS = 2
Q_PER_KV = 4
ATTN_WIDTH = 1024
KV_WIDTH = 256
IN_PROJ_WIDTH = 2560
MIX_WIDTH = 2048
WINDOW = 128
BLOCK = 128
BAND = 3 * BLOCK
ROPE_THETA = 10000.0
N_GROUPS = 8
EXPERTS_PER_GROUP = 8
N_EXPERTS = 64
TOP_K = 2
D_EXPERT = 512
MOE_BLOCK = 256
EPS = 1e-6

LANES = 128
SUBLANES = 8
HALF = D_MODEL // 2
PACK_CHUNKS = HALF // LANES
NEG_BIG = -1e30
VMEM_LIMIT = 56 * 1024 * 1024

BF16 = jnp.bfloat16
F32 = jnp.float32
U32 = jnp.uint32
HI_MASK = 0xFFFF0000


def _cparams(n_axes):
    return pltpu.CompilerParams(dimension_semantics=("arbitrary",) * n_axes,
                                vmem_limit_bytes=VMEM_LIMIT)


def _pack_pair(lo, hi):
    lo_bits = lax.bitcast_convert_type(lo.astype(BF16).astype(F32), U32) >> 16
    hi_bits = lax.bitcast_convert_type(hi.astype(BF16).astype(F32), U32) & jnp.uint32(HI_MASK)
    return lo_bits | hi_bits


def _unpack_pair(words):
    lo = lax.bitcast_convert_type(words << 16, F32)
    hi = lax.bitcast_convert_type(words & jnp.uint32(HI_MASK), F32)
    return lo, hi


IN_TM = 512
ROW_SPLIT = 2


def _in_proj_kernel(x_ref, nw_ref, w_ref, qnw_ref, knw_ref, cos_ref, sin_ref,
                    u_ref, q_ref, k_ref, v_ref):
    sub = IN_TM // ROW_SPLIT
    for part in range(ROW_SPLIT):
        rows = slice(part * sub, (part + 1) * sub)
        x = x_ref[rows, :]
        ms = jnp.mean(x * x, axis=-1, keepdims=True)
        xn = (x * lax.rsqrt(ms + EPS) * nw_ref[...]).astype(BF16)
        cos = cos_ref[rows, :]
        sin = sin_ref[rows, :]

        def head_norm_rope(t, w, scale):
            hms = jnp.mean(t * t, axis=-1, keepdims=True)
            t = t * lax.rsqrt(hms + EPS) * w
            t = t * cos + pltpu.roll(t, HEAD_DIM // 2, axis=1) * sin
            return t * scale

        u_ref[rows, :] = jnp.dot(xn, w_ref[:, :POOL_WIDTH],
                                 preferred_element_type=F32).astype(BF16)
        zq = jnp.dot(xn, w_ref[:, POOL_WIDTH:POOL_WIDTH + ATTN_WIDTH],
                     preferred_element_type=F32)
        qnw = qnw_ref[...]
        for h in range(N_Q_HEADS):
            sl = slice(h * HEAD_DIM, (h + 1) * HEAD_DIM)
            q_ref[rows, sl] = head_norm_rope(zq[:, sl], qnw, HEAD_DIM ** -0.5).astype(BF16)
        o_k = POOL_WIDTH + ATTN_WIDTH
        zk = jnp.dot(xn, w_ref[:, o_k:o_k + KV_WIDTH], preferred_element_type=F32)
        knw = knw_ref[...]
        for h in range(N_KV_HEADS):
            sl = slice(h * HEAD_DIM, (h + 1) * HEAD_DIM)
            k_ref[rows, sl] = head_norm_rope(zk[:, sl], knw, 1.0).astype(BF16)
        v_ref[rows, :] = jnp.dot(xn, w_ref[:, o_k + KV_WIDTH:],
                                 preferred_element_type=F32).astype(BF16)


def _in_proj(x2, nw, w_in, qnw, knw, cos, sin, seq):
    T = x2.shape[0]
    tm = IN_TM
    pos_blocks = seq // tm
    full = lambda shape: pl.BlockSpec(shape, lambda i: (0,) * len(shape))
    return pl.pallas_call(
        _in_proj_kernel,
        grid=(T // tm,),
        in_specs=[
            pl.BlockSpec((tm, D_MODEL), lambda i: (i, 0)),
            full((1, D_MODEL)),
            full((D_MODEL, IN_PROJ_WIDTH)),
            full((1, HEAD_DIM)),
            full((1, HEAD_DIM)),
            pl.BlockSpec((tm, HEAD_DIM), lambda i: (i % pos_blocks, 0)),
            pl.BlockSpec((tm, HEAD_DIM), lambda i: (i % pos_blocks, 0)),
        ],
        out_specs=[
            pl.BlockSpec((tm, POOL_WIDTH), lambda i: (i, 0)),
            pl.BlockSpec((tm, ATTN_WIDTH), lambda i: (i, 0)),
            pl.BlockSpec((tm, KV_WIDTH), lambda i: (i, 0)),
            pl.BlockSpec((tm, KV_WIDTH), lambda i: (i, 0)),
        ],
        out_shape=[
            jax.ShapeDtypeStruct((T, POOL_WIDTH), BF16),
            jax.ShapeDtypeStruct((T, ATTN_WIDTH), BF16),
            jax.ShapeDtypeStruct((T, KV_WIDTH), BF16),
            jax.ShapeDtypeStruct((T, KV_WIDTH), BF16),
        ],
        compiler_params=_cparams(1),
        name="in_proj",
    )(x2, nw, w_in, qnw, knw, cos, sin)


MIX_TQ = 512


def _band_constants(seq):
    nb = seq // BLOCK
    coef = np.zeros((3, N_POOL_GROUPS, BLOCK, BAND), np.float32)
    inv_count = np.zeros((3, N_POOL_GROUPS, BLOCK, 1), np.float32)
    bias = np.zeros((3, BLOCK, BAND), np.float32)
    for kind, n in enumerate((0, 1, nb - 1)):
        start = min(max((n - 1) * BLOCK, 0), seq - BAND)
        t = n * BLOCK + np.arange(BLOCK)[:, None]
        s = start + np.arange(BAND)[None, :]
        bias[kind] = np.where(np.abs(s - t) <= WINDOW, 0.0, NEG_BIG)
        for g, win in enumerate(POOL_WINDOWS):
            half = win // 2
            lo = np.clip(t - half, 0, seq)
            hi = np.clip(t + half, 0, seq)
            count = (hi - lo).astype(np.float32)
            inside = ((s >= lo) & (s < hi)).astype(np.float32)
            coef[kind, g] = inside - count * (s == t)
            inv_count[kind, g] = 1.0 / count
    return coef, inv_count, bias


def _mixers_kernel(sink_ref, u_ref, q_ref, k_ref, v_ref, coef_ref, invc_ref, bias_ref,
                   wpool_ref, pscale_ref, ab_ref, *, seq):
    nb = seq // BLOCK
    j = pl.program_id(1)
    n_sub = MIX_TQ // BLOCK
    subs = []
    for r in range(n_sub):
        n = j * n_sub + r
        start = pl.multiple_of(jnp.clip((n - 1) * BLOCK, 0, seq - BAND), BLOCK)
        kind = jnp.where(n == 0, 0, jnp.where(n == nb - 1, 2, 1))
        subs.append((start, kind, slice(r * BLOCK, (r + 1) * BLOCK)))

    scores = {}
    for r, (start, kind, rows) in enumerate(subs):
        for hk in range(N_KV_HEADS):
            kb = k_ref[0, pl.ds(start, BAND), hk * HEAD_DIM:(hk + 1) * HEAD_DIM]
            qs = jnp.concatenate(
                [q_ref[0, rows, h * HEAD_DIM:(h + 1) * HEAD_DIM]
                 for h in range(hk * Q_PER_KV, (hk + 1) * Q_PER_KV)], axis=0)
            scores[r, hk] = lax.dot_general(qs, kb, (((1,), (1,)), ((), ())),
                                            preferred_element_type=F32)
    diffs = {}
    for r, (start, kind, rows) in enumerate(subs):
        for g in range(N_POOL_GROUPS):
            ub = u_ref[0, pl.ds(start, BAND), g * POOL_GROUP:(g + 1) * POOL_GROUP]
            d = jnp.dot(coef_ref[kind, g], ub, preferred_element_type=F32) * invc_ref[kind, g]
            diffs[r, g] = d.astype(BF16)
    probs = {}
    for r, (start, kind, rows) in enumerate(subs):
        bias = bias_ref[kind]
        for hk in range(N_KV_HEADS):
            for gi in range(Q_PER_KV):
                h = hk * Q_PER_KV + gi
                sh = scores[r, hk][gi * BLOCK:(gi + 1) * BLOCK] + bias
                sink = sink_ref[h]
                m = jnp.maximum(jnp.max(sh, axis=-1, keepdims=True), sink)
                p = jnp.exp(sh - m)
                denom = jnp.sum(p, axis=-1, keepdims=True) + jnp.exp(sink - m)
                probs[r, h] = (p.astype(BF16), denom)
    for r, (start, kind, rows) in enumerate(subs):
        for g in range(N_POOL_GROUPS):
            cols = slice(g * POOL_GROUP, (g + 1) * POOL_GROUP)
            y = jnp.dot(diffs[r, g], wpool_ref[g], preferred_element_type=F32)
            ab_ref[0, rows, cols] = (y * pscale_ref[:, cols]).astype(BF16)
    for r, (start, kind, rows) in enumerate(subs):
        for hk in range(N_KV_HEADS):
            vb = v_ref[0, pl.ds(start, BAND), hk * HEAD_DIM:(hk + 1) * HEAD_DIM]
            for gi in range(Q_PER_KV):
                h = hk * Q_PER_KV + gi
                p, denom = probs[r, h]
                o = jnp.dot(p, vb, preferred_element_type=F32) / denom
                ocols = slice(POOL_WIDTH + h * HEAD_DIM, POOL_WIDTH + (h + 1) * HEAD_DIM)
                ab_ref[0, rows, ocols] = o.astype(BF16)


def _mixers(sink, u, q, k, v, coef, invc, bias, wpool, pscale):
    B, S, _ = u.shape
    tq = MIX_TQ
    full = lambda shape: pl.BlockSpec(shape, lambda b, j: (0,) * len(shape))
    return pl.pallas_call(
        functools.partial(_mixers_kernel, seq=S),
        grid=(B, S // tq),
        in_specs=[
            pl.BlockSpec(memory_space=pltpu.SMEM),
            pl.BlockSpec((1, S, POOL_WIDTH), lambda b, j: (b, 0, 0)),
            pl.BlockSpec((1, tq, ATTN_WIDTH), lambda b, j: (b, j, 0)),
            pl.BlockSpec((1, S, KV_WIDTH), lambda b, j: (b, 0, 0)),
            pl.BlockSpec((1, S, KV_WIDTH), lambda b, j: (b, 0, 0)),
            full((3, N_POOL_GROUPS, BLOCK, BAND)),
            full((3, N_POOL_GROUPS, BLOCK, 1)),
            full((3, BLOCK, BAND)),
            full((N_POOL_GROUPS, POOL_GROUP, POOL_GROUP)),
            full((1, POOL_WIDTH)),
        ],
        out_specs=pl.BlockSpec((1, tq, MIX_WIDTH), lambda b, j: (b, j, 0)),
        out_shape=jax.ShapeDtypeStruct((B, S, MIX_WIDTH), BF16),
        compiler_params=_cparams(2),
        name="mixers",
    )(sink, u, q, k, v, coef, invc, bias, wpool, pscale)


OUT_TM = 512
ROUTE_E0, ROUTE_E1, ROUTE_W0, ROUTE_W1 = 0, 1, 2, 3


def _out_proj_kernel(ab_ref, x_ref, w_ref, nw_ref, wr_ref, br_ref,
                     h_ref, xn_ref, route_ref, counts_ref):
    @pl.when(pl.program_id(0) == 0)
    def _():
        counts_ref[...] = jnp.zeros_like(counts_ref)

    sub = OUT_TM // ROW_SPLIT
    lane = lax.broadcasted_iota(jnp.int32, (sub, LANES), 1)

    def first_argmax(vals):
        m = jnp.max(vals, axis=-1, keepdims=True)
        idx = jnp.min(jnp.where(vals == m, lane, LANES), axis=-1, keepdims=True)
        return m, idx

    new_counts = jnp.zeros((1, LANES), F32)
    for part in range(ROW_SPLIT):
        rows = slice(part * sub, (part + 1) * sub)
        h = x_ref[rows, :] + jnp.dot(ab_ref[rows, :], w_ref[...], preferred_element_type=F32)
        h_ref[rows, :] = h
        ms = jnp.mean(h * h, axis=-1, keepdims=True)
        xn = h * lax.rsqrt(ms + EPS) * nw_ref[...]
        packed = _pack_pair(xn[:, :HALF], xn[:, HALF:])
        for c in range(PACK_CHUNKS):
            xn_ref[pl.ds(part * sub * SUBLANES + c, sub, stride=SUBLANES), :] = (
                packed[:, c * LANES:(c + 1) * LANES])
        logits = jnp.dot(xn.astype(BF16), wr_ref[...], preferred_element_type=F32) + br_ref[...]

        gl = jnp.where(lane < N_GROUPS, logits, NEG_BIG)
        gmax, gidx = first_argmax(gl)
        gsum = jnp.sum(jnp.where(lane < N_GROUPS, jnp.exp(gl - gmax), 0.0),
                       axis=-1, keepdims=True)
        g_w = 1.0 / gsum
        e_lo = N_GROUPS + gidx * EXPERTS_PER_GROUP
        el = jnp.where((lane >= e_lo) & (lane < e_lo + EXPERTS_PER_GROUP), logits, NEG_BIG)
        m1, i1 = first_argmax(el)
        m2, i2 = first_argmax(jnp.where(lane == i1, NEG_BIG, el))
        t = jnp.exp(m2 - m1)
        p1 = 1.0 / (1.0 + t)
        p2 = t * p1
        e1 = i1 - N_GROUPS
        e2 = i2 - N_GROUPS
        route_ref[rows, :] = jnp.where(
            lane == ROUTE_E0, e1.astype(F32),
            jnp.where(lane == ROUTE_E1, e2.astype(F32),
                      jnp.where(lane == ROUTE_W0, g_w * p1,
                                jnp.where(lane == ROUTE_W1, g_w * p2, 0.0))))
        chosen = ((lane == e1) | (lane == e2)).astype(F32)
        new_counts = new_counts + jnp.sum(chosen, axis=0, keepdims=True)
    counts_ref[...] += new_counts


def _out_proj(ab, x2, w_out, nw, w_router, b_router):
    T = x2.shape[0]
    tm = OUT_TM
    full = lambda shape: pl.BlockSpec(shape, lambda i: (0,) * len(shape))
    row = lambda width: pl.BlockSpec((tm, width), lambda i: (i, 0))
    return pl.pallas_call(
        _out_proj_kernel,
        grid=(T // tm,),
        in_specs=[row(MIX_WIDTH), row(D_MODEL), full((MIX_WIDTH, D_MODEL)), full((1, D_MODEL)),
                  full((D_MODEL, LANES)), full((1, LANES))],
        out_specs=[row(D_MODEL),
                   pl.BlockSpec((tm * SUBLANES, LANES), lambda i: (i, 0)),
                   row(LANES),
                   full((1, LANES))],
        out_shape=[jax.ShapeDtypeStruct((T, D_MODEL), F32),
                   jax.ShapeDtypeStruct((T * SUBLANES, LANES), U32),
                   jax.ShapeDtypeStruct((T, LANES), F32),
                   jax.ShapeDtypeStruct((1, LANES), F32)],
        compiler_params=_cparams(1),
        name="out_proj",
    )(ab, x2, w_out, nw, w_router, b_router)


PLAN_TM = 1024


TAB_EXPERT, TAB_FIRST, TAB_BUFFER, TAB_NEXT, TAB_VALID = 0, 1, 2, 3, 4
TAB_ROWS = 5
ETAB_PAD_LO, ETAB_PAD_HI, ETAB_MISC = 0, 1, 2
ETAB_ROWS = 3
MISC_N_USED, MISC_RUN_EXPERT = 0, 1
FETCH_AHEAD = 2
TABLE_ROWS = SUBLANES


def _slot_tables(counts_ref, tab_ref, etab_ref, run_expert, run_of_block, pstart_s, n_blocks):
    log_block = MOE_BLOCK.bit_length() - 1

    def clear(r, carry):
        run_expert[r] = -1
        return carry
    lax.fori_loop(0, run_expert.shape[0], clear, 0)

    def per_expert(e, carry):
        slot, blk, run, buf = carry
        c = counts_ref[e]
        n_blk = lax.shift_right_logical(c + (MOE_BLOCK - 1), log_block)
        pstart_s[e] = slot
        etab_ref[ETAB_PAD_LO, e] = slot + c
        etab_ref[ETAB_PAD_HI, e] = slot + n_blk * MOE_BLOCK

        def per_block(b, inner):
            tab_ref[TAB_EXPERT, b] = e
            tab_ref[TAB_FIRST, b] = (b == blk).astype(jnp.int32)
            tab_ref[TAB_BUFFER, b] = buf
            tab_ref[TAB_VALID, b] = jnp.minimum(c - (b - blk) * MOE_BLOCK, MOE_BLOCK)
            run_of_block[b] = run
            return inner
        lax.fori_loop(blk, blk + n_blk, per_block, 0)

        @pl.when(n_blk > 0)
        def _():
            run_expert[run] = e
        has_run = (n_blk > 0).astype(jnp.int32)
        next_buf = jnp.where(buf + has_run == FETCH_AHEAD, 0, buf + has_run)
        return slot + n_blk * MOE_BLOCK, blk + n_blk, run + has_run, next_buf

    zero = jnp.int32(0)
    _, n_used, _, _ = lax.fori_loop(0, N_EXPERTS, per_expert, (zero, zero, zero, zero))

    def per_used_block(b, carry):
        tab_ref[TAB_NEXT, b] = run_expert[run_of_block[b] + FETCH_AHEAD]
        return carry
    lax.fori_loop(0, n_used, per_used_block, 0)

    def per_unused_block(b, carry):
        for row in (TAB_EXPERT, TAB_FIRST, TAB_BUFFER, TAB_VALID):
            tab_ref[row, b] = 0
        tab_ref[TAB_NEXT, b] = -1
        return carry
    lax.fori_loop(n_used, n_blocks, per_unused_block, 0)

    def clear_misc(e, carry):
        etab_ref[ETAB_MISC, e] = 0
        return carry
    lax.fori_loop(0, N_EXPERTS, clear_misc, 0)
    etab_ref[ETAB_MISC, MISC_N_USED] = n_used
    for r in range(FETCH_AHEAD):
        etab_ref[ETAB_MISC, MISC_RUN_EXPERT + r] = run_expert[r]


def _plan_kernel(counts_ref, route_ref, tri_ref, dest_ref, tab_ref, etab_ref,
                 carry_ref, pstart_ref, run_expert, run_of_block, pstart_s, *, n_blocks):
    @pl.when(pl.program_id(0) == 0)
    def _():
        carry_ref[...] = jnp.zeros_like(carry_ref)
        _slot_tables(counts_ref, tab_ref, etab_ref, run_expert, run_of_block, pstart_s, n_blocks)
        lane_row = lax.broadcasted_iota(jnp.int32, (1, LANES), 1)

        def place(e, row):
            return jnp.where(lane_row == e, pstart_s[e].astype(F32), row)
        pstart_ref[...] = lax.fori_loop(0, N_EXPERTS, place, jnp.zeros((1, LANES), F32))

    route = route_ref[...]
    lane = lax.broadcasted_iota(jnp.int32, route.shape, 1)
    e0 = route[:, ROUTE_E0:ROUTE_E0 + 1].astype(jnp.int32)
    e1 = route[:, ROUTE_E1:ROUTE_E1 + 1].astype(jnp.int32)
    oh0 = lane == e0
    oh1 = lane == e1
    both = (oh0 | oh1).astype(F32)
    earlier = jnp.dot(tri_ref[...], both.astype(BF16), preferred_element_type=F32)
    base = pstart_ref[...] + carry_ref[...] + earlier
    d0 = jnp.sum(jnp.where(oh0, base, 0.0), axis=-1, keepdims=True)
    d1 = jnp.sum(jnp.where(oh1, base, 0.0), axis=-1, keepdims=True)
    by_token = jnp.where(lane == 0, d0, jnp.where(lane == 1, d1, 0.0))
    dest_ref[...] = by_token.T[:TABLE_ROWS].astype(jnp.int32)
    carry_ref[...] += jnp.sum(both, axis=0, keepdims=True)


def _plan(counts, route, n_blocks):
    T = route.shape[0]
    tm = PLAN_TM
    tri = jnp.asarray(np.tril(np.ones((tm, tm), np.float32), -1), BF16)
    smem_out = pl.BlockSpec(memory_space=pltpu.SMEM)
    grid_spec = pltpu.PrefetchScalarGridSpec(
        num_scalar_prefetch=1,
        grid=(T // tm,),
        in_specs=[pl.BlockSpec((tm, LANES), lambda i, c: (i, 0)),
                  pl.BlockSpec((tm, tm), lambda i, c: (0, 0))],
        out_specs=[pl.BlockSpec((TABLE_ROWS, tm), lambda i, c: (0, i)), smem_out, smem_out],
        scratch_shapes=[pltpu.VMEM((1, LANES), F32), pltpu.VMEM((1, LANES), F32),
                        pltpu.SMEM((N_EXPERTS + FETCH_AHEAD,), jnp.int32),
                        pltpu.SMEM((n_blocks,), jnp.int32),
                        pltpu.SMEM((N_EXPERTS,), jnp.int32)],
    )
    return pl.pallas_call(
        functools.partial(_plan_kernel, n_blocks=n_blocks),
        grid_spec=grid_spec,
        out_shape=[jax.ShapeDtypeStruct((TABLE_ROWS, T), jnp.int32),
                   jax.ShapeDtypeStruct((TAB_ROWS, n_blocks), jnp.int32),
                   jax.ShapeDtypeStruct((ETAB_ROWS, N_EXPERTS), jnp.int32)],
        compiler_params=_cparams(1),
        name="plan",
    )(counts, route, tri)


DISP_TM = 1024
DMA_GROUP = 8
ROW_WORDS = SUBLANES


def _packed_row(ref, row):
    return ref.at[pl.ds(pl.multiple_of(row * ROW_WORDS, ROW_WORDS), ROW_WORDS)]


def _dispatch_kernel(etab_ref, dest0_ref, dest1_ref, xn_ref, xs_hbm,
                     zeros, sem, tail_sem, *, experts_per_step, tails_per_step, n_blocks):
    i = pl.program_id(0)

    @pl.when(i == 0)
    def _():
        zeros[...] = jnp.zeros_like(zeros)

    def scatter_group(t, carry):
        base = pl.multiple_of(t * DMA_GROUP, DMA_GROUP)
        window = xn_ref.at[pl.ds(base * ROW_WORDS, DMA_GROUP * ROW_WORDS)]
        for k in range(DMA_GROUP):
            src = window.at[pl.ds(k * ROW_WORDS, ROW_WORDS)]
            for choice, dest_ref in enumerate((dest0_ref, dest1_ref)):
                dst = _packed_row(xs_hbm, dest_ref[0, 0, 0, base + k])
                pltpu.make_async_copy(src, dst, sem).start(priority=choice)
        return carry

    lax.fori_loop(0, DISP_TM // DMA_GROUP, scatter_group, 0)

    def pad_fill(start):
        for j in range(experts_per_step):
            e = jnp.minimum(i * experts_per_step + j, N_EXPERTS - 1)
            live = i * experts_per_step + j < N_EXPERTS
            pos = etab_ref[ETAB_PAD_LO, e]
            n_pad = jnp.where(live, etab_ref[ETAB_PAD_HI, e] - pos, 0)
            for bit in reversed(range(MOE_BLOCK.bit_length() - 1)):
                size = 1 << bit
                take = jnp.bitwise_and(lax.shift_right_logical(n_pad, bit), 1)
                dst = xs_hbm.at[pl.ds(pl.multiple_of(pos * ROW_WORDS, ROW_WORDS), size * ROW_WORDS)]
                cp = pltpu.make_async_copy(zeros.at[pl.ds(0, size * ROW_WORDS)], dst, sem)

                @pl.when(take == 1)
                def _():
                    cp.start() if start else cp.wait()
                pos = pos + take * size

    def tail_copies(start):
        for j in range(tails_per_step):
            blk = etab_ref[ETAB_MISC, MISC_N_USED] + i * tails_per_step + j
            rows = pl.ds(pl.multiple_of(jnp.minimum(blk, n_blocks - 1) * (MOE_BLOCK * ROW_WORDS),
                                        MOE_BLOCK * ROW_WORDS), MOE_BLOCK * ROW_WORDS)
            cp = pltpu.make_async_copy(zeros, xs_hbm.at[rows], tail_sem)

            @pl.when(blk < n_blocks)
            def _():
                cp.start() if start else cp.wait()

    pad_fill(True)
    tail_copies(True)

    tile_rows = DISP_TM * ROW_WORDS
    for _ in range(TOP_K):
        pltpu.make_async_copy(xn_ref, xs_hbm.at[pl.ds(0, tile_rows)], sem).wait()
    pad_fill(False)
    tail_copies(False)


def _slot_tiles(dest, tile):
    return dest.reshape(dest.shape[0], dest.shape[1] // tile, 1, tile)


def _slot_spec(tile, choice, index):
    return pl.BlockSpec((1, 1, 1, tile), lambda i, *_: (choice, index(i), 0, 0),
                        memory_space=pltpu.SMEM)


def _dispatch(etab, dest, xn_packed, n_blocks):
    dest4 = _slot_tiles(dest, DISP_TM)
    n = dest4.shape[1]
    min_used = (n * DISP_TM * TOP_K) // MOE_BLOCK
    tails_per_step = -(-(n_blocks - min_used) // n)
    grid_spec = pltpu.PrefetchScalarGridSpec(
        num_scalar_prefetch=1,
        grid=(n,),
        in_specs=[_slot_spec(DISP_TM, 0, lambda i: i), _slot_spec(DISP_TM, 1, lambda i: i),
                  pl.BlockSpec((DISP_TM * ROW_WORDS, LANES), lambda i, *_: (i, 0))],
        out_specs=pl.BlockSpec(memory_space=pl.ANY),
        scratch_shapes=[pltpu.VMEM((MOE_BLOCK * ROW_WORDS, LANES), U32),
                        pltpu.SemaphoreType.DMA(()), pltpu.SemaphoreType.DMA(())],
    )
    return pl.pallas_call(
        functools.partial(_dispatch_kernel, experts_per_step=-(-N_EXPERTS // n),
                          tails_per_step=tails_per_step, n_blocks=n_blocks),
        grid_spec=grid_spec,
        out_shape=jax.ShapeDtypeStruct((n_blocks * MOE_BLOCK * ROW_WORDS, LANES), U32),
        compiler_params=_cparams(1),
        name="dispatch",
    )(etab, dest4, dest4, xn_packed)


CAST_ROWS = 256
FETCH_SPLIT = 4
BLOCKS_PER_STEP = 4


def _experts_kernel(tab_ref, etab_ref,
                    xs_ref, wg_hbm, wu_hbm, wd_hbm, ys_ref,
                    fg, fu, fd, wg, wu, wd, sem):
    n_used = etab_ref[ETAB_MISC, MISC_N_USED]
    for j in range(BLOCKS_PER_STEP):
        rows = pl.ds(j * MOE_BLOCK * SUBLANES, MOE_BLOCK * SUBLANES)
        _expert_block(pl.program_id(0) * BLOCKS_PER_STEP + j, n_used, tab_ref, etab_ref,
                      xs_ref.at[rows], wg_hbm, wu_hbm, wd_hbm, ys_ref.at[rows],
                      fg, fu, fd, wg, wu, wd, sem)


def _expert_block(b, n_used, tab_ref, etab_ref, xs_ref, wg_hbm, wu_hbm, wd_hbm, ys_ref,
                  fg, fu, fd, wg, wu, wd, sem):

    def fetch(e, p):
        copies = []
        for k, (src, dst) in enumerate(((wg_hbm, fg), (wu_hbm, fu), (wd_hbm, fd))):
            rows = src.shape[1] // FETCH_SPLIT
            for c in range(FETCH_SPLIT):
                sl = pl.ds(c * rows, rows)
                copies.append(pltpu.make_async_copy(src.at[e, sl], dst.at[p, sl], sem.at[p, k]))
        return copies

    def start_fetch(e, p):
        for k, cp in enumerate(fetch(e, p)):
            cp.start(priority=k % 2)

    @pl.when(b == 0)
    def _():
        for r in range(FETCH_AHEAD):
            e = etab_ref[ETAB_MISC, MISC_RUN_EXPERT + r]

            @pl.when(e >= 0)
            def _():
                start_fetch(e, r)

    @pl.when((b < n_used) & (tab_ref[TAB_FIRST, b] == 1))
    def _():
        p = tab_ref[TAB_BUFFER, b]
        for cp in fetch(tab_ref[TAB_EXPERT, b], p):
            cp.wait()

        def cast(src, dst, n_rows):
            def body(i, carry):
                rows = pl.ds(pl.multiple_of(i * CAST_ROWS, CAST_ROWS), CAST_ROWS)
                dst[rows, :] = src[p, rows, :].astype(BF16)
                return carry
            lax.fori_loop(0, n_rows // CAST_ROWS, body, 0)

        cast(fg, wg, D_MODEL)
        cast(fu, wu, D_MODEL)
        cast(fd, wd, D_EXPERT)

        @pl.when(tab_ref[TAB_NEXT, b] >= 0)
        def _():
            start_fetch(tab_ref[TAB_NEXT, b], p)

    def swiglu_rows(m_rows):
        lo_parts, hi_parts = [], []
        for c in range(PACK_CHUNKS):
            lo, hi = _unpack_pair(xs_ref[pl.ds(c, m_rows, stride=SUBLANES), :])
            lo_parts.append(lo.astype(BF16))
            hi_parts.append(hi.astype(BF16))
        xb = jnp.concatenate(lo_parts + hi_parts, axis=1)
        g = jnp.dot(xb, wg[...], preferred_element_type=F32)
        u = jnp.dot(xb, wu[...], preferred_element_type=F32)
        hmid = (g * jax.nn.sigmoid(g) * u).astype(BF16)
        y = jnp.dot(hmid, wd[...], preferred_element_type=F32)
        packed = _pack_pair(y[:, :HALF], y[:, HALF:])
        for c in range(PACK_CHUNKS):
            ys_ref[pl.ds(c, m_rows, stride=SUBLANES), :] = packed[:, c * LANES:(c + 1) * LANES]
        if m_rows < MOE_BLOCK:
            rest = pl.ds(m_rows * SUBLANES, (MOE_BLOCK - m_rows) * SUBLANES)
            ys_ref[rest, :] = jnp.zeros(((MOE_BLOCK - m_rows) * SUBLANES, LANES), U32)

    valid = tab_ref[TAB_VALID, b]

    @pl.when((b < n_used) & (valid > MOE_BLOCK // 2))
    def _():
        swiglu_rows(MOE_BLOCK)

    @pl.when((b < n_used) & (valid <= MOE_BLOCK // 2))
    def _():
        swiglu_rows(MOE_BLOCK // 2)

    @pl.when(b >= n_used)
    def _():
        ys_ref[...] = jnp.zeros(ys_ref.shape, ys_ref.dtype)


def _experts(tab, etab, xs, w_gate, w_up, w_down):
    n_blocks = tab.shape[1]
    assert n_blocks % BLOCKS_PER_STEP == 0
    blk = lambda: pl.BlockSpec((BLOCKS_PER_STEP * MOE_BLOCK * SUBLANES, LANES),
                               lambda s, *_: (s, 0))
    grid_spec = pltpu.PrefetchScalarGridSpec(
        num_scalar_prefetch=2,
        grid=(n_blocks // BLOCKS_PER_STEP,),
        in_specs=[blk(),
                  pl.BlockSpec(memory_space=pl.ANY),
                  pl.BlockSpec(memory_space=pl.ANY),
                  pl.BlockSpec(memory_space=pl.ANY)],
        out_specs=blk(),
        scratch_shapes=[pltpu.VMEM((FETCH_AHEAD, D_MODEL, D_EXPERT), F32),
                        pltpu.VMEM((FETCH_AHEAD, D_MODEL, D_EXPERT), F32),
                        pltpu.VMEM((FETCH_AHEAD, D_EXPERT, D_MODEL), F32),
                        pltpu.VMEM((D_MODEL, D_EXPERT), BF16),
                        pltpu.VMEM((D_MODEL, D_EXPERT), BF16),
                        pltpu.VMEM((D_EXPERT, D_MODEL), BF16),
                        pltpu.SemaphoreType.DMA((FETCH_AHEAD, 3))],
    )
    return pl.pallas_call(
        _experts_kernel,
        grid_spec=grid_spec,
        out_shape=jax.ShapeDtypeStruct((n_blocks * MOE_BLOCK * SUBLANES, LANES), U32),
        compiler_params=_cparams(1),
        name="experts",
    )(tab, etab, xs, w_gate, w_up, w_down)


CMB_TM = 256


def _combine_kernel(d0_cur_ref, d1_cur_ref, d0_next_ref, d1_next_ref, h_ref, route_ref, ys_hbm,
                    o_ref, ybuf, sem):
    i = pl.program_id(0)
    n = pl.num_programs(0)
    slot = i % 2
    n_rows = TOP_K * CMB_TM

    def gather(idx_refs, s):
        def gather_group(t, carry):
            base = pl.multiple_of(t * DMA_GROUP, DMA_GROUP)
            for choice, idx_ref in enumerate(idx_refs):
                first = (choice * CMB_TM + base) * ROW_WORDS
                window = ybuf.at[s, pl.ds(first, DMA_GROUP * ROW_WORDS)]
                for k in range(DMA_GROUP):
                    pltpu.make_async_copy(_packed_row(ys_hbm, idx_ref[0, 0, 0, base + k]),
                                          window.at[pl.ds(k * ROW_WORDS, ROW_WORDS)],
                                          sem.at[s]).start(priority=k % 2)
            return carry
        lax.fori_loop(0, CMB_TM // DMA_GROUP, gather_group, 0)

    @pl.when(i == 0)
    def _():
        gather((d0_cur_ref, d1_cur_ref), 0)

    @pl.when(i + 1 < n)
    def _():
        gather((d0_next_ref, d1_next_ref), 1 - slot)

    pltpu.make_async_copy(ys_hbm.at[pl.ds(0, n_rows * ROW_WORDS)], ybuf.at[slot],
                          sem.at[slot]).wait()
    route = route_ref[...]
    w0 = route[:, ROUTE_W0:ROUTE_W0 + 1]
    w1 = route[:, ROUTE_W1:ROUTE_W1 + 1]
    yb = ybuf.at[slot]
    for c in range(PACK_CHUNKS):
        lo0, hi0 = _unpack_pair(yb[pl.ds(c, CMB_TM, stride=SUBLANES), :])
        lo1, hi1 = _unpack_pair(yb[pl.ds(CMB_TM * SUBLANES + c, CMB_TM, stride=SUBLANES), :])
        lo_cols = slice(c * LANES, (c + 1) * LANES)
        hi_cols = slice(HALF + c * LANES, HALF + (c + 1) * LANES)
        o_ref[:, lo_cols] = h_ref[:, lo_cols] + (w0 * lo0 + w1 * lo1)
        o_ref[:, hi_cols] = h_ref[:, hi_cols] + (w0 * hi0 + w1 * hi1)


def _combine(dest, h, route, ys):
    T = h.shape[0]
    tm = CMB_TM
    n = T // tm
    dest4 = _slot_tiles(dest, tm)
    cur = lambda i: i
    nxt = lambda i: jnp.minimum(i + 1, n - 1)
    return pl.pallas_call(
        _combine_kernel,
        grid=(n,),
        in_specs=[
            _slot_spec(tm, 0, cur), _slot_spec(tm, 1, cur),
            _slot_spec(tm, 0, nxt), _slot_spec(tm, 1, nxt),
            pl.BlockSpec((tm, D_MODEL), lambda i: (i, 0)),
            pl.BlockSpec((tm, LANES), lambda i: (i, 0)),
            pl.BlockSpec(memory_space=pl.ANY),
        ],
        out_specs=pl.BlockSpec((tm, D_MODEL), lambda i: (i, 0)),
        out_shape=jax.ShapeDtypeStruct((T, D_MODEL), F32),
        scratch_shapes=[pltpu.VMEM((2, TOP_K * tm * SUBLANES, LANES), U32),
                        pltpu.SemaphoreType.DMA((2,))],
        compiler_params=_cparams(1),
        name="combine",
    )(dest4, dest4, dest4, dest4, h, route, ys)


def _rope_tables(S):
    inv = ROPE_THETA ** (-jnp.arange(0, HEAD_DIM, 2, dtype=F32) / HEAD_DIM)
    ang = jnp.arange(S, dtype=F32)[:, None] * inv[None, :]
    cos, sin = jnp.cos(ang), jnp.sin(ang)
    return jnp.concatenate([cos, cos], axis=-1), jnp.concatenate([-sin, sin], axis=-1)


def kernel(x, norm_mix_w, w_in, w_pool, pool_scale, q_norm_w, k_norm_w, sink_logits, w_out,
           norm_ffn_w, w_group_router, b_group_router, w_expert_router, b_expert_router,
           w_gate, w_up, w_down):
    B, S, D = x.shape
    T = B * S
    depth = w_in.shape[0]
    cos, sin = _rope_tables(S)
    coef_np, invc_np, bias_np = _band_constants(S)
    coef = jnp.asarray(coef_np, BF16)
    invc = jnp.asarray(invc_np, F32)
    bias = jnp.asarray(bias_np, F32)
    pad_lanes = LANES - N_GROUPS - N_EXPERTS

    h = x.reshape(T, D)
    for l in range(depth):
        u, q, k, v = _in_proj(h, norm_mix_w[l].reshape(1, D), w_in[l].astype(BF16),
                              q_norm_w[l].reshape(1, HEAD_DIM), k_norm_w[l].reshape(1, HEAD_DIM),
                              cos, sin, S)
        ab = _mixers(sink_logits[l], u.reshape(B, S, -1), q.reshape(B, S, -1),
                     k.reshape(B, S, -1), v.reshape(B, S, -1), coef, invc, bias,
                     w_pool[l].astype(BF16), pool_scale[l].reshape(1, POOL_WIDTH))
        w_router = jnp.concatenate(
            [w_group_router[l], w_expert_router[l], jnp.zeros((D, pad_lanes), F32)], axis=1)
        b_router = jnp.concatenate(
            [b_group_router[l], b_expert_router[l], jnp.zeros((pad_lanes,), F32)]).reshape(1, LANES)
        hmix, xn_packed, route, counts = _out_proj(
            ab.reshape(T, MIX_WIDTH), h, w_out[l].astype(BF16), norm_ffn_w[l].reshape(1, D),
            w_router.astype(BF16), b_router)
        n_blocks = -(-(T * TOP_K + N_EXPERTS * (MOE_BLOCK - 1)) // MOE_BLOCK)
        dest, tab, etab = _plan(counts[0].astype(jnp.int32), route, n_blocks)
        xs = _dispatch(etab, dest, xn_packed, n_blocks)
        ys = _experts(tab, etab, xs, w_gate[l], w_up[l], w_down[l])
        h = _combine(dest, hmix, route, ys)
    return h.reshape(B, S, D)
```

```python
import functools

import numpy as np
import jax
import jax.numpy as jnp
from jax import lax
from jax.experimental import pallas as pl
from jax.experimental.pallas import tpu as pltpu

D_MODEL = 2048
POOL_WIDTH = 1024
POOL_WINDOWS = (2, 4, 8, 16)
N_POOL_GROUPS = 4
POOL_GROUP = 256
HEAD_DIM = 128
N_Q_HEADS = 8
N_KV_HEADS = 2
Q_PER_KV = 4
ATTN_WIDTH = 1024
KV_WIDTH = 256
IN_PROJ_WIDTH = 2560
MIX_WIDTH = 2048
WINDOW = 128
BLOCK = 128
BAND = 3 * BLOCK
ROPE_THETA = 10000.0
N_GROUPS = 8
EXPERTS_PER_GROUP = 8
N_EXPERTS = 64
TOP_K = 2
D_EXPERT = 512
MOE_BLOCK = 256
EPS = 1e-6

LANES = 128
SUBLANES = 8
HALF = D_MODEL // 2
PACK_CHUNKS = HALF // LANES
NEG_BIG = -1e30
VMEM_LIMIT = 56 * 1024 * 1024

BF16 = jnp.bfloat16
F32 = jnp.float32
U32 = jnp.uint32
HI_MASK = 0xFFFF0000


def _cparams(n_axes):
    return pltpu.CompilerParams(dimension_semantics=("arbitrary",) * n_axes,
                                vmem_limit_bytes=VMEM_LIMIT)


def _pack_pair(lo, hi):
    lo_bits = lax.bitcast_convert_type(lo.astype(BF16).astype(F32), U32) >> 16
    hi_bits = lax.bitcast_convert_type(hi.astype(BF16).astype(F32), U32) & jnp.uint32(HI_MASK)
    return lo_bits | hi_bits


def _unpack_pair(words):
    lo = lax.bitcast_convert_type(words << 16, F32)
    hi = lax.bitcast_convert_type(words & jnp.uint32(HI_MASK), F32)
    return lo, hi


IN_TM = 512
ROW_SPLIT = 2


def _in_proj_kernel(x_ref, nw_ref, w_ref, qnw_ref, knw_ref, cos_ref, sin_ref,
                    u_ref, q_ref, k_ref, v_ref):
    sub = IN_TM // ROW_SPLIT
    for part in range(ROW_SPLIT):
        rows = slice(part * sub, (part + 1) * sub)
        x = x_ref[rows, :]
        ms = jnp.mean(x * x, axis=-1, keepdims=True)
        xn = (x * lax.rsqrt(ms + EPS) * nw_ref[...]).astype(BF16)
        cos = cos_ref[rows, :]
        sin = sin_ref[rows, :]

        def head_norm_rope(t, w, scale):
            hms = jnp.mean(t * t, axis=-1, keepdims=True)
            t = t * lax.rsqrt(hms + EPS) * w
            t = t * cos + pltpu.roll(t, HEAD_DIM // 2, axis=1) * sin
            return t * scale

        u_ref[rows, :] = jnp.dot(xn, w_ref[:, :POOL_WIDTH],
                                 preferred_element_type=F32).astype(BF16)
        zq = jnp.dot(xn, w_ref[:, POOL_WIDTH:POOL_WIDTH + ATTN_WIDTH],
                     preferred_element_type=F32)
        qnw = qnw_ref[...]
        for h in range(N_Q_HEADS):
            sl = slice(h * HEAD_DIM, (h + 1) * HEAD_DIM)
            q_ref[rows, sl] = head_norm_rope(zq[:, sl], qnw, HEAD_DIM ** -0.5).astype(BF16)
        o_k = POOL_WIDTH + ATTN_WIDTH
        zk = jnp.dot(xn, w_ref[:, o_k:o_k + KV_WIDTH], preferred_element_type=F32)
        knw = knw_ref[...]
        for h in range(N_KV_HEADS):
            sl = slice(h * HEAD_DIM, (h + 1) * HEAD_DIM)
            k_ref[rows, sl] = head_norm_rope(zk[:, sl], knw, 1.0).astype(BF16)
        v_ref[rows, :] = jnp.dot(xn, w_ref[:, o_k + KV_WIDTH:],
                                 preferred_element_type=F32).astype(BF16)


def _in_proj(x2, nw, w_in, qnw, knw, cos, sin, seq):
    T = x2.shape[0]
    tm = IN_TM
    pos_blocks = seq // tm
    full = lambda shape: pl.BlockSpec(shape, lambda i: (0,) * len(shape))
    return pl.pallas_call(
        _in_proj_kernel,
        grid=(T // tm,),
        in_specs=[
            pl.BlockSpec((tm, D_MODEL), lambda i: (i, 0)),
            full((1, D_MODEL)),
            full((D_MODEL, IN_PROJ_WIDTH)),
            full((1, HEAD_DIM)),
            full((1, HEAD_DIM)),
            pl.BlockSpec((tm, HEAD_DIM), lambda i: (i % pos_blocks, 0)),
            pl.BlockSpec((tm, HEAD_DIM), lambda i: (i % pos_blocks, 0)),
        ],
        out_specs=[
            pl.BlockSpec((tm, POOL_WIDTH), lambda i: (i, 0)),
            pl.BlockSpec((tm, ATTN_WIDTH), lambda i: (i, 0)),
            pl.BlockSpec((tm, KV_WIDTH), lambda i: (i, 0)),
            pl.BlockSpec((tm, KV_WIDTH), lambda i: (i, 0)),
        ],
        out_shape=[
            jax.ShapeDtypeStruct((T, POOL_WIDTH), BF16),
            jax.ShapeDtypeStruct((T, ATTN_WIDTH), BF16),
            jax.ShapeDtypeStruct((T, KV_WIDTH), BF16),
            jax.ShapeDtypeStruct((T, KV_WIDTH), BF16),
        ],
        compiler_params=_cparams(1),
        name="in_proj",
    )(x2, nw, w_in, qnw, knw, cos, sin)


MIX_TQ = 512
MIX_STAGE_SUBS = 2


def _band_constants(seq):
    nb = seq // BLOCK
    coef = np.zeros((3, N_POOL_GROUPS, BLOCK, BAND), np.float32)
    inv_count = np.zeros((3, N_POOL_GROUPS, BLOCK, 1), np.float32)
    bias = np.zeros((3, BLOCK, BAND), np.float32)
    for kind, n in enumerate((0, 1, nb - 1)):
        start = min(max((n - 1) * BLOCK, 0), seq - BAND)
        t = n * BLOCK + np.arange(BLOCK)[:, None]
        s = start + np.arange(BAND)[None, :]
        bias[kind] = np.where(np.abs(s - t) <= WINDOW, 0.0, NEG_BIG)
        for g, win in enumerate(POOL_WINDOWS):
            half = win // 2
            lo = np.clip(t - half, 0, seq)
            hi = np.clip(t + half, 0, seq)
            count = (hi - lo).astype(np.float32)
            inside = ((s >= lo) & (s < hi)).astype(np.float32)
            coef[kind, g] = inside - count * (s == t)
            inv_count[kind, g] = 1.0 / count
    return coef, inv_count, bias


def _mixers_kernel(sink_ref, u_ref, q_ref, k_ref, v_ref, coef_ref, invc_ref, bias_ref,
                   wpool_ref, pscale_ref, ab_ref, *, seq):
    nb = seq // BLOCK
    j = pl.program_id(1)
    n_sub = MIX_TQ // BLOCK
    subs = []
    for r in range(n_sub):
        n = j * n_sub + r
        start = pl.multiple_of(jnp.clip((n - 1) * BLOCK, 0, seq - BAND), BLOCK)
        kind = jnp.where(n == 0, 0, jnp.where(n == nb - 1, 2, 1))
        subs.append((start, kind, slice(r * BLOCK, (r + 1) * BLOCK)))

    def staged(group):
        scores = {}
        for r, (start, kind, rows) in enumerate(group):
            for hk in range(N_KV_HEADS):
                kb = k_ref[0, pl.ds(start, BAND), hk * HEAD_DIM:(hk + 1) * HEAD_DIM]
                qs = jnp.concatenate(
                    [q_ref[0, rows, h * HEAD_DIM:(h + 1) * HEAD_DIM]
                     for h in range(hk * Q_PER_KV, (hk + 1) * Q_PER_KV)], axis=0)
                scores[r, hk] = lax.dot_general(qs, kb, (((1,), (1,)), ((), ())),
                                                preferred_element_type=F32)
        diffs = {}
        for r, (start, kind, rows) in enumerate(group):
            for g in range(N_POOL_GROUPS):
                ub = u_ref[0, pl.ds(start, BAND), g * POOL_GROUP:(g + 1) * POOL_GROUP]
                d = jnp.dot(coef_ref[kind, g], ub, preferred_element_type=F32) * invc_ref[kind, g]
                diffs[r, g] = d.astype(BF16)
        probs = {}
        for r, (start, kind, rows) in enumerate(group):
            bias = bias_ref[kind]
            for hk in range(N_KV_HEADS):
                for gi in range(Q_PER_KV):
                    h = hk * Q_PER_KV + gi
                    sh = scores[r, hk][gi * BLOCK:(gi + 1) * BLOCK] + bias
                    sink = sink_ref[h]
                    m = jnp.maximum(jnp.max(sh, axis=-1, keepdims=True), sink)
                    p = jnp.exp(sh - m)
                    denom = jnp.sum(p, axis=-1, keepdims=True) + jnp.exp(sink - m)
                    probs[r, h] = (p.astype(BF16), denom)
        for r, (start, kind, rows) in enumerate(group):
            for g in range(N_POOL_GROUPS):
                cols = slice(g * POOL_GROUP, (g + 1) * POOL_GROUP)
                y = jnp.dot(diffs[r, g], wpool_ref[g], preferred_element_type=F32)
                ab_ref[0, rows, cols] = (y * pscale_ref[:, cols]).astype(BF16)
        for r, (start, kind, rows) in enumerate(group):
            for hk in range(N_KV_HEADS):
                vb = v_ref[0, pl.ds(start, BAND), hk * HEAD_DIM:(hk + 1) * HEAD_DIM]
                for gi in range(Q_PER_KV):
                    h = hk * Q_PER_KV + gi
                    p, denom = probs[r, h]
                    o = jnp.dot(p, vb, preferred_element_type=F32) / denom
                    ocols = slice(POOL_WIDTH + h * HEAD_DIM, POOL_WIDTH + (h + 1) * HEAD_DIM)
                    ab_ref[0, rows, ocols] = o.astype(BF16)

    for first in range(0, n_sub, MIX_STAGE_SUBS):
        staged(subs[first:first + MIX_STAGE_SUBS])


def _mixers(sink, u, q, k, v, coef, invc, bias, wpool, pscale):
    B, S, _ = u.shape
    tq = MIX_TQ
    full = lambda shape: pl.BlockSpec(shape, lambda b, j: (0,) * len(shape))
    return pl.pallas_call(
        functools.partial(_mixers_kernel, seq=S),
        grid=(B, S // tq),
        in_specs=[
            pl.BlockSpec(memory_space=pltpu.SMEM),
            pl.BlockSpec((1, S, POOL_WIDTH), lambda b, j: (b, 0, 0)),
            pl.BlockSpec((1, tq, ATTN_WIDTH), lambda b, j: (b, j, 0)),
            pl.BlockSpec((1, S, KV_WIDTH), lambda b, j: (b, 0, 0)),
            pl.BlockSpec((1, S, KV_WIDTH), lambda b, j: (b, 0, 0)),
            full((3, N_POOL_GROUPS, BLOCK, BAND)),
            full((3, N_POOL_GROUPS, BLOCK, 1)),
            full((3, BLOCK, BAND)),
            full((N_POOL_GROUPS, POOL_GROUP, POOL_GROUP)),
            full((1, POOL_WIDTH)),
        ],
        out_specs=pl.BlockSpec((1, tq, MIX_WIDTH), lambda b, j: (b, j, 0)),
        out_shape=jax.ShapeDtypeStruct((B, S, MIX_WIDTH), BF16),
        compiler_params=_cparams(2),
        name="mixers",
    )(sink, u, q, k, v, coef, invc, bias, wpool, pscale)


OUT_TM = 512
ROUTE_E0, ROUTE_E1, ROUTE_W0, ROUTE_W1 = 0, 1, 2, 3


def _out_proj_kernel(ab_ref, x_ref, w_ref, nw_ref, wr_ref, br_ref,
                     h_ref, xn_ref, route_ref, counts_ref):
    @pl.when(pl.program_id(0) == 0)
    def _():
        counts_ref[...] = jnp.zeros_like(counts_ref)

    sub = OUT_TM // ROW_SPLIT
    lane = lax.broadcasted_iota(jnp.int32, (sub, LANES), 1)

    def first_argmax(vals):
        m = jnp.max(vals, axis=-1, keepdims=True)
        idx = jnp.min(jnp.where(vals == m, lane, LANES), axis=-1, keepdims=True)
        return m, idx

    new_counts = jnp.zeros((1, LANES), F32)
    for part in range(ROW_SPLIT):
        rows = slice(part * sub, (part + 1) * sub)
        h = x_ref[rows, :] + jnp.dot(ab_ref[rows, :], w_ref[...], preferred_element_type=F32)
        h_ref[rows, :] = h
        ms = jnp.mean(h * h, axis=-1, keepdims=True)
        xn = h * lax.rsqrt(ms + EPS) * nw_ref[...]
        packed = _pack_pair(xn[:, :HALF], xn[:, HALF:])
        for c in range(PACK_CHUNKS):
            xn_ref[pl.ds(part * sub * SUBLANES + c, sub, stride=SUBLANES), :] = (
                packed[:, c * LANES:(c + 1) * LANES])
        logits = jnp.dot(xn.astype(BF16), wr_ref[...], preferred_element_type=F32) + br_ref[...]

        gl = jnp.where(lane < N_GROUPS, logits, NEG_BIG)
        gmax, gidx = first_argmax(gl)
        gsum = jnp.sum(jnp.where(lane < N_GROUPS, jnp.exp(gl - gmax), 0.0),
                       axis=-1, keepdims=True)
        g_w = 1.0 / gsum
        e_lo = N_GROUPS + gidx * EXPERTS_PER_GROUP
        el = jnp.where((lane >= e_lo) & (lane < e_lo + EXPERTS_PER_GROUP), logits, NEG_BIG)
        m1, i1 = first_argmax(el)
        m2, i2 = first_argmax(jnp.where(lane == i1, NEG_BIG, el))
        t = jnp.exp(m2 - m1)
        p1 = 1.0 / (1.0 + t)
        p2 = t * p1
        e1 = i1 - N_GROUPS
        e2 = i2 - N_GROUPS
        route_ref[rows, :] = jnp.where(
            lane == ROUTE_E0, e1.astype(F32),
            jnp.where(lane == ROUTE_E1, e2.astype(F32),
                      jnp.where(lane == ROUTE_W0, g_w * p1,
                                jnp.where(lane == ROUTE_W1, g_w * p2, 0.0))))
        chosen = ((lane == e1) | (lane == e2)).astype(F32)
        new_counts = new_counts + jnp.sum(chosen, axis=0, keepdims=True)
    counts_ref[...] += new_counts


def _out_proj(ab, x2, w_out, nw, w_router, b_router):
    T = x2.shape[0]
    tm = OUT_TM
    full = lambda shape: pl.BlockSpec(shape, lambda i: (0,) * len(shape))
    row = lambda width: pl.BlockSpec((tm, width), lambda i: (i, 0))
    return pl.pallas_call(
        _out_proj_kernel,
        grid=(T // tm,),
        in_specs=[row(MIX_WIDTH), row(D_MODEL), full((MIX_WIDTH, D_MODEL)), full((1, D_MODEL)),
                  full((D_MODEL, LANES)), full((1, LANES))],
        out_specs=[row(D_MODEL),
                   pl.BlockSpec((tm * SUBLANES, LANES), lambda i: (i, 0)),
                   row(LANES),
                   full((1, LANES))],
        out_shape=[jax.ShapeDtypeStruct((T, D_MODEL), F32),
                   jax.ShapeDtypeStruct((T * SUBLANES, LANES), U32),
                   jax.ShapeDtypeStruct((T, LANES), F32),
                   jax.ShapeDtypeStruct((1, LANES), F32)],
        compiler_params=_cparams(1),
        name="out_proj",
    )(ab, x2, w_out, nw, w_router, b_router)


PLAN_TM = 1024
SLOT_TILE = 256


TAB_EXPERT, TAB_FIRST, TAB_BUFFER, TAB_NEXT, TAB_VALID = 0, 1, 2, 3, 4
TAB_ROWS = 5
ETAB_PAD_LO, ETAB_PAD_HI, ETAB_MISC = 0, 1, 2
ETAB_ROWS = 3
MISC_N_USED, MISC_RUN_EXPERT = 0, 1
FETCH_AHEAD = 2
TABLE_ROWS = SUBLANES


def _slot_tables(counts_ref, tab_ref, etab_ref, run_expert, run_of_block, pstart_s, n_blocks):
    log_block = MOE_BLOCK.bit_length() - 1

    def clear(r, carry):
        run_expert[r] = -1
        return carry
    lax.fori_loop(0, run_expert.shape[0], clear, 0)

    def per_expert(e, carry):
        slot, blk, run, buf = carry
        c = counts_ref[e]
        n_blk = lax.shift_right_logical(c + (MOE_BLOCK - 1), log_block)
        pstart_s[e] = slot
        etab_ref[ETAB_PAD_LO, e] = slot + c
        etab_ref[ETAB_PAD_HI, e] = slot + n_blk * MOE_BLOCK

        def per_block(b, inner):
            tab_ref[TAB_EXPERT, b] = e
            tab_ref[TAB_FIRST, b] = (b == blk).astype(jnp.int32)
            tab_ref[TAB_BUFFER, b] = buf
            tab_ref[TAB_VALID, b] = jnp.minimum(c - (b - blk) * MOE_BLOCK, MOE_BLOCK)
            run_of_block[b] = run
            return inner
        lax.fori_loop(blk, blk + n_blk, per_block, 0)

        @pl.when(n_blk > 0)
        def _():
            run_expert[run] = e
        has_run = (n_blk > 0).astype(jnp.int32)
        next_buf = jnp.where(buf + has_run == FETCH_AHEAD, 0, buf + has_run)
        return slot + n_blk * MOE_BLOCK, blk + n_blk, run + has_run, next_buf

    zero = jnp.int32(0)
    _, n_used, _, _ = lax.fori_loop(0, N_EXPERTS, per_expert, (zero, zero, zero, zero))

    def per_used_block(b, carry):
        tab_ref[TAB_NEXT, b] = run_expert[run_of_block[b] + FETCH_AHEAD]
        return carry
    lax.fori_loop(0, n_used, per_used_block, 0)

    def per_unused_block(b, carry):
        for row in (TAB_EXPERT, TAB_FIRST, TAB_BUFFER, TAB_VALID):
            tab_ref[row, b] = 0
        tab_ref[TAB_NEXT, b] = -1
        return carry
    lax.fori_loop(n_used, n_blocks, per_unused_block, 0)

    def clear_misc(e, carry):
        etab_ref[ETAB_MISC, e] = 0
        return carry
    lax.fori_loop(0, N_EXPERTS, clear_misc, 0)
    etab_ref[ETAB_MISC, MISC_N_USED] = n_used
    for r in range(FETCH_AHEAD):
        etab_ref[ETAB_MISC, MISC_RUN_EXPERT + r] = run_expert[r]


def _plan_kernel(counts_ref, route_ref, tri_ref, dest_ref, tab_ref, etab_ref,
                 carry_ref, pstart_ref, run_expert, run_of_block, pstart_s, *, n_blocks):
    @pl.when(pl.program_id(0) == 0)
    def _():
        carry_ref[...] = jnp.zeros_like(carry_ref)
        _slot_tables(counts_ref, tab_ref, etab_ref, run_expert, run_of_block, pstart_s, n_blocks)
        lane_row = lax.broadcasted_iota(jnp.int32, (1, LANES), 1)

        def place(e, row):
            return jnp.where(lane_row == e, pstart_s[e].astype(F32), row)
        pstart_ref[...] = lax.fori_loop(0, N_EXPERTS, place, jnp.zeros((1, LANES), F32))

    route = route_ref[...]
    lane = lax.broadcasted_iota(jnp.int32, route.shape, 1)
    e0 = route[:, ROUTE_E0:ROUTE_E0 + 1].astype(jnp.int32)
    e1 = route[:, ROUTE_E1:ROUTE_E1 + 1].astype(jnp.int32)
    oh0 = lane == e0
    oh1 = lane == e1
    both = (oh0 | oh1).astype(F32)
    earlier = jnp.dot(tri_ref[...], both.astype(BF16), preferred_element_type=F32)
    base = pstart_ref[...] + carry_ref[...] + earlier
    d0 = jnp.sum(jnp.where(oh0, base, 0.0), axis=-1, keepdims=True)
    d1 = jnp.sum(jnp.where(oh1, base, 0.0), axis=-1, keepdims=True)
    by_token = jnp.where(lane == 0, d0, jnp.where(lane == 1, d1, 0.0))
    slots = by_token.T[:TABLE_ROWS].astype(jnp.int32)
    for piece in range(PLAN_TM // SLOT_TILE):
        dest_ref[:, piece, 0, :] = slots[:, piece * SLOT_TILE:(piece + 1) * SLOT_TILE]
    carry_ref[...] += jnp.sum(both, axis=0, keepdims=True)


def _plan(counts, route, n_blocks):
    T = route.shape[0]
    tm = PLAN_TM
    tri = jnp.asarray(np.tril(np.ones((tm, tm), np.float32), -1), BF16)
    smem_out = pl.BlockSpec(memory_space=pltpu.SMEM)
    grid_spec = pltpu.PrefetchScalarGridSpec(
        num_scalar_prefetch=1,
        grid=(T // tm,),
        in_specs=[pl.BlockSpec((tm, LANES), lambda i, c: (i, 0)),
                  pl.BlockSpec((tm, tm), lambda i, c: (0, 0))],
        out_specs=[pl.BlockSpec((TABLE_ROWS, tm // SLOT_TILE, 1, SLOT_TILE),
                                lambda i, c: (0, i, 0, 0)), smem_out, smem_out],
        scratch_shapes=[pltpu.VMEM((1, LANES), F32), pltpu.VMEM((1, LANES), F32),
                        pltpu.SMEM((N_EXPERTS + FETCH_AHEAD,), jnp.int32),
                        pltpu.SMEM((n_blocks,), jnp.int32),
                        pltpu.SMEM((N_EXPERTS,), jnp.int32)],
    )
    return pl.pallas_call(
        functools.partial(_plan_kernel, n_blocks=n_blocks),
        grid_spec=grid_spec,
        out_shape=[jax.ShapeDtypeStruct((TABLE_ROWS, T // SLOT_TILE, 1, SLOT_TILE), jnp.int32),
                   jax.ShapeDtypeStruct((TAB_ROWS, n_blocks), jnp.int32),
                   jax.ShapeDtypeStruct((ETAB_ROWS, N_EXPERTS), jnp.int32)],
        compiler_params=_cparams(1),
        name="plan",
    )(counts, route, tri)


DISP_TM = 1024
DMA_GROUP = 8
ROW_WORDS = SUBLANES


def _packed_row(ref, row):
    return ref.at[pl.ds(pl.multiple_of(row * ROW_WORDS, ROW_WORDS), ROW_WORDS)]


def _dispatch_kernel(etab_ref, dest0_ref, dest1_ref, xn_ref, xs_hbm,
                     zeros, sem, tail_sem, *, experts_per_step, tails_per_step, n_blocks):
    i = pl.program_id(0)

    @pl.when(i == 0)
    def _():
        zeros[...] = jnp.zeros_like(zeros)

    def scatter_group(t, carry):
        base = pl.multiple_of(t * DMA_GROUP, DMA_GROUP)
        window = xn_ref.at[pl.ds(base * ROW_WORDS, DMA_GROUP * ROW_WORDS)]
        piece = lax.shift_right_logical(base, SLOT_TILE.bit_length() - 1)
        offset = jnp.bitwise_and(base, SLOT_TILE - 1)
        for k in range(DMA_GROUP):
            src = window.at[pl.ds(k * ROW_WORDS, ROW_WORDS)]
            for choice, dest_ref in enumerate((dest0_ref, dest1_ref)):
                dst = _packed_row(xs_hbm, dest_ref[0, piece, 0, offset + k])
                pltpu.make_async_copy(src, dst, sem).start(priority=choice)
        return carry

    lax.fori_loop(0, DISP_TM // DMA_GROUP, scatter_group, 0)

    def pad_fill(start):
        for j in range(experts_per_step):
            e = jnp.minimum(i * experts_per_step + j, N_EXPERTS - 1)
            live = i * experts_per_step + j < N_EXPERTS
            pos = etab_ref[ETAB_PAD_LO, e]
            n_pad = jnp.where(live, etab_ref[ETAB_PAD_HI, e] - pos, 0)
            for bit in reversed(range(MOE_BLOCK.bit_length() - 1)):
                size = 1 << bit
                take = jnp.bitwise_and(lax.shift_right_logical(n_pad, bit), 1)
                dst = xs_hbm.at[pl.ds(pl.multiple_of(pos * ROW_WORDS, ROW_WORDS), size * ROW_WORDS)]
                cp = pltpu.make_async_copy(zeros.at[pl.ds(0, size * ROW_WORDS)], dst, sem)

                @pl.when(take == 1)
                def _():
                    cp.start() if start else cp.wait()
                pos = pos + take * size

    def tail_copies(start):
        for j in range(tails_per_step):
            blk = etab_ref[ETAB_MISC, MISC_N_USED] + i * tails_per_step + j
            rows = pl.ds(pl.multiple_of(jnp.minimum(blk, n_blocks - 1) * (MOE_BLOCK * ROW_WORDS),
                                        MOE_BLOCK * ROW_WORDS), MOE_BLOCK * ROW_WORDS)
            cp = pltpu.make_async_copy(zeros, xs_hbm.at[rows], tail_sem)

            @pl.when(blk < n_blocks)
            def _():
                cp.start() if start else cp.wait()

    pad_fill(True)
    tail_copies(True)

    tile_rows = DISP_TM * ROW_WORDS
    for _ in range(TOP_K):
        pltpu.make_async_copy(xn_ref, xs_hbm.at[pl.ds(0, tile_rows)], sem).wait()
    pad_fill(False)
    tail_copies(False)


def _slot_spec(tile, choice, index):
    return pl.BlockSpec((1, tile // SLOT_TILE, 1, SLOT_TILE),
                        lambda i, *_: (choice, index(i), 0, 0), memory_space=pltpu.SMEM)


def _dispatch(etab, dest4, xn_packed, n_blocks):
    n = dest4.shape[1] * SLOT_TILE // DISP_TM
    min_used = (n * DISP_TM * TOP_K) // MOE_BLOCK
    tails_per_step = -(-(n_blocks - min_used) // n)
    grid_spec = pltpu.PrefetchScalarGridSpec(
        num_scalar_prefetch=1,
        grid=(n,),
        in_specs=[_slot_spec(DISP_TM, 0, lambda i: i), _slot_spec(DISP_TM, 1, lambda i: i),
                  pl.BlockSpec((DISP_TM * ROW_WORDS, LANES), lambda i, *_: (i, 0))],
        out_specs=pl.BlockSpec(memory_space=pl.ANY),
        scratch_shapes=[pltpu.VMEM((MOE_BLOCK * ROW_WORDS, LANES), U32),
                        pltpu.SemaphoreType.DMA(()), pltpu.SemaphoreType.DMA(())],
    )
    return pl.pallas_call(
        functools.partial(_dispatch_kernel, experts_per_step=-(-N_EXPERTS // n),
                          tails_per_step=tails_per_step, n_blocks=n_blocks),
        grid_spec=grid_spec,
        out_shape=jax.ShapeDtypeStruct((n_blocks * MOE_BLOCK * ROW_WORDS, LANES), U32),
        compiler_params=_cparams(1),
        name="dispatch",
    )(etab, dest4, dest4, xn_packed)


CAST_ROWS = 256
FETCH_SPLIT = 4
BLOCKS_PER_STEP = 4


def _experts_kernel(tab_ref, etab_ref,
                    xs_ref, wg_hbm, wu_hbm, wd_hbm, ys_ref,
                    fg, fu, fd, wg, wu, wd, sem):
    n_used = etab_ref[ETAB_MISC, MISC_N_USED]
    for j in range(BLOCKS_PER_STEP):
        rows = pl.ds(j * MOE_BLOCK * SUBLANES, MOE_BLOCK * SUBLANES)
        _expert_block(pl.program_id(0) * BLOCKS_PER_STEP + j, n_used, tab_ref, etab_ref,
                      xs_ref.at[rows], wg_hbm, wu_hbm, wd_hbm, ys_ref.at[rows],
                      fg, fu, fd, wg, wu, wd, sem)


def _expert_block(b, n_used, tab_ref, etab_ref, xs_ref, wg_hbm, wu_hbm, wd_hbm, ys_ref,
                  fg, fu, fd, wg, wu, wd, sem):

    def fetch(e, p):
        copies = []
        for k, (src, dst) in enumerate(((wg_hbm, fg), (wu_hbm, fu), (wd_hbm, fd))):
            rows = src.shape[1] // FETCH_SPLIT
            for c in range(FETCH_SPLIT):
                sl = pl.ds(c * rows, rows)
                copies.append(pltpu.make_async_copy(src.at[e, sl], dst.at[p, sl], sem.at[p, k]))
        return copies

    def start_fetch(e, p):
        for k, cp in enumerate(fetch(e, p)):
            cp.start(priority=k % 2)

    @pl.when(b == 0)
    def _():
        for r in range(FETCH_AHEAD):
            e = etab_ref[ETAB_MISC, MISC_RUN_EXPERT + r]

            @pl.when(e >= 0)
            def _():
                start_fetch(e, r)

    @pl.when((b < n_used) & (tab_ref[TAB_FIRST, b] == 1))
    def _():
        p = tab_ref[TAB_BUFFER, b]
        for cp in fetch(tab_ref[TAB_EXPERT, b], p):
            cp.wait()

        def cast(src, dst, n_rows):
            def body(i, carry):
                rows = pl.ds(pl.multiple_of(i * CAST_ROWS, CAST_ROWS), CAST_ROWS)
                dst[rows, :] = src[p, rows, :].astype(BF16)
                return carry
            lax.fori_loop(0, n_rows // CAST_ROWS, body, 0)

        cast(fg, wg, D_MODEL)
        cast(fu, wu, D_MODEL)
        cast(fd, wd, D_EXPERT)

        @pl.when(tab_ref[TAB_NEXT, b] >= 0)
        def _():
            start_fetch(tab_ref[TAB_NEXT, b], p)

    def swiglu_rows(m_rows):
        lo_parts, hi_parts = [], []
        for c in range(PACK_CHUNKS):
            lo, hi = _unpack_pair(xs_ref[pl.ds(c, m_rows, stride=SUBLANES), :])
            lo_parts.append(lo.astype(BF16))
            hi_parts.append(hi.astype(BF16))
        xb = jnp.concatenate(lo_parts + hi_parts, axis=1)
        g = jnp.dot(xb, wg[...], preferred_element_type=F32)
        u = jnp.dot(xb, wu[...], preferred_element_type=F32)
        hmid = (g * jax.nn.sigmoid(g) * u).astype(BF16)
        y = jnp.dot(hmid, wd[...], preferred_element_type=F32)
        packed = _pack_pair(y[:, :HALF], y[:, HALF:])
        for c in range(PACK_CHUNKS):
            ys_ref[pl.ds(c, m_rows, stride=SUBLANES), :] = packed[:, c * LANES:(c + 1) * LANES]
        if m_rows < MOE_BLOCK:
            rest = pl.ds(m_rows * SUBLANES, (MOE_BLOCK - m_rows) * SUBLANES)
            ys_ref[rest, :] = jnp.zeros(((MOE_BLOCK - m_rows) * SUBLANES, LANES), U32)

    valid = tab_ref[TAB_VALID, b]

    @pl.when((b < n_used) & (valid > MOE_BLOCK // 2))
    def _():
        swiglu_rows(MOE_BLOCK)

    @pl.when((b < n_used) & (valid <= MOE_BLOCK // 2))
    def _():
        swiglu_rows(MOE_BLOCK // 2)

    @pl.when(b >= n_used)
    def _():
        ys_ref[...] = jnp.zeros(ys_ref.shape, ys_ref.dtype)


def _experts(tab, etab, xs, w_gate, w_up, w_down):
    n_blocks = tab.shape[1]
    assert n_blocks % BLOCKS_PER_STEP == 0
    blk = lambda: pl.BlockSpec((BLOCKS_PER_STEP * MOE_BLOCK * SUBLANES, LANES),
                               lambda s, *_: (s, 0))
    grid_spec = pltpu.PrefetchScalarGridSpec(
        num_scalar_prefetch=2,
        grid=(n_blocks // BLOCKS_PER_STEP,),
        in_specs=[blk(),
                  pl.BlockSpec(memory_space=pl.ANY),
                  pl.BlockSpec(memory_space=pl.ANY),
                  pl.BlockSpec(memory_space=pl.ANY)],
        out_specs=blk(),
        scratch_shapes=[pltpu.VMEM((FETCH_AHEAD, D_MODEL, D_EXPERT), F32),
                        pltpu.VMEM((FETCH_AHEAD, D_MODEL, D_EXPERT), F32),
                        pltpu.VMEM((FETCH_AHEAD, D_EXPERT, D_MODEL), F32),
                        pltpu.VMEM((D_MODEL, D_EXPERT), BF16),
                        pltpu.VMEM((D_MODEL, D_EXPERT), BF16),
                        pltpu.VMEM((D_EXPERT, D_MODEL), BF16),
                        pltpu.SemaphoreType.DMA((FETCH_AHEAD, 3))],
    )
    return pl.pallas_call(
        _experts_kernel,
        grid_spec=grid_spec,
        out_shape=jax.ShapeDtypeStruct((n_blocks * MOE_BLOCK * SUBLANES, LANES), U32),
        compiler_params=_cparams(1),
        name="experts",
    )(tab, etab, xs, w_gate, w_up, w_down)


CMB_TM = 256


def _combine_kernel(d0_cur_ref, d1_cur_ref, d0_next_ref, d1_next_ref, h_ref, route_ref, ys_hbm,
                    o_ref, ybuf, sem):
    i = pl.program_id(0)
    n = pl.num_programs(0)
    slot = i % 2
    n_rows = TOP_K * CMB_TM

    def gather(idx_refs, s):
        def gather_group(t, carry):
            base = pl.multiple_of(t * DMA_GROUP, DMA_GROUP)
            for choice, idx_ref in enumerate(idx_refs):
                first = (choice * CMB_TM + base) * ROW_WORDS
                window = ybuf.at[s, pl.ds(first, DMA_GROUP * ROW_WORDS)]
                for k in range(DMA_GROUP):
                    pltpu.make_async_copy(_packed_row(ys_hbm, idx_ref[0, 0, 0, base + k]),
                                          window.at[pl.ds(k * ROW_WORDS, ROW_WORDS)],
                                          sem.at[s]).start(priority=k % 2)
            return carry
        lax.fori_loop(0, CMB_TM // DMA_GROUP, gather_group, 0)

    @pl.when(i == 0)
    def _():
        gather((d0_cur_ref, d1_cur_ref), 0)

    @pl.when(i + 1 < n)
    def _():
        gather((d0_next_ref, d1_next_ref), 1 - slot)

    pltpu.make_async_copy(ys_hbm.at[pl.ds(0, n_rows * ROW_WORDS)], ybuf.at[slot],
                          sem.at[slot]).wait()
    route = route_ref[...]
    w0 = route[:, ROUTE_W0:ROUTE_W0 + 1]
    w1 = route[:, ROUTE_W1:ROUTE_W1 + 1]
    yb = ybuf.at[slot]
    for c in range(PACK_CHUNKS):
        lo0, hi0 = _unpack_pair(yb[pl.ds(c, CMB_TM, stride=SUBLANES), :])
        lo1, hi1 = _unpack_pair(yb[pl.ds(CMB_TM * SUBLANES + c, CMB_TM, stride=SUBLANES), :])
        lo_cols = slice(c * LANES, (c + 1) * LANES)
        hi_cols = slice(HALF + c * LANES, HALF + (c + 1) * LANES)
        o_ref[:, lo_cols] = h_ref[:, lo_cols] + (w0 * lo0 + w1 * lo1)
        o_ref[:, hi_cols] = h_ref[:, hi_cols] + (w0 * hi0 + w1 * hi1)


def _combine(dest4, h, route, ys):
    T = h.shape[0]
    tm = CMB_TM
    assert tm == SLOT_TILE
    n = T // tm
    cur = lambda i: i
    nxt = lambda i: jnp.minimum(i + 1, n - 1)
    return pl.pallas_call(
        _combine_kernel,
        grid=(n,),
        in_specs=[
            _slot_spec(tm, 0, cur), _slot_spec(tm, 1, cur),
            _slot_spec(tm, 0, nxt), _slot_spec(tm, 1, nxt),
            pl.BlockSpec((tm, D_MODEL), lambda i: (i, 0)),
            pl.BlockSpec((tm, LANES), lambda i: (i, 0)),
            pl.BlockSpec(memory_space=pl.ANY),
        ],
        out_specs=pl.BlockSpec((tm, D_MODEL), lambda i: (i, 0)),
        out_shape=jax.ShapeDtypeStruct((T, D_MODEL), F32),
        scratch_shapes=[pltpu.VMEM((2, TOP_K * tm * SUBLANES, LANES), U32),
                        pltpu.SemaphoreType.DMA((2,))],
        compiler_params=_cparams(1),
        name="combine",
    )(dest4, dest4, dest4, dest4, h, route, ys)


def _rope_tables(S):
    inv = (np.float32(ROPE_THETA) ** (-np.arange(0, HEAD_DIM, 2, dtype=np.float32)
                                      / np.float32(HEAD_DIM))).astype(np.float32)
    ang = (np.arange(S, dtype=np.float32)[:, None] * inv[None, :]).astype(np.float64)
    cos, sin = np.cos(ang).astype(np.float32), np.sin(ang).astype(np.float32)
    return (jnp.asarray(np.concatenate([cos, cos], axis=-1)),
            jnp.asarray(np.concatenate([-sin, sin], axis=-1)))


def kernel(x, norm_mix_w, w_in, w_pool, pool_scale, q_norm_w, k_norm_w, sink_logits, w_out,
           norm_ffn_w, w_group_router, b_group_router, w_expert_router, b_expert_router,
           w_gate, w_up, w_down):
    B, S, D = x.shape
    T = B * S
    depth = w_in.shape[0]
    cos, sin = _rope_tables(S)
    coef_np, invc_np, bias_np = _band_constants(S)
    coef = jnp.asarray(coef_np, BF16)
    invc = jnp.asarray(invc_np, F32)
    bias = jnp.asarray(bias_np, F32)
    pad_lanes = LANES - N_GROUPS - N_EXPERTS

    h = x.reshape(T, D)
    for l in range(depth):
        u, q, k, v = _in_proj(h, norm_mix_w[l].reshape(1, D), w_in[l].astype(BF16),
                              q_norm_w[l].reshape(1, HEAD_DIM), k_norm_w[l].reshape(1, HEAD_DIM),
                              cos, sin, S)
        ab = _mixers(sink_logits[l], u.reshape(B, S, -1), q.reshape(B, S, -1),
                     k.reshape(B, S, -1), v.reshape(B, S, -1), coef, invc, bias,
                     w_pool[l].astype(BF16), pool_scale[l].reshape(1, POOL_WIDTH))
        w_router = jnp.concatenate(
            [w_group_router[l], w_expert_router[l], jnp.zeros((D, pad_lanes), F32)], axis=1)
        b_router = jnp.concatenate(
            [b_group_router[l], b_expert_router[l], jnp.zeros((pad_lanes,), F32)]).reshape(1, LANES)
        hmix, xn_packed, route, counts = _out_proj(
            ab.reshape(T, MIX_WIDTH), h, w_out[l].astype(BF16), norm_ffn_w[l].reshape(1, D),
            w_router.astype(BF16), b_router)
        n_blocks = -(-(T * TOP_K + N_EXPERTS * (MOE_BLOCK - 1)) // MOE_BLOCK)
        dest, tab, etab = _plan(counts[0].astype(jnp.int32), route, n_blocks)
        xs = _dispatch(etab, dest, xn_packed, n_blocks)
        ys = _experts(tab, etab, xs, w_gate[l], w_up[l], w_down[l])
        h = _combine(dest, hmix, route, ys)
    return h.reshape(B, S, D)
```

```python
import functools

import numpy as np
import jax
import jax.numpy as jnp
from jax import lax
from jax.experimental import pallas as pl
from jax.experimental.pallas import tpu as pltpu

D_MODEL = 2048
POOL_WIDTH = 1024
POOL_WINDOWS = (2, 4, 8, 16)
N_POOL_GROUPS = 4
POOL_GROUP = 256
HEAD_DIM = 128
N_Q_HEADS = 8
N_KV_HEADS = 2
Q_PER_KV = 4
ATTN_WIDTH = 1024
KV_WIDTH = 256
IN_PROJ_WIDTH = 2560
MIX_WIDTH = 2048
WINDOW = 128
BLOCK = 128
BAND = 3 * BLOCK
ROPE_THETA = 10000.0
N_GROUPS = 8
EXPERTS_PER_GROUP = 8
N_EXPERTS = 64
TOP_K = 2
D_EXPERT = 512
MOE_BLOCK = 256
EPS = 1e-6

LANES = 128
SUBLANES = 8
HALF = D_MODEL // 2
PACK_CHUNKS = HALF // LANES
NEG_BIG = -1e30
VMEM_LIMIT = 56 * 1024 * 1024

BF16 = jnp.bfloat16
F32 = jnp.float32
U32 = jnp.uint32
HI_MASK = 0xFFFF0000


def _cparams(n_axes):
    return pltpu.CompilerParams(dimension_semantics=("arbitrary",) * n_axes,
                                vmem_limit_bytes=VMEM_LIMIT)


def _pack_pair(lo, hi):
    lo_bits = lax.bitcast_convert_type(lo.astype(BF16).astype(F32), U32) >> 16
    hi_bits = lax.bitcast_convert_type(hi.astype(BF16).astype(F32), U32) & jnp.uint32(HI_MASK)
    return lo_bits | hi_bits


def _unpack_pair(words):
    lo = lax.bitcast_convert_type(words << 16, F32)
    hi = lax.bitcast_convert_type(words & jnp.uint32(HI_MASK), F32)
    return lo, hi


IN_TM = 512
ROW_SPLIT = 2


def _in_proj_kernel(x_ref, nw_ref, w_ref, qnw_ref, knw_ref, cos_ref, sin_ref,
                    u_ref, q_ref, k_ref, v_ref):
    sub = IN_TM // ROW_SPLIT
    for part in range(ROW_SPLIT):
        rows = slice(part * sub, (part + 1) * sub)
        x = x_ref[rows, :]
        ms = jnp.mean(x * x, axis=-1, keepdims=True)
        xn = (x * lax.rsqrt(ms + EPS) * nw_ref[...]).astype(BF16)
        cos = cos_ref[rows, :]
        sin = sin_ref[rows, :]

        def head_norm_rope(t, w, scale):
            hms = jnp.mean(t * t, axis=-1, keepdims=True)
            t = t * lax.rsqrt(hms + EPS) * w
            t = t * cos + pltpu.roll(t, HEAD_DIM // 2, axis=1) * sin
            return t * scale

        u_ref[rows, :] = jnp.dot(xn, w_ref[:, :POOL_WIDTH],
                                 preferred_element_type=F32).astype(BF16)
        zq = jnp.dot(xn, w_ref[:, POOL_WIDTH:POOL_WIDTH + ATTN_WIDTH],
                     preferred_element_type=F32)
        qnw = qnw_ref[...]
        for h in range(N_Q_HEADS):
            sl = slice(h * HEAD_DIM, (h + 1) * HEAD_DIM)
            q_ref[rows, sl] = head_norm_rope(zq[:, sl], qnw, HEAD_DIM ** -0.5).astype(BF16)
        o_k = POOL_WIDTH + ATTN_WIDTH
        zk = jnp.dot(xn, w_ref[:, o_k:o_k + KV_WIDTH], preferred_element_type=F32)
        knw = knw_ref[...]
        for h in range(N_KV_HEADS):
            sl = slice(h * HEAD_DIM, (h + 1) * HEAD_DIM)
            k_ref[rows, sl] = head_norm_rope(zk[:, sl], knw, 1.0).astype(BF16)
        v_ref[rows, :] = jnp.dot(xn, w_ref[:, o_k + KV_WIDTH:],
                                 preferred_element_type=F32).astype(BF16)


def _in_proj(x2, nw, w_in, qnw, knw, cos, sin, seq):
    T = x2.shape[0]
    tm = IN_TM
    pos_blocks = seq // tm
    full = lambda shape: pl.BlockSpec(shape, lambda i: (0,) * len(shape))
    return pl.pallas_call(
        _in_proj_kernel,
        grid=(T // tm,),
        in_specs=[
            pl.BlockSpec((tm, D_MODEL), lambda i: (i, 0)),
            full((1, D_MODEL)),
            full((D_MODEL, IN_PROJ_WIDTH)),
            full((1, HEAD_DIM)),
            full((1, HEAD_DIM)),
            pl.BlockSpec((tm, HEAD_DIM), lambda i: (i % pos_blocks, 0)),
            pl.BlockSpec((tm, HEAD_DIM), lambda i: (i % pos_blocks, 0)),
        ],
        out_specs=[
            pl.BlockSpec((tm, POOL_WIDTH), lambda i: (i, 0)),
            pl.BlockSpec((tm, ATTN_WIDTH), lambda i: (i, 0)),
            pl.BlockSpec((tm, KV_WIDTH), lambda i: (i, 0)),
            pl.BlockSpec((tm, KV_WIDTH), lambda i: (i, 0)),
        ],
        out_shape=[
            jax.ShapeDtypeStruct((T, POOL_WIDTH), BF16),
            jax.ShapeDtypeStruct((T, ATTN_WIDTH), BF16),
            jax.ShapeDtypeStruct((T, KV_WIDTH), BF16),
            jax.ShapeDtypeStruct((T, KV_WIDTH), BF16),
        ],
        compiler_params=_cparams(1),
        name="in_proj",
    )(x2, nw, w_in, qnw, knw, cos, sin)


MIX_TQ = 512
MIX_STAGE_SUBS = 2


def _band_constants(seq):
    nb = seq // BLOCK
    coef = np.zeros((3, N_POOL_GROUPS, BLOCK, BAND), np.float32)
    inv_count = np.zeros((3, N_POOL_GROUPS, BLOCK, 1), np.float32)
    bias = np.zeros((3, BLOCK, BAND), np.float32)
    for kind, n in enumerate((0, 1, nb - 1)):
        start = min(max((n - 1) * BLOCK, 0), seq - BAND)
        t = n * BLOCK + np.arange(BLOCK)[:, None]
        s = start + np.arange(BAND)[None, :]
        bias[kind] = np.where(np.abs(s - t) <= WINDOW, 0.0, NEG_BIG)
        for g, win in enumerate(POOL_WINDOWS):
            half = win // 2
            lo = np.clip(t - half, 0, seq)
            hi = np.clip(t + half, 0, seq)
            count = (hi - lo).astype(np.float32)
            inside = ((s >= lo) & (s < hi)).astype(np.float32)
            coef[kind, g] = inside - count * (s == t)
            inv_count[kind, g] = 1.0 / count
    return coef, inv_count, bias


def _mixers_kernel(sink_ref, u_ref, q_ref, k_ref, v_ref, coef_ref, invc_ref, bias_ref,
                   wpool_ref, pscale_ref, ab_ref, *, seq):
    nb = seq // BLOCK
    j = pl.program_id(1)
    n_sub = MIX_TQ // BLOCK
    subs = []
    for r in range(n_sub):
        n = j * n_sub + r
        start = pl.multiple_of(jnp.clip((n - 1) * BLOCK, 0, seq - BAND), BLOCK)
        kind = jnp.where(n == 0, 0, jnp.where(n == nb - 1, 2, 1))
        subs.append((start, kind, slice(r * BLOCK, (r + 1) * BLOCK)))

    def staged(group):
        scores = {}
        for r, (start, kind, rows) in enumerate(group):
            for hk in range(N_KV_HEADS):
                kb = k_ref[0, pl.ds(start, BAND), hk * HEAD_DIM:(hk + 1) * HEAD_DIM]
                qs = jnp.concatenate(
                    [q_ref[0, rows, h * HEAD_DIM:(h + 1) * HEAD_DIM]
                     for h in range(hk * Q_PER_KV, (hk + 1) * Q_PER_KV)], axis=0)
                scores[r, hk] = lax.dot_general(qs, kb, (((1,), (1,)), ((), ())),
                                                preferred_element_type=F32)
        diffs = {}
        for r, (start, kind, rows) in enumerate(group):
            for g in range(N_POOL_GROUPS):
                ub = u_ref[0, pl.ds(start, BAND), g * POOL_GROUP:(g + 1) * POOL_GROUP]
                d = jnp.dot(coef_ref[kind, g], ub, preferred_element_type=F32) * invc_ref[kind, g]
                diffs[r, g] = d.astype(BF16)
        probs = {}
        for r, (start, kind, rows) in enumerate(group):
            bias = bias_ref[kind]
            for hk in range(N_KV_HEADS):
                for gi in range(Q_PER_KV):
                    h = hk * Q_PER_KV + gi
                    sh = scores[r, hk][gi * BLOCK:(gi + 1) * BLOCK] + bias
                    sink = sink_ref[h]
                    m = jnp.maximum(jnp.max(sh, axis=-1, keepdims=True), sink)
                    p = jnp.exp(sh - m)
                    denom = jnp.sum(p, axis=-1, keepdims=True) + jnp.exp(sink - m)
                    probs[r, h] = (p.astype(BF16), denom)
        for r, (start, kind, rows) in enumerate(group):
            for g in range(N_POOL_GROUPS):
                cols = slice(g * POOL_GROUP, (g + 1) * POOL_GROUP)
                y = jnp.dot(diffs[r, g], wpool_ref[g], preferred_element_type=F32)
                ab_ref[0, rows, cols] = (y * pscale_ref[:, cols]).astype(BF16)
        for r, (start, kind, rows) in enumerate(group):
            for hk in range(N_KV_HEADS):
                vb = v_ref[0, pl.ds(start, BAND), hk * HEAD_DIM:(hk + 1) * HEAD_DIM]
                for gi in range(Q_PER_KV):
                    h = hk * Q_PER_KV + gi
                    p, denom = probs[r, h]
                    o = jnp.dot(p, vb, preferred_element_type=F32) / denom
                    ocols = slice(POOL_WIDTH + h * HEAD_DIM, POOL_WIDTH + (h + 1) * HEAD_DIM)
                    ab_ref[0, rows, ocols] = o.astype(BF16)

    for first in range(0, n_sub, MIX_STAGE_SUBS):
        staged(subs[first:first + MIX_STAGE_SUBS])


def _mixers(sink, u, q, k, v, coef, invc, bias, wpool, pscale):
    B, S, _ = u.shape
    tq = MIX_TQ
    full = lambda shape: pl.BlockSpec(shape, lambda b, j: (0,) * len(shape))
    return pl.pallas_call(
        functools.partial(_mixers_kernel, seq=S),
        grid=(B, S // tq),
        in_specs=[
            pl.BlockSpec(memory_space=pltpu.SMEM),
            pl.BlockSpec((1, S, POOL_WIDTH), lambda b, j: (b, 0, 0)),
            pl.BlockSpec((1, tq, ATTN_WIDTH), lambda b, j: (b, j, 0)),
            pl.BlockSpec((1, S, KV_WIDTH), lambda b, j: (b, 0, 0)),
            pl.BlockSpec((1, S, KV_WIDTH), lambda b, j: (b, 0, 0)),
            full((3, N_POOL_GROUPS, BLOCK, BAND)),
            full((3, N_POOL_GROUPS, BLOCK, 1)),
            full((3, BLOCK, BAND)),
            full((N_POOL_GROUPS, POOL_GROUP, POOL_GROUP)),
            full((1, POOL_WIDTH)),
        ],
        out_specs=pl.BlockSpec((1, tq, MIX_WIDTH), lambda b, j: (b, j, 0)),
        out_shape=jax.ShapeDtypeStruct((B, S, MIX_WIDTH), BF16),
        compiler_params=_cparams(2),
        name="mixers",
    )(sink, u, q, k, v, coef, invc, bias, wpool, pscale)


OUT_TM = 512
ROUTE_E0, ROUTE_E1, ROUTE_W0, ROUTE_W1 = 0, 1, 2, 3


def _out_proj_kernel(ab_ref, x_ref, w_ref, nw_ref, wr_ref, br_ref,
                     h_ref, xn_ref, route_ref, counts_ref):
    @pl.when(pl.program_id(0) == 0)
    def _():
        counts_ref[...] = jnp.zeros_like(counts_ref)

    sub = OUT_TM // ROW_SPLIT
    lane = lax.broadcasted_iota(jnp.int32, (sub, LANES), 1)

    def first_argmax(vals):
        m = jnp.max(vals, axis=-1, keepdims=True)
        idx = jnp.min(jnp.where(vals == m, lane, LANES), axis=-1, keepdims=True)
        return m, idx

    new_counts = jnp.zeros((1, LANES), F32)
    for part in range(ROW_SPLIT):
        rows = slice(part * sub, (part + 1) * sub)
        h = x_ref[rows, :] + jnp.dot(ab_ref[rows, :], w_ref[...], preferred_element_type=F32)
        h_ref[rows, :] = h
        ms = jnp.mean(h * h, axis=-1, keepdims=True)
        xn = h * lax.rsqrt(ms + EPS) * nw_ref[...]
        packed = _pack_pair(xn[:, :HALF], xn[:, HALF:])
        for c in range(PACK_CHUNKS):
            xn_ref[pl.ds(part * sub * SUBLANES + c, sub, stride=SUBLANES), :] = (
                packed[:, c * LANES:(c + 1) * LANES])
        logits = jnp.dot(xn.astype(BF16), wr_ref[...], preferred_element_type=F32) + br_ref[...]

        gl = jnp.where(lane < N_GROUPS, logits, NEG_BIG)
        gmax, gidx = first_argmax(gl)
        gsum = jnp.sum(jnp.where(lane < N_GROUPS, jnp.exp(gl - gmax), 0.0),
                       axis=-1, keepdims=True)
        g_w = 1.0 / gsum
        e_lo = N_GROUPS + gidx * EXPERTS_PER_GROUP
        el = jnp.where((lane >= e_lo) & (lane < e_lo + EXPERTS_PER_GROUP), logits, NEG_BIG)
        m1, i1 = first_argmax(el)
        m2, i2 = first_argmax(jnp.where(lane == i1, NEG_BIG, el))
        t = jnp.exp(m2 - m1)
        p1 = 1.0 / (1.0 + t)
        p2 = t * p1
        e1 = i1 - N_GROUPS
        e2 = i2 - N_GROUPS
        route_ref[rows, :] = jnp.where(
            lane == ROUTE_E0, e1.astype(F32),
            jnp.where(lane == ROUTE_E1, e2.astype(F32),
                      jnp.where(lane == ROUTE_W0, g_w * p1,
                                jnp.where(lane == ROUTE_W1, g_w * p2, 0.0))))
        chosen = ((lane == e1) | (lane == e2)).astype(F32)
        new_counts = new_counts + jnp.sum(chosen, axis=0, keepdims=True)
    counts_ref[...] += new_counts


def _out_proj(ab, x2, w_out, nw, w_router, b_router):
    T = x2.shape[0]
    tm = OUT_TM
    full = lambda shape: pl.BlockSpec(shape, lambda i: (0,) * len(shape))
    row = lambda width: pl.BlockSpec((tm, width), lambda i: (i, 0))
    return pl.pallas_call(
        _out_proj_kernel,
        grid=(T // tm,),
        in_specs=[row(MIX_WIDTH), row(D_MODEL), full((MIX_WIDTH, D_MODEL)), full((1, D_MODEL)),
                  full((D_MODEL, LANES)), full((1, LANES))],
        out_specs=[row(D_MODEL),
                   pl.BlockSpec((tm * SUBLANES, LANES), lambda i: (i, 0)),
                   row(LANES),
                   full((1, LANES))],
        out_shape=[jax.ShapeDtypeStruct((T, D_MODEL), F32),
                   jax.ShapeDtypeStruct((T * SUBLANES, LANES), U32),
                   jax.ShapeDtypeStruct((T, LANES), F32),
                   jax.ShapeDtypeStruct((1, LANES), F32)],
        compiler_params=_cparams(1),
        name="out_proj",
    )(ab, x2, w_out, nw, w_router, b_router)


PLAN_TM = 1024
SLOT_TILE = 1024


TAB_EXPERT, TAB_FIRST, TAB_BUFFER, TAB_NEXT, TAB_VALID = 0, 1, 2, 3, 4
TAB_ROWS = 5
ETAB_PAD_LO, ETAB_PAD_HI, ETAB_MISC = 0, 1, 2
ETAB_ROWS = 3
MISC_N_USED, MISC_RUN_EXPERT = 0, 1
FETCH_AHEAD = 2
TABLE_ROWS = SUBLANES


def _slot_tables(counts_ref, tab_ref, etab_ref, run_expert, run_of_block, pstart_s, n_blocks):
    log_block = MOE_BLOCK.bit_length() - 1

    def clear(r, carry):
        run_expert[r] = -1
        return carry
    lax.fori_loop(0, run_expert.shape[0], clear, 0)

    def per_expert(e, carry):
        slot, blk, run, buf = carry
        c = counts_ref[e]
        n_blk = lax.shift_right_logical(c + (MOE_BLOCK - 1), log_block)
        pstart_s[e] = slot
        etab_ref[ETAB_PAD_LO, e] = slot + c
        etab_ref[ETAB_PAD_HI, e] = slot + n_blk * MOE_BLOCK

        def per_block(b, inner):
            tab_ref[TAB_EXPERT, b] = e
            tab_ref[TAB_FIRST, b] = (b == blk).astype(jnp.int32)
            tab_ref[TAB_BUFFER, b] = buf
            tab_ref[TAB_VALID, b] = jnp.minimum(c - (b - blk) * MOE_BLOCK, MOE_BLOCK)
            run_of_block[b] = run
            return inner
        lax.fori_loop(blk, blk + n_blk, per_block, 0)

        @pl.when(n_blk > 0)
        def _():
            run_expert[run] = e
        has_run = (n_blk > 0).astype(jnp.int32)
        next_buf = jnp.where(buf + has_run == FETCH_AHEAD, 0, buf + has_run)
        return slot + n_blk * MOE_BLOCK, blk + n_blk, run + has_run, next_buf

    zero = jnp.int32(0)
    _, n_used, _, _ = lax.fori_loop(0, N_EXPERTS, per_expert, (zero, zero, zero, zero))

    def per_used_block(b, carry):
        tab_ref[TAB_NEXT, b] = run_expert[run_of_block[b] + FETCH_AHEAD]
        return carry
    lax.fori_loop(0, n_used, per_used_block, 0)

    def per_unused_block(b, carry):
        for row in (TAB_EXPERT, TAB_FIRST, TAB_BUFFER, TAB_VALID):
            tab_ref[row, b] = 0
        tab_ref[TAB_NEXT, b] = -1
        return carry
    lax.fori_loop(n_used, n_blocks, per_unused_block, 0)

    def clear_misc(e, carry):
        etab_ref[ETAB_MISC, e] = 0
        return carry
    lax.fori_loop(0, N_EXPERTS, clear_misc, 0)
    etab_ref[ETAB_MISC, MISC_N_USED] = n_used
    for r in range(FETCH_AHEAD):
        etab_ref[ETAB_MISC, MISC_RUN_EXPERT + r] = run_expert[r]


def _plan_kernel(counts_ref, route_ref, tri_ref, dest_ref, tab_ref, etab_ref,
                 carry_ref, pstart_ref, run_expert, run_of_block, pstart_s, *, n_blocks):
    @pl.when(pl.program_id(0) == 0)
    def _():
        carry_ref[...] = jnp.zeros_like(carry_ref)
        _slot_tables(counts_ref, tab_ref, etab_ref, run_expert, run_of_block, pstart_s, n_blocks)
        lane_row = lax.broadcasted_iota(jnp.int32, (1, LANES), 1)

        def place(e, row):
            return jnp.where(lane_row == e, pstart_s[e].astype(F32), row)
        pstart_ref[...] = lax.fori_loop(0, N_EXPERTS, place, jnp.zeros((1, LANES), F32))

    route = route_ref[...]
    lane = lax.broadcasted_iota(jnp.int32, route.shape, 1)
    e0 = route[:, ROUTE_E0:ROUTE_E0 + 1].astype(jnp.int32)
    e1 = route[:, ROUTE_E1:ROUTE_E1 + 1].astype(jnp.int32)
    oh0 = lane == e0
    oh1 = lane == e1
    both = (oh0 | oh1).astype(F32)
    earlier = jnp.dot(tri_ref[...], both.astype(BF16), preferred_element_type=F32)
    base = pstart_ref[...] + carry_ref[...] + earlier
    d0 = jnp.sum(jnp.where(oh0, base, 0.0), axis=-1, keepdims=True)
    d1 = jnp.sum(jnp.where(oh1, base, 0.0), axis=-1, keepdims=True)
    by_token = jnp.where(lane == 0, d0, jnp.where(lane == 1, d1, 0.0))
    slots = by_token.T[:TABLE_ROWS].astype(jnp.int32)
    for piece in range(PLAN_TM // SLOT_TILE):
        dest_ref[:, piece, 0, :] = slots[:, piece * SLOT_TILE:(piece + 1) * SLOT_TILE]
    carry_ref[...] += jnp.sum(both, axis=0, keepdims=True)


def _plan(counts, route, n_blocks):
    T = route.shape[0]
    tm = PLAN_TM
    tri = jnp.asarray(np.tril(np.ones((tm, tm), np.float32), -1), BF16)
    smem_out = pl.BlockSpec(memory_space=pltpu.SMEM)
    grid_spec = pltpu.PrefetchScalarGridSpec(
        num_scalar_prefetch=1,
        grid=(T // tm,),
        in_specs=[pl.BlockSpec((tm, LANES), lambda i, c: (i, 0)),
                  pl.BlockSpec((tm, tm), lambda i, c: (0, 0))],
        out_specs=[pl.BlockSpec((TABLE_ROWS, tm // SLOT_TILE, 1, SLOT_TILE),
                                lambda i, c: (0, i, 0, 0)), smem_out, smem_out],
        scratch_shapes=[pltpu.VMEM((1, LANES), F32), pltpu.VMEM((1, LANES), F32),
                        pltpu.SMEM((N_EXPERTS + FETCH_AHEAD,), jnp.int32),
                        pltpu.SMEM((n_blocks,), jnp.int32),
                        pltpu.SMEM((N_EXPERTS,), jnp.int32)],
    )
    return pl.pallas_call(
        functools.partial(_plan_kernel, n_blocks=n_blocks),
        grid_spec=grid_spec,
        out_shape=[jax.ShapeDtypeStruct((TABLE_ROWS, T // SLOT_TILE, 1, SLOT_TILE), jnp.int32),
                   jax.ShapeDtypeStruct((TAB_ROWS, n_blocks), jnp.int32),
                   jax.ShapeDtypeStruct((ETAB_ROWS, N_EXPERTS), jnp.int32)],
        compiler_params=_cparams(1),
        name="plan",
    )(counts, route, tri)


DISP_TM = 1024
DMA_GROUP = 8
ROW_WORDS = SUBLANES


def _packed_row(ref, row):
    return ref.at[pl.ds(pl.multiple_of(row * ROW_WORDS, ROW_WORDS), ROW_WORDS)]


def _dispatch_kernel(etab_ref, dest0_ref, dest1_ref, xn_ref, xs_hbm,
                     zeros, sem, tail_sem, *, experts_per_step, tails_per_step, n_blocks):
    i = pl.program_id(0)

    @pl.when(i == 0)
    def _():
        zeros[...] = jnp.zeros_like(zeros)

    def scatter_group(t, carry):
        base = pl.multiple_of(t * DMA_GROUP, DMA_GROUP)
        window = xn_ref.at[pl.ds(base * ROW_WORDS, DMA_GROUP * ROW_WORDS)]
        for k in range(DMA_GROUP):
            src = window.at[pl.ds(k * ROW_WORDS, ROW_WORDS)]
            for choice, dest_ref in enumerate((dest0_ref, dest1_ref)):
                dst = _packed_row(xs_hbm, dest_ref[0, 0, 0, base + k])
                pltpu.make_async_copy(src, dst, sem).start(priority=choice)
        return carry

    lax.fori_loop(0, DISP_TM // DMA_GROUP, scatter_group, 0)

    def pad_fill(start):
        for j in range(experts_per_step):
            e = jnp.minimum(i * experts_per_step + j, N_EXPERTS - 1)
            live = i * experts_per_step + j < N_EXPERTS
            pos = etab_ref[ETAB_PAD_LO, e]
            n_pad = jnp.where(live, etab_ref[ETAB_PAD_HI, e] - pos, 0)
            for bit in reversed(range(MOE_BLOCK.bit_length() - 1)):
                size = 1 << bit
                take = jnp.bitwise_and(lax.shift_right_logical(n_pad, bit), 1)
                dst = xs_hbm.at[pl.ds(pl.multiple_of(pos * ROW_WORDS, ROW_WORDS), size * ROW_WORDS)]
                cp = pltpu.make_async_copy(zeros.at[pl.ds(0, size * ROW_WORDS)], dst, sem)

                @pl.when(take == 1)
                def _():
                    cp.start() if start else cp.wait()
                pos = pos + take * size

    def tail_copies(start):
        for j in range(tails_per_step):
            blk = etab_ref[ETAB_MISC, MISC_N_USED] + i * tails_per_step + j
            rows = pl.ds(pl.multiple_of(jnp.minimum(blk, n_blocks - 1) * (MOE_BLOCK * ROW_WORDS),
                                        MOE_BLOCK * ROW_WORDS), MOE_BLOCK * ROW_WORDS)
            cp = pltpu.make_async_copy(zeros, xs_hbm.at[rows], tail_sem)

            @pl.when(blk < n_blocks)
            def _():
                cp.start() if start else cp.wait()

    pad_fill(True)
    tail_copies(True)

    tile_rows = DISP_TM * ROW_WORDS
    for _ in range(TOP_K):
        pltpu.make_async_copy(xn_ref, xs_hbm.at[pl.ds(0, tile_rows)], sem).wait()
    pad_fill(False)
    tail_copies(False)


def _slot_spec(choice, index):
    return pl.BlockSpec((1, 1, 1, SLOT_TILE), lambda i, *_: (choice, index(i), 0, 0),
                        memory_space=pltpu.SMEM)


def _dispatch(etab, dest4, xn_packed, n_blocks):
    assert DISP_TM == SLOT_TILE
    n = dest4.shape[1]
    min_used = (n * DISP_TM * TOP_K) // MOE_BLOCK
    tails_per_step = -(-(n_blocks - min_used) // n)
    grid_spec = pltpu.PrefetchScalarGridSpec(
        num_scalar_prefetch=1,
        grid=(n,),
        in_specs=[_slot_spec(0, lambda i: i), _slot_spec(1, lambda i: i),
                  pl.BlockSpec((DISP_TM * ROW_WORDS, LANES), lambda i, *_: (i, 0))],
        out_specs=pl.BlockSpec(memory_space=pl.ANY),
        scratch_shapes=[pltpu.VMEM((MOE_BLOCK * ROW_WORDS, LANES), U32),
                        pltpu.SemaphoreType.DMA(()), pltpu.SemaphoreType.DMA(())],
    )
    return pl.pallas_call(
        functools.partial(_dispatch_kernel, experts_per_step=-(-N_EXPERTS // n),
                          tails_per_step=tails_per_step, n_blocks=n_blocks),
        grid_spec=grid_spec,
        out_shape=jax.ShapeDtypeStruct((n_blocks * MOE_BLOCK * ROW_WORDS, LANES), U32),
        compiler_params=_cparams(1),
        name="dispatch",
    )(etab, dest4, dest4, xn_packed)


CAST_ROWS = 256
FETCH_SPLIT = 4
BLOCKS_PER_STEP = 4


def _experts_kernel(tab_ref, etab_ref,
                    xs_ref, wg_hbm, wu_hbm, wd_hbm, ys_ref,
                    fg, fu, fd, wg, wu, wd, sem):
    n_used = etab_ref[ETAB_MISC, MISC_N_USED]
    for j in range(BLOCKS_PER_STEP):
        rows = pl.ds(j * MOE_BLOCK * SUBLANES, MOE_BLOCK * SUBLANES)
        _expert_block(pl.program_id(0) * BLOCKS_PER_STEP + j, n_used, tab_ref, etab_ref,
                      xs_ref.at[rows], wg_hbm, wu_hbm, wd_hbm, ys_ref.at[rows],
                      fg, fu, fd, wg, wu, wd, sem)


def _expert_block(b, n_used, tab_ref, etab_ref, xs_ref, wg_hbm, wu_hbm, wd_hbm, ys_ref,
                  fg, fu, fd, wg, wu, wd, sem):

    def fetch(e, p):
        copies = []
        for k, (src, dst) in enumerate(((wg_hbm, fg), (wu_hbm, fu), (wd_hbm, fd))):
            rows = src.shape[1] // FETCH_SPLIT
            for c in range(FETCH_SPLIT):
                sl = pl.ds(c * rows, rows)
                copies.append(pltpu.make_async_copy(src.at[e, sl], dst.at[p, sl], sem.at[p, k]))
        return copies

    def start_fetch(e, p):
        for k, cp in enumerate(fetch(e, p)):
            cp.start(priority=k % 2)

    @pl.when(b == 0)
    def _():
        for r in range(FETCH_AHEAD):
            e = etab_ref[ETAB_MISC, MISC_RUN_EXPERT + r]

            @pl.when(e >= 0)
            def _():
                start_fetch(e, r)

    @pl.when((b < n_used) & (tab_ref[TAB_FIRST, b] == 1))
    def _():
        p = tab_ref[TAB_BUFFER, b]
        for cp in fetch(tab_ref[TAB_EXPERT, b], p):
            cp.wait()

        def cast(src, dst, n_rows):
            def body(i, carry):
                rows = pl.ds(pl.multiple_of(i * CAST_ROWS, CAST_ROWS), CAST_ROWS)
                dst[rows, :] = src[p, rows, :].astype(BF16)
                return carry
            lax.fori_loop(0, n_rows // CAST_ROWS, body, 0)

        cast(fg, wg, D_MODEL)
        cast(fu, wu, D_MODEL)
        cast(fd, wd, D_EXPERT)

        @pl.when(tab_ref[TAB_NEXT, b] >= 0)
        def _():
            start_fetch(tab_ref[TAB_NEXT, b], p)

    def swiglu_rows(m_rows):
        lo_parts, hi_parts = [], []
        for c in range(PACK_CHUNKS):
            lo, hi = _unpack_pair(xs_ref[pl.ds(c, m_rows, stride=SUBLANES), :])
            lo_parts.append(lo.astype(BF16))
            hi_parts.append(hi.astype(BF16))
        xb = jnp.concatenate(lo_parts + hi_parts, axis=1)
        g = jnp.dot(xb, wg[...], preferred_element_type=F32)
        u = jnp.dot(xb, wu[...], preferred_element_type=F32)
        hmid = (g * jax.nn.sigmoid(g) * u).astype(BF16)
        y = jnp.dot(hmid, wd[...], preferred_element_type=F32)
        packed = _pack_pair(y[:, :HALF], y[:, HALF:])
        for c in range(PACK_CHUNKS):
            ys_ref[pl.ds(c, m_rows, stride=SUBLANES), :] = packed[:, c * LANES:(c + 1) * LANES]
        if m_rows < MOE_BLOCK:
            rest = pl.ds(m_rows * SUBLANES, (MOE_BLOCK - m_rows) * SUBLANES)
            ys_ref[rest, :] = jnp.zeros(((MOE_BLOCK - m_rows) * SUBLANES, LANES), U32)

    valid = tab_ref[TAB_VALID, b]

    @pl.when((b < n_used) & (valid > MOE_BLOCK // 2))
    def _():
        swiglu_rows(MOE_BLOCK)

    @pl.when((b < n_used) & (valid <= MOE_BLOCK // 2))
    def _():
        swiglu_rows(MOE_BLOCK // 2)

    @pl.when(b >= n_used)
    def _():
        ys_ref[...] = jnp.zeros(ys_ref.shape, ys_ref.dtype)


def _experts(tab, etab, xs, w_gate, w_up, w_down):
    n_blocks = tab.shape[1]
    assert n_blocks % BLOCKS_PER_STEP == 0
    blk = lambda: pl.BlockSpec((BLOCKS_PER_STEP * MOE_BLOCK * SUBLANES, LANES),
                               lambda s, *_: (s, 0))
    grid_spec = pltpu.PrefetchScalarGridSpec(
        num_scalar_prefetch=2,
        grid=(n_blocks // BLOCKS_PER_STEP,),
        in_specs=[blk(),
                  pl.BlockSpec(memory_space=pl.ANY),
                  pl.BlockSpec(memory_space=pl.ANY),
                  pl.BlockSpec(memory_space=pl.ANY)],
        out_specs=blk(),
        scratch_shapes=[pltpu.VMEM((FETCH_AHEAD, D_MODEL, D_EXPERT), F32),
                        pltpu.VMEM((FETCH_AHEAD, D_MODEL, D_EXPERT), F32),
                        pltpu.VMEM((FETCH_AHEAD, D_EXPERT, D_MODEL), F32),
                        pltpu.VMEM((D_MODEL, D_EXPERT), BF16),
                        pltpu.VMEM((D_MODEL, D_EXPERT), BF16),
                        pltpu.VMEM((D_EXPERT, D_MODEL), BF16),
                        pltpu.SemaphoreType.DMA((FETCH_AHEAD, 3))],
    )
    return pl.pallas_call(
        _experts_kernel,
        grid_spec=grid_spec,
        out_shape=jax.ShapeDtypeStruct((n_blocks * MOE_BLOCK * SUBLANES, LANES), U32),
        compiler_params=_cparams(1),
        name="experts",
    )(tab, etab, xs, w_gate, w_up, w_down)


CMB_TM = 256


def _combine_kernel(d0_cur_ref, d1_cur_ref, d0_next_ref, d1_next_ref, h_ref, route_ref, ys_hbm,
                    o_ref, ybuf, sem):
    i = pl.program_id(0)
    n = pl.num_programs(0)
    slot = i % 2
    n_rows = TOP_K * CMB_TM

    def gather(idx_refs, tile, s):
        in_piece = jnp.bitwise_and(tile, SLOT_TILE // CMB_TM - 1) * CMB_TM

        def gather_group(t, carry):
            base = pl.multiple_of(t * DMA_GROUP, DMA_GROUP)
            for choice, idx_ref in enumerate(idx_refs):
                first = (choice * CMB_TM + base) * ROW_WORDS
                window = ybuf.at[s, pl.ds(first, DMA_GROUP * ROW_WORDS)]
                for k in range(DMA_GROUP):
                    row = idx_ref[0, 0, 0, in_piece + base + k]
                    pltpu.make_async_copy(_packed_row(ys_hbm, row),
                                          window.at[pl.ds(k * ROW_WORDS, ROW_WORDS)],
                                          sem.at[s]).start(priority=k % 2)
            return carry
        lax.fori_loop(0, CMB_TM // DMA_GROUP, gather_group, 0)

    @pl.when(i == 0)
    def _():
        gather((d0_cur_ref, d1_cur_ref), i, 0)

    @pl.when(i + 1 < n)
    def _():
        gather((d0_next_ref, d1_next_ref), i + 1, 1 - slot)

    pltpu.make_async_copy(ys_hbm.at[pl.ds(0, n_rows * ROW_WORDS)], ybuf.at[slot],
                          sem.at[slot]).wait()
    route = route_ref[...]
    w0 = route[:, ROUTE_W0:ROUTE_W0 + 1]
    w1 = route[:, ROUTE_W1:ROUTE_W1 + 1]
    yb = ybuf.at[slot]
    for c in range(PACK_CHUNKS):
        lo0, hi0 = _unpack_pair(yb[pl.ds(c, CMB_TM, stride=SUBLANES), :])
        lo1, hi1 = _unpack_pair(yb[pl.ds(CMB_TM * SUBLANES + c, CMB_TM, stride=SUBLANES), :])
        lo_cols = slice(c * LANES, (c + 1) * LANES)
        hi_cols = slice(HALF + c * LANES, HALF + (c + 1) * LANES)
        o_ref[:, lo_cols] = h_ref[:, lo_cols] + (w0 * lo0 + w1 * lo1)
        o_ref[:, hi_cols] = h_ref[:, hi_cols] + (w0 * hi0 + w1 * hi1)


def _combine(dest4, h, route, ys):
    T = h.shape[0]
    tm = CMB_TM
    per_piece = SLOT_TILE // tm
    assert per_piece & (per_piece - 1) == 0
    n = T // tm
    cur = lambda i: i // per_piece
    nxt = lambda i: jnp.minimum(i + 1, n - 1) // per_piece
    return pl.pallas_call(
        _combine_kernel,
        grid=(n,),
        in_specs=[
            _slot_spec(0, cur), _slot_spec(1, cur),
            _slot_spec(0, nxt), _slot_spec(1, nxt),
            pl.BlockSpec((tm, D_MODEL), lambda i: (i, 0)),
            pl.BlockSpec((tm, LANES), lambda i: (i, 0)),
            pl.BlockSpec(memory_space=pl.ANY),
        ],
        out_specs=pl.BlockSpec((tm, D_MODEL), lambda i: (i, 0)),
        out_shape=jax.ShapeDtypeStruct((T, D_MODEL), F32),
        scratch_shapes=[pltpu.VMEM((2, TOP_K * tm * SUBLANES, LANES), U32),
                        pltpu.SemaphoreType.DMA((2,))],
        compiler_params=_cparams(1),
        name="combine",
    )(dest4, dest4, dest4, dest4, h, route, ys)


def _rope_tables(S):
    inv = (np.float32(ROPE_THETA) ** (-np.arange(0, HEAD_DIM, 2, dtype=np.float32)
                                      / np.float32(HEAD_DIM))).astype(np.float32)
    ang = (np.arange(S, dtype=np.float32)[:, None] * inv[None, :]).astype(np.float64)
    cos, sin = np.cos(ang).astype(np.float32), np.sin(ang).astype(np.float32)
    return (jnp.asarray(np.concatenate([cos, cos], axis=-1)),
            jnp.asarray(np.concatenate([-sin, sin], axis=-1)))


def kernel(x, norm_mix_w, w_in, w_pool, pool_scale, q_norm_w, k_norm_w, sink_logits, w_out,
           norm_ffn_w, w_group_router, b_group_router, w_expert_router, b_expert_router,
           w_gate, w_up, w_down):
    B, S, D = x.shape
    T = B * S
    depth = w_in.shape[0]
    cos, sin = _rope_tables(S)
    coef_np, invc_np, bias_np = _band_constants(S)
    coef = jnp.asarray(coef_np, BF16)
    invc = jnp.asarray(invc_np, F32)
    bias = jnp.asarray(bias_np, F32)
    pad_lanes = LANES - N_GROUPS - N_EXPERTS

    h = x.reshape(T, D)
    for l in range(depth):
        u, q, k, v = _in_proj(h, norm_mix_w[l].reshape(1, D), w_in[l].astype(BF16),
                              q_norm_w[l].reshape(1, HEAD_DIM), k_norm_w[l].reshape(1, HEAD_DIM),
                              cos, sin, S)
        ab = _mixers(sink_logits[l], u.reshape(B, S, -1), q.reshape(B, S, -1),
                     k.reshape(B, S, -1), v.reshape(B, S, -1), coef, invc, bias,
                     w_pool[l].astype(BF16), pool_scale[l].reshape(1, POOL_WIDTH))
        w_router = jnp.concatenate(
            [w_group_router[l], w_expert_router[l], jnp.zeros((D, pad_lanes), F32)], axis=1)
        b_router = jnp.concatenate(
            [b_group_router[l], b_expert_router[l], jnp.zeros((pad_lanes,), F32)]).reshape(1, LANES)
        hmix, xn_packed, route, counts = _out_proj(
            ab.reshape(T, MIX_WIDTH), h, w_out[l].astype(BF16), norm_ffn_w[l].reshape(1, D),
            w_router.astype(BF16), b_router)
        n_blocks = -(-(T * TOP_K + N_EXPERTS * (MOE_BLOCK - 1)) // MOE_BLOCK)
        dest, tab, etab = _plan(counts[0].astype(jnp.int32), route, n_blocks)
        xs = _dispatch(etab, dest, xn_packed, n_blocks)
        ys = _experts(tab, etab, xs, w_gate[l], w_up[l], w_down[l])
        h = _combine(dest, hmix, route, ys)
    return h.reshape(B, S, D)
```

```python
import functools

import numpy as np
import jax
import jax.numpy as jnp
from jax import lax
from jax.experimental import pallas as pl
from jax.experimental.pallas import tpu as pltpu

D_MODEL = 2048
POOL_WIDTH = 1024
POOL_WINDOWS = (2, 4, 8, 16)
N_POOL_GROUPS = 4
POOL_GROUP = 256
HEAD_DIM = 128
N_Q_HEADS = 8
N_KV_HEADS = 2
Q_PER_KV = 4
ATTN_WIDTH = 1024
KV_WIDTH = 256
IN_PROJ_WIDTH = 2560
MIX_WIDTH = 2048
WINDOW = 128
BLOCK = 128
BAND = 3 * BLOCK
ROPE_THETA = 10000.0
N_GROUPS = 8
EXPERTS_PER_GROUP = 8
N_EXPERTS = 64
TOP_K = 2
D_EXPERT = 512
MOE_BLOCK = 256
EPS = 1e-6

LANES = 128
SUBLANES = 8
HALF = D_MODEL // 2
PACK_CHUNKS = HALF // LANES
NEG_BIG = -1e30
VMEM_LIMIT = 56 * 1024 * 1024

BF16 = jnp.bfloat16
F32 = jnp.float32
U32 = jnp.uint32
HI_MASK = 0xFFFF0000


def _cparams(n_axes):
    return pltpu.CompilerParams(dimension_semantics=("arbitrary",) * n_axes,
                                vmem_limit_bytes=VMEM_LIMIT)


def _pack_pair(lo, hi):
    lo_bits = lax.bitcast_convert_type(lo.astype(BF16).astype(F32), U32) >> 16
    hi_bits = lax.bitcast_convert_type(hi.astype(BF16).astype(F32), U32) & jnp.uint32(HI_MASK)
    return lo_bits | hi_bits


def _unpack_pair(words):
    lo = lax.bitcast_convert_type(words << 16, F32)
    hi = lax.bitcast_convert_type(words & jnp.uint32(HI_MASK), F32)
    return lo, hi


IN_TM = 1024
IN_ROWS = 256
ROW_SPLIT = 2


def _in_proj_kernel(x_ref, nw_ref, w_ref, qnw_ref, knw_ref, cos_ref, sin_ref,
                    u_ref, q_ref, k_ref, v_ref):
    sub = IN_ROWS
    for part in range(IN_TM // sub):
        rows = slice(part * sub, (part + 1) * sub)
        x = x_ref[rows, :]
        ms = jnp.mean(x * x, axis=-1, keepdims=True)
        xn = (x * lax.rsqrt(ms + EPS) * nw_ref[...]).astype(BF16)
        cos = cos_ref[rows, :]
        sin = sin_ref[rows, :]

        def head_norm_rope(t, w, scale):
            hms = jnp.mean(t * t, axis=-1, keepdims=True)
            t = t * lax.rsqrt(hms + EPS) * w
            t = t * cos + pltpu.roll(t, HEAD_DIM // 2, axis=1) * sin
            return t * scale

        u_ref[rows, :] = jnp.dot(xn, w_ref[:, :POOL_WIDTH],
                                 preferred_element_type=F32).astype(BF16)
        zq = jnp.dot(xn, w_ref[:, POOL_WIDTH:POOL_WIDTH + ATTN_WIDTH],
                     preferred_element_type=F32)
        qnw = qnw_ref[...]
        for h in range(N_Q_HEADS):
            sl = slice(h * HEAD_DIM, (h + 1) * HEAD_DIM)
            q_ref[rows, sl] = head_norm_rope(zq[:, sl], qnw, HEAD_DIM ** -0.5).astype(BF16)
        o_k = POOL_WIDTH + ATTN_WIDTH
        zk = jnp.dot(xn, w_ref[:, o_k:o_k + KV_WIDTH], preferred_element_type=F32)
        knw = knw_ref[...]
        for h in range(N_KV_HEADS):
            sl = slice(h * HEAD_DIM, (h + 1) * HEAD_DIM)
            k_ref[rows, sl] = head_norm_rope(zk[:, sl], knw, 1.0).astype(BF16)
        v_ref[rows, :] = jnp.dot(xn, w_ref[:, o_k + KV_WIDTH:],
                                 preferred_element_type=F32).astype(BF16)


def _in_proj(x2, nw, w_in, qnw, knw, cos, sin, seq):
    T = x2.shape[0]
    tm = IN_TM
    pos_blocks = seq // tm
    full = lambda shape: pl.BlockSpec(shape, lambda i: (0,) * len(shape))
    return pl.pallas_call(
        _in_proj_kernel,
        grid=(T // tm,),
        in_specs=[
            pl.BlockSpec((tm, D_MODEL), lambda i: (i, 0)),
            full((1, D_MODEL)),
            pl.BlockSpec((D_MODEL, IN_PROJ_WIDTH), lambda i: (0, 0),
                         pipeline_mode=pl.Buffered(1)),
            full((1, HEAD_DIM)),
            full((1, HEAD_DIM)),
            pl.BlockSpec((tm, HEAD_DIM), lambda i: (i % pos_blocks, 0)),
            pl.BlockSpec((tm, HEAD_DIM), lambda i: (i % pos_blocks, 0)),
        ],
        out_specs=[
            pl.BlockSpec((tm, POOL_WIDTH), lambda i: (i, 0)),
            pl.BlockSpec((tm, ATTN_WIDTH), lambda i: (i, 0)),
            pl.BlockSpec((tm, KV_WIDTH), lambda i: (i, 0)),
            pl.BlockSpec((tm, KV_WIDTH), lambda i: (i, 0)),
        ],
        out_shape=[
            jax.ShapeDtypeStruct((T, POOL_WIDTH), BF16),
            jax.ShapeDtypeStruct((T, ATTN_WIDTH), BF16),
            jax.ShapeDtypeStruct((T, KV_WIDTH), BF16),
            jax.ShapeDtypeStruct((T, KV_WIDTH), BF16),
        ],
        compiler_params=_cparams(1),
        name="in_proj",
    )(x2, nw, w_in, qnw, knw, cos, sin)


MIX_TQ = 1024
MIX_STAGE_SUBS = 2


def _band_constants(seq):
    nb = seq // BLOCK
    coef = np.zeros((3, N_POOL_GROUPS, BLOCK, BAND), np.float32)
    inv_count = np.zeros((3, N_POOL_GROUPS, BLOCK, 1), np.float32)
    bias = np.zeros((3, BLOCK, BAND), np.float32)
    for kind, n in enumerate((0, 1, nb - 1)):
        start = min(max((n - 1) * BLOCK, 0), seq - BAND)
        t = n * BLOCK + np.arange(BLOCK)[:, None]
        s = start + np.arange(BAND)[None, :]
        bias[kind] = np.where(np.abs(s - t) <= WINDOW, 0.0, NEG_BIG)
        for g, win in enumerate(POOL_WINDOWS):
            half = win // 2
            lo = np.clip(t - half, 0, seq)
            hi = np.clip(t + half, 0, seq)
            count = (hi - lo).astype(np.float32)
            inside = ((s >= lo) & (s < hi)).astype(np.float32)
            coef[kind, g] = inside - count * (s == t)
            inv_count[kind, g] = 1.0 / count
    return coef, inv_count, bias


def _mixers_kernel(sink_ref, u_ref, q_ref, k_ref, v_ref, coef_ref, invc_ref, bias_ref,
                   wpool_ref, pscale_ref, ab_ref, *, seq):
    nb = seq // BLOCK
    j = pl.program_id(1)
    n_sub = MIX_TQ // BLOCK
    subs = []
    for r in range(n_sub):
        n = j * n_sub + r
        start = pl.multiple_of(jnp.clip((n - 1) * BLOCK, 0, seq - BAND), BLOCK)
        kind = jnp.where(n == 0, 0, jnp.where(n == nb - 1, 2, 1))
        subs.append((start, kind, slice(r * BLOCK, (r + 1) * BLOCK)))

    def staged(group):
        scores = {}
        for r, (start, kind, rows) in enumerate(group):
            for hk in range(N_KV_HEADS):
                kb = k_ref[0, pl.ds(start, BAND), hk * HEAD_DIM:(hk + 1) * HEAD_DIM]
                qs = jnp.concatenate(
                    [q_ref[0, rows, h * HEAD_DIM:(h + 1) * HEAD_DIM]
                     for h in range(hk * Q_PER_KV, (hk + 1) * Q_PER_KV)], axis=0)
                scores[r, hk] = lax.dot_general(qs, kb, (((1,), (1,)), ((), ())),
                                                preferred_element_type=F32)
        diffs = {}
        for r, (start, kind, rows) in enumerate(group):
            for g in range(N_POOL_GROUPS):
                ub = u_ref[0, pl.ds(start, BAND), g * POOL_GROUP:(g + 1) * POOL_GROUP]
                d = jnp.dot(coef_ref[kind, g], ub, preferred_element_type=F32) * invc_ref[kind, g]
                diffs[r, g] = d.astype(BF16)
        probs = {}
        for r, (start, kind, rows) in enumerate(group):
            bias = bias_ref[kind]
            for hk in range(N_KV_HEADS):
                for gi in range(Q_PER_KV):
                    h = hk * Q_PER_KV + gi
                    sh = scores[r, hk][gi * BLOCK:(gi + 1) * BLOCK] + bias
                    sink = sink_ref[h]
                    m = jnp.maximum(jnp.max(sh, axis=-1, keepdims=True), sink)
                    p = jnp.exp(sh - m)
                    denom = jnp.sum(p, axis=-1, keepdims=True) + jnp.exp(sink - m)
                    probs[r, h] = (p.astype(BF16), denom)
        for r, (start, kind, rows) in enumerate(group):
            for g in range(N_POOL_GROUPS):
                cols = slice(g * POOL_GROUP, (g + 1) * POOL_GROUP)
                y = jnp.dot(diffs[r, g], wpool_ref[g], preferred_element_type=F32)
                ab_ref[0, rows, cols] = (y * pscale_ref[:, cols]).astype(BF16)
        for r, (start, kind, rows) in enumerate(group):
            for hk in range(N_KV_HEADS):
                vb = v_ref[0, pl.ds(start, BAND), hk * HEAD_DIM:(hk + 1) * HEAD_DIM]
                for gi in range(Q_PER_KV):
                    h = hk * Q_PER_KV + gi
                    p, denom = probs[r, h]
                    o = jnp.dot(p, vb, preferred_element_type=F32) / denom
                    ocols = slice(POOL_WIDTH + h * HEAD_DIM, POOL_WIDTH + (h + 1) * HEAD_DIM)
                    ab_ref[0, rows, ocols] = o.astype(BF16)

    for first in range(0, n_sub, MIX_STAGE_SUBS):
        staged(subs[first:first + MIX_STAGE_SUBS])


def _mixers(sink, u, q, k, v, coef, invc, bias, wpool, pscale):
    B, S, _ = u.shape
    tq = MIX_TQ
    full = lambda shape: pl.BlockSpec(shape, lambda b, j: (0,) * len(shape))
    return pl.pallas_call(
        functools.partial(_mixers_kernel, seq=S),
        grid=(B, S // tq),
        in_specs=[
            pl.BlockSpec(memory_space=pltpu.SMEM),
            pl.BlockSpec((1, S, POOL_WIDTH), lambda b, j: (b, 0, 0)),
            pl.BlockSpec((1, tq, ATTN_WIDTH), lambda b, j: (b, j, 0)),
            pl.BlockSpec((1, S, KV_WIDTH), lambda b, j: (b, 0, 0)),
            pl.BlockSpec((1, S, KV_WIDTH), lambda b, j: (b, 0, 0)),
            full((3, N_POOL_GROUPS, BLOCK, BAND)),
            full((3, N_POOL_GROUPS, BLOCK, 1)),
            full((3, BLOCK, BAND)),
            full((N_POOL_GROUPS, POOL_GROUP, POOL_GROUP)),
            full((1, POOL_WIDTH)),
        ],
        out_specs=pl.BlockSpec((1, tq, MIX_WIDTH), lambda b, j: (b, j, 0)),
        out_shape=jax.ShapeDtypeStruct((B, S, MIX_WIDTH), BF16),
        compiler_params=_cparams(2),
        name="mixers",
    )(sink, u, q, k, v, coef, invc, bias, wpool, pscale)


OUT_TM = 512
ROUTE_E0, ROUTE_E1, ROUTE_W0, ROUTE_W1 = 0, 1, 2, 3


def _out_proj_kernel(ab_ref, x_ref, w_ref, nw_ref, wr_ref, br_ref,
                     h_ref, xn_ref, route_ref, counts_ref):
    @pl.when(pl.program_id(0) == 0)
    def _():
        counts_ref[...] = jnp.zeros_like(counts_ref)

    sub = OUT_TM // ROW_SPLIT
    lane = lax.broadcasted_iota(jnp.int32, (sub, LANES), 1)

    def first_argmax(vals):
        m = jnp.max(vals, axis=-1, keepdims=True)
        idx = jnp.min(jnp.where(vals == m, lane, LANES), axis=-1, keepdims=True)
        return m, idx

    new_counts = jnp.zeros((1, LANES), F32)
    for part in range(ROW_SPLIT):
        rows = slice(part * sub, (part + 1) * sub)
        h = x_ref[rows, :] + jnp.dot(ab_ref[rows, :], w_ref[...], preferred_element_type=F32)
        h_ref[rows, :] = h
        ms = jnp.mean(h * h, axis=-1, keepdims=True)
        xn = h * lax.rsqrt(ms + EPS) * nw_ref[...]
        packed = _pack_pair(xn[:, :HALF], xn[:, HALF:])
        for c in range(PACK_CHUNKS):
            xn_ref[pl.ds(part * sub * SUBLANES + c, sub, stride=SUBLANES), :] = (
                packed[:, c * LANES:(c + 1) * LANES])
        logits = jnp.dot(xn.astype(BF16), wr_ref[...], preferred_element_type=F32) + br_ref[...]

        gl = jnp.where(lane < N_GROUPS, logits, NEG_BIG)
        gmax, gidx = first_argmax(gl)
        gsum = jnp.sum(jnp.where(lane < N_GROUPS, jnp.exp(gl - gmax), 0.0),
                       axis=-1, keepdims=True)
        g_w = 1.0 / gsum
        e_lo = N_GROUPS + gidx * EXPERTS_PER_GROUP
        el = jnp.where((lane >= e_lo) & (lane < e_lo + EXPERTS_PER_GROUP), logits, NEG_BIG)
        m1, i1 = first_argmax(el)
        m2, i2 = first_argmax(jnp.where(lane == i1, NEG_BIG, el))
        t = jnp.exp(m2 - m1)
        p1 = 1.0 / (1.0 + t)
        p2 = t * p1
        e1 = i1 - N_GROUPS
        e2 = i2 - N_GROUPS
        route_ref[rows, :] = jnp.where(
            lane == ROUTE_E0, e1.astype(F32),
            jnp.where(lane == ROUTE_E1, e2.astype(F32),
                      jnp.where(lane == ROUTE_W0, g_w * p1,
                                jnp.where(lane == ROUTE_W1, g_w * p2, 0.0))))
        chosen = ((lane == e1) | (lane == e2)).astype(F32)
        new_counts = new_counts + jnp.sum(chosen, axis=0, keepdims=True)
    counts_ref[...] += new_counts


def _out_proj(ab, x2, w_out, nw, w_router, b_router):
    T = x2.shape[0]
    tm = OUT_TM
    full = lambda shape: pl.BlockSpec(shape, lambda i: (0,) * len(shape))
    row = lambda width: pl.BlockSpec((tm, width), lambda i: (i, 0))
    return pl.pallas_call(
        _out_proj_kernel,
        grid=(T // tm,),
        in_specs=[row(MIX_WIDTH), row(D_MODEL), full((MIX_WIDTH, D_MODEL)), full((1, D_MODEL)),
                  full((D_MODEL, LANES)), full((1, LANES))],
        out_specs=[row(D_MODEL),
                   pl.BlockSpec((tm * SUBLANES, LANES), lambda i: (i, 0)),
                   row(LANES),
                   full((1, LANES))],
        out_shape=[jax.ShapeDtypeStruct((T, D_MODEL), F32),
                   jax.ShapeDtypeStruct((T * SUBLANES, LANES), U32),
                   jax.ShapeDtypeStruct((T, LANES), F32),
                   jax.ShapeDtypeStruct((1, LANES), F32)],
        compiler_params=_cparams(1),
        name="out_proj",
    )(ab, x2, w_out, nw, w_router, b_router)


PLAN_TM = 1024
SLOT_TILE = 1024


TAB_EXPERT, TAB_FIRST, TAB_BUFFER, TAB_NEXT, TAB_VALID = 0, 1, 2, 3, 4
TAB_ROWS = 5
ETAB_PAD_LO, ETAB_PAD_HI, ETAB_MISC = 0, 1, 2
ETAB_ROWS = 3
MISC_N_USED, MISC_RUN_EXPERT = 0, 1
FETCH_AHEAD = 2
TABLE_ROWS = SUBLANES


def _slot_tables(counts_ref, tab_ref, etab_ref, run_expert, run_of_block, pstart_s, n_blocks):
    log_block = MOE_BLOCK.bit_length() - 1

    def clear(r, carry):
        run_expert[r] = -1
        return carry
    lax.fori_loop(0, run_expert.shape[0], clear, 0)

    def per_expert(e, carry):
        slot, blk, run, buf = carry
        c = counts_ref[e]
        n_blk = lax.shift_right_logical(c + (MOE_BLOCK - 1), log_block)
        pstart_s[e] = slot
        etab_ref[ETAB_PAD_LO, e] = slot + c
        etab_ref[ETAB_PAD_HI, e] = slot + n_blk * MOE_BLOCK

        def per_block(b, inner):
            tab_ref[TAB_EXPERT, b] = e
            tab_ref[TAB_FIRST, b] = (b == blk).astype(jnp.int32)
            tab_ref[TAB_BUFFER, b] = buf
            tab_ref[TAB_VALID, b] = jnp.minimum(c - (b - blk) * MOE_BLOCK, MOE_BLOCK)
            run_of_block[b] = run
            return inner
        lax.fori_loop(blk, blk + n_blk, per_block, 0)

        @pl.when(n_blk > 0)
        def _():
            run_expert[run] = e
        has_run = (n_blk > 0).astype(jnp.int32)
        next_buf = jnp.where(buf + has_run == FETCH_AHEAD, 0, buf + has_run)
        return slot + n_blk * MOE_BLOCK, blk + n_blk, run + has_run, next_buf

    zero = jnp.int32(0)
    _, n_used, _, _ = lax.fori_loop(0, N_EXPERTS, per_expert, (zero, zero, zero, zero))

    def per_used_block(b, carry):
        tab_ref[TAB_NEXT, b] = run_expert[run_of_block[b] + FETCH_AHEAD]
        return carry
    lax.fori_loop(0, n_used, per_used_block, 0)

    def per_unused_block(b, carry):
        for row in (TAB_EXPERT, TAB_FIRST, TAB_BUFFER, TAB_VALID):
            tab_ref[row, b] = 0
        tab_ref[TAB_NEXT, b] = -1
        return carry
    lax.fori_loop(n_used, n_blocks, per_unused_block, 0)

    def clear_misc(e, carry):
        etab_ref[ETAB_MISC, e] = 0
        return carry
    lax.fori_loop(0, N_EXPERTS, clear_misc, 0)
    etab_ref[ETAB_MISC, MISC_N_USED] = n_used
    for r in range(FETCH_AHEAD):
        etab_ref[ETAB_MISC, MISC_RUN_EXPERT + r] = run_expert[r]


def _plan_kernel(counts_ref, route_ref, tri_ref, dest_ref, tab_ref, etab_ref,
                 carry_ref, pstart_ref, run_expert, run_of_block, pstart_s, *, n_blocks):
    @pl.when(pl.program_id(0) == 0)
    def _():
        carry_ref[...] = jnp.zeros_like(carry_ref)
        _slot_tables(counts_ref, tab_ref, etab_ref, run_expert, run_of_block, pstart_s, n_blocks)
        lane_row = lax.broadcasted_iota(jnp.int32, (1, LANES), 1)

        def place(e, row):
            return jnp.where(lane_row == e, pstart_s[e].astype(F32), row)
        pstart_ref[...] = lax.fori_loop(0, N_EXPERTS, place, jnp.zeros((1, LANES), F32))

    route = route_ref[...]
    lane = lax.broadcasted_iota(jnp.int32, route.shape, 1)
    e0 = route[:, ROUTE_E0:ROUTE_E0 + 1].astype(jnp.int32)
    e1 = route[:, ROUTE_E1:ROUTE_E1 + 1].astype(jnp.int32)
    oh0 = lane == e0
    oh1 = lane == e1
    both = (oh0 | oh1).astype(F32)
    earlier = jnp.dot(tri_ref[...], both.astype(BF16), preferred_element_type=F32)
    base = pstart_ref[...] + carry_ref[...] + earlier
    d0 = jnp.sum(jnp.where(oh0, base, 0.0), axis=-1, keepdims=True)
    d1 = jnp.sum(jnp.where(oh1, base, 0.0), axis=-1, keepdims=True)
    by_token = jnp.where(lane == 0, d0, jnp.where(lane == 1, d1, 0.0))
    slots = by_token.T[:TABLE_ROWS].astype(jnp.int32)
    for piece in range(PLAN_TM // SLOT_TILE):
        dest_ref[:, piece, 0, :] = slots[:, piece * SLOT_TILE:(piece + 1) * SLOT_TILE]
    carry_ref[...] += jnp.sum(both, axis=0, keepdims=True)


def _plan(counts, route, n_blocks):
    T = route.shape[0]
    tm = PLAN_TM
    tri = jnp.asarray(np.tril(np.ones((tm, tm), np.float32), -1), BF16)
    smem_out = pl.BlockSpec(memory_space=pltpu.SMEM)
    grid_spec = pltpu.PrefetchScalarGridSpec(
        num_scalar_prefetch=1,
        grid=(T // tm,),
        in_specs=[pl.BlockSpec((tm, LANES), lambda i, c: (i, 0)),
                  pl.BlockSpec((tm, tm), lambda i, c: (0, 0))],
        out_specs=[pl.BlockSpec((TABLE_ROWS, tm // SLOT_TILE, 1, SLOT_TILE),
                                lambda i, c: (0, i, 0, 0)), smem_out, smem_out],
        scratch_shapes=[pltpu.VMEM((1, LANES), F32), pltpu.VMEM((1, LANES), F32),
                        pltpu.SMEM((N_EXPERTS + FETCH_AHEAD,), jnp.int32),
                        pltpu.SMEM((n_blocks,), jnp.int32),
                        pltpu.SMEM((N_EXPERTS,), jnp.int32)],
    )
    return pl.pallas_call(
        functools.partial(_plan_kernel, n_blocks=n_blocks),
        grid_spec=grid_spec,
        out_shape=[jax.ShapeDtypeStruct((TABLE_ROWS, T // SLOT_TILE, 1, SLOT_TILE), jnp.int32),
                   jax.ShapeDtypeStruct((TAB_ROWS, n_blocks), jnp.int32),
                   jax.ShapeDtypeStruct((ETAB_ROWS, N_EXPERTS), jnp.int32)],
        compiler_params=_cparams(1),
        name="plan",
    )(counts, route, tri)


DISP_TM = 1024
DMA_GROUP = 8
ROW_WORDS = SUBLANES


def _packed_row(ref, row):
    return ref.at[pl.ds(pl.multiple_of(row * ROW_WORDS, ROW_WORDS), ROW_WORDS)]


def _dispatch_kernel(etab_ref, dest0_ref, dest1_ref, xn_ref, xs_hbm,
                     zeros, sem, tail_sem, *, experts_per_step, tails_per_step, n_blocks):
    i = pl.program_id(0)

    @pl.when(i == 0)
    def _():
        zeros[...] = jnp.zeros_like(zeros)

    def scatter_group(t, carry):
        base = pl.multiple_of(t * DMA_GROUP, DMA_GROUP)
        window = xn_ref.at[pl.ds(base * ROW_WORDS, DMA_GROUP * ROW_WORDS)]
        for k in range(DMA_GROUP):
            src = window.at[pl.ds(k * ROW_WORDS, ROW_WORDS)]
            for choice, dest_ref in enumerate((dest0_ref, dest1_ref)):
                dst = _packed_row(xs_hbm, dest_ref[0, 0, 0, base + k])
                pltpu.make_async_copy(src, dst, sem).start(priority=choice)
        return carry

    lax.fori_loop(0, DISP_TM // DMA_GROUP, scatter_group, 0)

    def pad_fill(start):
        for j in range(experts_per_step):
            e = jnp.minimum(i * experts_per_step + j, N_EXPERTS - 1)
            live = i * experts_per_step + j < N_EXPERTS
            pos = etab_ref[ETAB_PAD_LO, e]
            n_pad = jnp.where(live, etab_ref[ETAB_PAD_HI, e] - pos, 0)
            for bit in reversed(range(MOE_BLOCK.bit_length() - 1)):
                size = 1 << bit
                take = jnp.bitwise_and(lax.shift_right_logical(n_pad, bit), 1)
                dst = xs_hbm.at[pl.ds(pl.multiple_of(pos * ROW_WORDS, ROW_WORDS), size * ROW_WORDS)]
                cp = pltpu.make_async_copy(zeros.at[pl.ds(0, size * ROW_WORDS)], dst, sem)

                @pl.when(take == 1)
                def _():
                    cp.start() if start else cp.wait()
                pos = pos + take * size

    def tail_copies(start):
        for j in range(tails_per_step):
            blk = etab_ref[ETAB_MISC, MISC_N_USED] + i * tails_per_step + j
            rows = pl.ds(pl.multiple_of(jnp.minimum(blk, n_blocks - 1) * (MOE_BLOCK * ROW_WORDS),
                                        MOE_BLOCK * ROW_WORDS), MOE_BLOCK * ROW_WORDS)
            cp = pltpu.make_async_copy(zeros, xs_hbm.at[rows], tail_sem)

            @pl.when(blk < n_blocks)
            def _():
                cp.start() if start else cp.wait()

    pad_fill(True)
    tail_copies(True)

    tile_rows = DISP_TM * ROW_WORDS
    for _ in range(TOP_K):
        pltpu.make_async_copy(xn_ref, xs_hbm.at[pl.ds(0, tile_rows)], sem).wait()
    pad_fill(False)
    tail_copies(False)


def _slot_spec(choice, index):
    return pl.BlockSpec((1, 1, 1, SLOT_TILE), lambda i, *_: (choice, index(i), 0, 0),
                        memory_space=pltpu.SMEM)


def _dispatch(etab, dest4, xn_packed, n_blocks):
    assert DISP_TM == SLOT_TILE
    n = dest4.shape[1]
    min_used = (n * DISP_TM * TOP_K) // MOE_BLOCK
    tails_per_step = -(-(n_blocks - min_used) // n)
    grid_spec = pltpu.PrefetchScalarGridSpec(
        num_scalar_prefetch=1,
        grid=(n,),
        in_specs=[_slot_spec(0, lambda i: i), _slot_spec(1, lambda i: i),
                  pl.BlockSpec((DISP_TM * ROW_WORDS, LANES), lambda i, *_: (i, 0))],
        out_specs=pl.BlockSpec(memory_space=pl.ANY),
        scratch_shapes=[pltpu.VMEM((MOE_BLOCK * ROW_WORDS, LANES), U32),
                        pltpu.SemaphoreType.DMA(()), pltpu.SemaphoreType.DMA(())],
    )
    return pl.pallas_call(
        functools.partial(_dispatch_kernel, experts_per_step=-(-N_EXPERTS // n),
                          tails_per_step=tails_per_step, n_blocks=n_blocks),
        grid_spec=grid_spec,
        out_shape=jax.ShapeDtypeStruct((n_blocks * MOE_BLOCK * ROW_WORDS, LANES), U32),
        compiler_params=_cparams(1),
        name="dispatch",
    )(etab, dest4, dest4, xn_packed)


CAST_ROWS = 256
FETCH_SPLIT = 4
BLOCKS_PER_STEP = 4


def _experts_kernel(tab_ref, etab_ref,
                    xs_ref, wg_hbm, wu_hbm, wd_hbm, ys_ref,
                    fg, fu, fd, wg, wu, wd, sem):
    n_used = etab_ref[ETAB_MISC, MISC_N_USED]
    for j in range(BLOCKS_PER_STEP):
        rows = pl.ds(j * MOE_BLOCK * SUBLANES, MOE_BLOCK * SUBLANES)
        _expert_block(pl.program_id(0) * BLOCKS_PER_STEP + j, n_used, tab_ref, etab_ref,
                      xs_ref.at[rows], wg_hbm, wu_hbm, wd_hbm, ys_ref.at[rows],
                      fg, fu, fd, wg, wu, wd, sem)


def _expert_block(b, n_used, tab_ref, etab_ref, xs_ref, wg_hbm, wu_hbm, wd_hbm, ys_ref,
                  fg, fu, fd, wg, wu, wd, sem):

    def fetch(e, p):
        copies = []
        for k, (src, dst) in enumerate(((wg_hbm, fg), (wu_hbm, fu), (wd_hbm, fd))):
            rows = src.shape[1] // FETCH_SPLIT
            for c in range(FETCH_SPLIT):
                sl = pl.ds(c * rows, rows)
                copies.append(pltpu.make_async_copy(src.at[e, sl], dst.at[p, sl], sem.at[p, k]))
        return copies

    def start_fetch(e, p):
        for k, cp in enumerate(fetch(e, p)):
            cp.start(priority=k % 2)

    @pl.when(b == 0)
    def _():
        for r in range(FETCH_AHEAD):
            e = etab_ref[ETAB_MISC, MISC_RUN_EXPERT + r]

            @pl.when(e >= 0)
            def _():
                start_fetch(e, r)

    @pl.when((b < n_used) & (tab_ref[TAB_FIRST, b] == 1))
    def _():
        p = tab_ref[TAB_BUFFER, b]
        for cp in fetch(tab_ref[TAB_EXPERT, b], p):
            cp.wait()

        def cast(src, dst, n_rows):
            def body(i, carry):
                rows = pl.ds(pl.multiple_of(i * CAST_ROWS, CAST_ROWS), CAST_ROWS)
                dst[rows, :] = src[p, rows, :].astype(BF16)
                return carry
            lax.fori_loop(0, n_rows // CAST_ROWS, body, 0)

        cast(fg, wg, D_MODEL)
        cast(fu, wu, D_MODEL)
        cast(fd, wd, D_EXPERT)

        @pl.when(tab_ref[TAB_NEXT, b] >= 0)
        def _():
            start_fetch(tab_ref[TAB_NEXT, b], p)

    def swiglu_rows(m_rows):
        lo_parts, hi_parts = [], []
        for c in range(PACK_CHUNKS):
            lo, hi = _unpack_pair(xs_ref[pl.ds(c, m_rows, stride=SUBLANES), :])
            lo_parts.append(lo.astype(BF16))
            hi_parts.append(hi.astype(BF16))
        xb = jnp.concatenate(lo_parts + hi_parts, axis=1)
        g = jnp.dot(xb, wg[...], preferred_element_type=F32)
        u = jnp.dot(xb, wu[...], preferred_element_type=F32)
        hmid = (g * jax.nn.sigmoid(g) * u).astype(BF16)
        y = jnp.dot(hmid, wd[...], preferred_element_type=F32)
        packed = _pack_pair(y[:, :HALF], y[:, HALF:])
        for c in range(PACK_CHUNKS):
            ys_ref[pl.ds(c, m_rows, stride=SUBLANES), :] = packed[:, c * LANES:(c + 1) * LANES]
        if m_rows < MOE_BLOCK:
            rest = pl.ds(m_rows * SUBLANES, (MOE_BLOCK - m_rows) * SUBLANES)
            ys_ref[rest, :] = jnp.zeros(((MOE_BLOCK - m_rows) * SUBLANES, LANES), U32)

    valid = tab_ref[TAB_VALID, b]

    @pl.when((b < n_used) & (valid > MOE_BLOCK // 2))
    def _():
        swiglu_rows(MOE_BLOCK)

    @pl.when((b < n_used) & (valid <= MOE_BLOCK // 2))
    def _():
        swiglu_rows(MOE_BLOCK // 2)

    @pl.when(b >= n_used)
    def _():
        ys_ref[...] = jnp.zeros(ys_ref.shape, ys_ref.dtype)


def _experts(tab, etab, xs, w_gate, w_up, w_down):
    n_blocks = tab.shape[1]
    assert n_blocks % BLOCKS_PER_STEP == 0
    blk = lambda: pl.BlockSpec((BLOCKS_PER_STEP * MOE_BLOCK * SUBLANES, LANES),
                               lambda s, *_: (s, 0))
    grid_spec = pltpu.PrefetchScalarGridSpec(
        num_scalar_prefetch=2,
        grid=(n_blocks // BLOCKS_PER_STEP,),
        in_specs=[blk(),
                  pl.BlockSpec(memory_space=pl.ANY),
                  pl.BlockSpec(memory_space=pl.ANY),
                  pl.BlockSpec(memory_space=pl.ANY)],
        out_specs=blk(),
        scratch_shapes=[pltpu.VMEM((FETCH_AHEAD, D_MODEL, D_EXPERT), F32),
                        pltpu.VMEM((FETCH_AHEAD, D_MODEL, D_EXPERT), F32),
                        pltpu.VMEM((FETCH_AHEAD, D_EXPERT, D_MODEL), F32),
                        pltpu.VMEM((D_MODEL, D_EXPERT), BF16),
                        pltpu.VMEM((D_MODEL, D_EXPERT), BF16),
                        pltpu.VMEM((D_EXPERT, D_MODEL), BF16),
                        pltpu.SemaphoreType.DMA((FETCH_AHEAD, 3))],
    )
    return pl.pallas_call(
        _experts_kernel,
        grid_spec=grid_spec,
        out_shape=jax.ShapeDtypeStruct((n_blocks * MOE_BLOCK * SUBLANES, LANES), U32),
        compiler_params=_cparams(1),
        name="experts",
    )(tab, etab, xs, w_gate, w_up, w_down)


CMB_TM = 256


def _combine_kernel(d0_cur_ref, d1_cur_ref, d0_next_ref, d1_next_ref, h_ref, route_ref, ys_hbm,
                    o_ref, ybuf, sem):
    i = pl.program_id(0)
    n = pl.num_programs(0)
    slot = i % 2
    n_rows = TOP_K * CMB_TM

    def gather(idx_refs, tile, s):
        in_piece = jnp.bitwise_and(tile, SLOT_TILE // CMB_TM - 1) * CMB_TM

        def gather_group(t, carry):
            base = pl.multiple_of(t * DMA_GROUP, DMA_GROUP)
            for choice, idx_ref in enumerate(idx_refs):
                first = (choice * CMB_TM + base) * ROW_WORDS
                window = ybuf.at[s, pl.ds(first, DMA_GROUP * ROW_WORDS)]
                for k in range(DMA_GROUP):
                    row = idx_ref[0, 0, 0, in_piece + base + k]
                    pltpu.make_async_copy(_packed_row(ys_hbm, row),
                                          window.at[pl.ds(k * ROW_WORDS, ROW_WORDS)],
                                          sem.at[s]).start(priority=k % 2)
            return carry
        lax.fori_loop(0, CMB_TM // DMA_GROUP, gather_group, 0)

    @pl.when(i == 0)
    def _():
        gather((d0_cur_ref, d1_cur_ref), i, 0)

    @pl.when(i + 1 < n)
    def _():
        gather((d0_next_ref, d1_next_ref), i + 1, 1 - slot)

    pltpu.make_async_copy(ys_hbm.at[pl.ds(0, n_rows * ROW_WORDS)], ybuf.at[slot],
                          sem.at[slot]).wait()
    route = route_ref[...]
    w0 = route[:, ROUTE_W0:ROUTE_W0 + 1]
    w1 = route[:, ROUTE_W1:ROUTE_W1 + 1]
    yb = ybuf.at[slot]
    for c in range(PACK_CHUNKS):
        lo0, hi0 = _unpack_pair(yb[pl.ds(c, CMB_TM, stride=SUBLANES), :])
        lo1, hi1 = _unpack_pair(yb[pl.ds(CMB_TM * SUBLANES + c, CMB_TM, stride=SUBLANES), :])
        lo_cols = slice(c * LANES, (c + 1) * LANES)
        hi_cols = slice(HALF + c * LANES, HALF + (c + 1) * LANES)
        o_ref[:, lo_cols] = h_ref[:, lo_cols] + (w0 * lo0 + w1 * lo1)
        o_ref[:, hi_cols] = h_ref[:, hi_cols] + (w0 * hi0 + w1 * hi1)


def _combine(dest4, h, route, ys):
    T = h.shape[0]
    tm = CMB_TM
    per_piece = SLOT_TILE // tm
    assert per_piece & (per_piece - 1) == 0
    n = T // tm
    cur = lambda i: i // per_piece
    nxt = lambda i: jnp.minimum(i + 1, n - 1) // per_piece
    return pl.pallas_call(
        _combine_kernel,
        grid=(n,),
        in_specs=[
            _slot_spec(0, cur), _slot_spec(1, cur),
            _slot_spec(0, nxt), _slot_spec(1, nxt),
            pl.BlockSpec((tm, D_MODEL), lambda i: (i, 0)),
            pl.BlockSpec((tm, LANES), lambda i: (i, 0)),
            pl.BlockSpec(memory_space=pl.ANY),
        ],
        out_specs=pl.BlockSpec((tm, D_MODEL), lambda i: (i, 0)),
        out_shape=jax.ShapeDtypeStruct((T, D_MODEL), F32),
        scratch_shapes=[pltpu.VMEM((2, TOP_K * tm * SUBLANES, LANES), U32),
                        pltpu.SemaphoreType.DMA((2,))],
        compiler_params=_cparams(1),
        name="combine",
    )(dest4, dest4, dest4, dest4, h, route, ys)


def _rope_tables(S):
    inv = (np.float32(ROPE_THETA) ** (-np.arange(0, HEAD_DIM, 2, dtype=np.float32)
                                      / np.float32(HEAD_DIM))).astype(np.float32)
    ang = (np.arange(S, dtype=np.float32)[:, None] * inv[None, :]).astype(np.float64)
    cos, sin = np.cos(ang).astype(np.float32), np.sin(ang).astype(np.float32)
    return (jnp.asarray(np.concatenate([cos, cos], axis=-1)),
            jnp.asarray(np.concatenate([-sin, sin], axis=-1)))


def kernel(x, norm_mix_w, w_in, w_pool, pool_scale, q_norm_w, k_norm_w, sink_logits, w_out,
           norm_ffn_w, w_group_router, b_group_router, w_expert_router, b_expert_router,
           w_gate, w_up, w_down):
    B, S, D = x.shape
    T = B * S
    depth = w_in.shape[0]
    cos, sin = _rope_tables(S)
    coef_np, invc_np, bias_np = _band_constants(S)
    coef = jnp.asarray(coef_np, BF16)
    invc = jnp.asarray(invc_np, F32)
    bias = jnp.asarray(bias_np, F32)
    pad_lanes = LANES - N_GROUPS - N_EXPERTS

    h = x.reshape(T, D)
    for l in range(depth):
        u, q, k, v = _in_proj(h, norm_mix_w[l].reshape(1, D), w_in[l].astype(BF16),
                              q_norm_w[l].reshape(1, HEAD_DIM), k_norm_w[l].reshape(1, HEAD_DIM),
                              cos, sin, S)
        ab = _mixers(sink_logits[l], u.reshape(B, S, -1), q.reshape(B, S, -1),
                     k.reshape(B, S, -1), v.reshape(B, S, -1), coef, invc, bias,
                     w_pool[l].astype(BF16), pool_scale[l].reshape(1, POOL_WIDTH))
        w_router = jnp.concatenate(
            [w_group_router[l], w_expert_router[l], jnp.zeros((D, pad_lanes), F32)], axis=1)
        b_router = jnp.concatenate(
            [b_group_router[l], b_expert_router[l], jnp.zeros((pad_lanes,), F32)]).reshape(1, LANES)
        hmix, xn_packed, route, counts = _out_proj(
            ab.reshape(T, MIX_WIDTH), h, w_out[l].astype(BF16), norm_ffn_w[l].reshape(1, D),
            w_router.astype(BF16), b_router)
        n_blocks = -(-(T * TOP_K + N_EXPERTS * (MOE_BLOCK - 1)) // MOE_BLOCK)
        dest, tab, etab = _plan(counts[0].astype(jnp.int32), route, n_blocks)
        xs = _dispatch(etab, dest, xn_packed, n_blocks)
        ys = _experts(tab, etab, xs, w_gate[l], w_up[l], w_down[l])
        h = _combine(dest, hmix, route, ys)
    return h.reshape(B, S, D)
```

```python
import functools

import numpy as np
import jax
import jax.numpy as jnp
from jax import lax
from jax.experimental import pallas as pl
from jax.experimental.pallas import tpu as pltpu

D_MODEL = 2048
POOL_WIDTH = 1024
POOL_WINDOWS = (2, 4, 8, 16)
N_POOL_GROUPS = 4
POOL_GROUP = 256
HEAD_DIM = 128
N_Q_HEADS = 8
N_KV_HEADS = 2
Q_PER_KV = 4
ATTN_WIDTH = 1024
KV_WIDTH = 256
IN_PROJ_WIDTH = 2560
MIX_WIDTH = 2048
WINDOW = 128
BLOCK = 128
BAND = 3 * BLOCK
ROPE_THETA = 10000.0
N_GROUPS = 8
EXPERTS_PER_GROUP = 8
N_EXPERTS = 64
TOP_K = 2
D_EXPERT = 512
MOE_BLOCK = 256
EPS = 1e-6

LANES = 128
SUBLANES = 8
HALF = D_MODEL // 2
PACK_CHUNKS = HALF // LANES
NEG_BIG = -1e30
VMEM_LIMIT = 56 * 1024 * 1024

BF16 = jnp.bfloat16
F32 = jnp.float32
U32 = jnp.uint32
HI_MASK = 0xFFFF0000


def _cparams(n_axes):
    return pltpu.CompilerParams(dimension_semantics=("arbitrary",) * n_axes,
                                vmem_limit_bytes=VMEM_LIMIT)


def _pack_pair(lo, hi):
    lo_bits = lax.bitcast_convert_type(lo.astype(BF16).astype(F32), U32) >> 16
    hi_bits = lax.bitcast_convert_type(hi.astype(BF16).astype(F32), U32) & jnp.uint32(HI_MASK)
    return lo_bits | hi_bits


def _unpack_pair(words):
    lo = lax.bitcast_convert_type(words << 16, F32)
    hi = lax.bitcast_convert_type(words & jnp.uint32(HI_MASK), F32)
    return lo, hi


IN_TM = 512
ROW_SPLIT = 2


def _in_proj_kernel(x_ref, nw_ref, w_ref, qnw_ref, knw_ref, cos_ref, sin_ref,
                    u_ref, q_ref, k_ref, v_ref):
    sub = IN_TM // ROW_SPLIT
    for part in range(ROW_SPLIT):
        rows = slice(part * sub, (part + 1) * sub)
        x = x_ref[rows, :]
        ms = jnp.mean(x * x, axis=-1, keepdims=True)
        xn = (x * lax.rsqrt(ms + EPS) * nw_ref[...]).astype(BF16)
        cos = cos_ref[rows, :]
        sin = sin_ref[rows, :]

        def head_norm_rope(t, w, scale):
            hms = jnp.mean(t * t, axis=-1, keepdims=True)
            t = t * lax.rsqrt(hms + EPS) * w
            t = t * cos + pltpu.roll(t, HEAD_DIM // 2, axis=1) * sin
            return t * scale

        u_ref[rows, :] = jnp.dot(xn, w_ref[:, :POOL_WIDTH],
                                 preferred_element_type=F32).astype(BF16)
        zq = jnp.dot(xn, w_ref[:, POOL_WIDTH:POOL_WIDTH + ATTN_WIDTH],
                     preferred_element_type=F32)
        qnw = qnw_ref[...]
        for h in range(N_Q_HEADS):
            sl = slice(h * HEAD_DIM, (h + 1) * HEAD_DIM)
            q_ref[rows, sl] = head_norm_rope(zq[:, sl], qnw, HEAD_DIM ** -0.5).astype(BF16)
        o_k = POOL_WIDTH + ATTN_WIDTH
        zk = jnp.dot(xn, w_ref[:, o_k:o_k + KV_WIDTH], preferred_element_type=F32)
        knw = knw_ref[...]
        for h in range(N_KV_HEADS):
            sl = slice(h * HEAD_DIM, (h + 1) * HEAD_DIM)
            k_ref[rows, sl] = head_norm_rope(zk[:, sl], knw, 1.0).astype(BF16)
        v_ref[rows, :] = jnp.dot(xn, w_ref[:, o_k + KV_WIDTH:],
                                 preferred_element_type=F32).astype(BF16)


def _in_proj(x2, nw, w_in, qnw, knw, cos, sin, seq):
    T = x2.shape[0]
    tm = IN_TM
    pos_blocks = seq // tm
    full = lambda shape: pl.BlockSpec(shape, lambda i: (0,) * len(shape))
    return pl.pallas_call(
        _in_proj_kernel,
        grid=(T // tm,),
        in_specs=[
            pl.BlockSpec((tm, D_MODEL), lambda i: (i, 0)),
            full((1, D_MODEL)),
            full((D_MODEL, IN_PROJ_WIDTH)),
            full((1, HEAD_DIM)),
            full((1, HEAD_DIM)),
            pl.BlockSpec((tm, HEAD_DIM), lambda i: (i % pos_blocks, 0)),
            pl.BlockSpec((tm, HEAD_DIM), lambda i: (i % pos_blocks, 0)),
        ],
        out_specs=[
            pl.BlockSpec((tm, POOL_WIDTH), lambda i: (i, 0)),
            pl.BlockSpec((tm, ATTN_WIDTH), lambda i: (i, 0)),
            pl.BlockSpec((tm, KV_WIDTH), lambda i: (i, 0)),
            pl.BlockSpec((tm, KV_WIDTH), lambda i: (i, 0)),
        ],
        out_shape=[
            jax.ShapeDtypeStruct((T, POOL_WIDTH), BF16),
            jax.ShapeDtypeStruct((T, ATTN_WIDTH), BF16),
            jax.ShapeDtypeStruct((T, KV_WIDTH), BF16),
            jax.ShapeDtypeStruct((T, KV_WIDTH), BF16),
        ],
        compiler_params=_cparams(1),
        name="in_proj",
    )(x2, nw, w_in, qnw, knw, cos, sin)


MIX_TQ = 512
MIX_STAGE_SUBS = 2


def _band_constants(seq):
    nb = seq // BLOCK
    coef = np.zeros((3, N_POOL_GROUPS, BLOCK, BAND), np.float32)
    inv_count = np.zeros((3, N_POOL_GROUPS, BLOCK, 1), np.float32)
    bias = np.zeros((3, BLOCK, BAND), np.float32)
    for kind, n in enumerate((0, 1, nb - 1)):
        start = min(max((n - 1) * BLOCK, 0), seq - BAND)
        t = n * BLOCK + np.arange(BLOCK)[:, None]
        s = start + np.arange(BAND)[None, :]
        bias[kind] = np.where(np.abs(s - t) <= WINDOW, 0.0, NEG_BIG)
        for g, win in enumerate(POOL_WINDOWS):
            half = win // 2
            lo = np.clip(t - half, 0, seq)
            hi = np.clip(t + half, 0, seq)
            count = (hi - lo).astype(np.float32)
            inside = ((s >= lo) & (s < hi)).astype(np.float32)
            coef[kind, g] = inside - count * (s == t)
            inv_count[kind, g] = 1.0 / count
    return coef, inv_count, bias


def _mixers_kernel(sink_ref, u_ref, q_ref, k_ref, v_ref, coef_ref, invc_ref, bias_ref,
                   wpool_ref, pscale_ref, ab_ref, *, seq):
    nb = seq // BLOCK
    j = pl.program_id(1)
    n_sub = MIX_TQ // BLOCK
    subs = []
    for r in range(n_sub):
        n = j * n_sub + r
        start = pl.multiple_of(jnp.clip((n - 1) * BLOCK, 0, seq - BAND), BLOCK)
        kind = jnp.where(n == 0, 0, jnp.where(n == nb - 1, 2, 1))
        subs.append((start, kind, slice(r * BLOCK, (r + 1) * BLOCK)))

    def staged(group):
        scores = {}
        for r, (start, kind, rows) in enumerate(group):
            for hk in range(N_KV_HEADS):
                kb = k_ref[0, pl.ds(start, BAND), hk * HEAD_DIM:(hk + 1) * HEAD_DIM]
                qs = jnp.concatenate(
                    [q_ref[0, rows, h * HEAD_DIM:(h + 1) * HEAD_DIM]
                     for h in range(hk * Q_PER_KV, (hk + 1) * Q_PER_KV)], axis=0)
                scores[r, hk] = lax.dot_general(qs, kb, (((1,), (1,)), ((), ())),
                                                preferred_element_type=F32)
        diffs = {}
        for r, (start, kind, rows) in enumerate(group):
            for g in range(N_POOL_GROUPS):
                ub = u_ref[0, pl.ds(start, BAND), g * POOL_GROUP:(g + 1) * POOL_GROUP]
                d = jnp.dot(coef_ref[kind, g], ub, preferred_element_type=F32) * invc_ref[kind, g]
                diffs[r, g] = d.astype(BF16)
        probs = {}
        for r, (start, kind, rows) in enumerate(group):
            bias = bias_ref[kind]
            for hk in range(N_KV_HEADS):
                for gi in range(Q_PER_KV):
                    h = hk * Q_PER_KV + gi
                    sh = scores[r, hk][gi * BLOCK:(gi + 1) * BLOCK] + bias
                    sink = sink_ref[h]
                    m = jnp.maximum(jnp.max(sh, axis=-1, keepdims=True), sink)
                    p = jnp.exp(sh - m)
                    denom = jnp.sum(p, axis=-1, keepdims=True) + jnp.exp(sink - m)
                    probs[r, h] = (p.astype(BF16), denom)
        for r, (start, kind, rows) in enumerate(group):
            for g in range(N_POOL_GROUPS):
                cols = slice(g * POOL_GROUP, (g + 1) * POOL_GROUP)
                y = jnp.dot(diffs[r, g], wpool_ref[g], preferred_element_type=F32)
                ab_ref[0, rows, cols] = (y * pscale_ref[:, cols]).astype(BF16)
        for r, (start, kind, rows) in enumerate(group):
            for hk in range(N_KV_HEADS):
                vb = v_ref[0, pl.ds(start, BAND), hk * HEAD_DIM:(hk + 1) * HEAD_DIM]
                for gi in range(Q_PER_KV):
                    h = hk * Q_PER_KV + gi
                    p, denom = probs[r, h]
                    o = jnp.dot(p, vb, preferred_element_type=F32) / denom
                    ocols = slice(POOL_WIDTH + h * HEAD_DIM, POOL_WIDTH + (h + 1) * HEAD_DIM)
                    ab_ref[0, rows, ocols] = o.astype(BF16)

    for first in range(0, n_sub, MIX_STAGE_SUBS):
        staged(subs[first:first + MIX_STAGE_SUBS])


def _mixers(sink, u, q, k, v, coef, invc, bias, wpool, pscale):
    B, S, _ = u.shape
    tq = MIX_TQ
    full = lambda shape: pl.BlockSpec(shape, lambda b, j: (0,) * len(shape))
    return pl.pallas_call(
        functools.partial(_mixers_kernel, seq=S),
        grid=(B, S // tq),
        in_specs=[
            pl.BlockSpec(memory_space=pltpu.SMEM),
            pl.BlockSpec((1, S, POOL_WIDTH), lambda b, j: (b, 0, 0)),
            pl.BlockSpec((1, tq, ATTN_WIDTH), lambda b, j: (b, j, 0)),
            pl.BlockSpec((1, S, KV_WIDTH), lambda b, j: (b, 0, 0)),
            pl.BlockSpec((1, S, KV_WIDTH), lambda b, j: (b, 0, 0)),
            full((3, N_POOL_GROUPS, BLOCK, BAND)),
            full((3, N_POOL_GROUPS, BLOCK, 1)),
            full((3, BLOCK, BAND)),
            full((N_POOL_GROUPS, POOL_GROUP, POOL_GROUP)),
            full((1, POOL_WIDTH)),
        ],
        out_specs=pl.BlockSpec((1, tq, MIX_WIDTH), lambda b, j: (b, j, 0)),
        out_shape=jax.ShapeDtypeStruct((B, S, MIX_WIDTH), BF16),
        compiler_params=_cparams(2),
        name="mixers",
    )(sink, u, q, k, v, coef, invc, bias, wpool, pscale)


OUT_TM = 512
ROUTE_E0, ROUTE_E1, ROUTE_W0, ROUTE_W1 = 0, 1, 2, 3


def _out_proj_kernel(ab_ref, x_ref, w_ref, nw_ref, wr_ref, br_ref,
                     h_ref, xn_ref, route_ref, counts_ref, logits_keep, *, n_tiles):
    i = pl.program_id(0)

    @pl.when(i == 0)
    def _():
        counts_ref[...] = jnp.zeros_like(counts_ref)
        logits_keep[...] = jnp.zeros_like(logits_keep)

    sub = OUT_TM // ROW_SPLIT
    lane = lax.broadcasted_iota(jnp.int32, (sub, LANES), 1)
    prev_slot = (i + 1) % 2
    cur_slot = i % 2

    def first_argmax(vals):
        m = jnp.max(vals, axis=-1, keepdims=True)
        idx = jnp.min(jnp.where(vals == m, lane, LANES), axis=-1, keepdims=True)
        return m, idx

    def project_tile():
        for part in range(ROW_SPLIT):
            rows = slice(part * sub, (part + 1) * sub)
            h = x_ref[rows, :] + jnp.dot(ab_ref[rows, :], w_ref[...], preferred_element_type=F32)
            h_ref[rows, :] = h
            ms = jnp.mean(h * h, axis=-1, keepdims=True)
            xn = h * lax.rsqrt(ms + EPS) * nw_ref[...]
            packed = _pack_pair(xn[:, :HALF], xn[:, HALF:])
            for c in range(PACK_CHUNKS):
                xn_ref[pl.ds(part * sub * SUBLANES + c, sub, stride=SUBLANES), :] = (
                    packed[:, c * LANES:(c + 1) * LANES])
            logits_keep[cur_slot, rows, :] = (
                jnp.dot(xn.astype(BF16), wr_ref[...], preferred_element_type=F32) + br_ref[...])

    def route_previous_tile():
        new_counts = jnp.zeros((1, LANES), F32)
        for part in range(ROW_SPLIT):
            new_counts = new_counts + route_rows(slice(part * sub, (part + 1) * sub))
        counts_ref[...] += jnp.where(i > 0, new_counts, 0.0)

    def route_rows(rows):
        logits = logits_keep[prev_slot, rows, :]
        gl = jnp.where(lane < N_GROUPS, logits, NEG_BIG)
        gmax, gidx = first_argmax(gl)
        gsum = jnp.sum(jnp.where(lane < N_GROUPS, jnp.exp(gl - gmax), 0.0),
                       axis=-1, keepdims=True)
        g_w = 1.0 / gsum
        e_lo = N_GROUPS + gidx * EXPERTS_PER_GROUP
        el = jnp.where((lane >= e_lo) & (lane < e_lo + EXPERTS_PER_GROUP), logits, NEG_BIG)
        m1, i1 = first_argmax(el)
        m2, i2 = first_argmax(jnp.where(lane == i1, NEG_BIG, el))
        t = jnp.exp(m2 - m1)
        p1 = 1.0 / (1.0 + t)
        p2 = t * p1
        e1 = i1 - N_GROUPS
        e2 = i2 - N_GROUPS
        route_ref[rows, :] = jnp.where(
            lane == ROUTE_E0, e1.astype(F32),
            jnp.where(lane == ROUTE_E1, e2.astype(F32),
                      jnp.where(lane == ROUTE_W0, g_w * p1,
                                jnp.where(lane == ROUTE_W1, g_w * p2, 0.0))))
        chosen = ((lane == e1) | (lane == e2)).astype(F32)
        return jnp.sum(chosen, axis=0, keepdims=True)

    @pl.when(i < n_tiles)
    def _():
        route_previous_tile()
        project_tile()

    @pl.when(i == n_tiles)
    def _():
        route_previous_tile()


def _out_proj(ab, x2, w_out, nw, w_router, b_router):
    T = x2.shape[0]
    tm = OUT_TM
    n_tiles = T // tm
    full = lambda shape: pl.BlockSpec(shape, lambda i: (0,) * len(shape))
    cur = lambda rows, width: pl.BlockSpec((rows, width),
                                           lambda i: (jnp.minimum(i, n_tiles - 1), 0))
    row = lambda width: cur(tm, width)
    return pl.pallas_call(
        functools.partial(_out_proj_kernel, n_tiles=n_tiles),
        grid=(n_tiles + 1,),
        in_specs=[row(MIX_WIDTH), row(D_MODEL), full((MIX_WIDTH, D_MODEL)), full((1, D_MODEL)),
                  full((D_MODEL, LANES)), full((1, LANES))],
        out_specs=[row(D_MODEL),
                   cur(tm * SUBLANES, LANES),
                   pl.BlockSpec((tm, LANES), lambda i: (jnp.maximum(i - 1, 0), 0)),
                   full((1, LANES))],
        scratch_shapes=[pltpu.VMEM((2, tm, LANES), F32)],
        out_shape=[jax.ShapeDtypeStruct((T, D_MODEL), F32),
                   jax.ShapeDtypeStruct((T * SUBLANES, LANES), U32),
                   jax.ShapeDtypeStruct((T, LANES), F32),
                   jax.ShapeDtypeStruct((1, LANES), F32)],
        compiler_params=_cparams(1),
        name="out_proj",
    )(ab, x2, w_out, nw, w_router, b_router)


PLAN_TM = 1024
SLOT_TILE = 1024


TAB_EXPERT, TAB_FIRST, TAB_BUFFER, TAB_NEXT, TAB_VALID = 0, 1, 2, 3, 4
TAB_ROWS = 5
ETAB_PAD_LO, ETAB_PAD_HI, ETAB_MISC = 0, 1, 2
ETAB_ROWS = 3
MISC_N_USED, MISC_RUN_EXPERT = 0, 1
FETCH_AHEAD = 2
TABLE_ROWS = SUBLANES


def _slot_tables(counts_ref, tab_ref, etab_ref, run_expert, run_of_block, pstart_s, n_blocks):
    log_block = MOE_BLOCK.bit_length() - 1

    def clear(r, carry):
        run_expert[r] = -1
        return carry
    lax.fori_loop(0, run_expert.shape[0], clear, 0)

    def per_expert(e, carry):
        slot, blk, run, buf = carry
        c = counts_ref[e]
        n_blk = lax.shift_right_logical(c + (MOE_BLOCK - 1), log_block)
        pstart_s[e] = slot
        etab_ref[ETAB_PAD_LO, e] = slot + c
        etab_ref[ETAB_PAD_HI, e] = slot + n_blk * MOE_BLOCK

        def per_block(b, inner):
            tab_ref[TAB_EXPERT, b] = e
            tab_ref[TAB_FIRST, b] = (b == blk).astype(jnp.int32)
            tab_ref[TAB_BUFFER, b] = buf
            tab_ref[TAB_VALID, b] = jnp.minimum(c - (b - blk) * MOE_BLOCK, MOE_BLOCK)
            run_of_block[b] = run
            return inner
        lax.fori_loop(blk, blk + n_blk, per_block, 0)

        @pl.when(n_blk > 0)
        def _():
            run_expert[run] = e
        has_run = (n_blk > 0).astype(jnp.int32)
        next_buf = jnp.where(buf + has_run == FETCH_AHEAD, 0, buf + has_run)
        return slot + n_blk * MOE_BLOCK, blk + n_blk, run + has_run, next_buf

    zero = jnp.int32(0)
    _, n_used, _, _ = lax.fori_loop(0, N_EXPERTS, per_expert, (zero, zero, zero, zero))

    def per_used_block(b, carry):
        tab_ref[TAB_NEXT, b] = run_expert[run_of_block[b] + FETCH_AHEAD]
        return carry
    lax.fori_loop(0, n_used, per_used_block, 0)

    def per_unused_block(b, carry):
        for row in (TAB_EXPERT, TAB_FIRST, TAB_BUFFER, TAB_VALID):
            tab_ref[row, b] = 0
        tab_ref[TAB_NEXT, b] = -1
        return carry
    lax.fori_loop(n_used, n_blocks, per_unused_block, 0)

    def clear_misc(e, carry):
        etab_ref[ETAB_MISC, e] = 0
        return carry
    lax.fori_loop(0, N_EXPERTS, clear_misc, 0)
    etab_ref[ETAB_MISC, MISC_N_USED] = n_used
    for r in range(FETCH_AHEAD):
        etab_ref[ETAB_MISC, MISC_RUN_EXPERT + r] = run_expert[r]


def _plan_kernel(counts_ref, route_ref, tri_ref, dest_ref, tab_ref, etab_ref,
                 carry_ref, pstart_ref, run_expert, run_of_block, pstart_s, *, n_blocks):
    @pl.when(pl.program_id(0) == 0)
    def _():
        carry_ref[...] = jnp.zeros_like(carry_ref)
        _slot_tables(counts_ref, tab_ref, etab_ref, run_expert, run_of_block, pstart_s, n_blocks)
        lane_row = lax.broadcasted_iota(jnp.int32, (1, LANES), 1)

        def place(e, row):
            return jnp.where(lane_row == e, pstart_s[e].astype(F32), row)
        pstart_ref[...] = lax.fori_loop(0, N_EXPERTS, place, jnp.zeros((1, LANES), F32))

    route = route_ref[...]
    lane = lax.broadcasted_iota(jnp.int32, route.shape, 1)
    e0 = route[:, ROUTE_E0:ROUTE_E0 + 1].astype(jnp.int32)
    e1 = route[:, ROUTE_E1:ROUTE_E1 + 1].astype(jnp.int32)
    oh0 = lane == e0
    oh1 = lane == e1
    both = (oh0 | oh1).astype(F32)
    earlier = jnp.dot(tri_ref[...], both.astype(BF16), preferred_element_type=F32)
    base = pstart_ref[...] + carry_ref[...] + earlier
    d0 = jnp.sum(jnp.where(oh0, base, 0.0), axis=-1, keepdims=True)
    d1 = jnp.sum(jnp.where(oh1, base, 0.0), axis=-1, keepdims=True)
    by_token = jnp.where(lane == 0, d0, jnp.where(lane == 1, d1, 0.0))
    slots = by_token.T[:TABLE_ROWS].astype(jnp.int32)
    for piece in range(PLAN_TM // SLOT_TILE):
        dest_ref[:, piece, 0, :] = slots[:, piece * SLOT_TILE:(piece + 1) * SLOT_TILE]
    carry_ref[...] += jnp.sum(both, axis=0, keepdims=True)


def _plan(counts, route, n_blocks):
    T = route.shape[0]
    tm = PLAN_TM
    tri = jnp.asarray(np.tril(np.ones((tm, tm), np.float32), -1), BF16)
    smem_out = pl.BlockSpec(memory_space=pltpu.SMEM)
    grid_spec = pltpu.PrefetchScalarGridSpec(
        num_scalar_prefetch=1,
        grid=(T // tm,),
        in_specs=[pl.BlockSpec((tm, LANES), lambda i, c: (i, 0)),
                  pl.BlockSpec((tm, tm), lambda i, c: (0, 0))],
        out_specs=[pl.BlockSpec((TABLE_ROWS, tm // SLOT_TILE, 1, SLOT_TILE),
                                lambda i, c: (0, i, 0, 0)), smem_out, smem_out],
        scratch_shapes=[pltpu.VMEM((1, LANES), F32), pltpu.VMEM((1, LANES), F32),
                        pltpu.SMEM((N_EXPERTS + FETCH_AHEAD,), jnp.int32),
                        pltpu.SMEM((n_blocks,), jnp.int32),
                        pltpu.SMEM((N_EXPERTS,), jnp.int32)],
    )
    return pl.pallas_call(
        functools.partial(_plan_kernel, n_blocks=n_blocks),
        grid_spec=grid_spec,
        out_shape=[jax.ShapeDtypeStruct((TABLE_ROWS, T // SLOT_TILE, 1, SLOT_TILE), jnp.int32),
                   jax.ShapeDtypeStruct((TAB_ROWS, n_blocks), jnp.int32),
                   jax.ShapeDtypeStruct((ETAB_ROWS, N_EXPERTS), jnp.int32)],
        compiler_params=_cparams(1),
        name="plan",
    )(counts, route, tri)


DISP_TM = 1024
DMA_GROUP = 8
ROW_WORDS = SUBLANES


def _packed_row(ref, row):
    return ref.at[pl.ds(pl.multiple_of(row * ROW_WORDS, ROW_WORDS), ROW_WORDS)]


def _dispatch_kernel(etab_ref, dest0_ref, dest1_ref, xn_ref, xs_hbm,
                     zeros, sem, tail_sem, *, experts_per_step, tails_per_step, n_blocks):
    i = pl.program_id(0)

    @pl.when(i == 0)
    def _():
        zeros[...] = jnp.zeros_like(zeros)

    def scatter_group(t, carry):
        base = pl.multiple_of(t * DMA_GROUP, DMA_GROUP)
        window = xn_ref.at[pl.ds(base * ROW_WORDS, DMA_GROUP * ROW_WORDS)]
        for k in range(DMA_GROUP):
            src = window.at[pl.ds(k * ROW_WORDS, ROW_WORDS)]
            for choice, dest_ref in enumerate((dest0_ref, dest1_ref)):
                dst = _packed_row(xs_hbm, dest_ref[0, 0, 0, base + k])
                pltpu.make_async_copy(src, dst, sem).start(priority=choice)
        return carry

    lax.fori_loop(0, DISP_TM // DMA_GROUP, scatter_group, 0)

    def pad_fill(start):
        for j in range(experts_per_step):
            e = jnp.minimum(i * experts_per_step + j, N_EXPERTS - 1)
            live = i * experts_per_step + j < N_EXPERTS
            pos = etab_ref[ETAB_PAD_LO, e]
            n_pad = jnp.where(live, etab_ref[ETAB_PAD_HI, e] - pos, 0)
            for bit in reversed(range(MOE_BLOCK.bit_length() - 1)):
                size = 1 << bit
                take = jnp.bitwise_and(lax.shift_right_logical(n_pad, bit), 1)
                dst = xs_hbm.at[pl.ds(pl.multiple_of(pos * ROW_WORDS, ROW_WORDS), size * ROW_WORDS)]
                cp = pltpu.make_async_copy(zeros.at[pl.ds(0, size * ROW_WORDS)], dst, sem)

                @pl.when(take == 1)
                def _():
                    cp.start() if start else cp.wait()
                pos = pos + take * size

    def tail_copies(start):
        for j in range(tails_per_step):
            blk = etab_ref[ETAB_MISC, MISC_N_USED] + i * tails_per_step + j
            rows = pl.ds(pl.multiple_of(jnp.minimum(blk, n_blocks - 1) * (MOE_BLOCK * ROW_WORDS),
                                        MOE_BLOCK * ROW_WORDS), MOE_BLOCK * ROW_WORDS)
            cp = pltpu.make_async_copy(zeros, xs_hbm.at[rows], tail_sem)

            @pl.when(blk < n_blocks)
            def _():
                cp.start() if start else cp.wait()

    pad_fill(True)
    tail_copies(True)

    tile_rows = DISP_TM * ROW_WORDS
    for _ in range(TOP_K):
        pltpu.make_async_copy(xn_ref, xs_hbm.at[pl.ds(0, tile_rows)], sem).wait()
    pad_fill(False)
    tail_copies(False)


def _slot_spec(choice, index):
    return pl.BlockSpec((1, 1, 1, SLOT_TILE), lambda i, *_: (choice, index(i), 0, 0),
                        memory_space=pltpu.SMEM)


def _dispatch(etab, dest4, xn_packed, n_blocks):
    assert DISP_TM == SLOT_TILE
    n = dest4.shape[1]
    min_used = (n * DISP_TM * TOP_K) // MOE_BLOCK
    tails_per_step = -(-(n_blocks - min_used) // n)
    grid_spec = pltpu.PrefetchScalarGridSpec(
        num_scalar_prefetch=1,
        grid=(n,),
        in_specs=[_slot_spec(0, lambda i: i), _slot_spec(1, lambda i: i),
                  pl.BlockSpec((DISP_TM * ROW_WORDS, LANES), lambda i, *_: (i, 0))],
        out_specs=pl.BlockSpec(memory_space=pl.ANY),
        scratch_shapes=[pltpu.VMEM((MOE_BLOCK * ROW_WORDS, LANES), U32),
                        pltpu.SemaphoreType.DMA(()), pltpu.SemaphoreType.DMA(())],
    )
    return pl.pallas_call(
        functools.partial(_dispatch_kernel, experts_per_step=-(-N_EXPERTS // n),
                          tails_per_step=tails_per_step, n_blocks=n_blocks),
        grid_spec=grid_spec,
        out_shape=jax.ShapeDtypeStruct((n_blocks * MOE_BLOCK * ROW_WORDS, LANES), U32),
        compiler_params=_cparams(1),
        name="dispatch",
    )(etab, dest4, dest4, xn_packed)


CAST_ROWS = 256
FETCH_SPLIT = 4
BLOCKS_PER_STEP = 4


def _experts_kernel(tab_ref, etab_ref,
                    xs_ref, wg_hbm, wu_hbm, wd_hbm, ys_ref,
                    fg, fu, fd, wg, wu, wd, sem):
    n_used = etab_ref[ETAB_MISC, MISC_N_USED]
    for j in range(BLOCKS_PER_STEP):
        rows = pl.ds(j * MOE_BLOCK * SUBLANES, MOE_BLOCK * SUBLANES)
        _expert_block(pl.program_id(0) * BLOCKS_PER_STEP + j, n_used, tab_ref, etab_ref,
                      xs_ref.at[rows], wg_hbm, wu_hbm, wd_hbm, ys_ref.at[rows],
                      fg, fu, fd, wg, wu, wd, sem)


def _expert_block(b, n_used, tab_ref, etab_ref, xs_ref, wg_hbm, wu_hbm, wd_hbm, ys_ref,
                  fg, fu, fd, wg, wu, wd, sem):

    def fetch(e, p):
        copies = []
        for k, (src, dst) in enumerate(((wg_hbm, fg), (wu_hbm, fu), (wd_hbm, fd))):
            rows = src.shape[1] // FETCH_SPLIT
            for c in range(FETCH_SPLIT):
                sl = pl.ds(c * rows, rows)
                copies.append(pltpu.make_async_copy(src.at[e, sl], dst.at[p, sl], sem.at[p, k]))
        return copies

    def start_fetch(e, p):
        for k, cp in enumerate(fetch(e, p)):
            cp.start(priority=k % 2)

    @pl.when(b == 0)
    def _():
        for r in range(FETCH_AHEAD):
            e = etab_ref[ETAB_MISC, MISC_RUN_EXPERT + r]

            @pl.when(e >= 0)
            def _():
                start_fetch(e, r)

    @pl.when((b < n_used) & (tab_ref[TAB_FIRST, b] == 1))
    def _():
        p = tab_ref[TAB_BUFFER, b]
        for cp in fetch(tab_ref[TAB_EXPERT, b], p):
            cp.wait()

        def cast(src, dst, n_rows):
            def body(i, carry):
                rows = pl.ds(pl.multiple_of(i * CAST_ROWS, CAST_ROWS), CAST_ROWS)
                dst[rows, :] = src[p, rows, :].astype(BF16)
                return carry
            lax.fori_loop(0, n_rows // CAST_ROWS, body, 0)

        cast(fg, wg, D_MODEL)
        cast(fu, wu, D_MODEL)
        cast(fd, wd, D_EXPERT)

        @pl.when(tab_ref[TAB_NEXT, b] >= 0)
        def _():
            start_fetch(tab_ref[TAB_NEXT, b], p)

    def swiglu_rows(m_rows):
        lo_parts, hi_parts = [], []
        for c in range(PACK_CHUNKS):
            lo, hi = _unpack_pair(xs_ref[pl.ds(c, m_rows, stride=SUBLANES), :])
            lo_parts.append(lo.astype(BF16))
            hi_parts.append(hi.astype(BF16))
        xb = jnp.concatenate(lo_parts + hi_parts, axis=1)
        g = jnp.dot(xb, wg[...], preferred_element_type=F32)
        u = jnp.dot(xb, wu[...], preferred_element_type=F32)
        hmid = (g * jax.nn.sigmoid(g) * u).astype(BF16)
        y = jnp.dot(hmid, wd[...], preferred_element_type=F32)
        packed = _pack_pair(y[:, :HALF], y[:, HALF:])
        for c in range(PACK_CHUNKS):
            ys_ref[pl.ds(c, m_rows, stride=SUBLANES), :] = packed[:, c * LANES:(c + 1) * LANES]
        if m_rows < MOE_BLOCK:
            rest = pl.ds(m_rows * SUBLANES, (MOE_BLOCK - m_rows) * SUBLANES)
            ys_ref[rest, :] = jnp.zeros(((MOE_BLOCK - m_rows) * SUBLANES, LANES), U32)

    valid = tab_ref[TAB_VALID, b]

    @pl.when((b < n_used) & (valid > MOE_BLOCK // 2))
    def _():
        swiglu_rows(MOE_BLOCK)

    @pl.when((b < n_used) & (valid <= MOE_BLOCK // 2))
    def _():
        swiglu_rows(MOE_BLOCK // 2)

    @pl.when(b >= n_used)
    def _():
        ys_ref[...] = jnp.zeros(ys_ref.shape, ys_ref.dtype)


def _experts(tab, etab, xs, w_gate, w_up, w_down):
    n_blocks = tab.shape[1]
    assert n_blocks % BLOCKS_PER_STEP == 0
    blk = lambda: pl.BlockSpec((BLOCKS_PER_STEP * MOE_BLOCK * SUBLANES, LANES),
                               lambda s, *_: (s, 0))
    grid_spec = pltpu.PrefetchScalarGridSpec(
        num_scalar_prefetch=2,
        grid=(n_blocks // BLOCKS_PER_STEP,),
        in_specs=[blk(),
                  pl.BlockSpec(memory_space=pl.ANY),
                  pl.BlockSpec(memory_space=pl.ANY),
                  pl.BlockSpec(memory_space=pl.ANY)],
        out_specs=blk(),
        scratch_shapes=[pltpu.VMEM((FETCH_AHEAD, D_MODEL, D_EXPERT), F32),
                        pltpu.VMEM((FETCH_AHEAD, D_MODEL, D_EXPERT), F32),
                        pltpu.VMEM((FETCH_AHEAD, D_EXPERT, D_MODEL), F32),
                        pltpu.VMEM((D_MODEL, D_EXPERT), BF16),
                        pltpu.VMEM((D_MODEL, D_EXPERT), BF16),
                        pltpu.VMEM((D_EXPERT, D_MODEL), BF16),
                        pltpu.SemaphoreType.DMA((FETCH_AHEAD, 3))],
    )
    return pl.pallas_call(
        _experts_kernel,
        grid_spec=grid_spec,
        out_shape=jax.ShapeDtypeStruct((n_blocks * MOE_BLOCK * SUBLANES, LANES), U32),
        compiler_params=_cparams(1),
        name="experts",
    )(tab, etab, xs, w_gate, w_up, w_down)


CMB_TM = 256


def _combine_kernel(d0_cur_ref, d1_cur_ref, d0_next_ref, d1_next_ref, h_ref, route_ref, ys_hbm,
                    o_ref, ybuf, sem):
    i = pl.program_id(0)
    n = pl.num_programs(0)
    slot = i % 2
    n_rows = TOP_K * CMB_TM

    def gather(idx_refs, tile, s):
        in_piece = jnp.bitwise_and(tile, SLOT_TILE // CMB_TM - 1) * CMB_TM

        def gather_group(t, carry):
            base = pl.multiple_of(t * DMA_GROUP, DMA_GROUP)
            for choice, idx_ref in enumerate(idx_refs):
                first = (choice * CMB_TM + base) * ROW_WORDS
                window = ybuf.at[s, pl.ds(first, DMA_GROUP * ROW_WORDS)]
                for k in range(DMA_GROUP):
                    row = idx_ref[0, 0, 0, in_piece + base + k]
                    pltpu.make_async_copy(_packed_row(ys_hbm, row),
                                          window.at[pl.ds(k * ROW_WORDS, ROW_WORDS)],
                                          sem.at[s]).start(priority=k % 2)
            return carry
        lax.fori_loop(0, CMB_TM // DMA_GROUP, gather_group, 0)

    @pl.when(i == 0)
    def _():
        gather((d0_cur_ref, d1_cur_ref), i, 0)

    @pl.when(i + 1 < n)
    def _():
        gather((d0_next_ref, d1_next_ref), i + 1, 1 - slot)

    pltpu.make_async_copy(ys_hbm.at[pl.ds(0, n_rows * ROW_WORDS)], ybuf.at[slot],
                          sem.at[slot]).wait()
    route = route_ref[...]
    w0 = route[:, ROUTE_W0:ROUTE_W0 + 1]
    w1 = route[:, ROUTE_W1:ROUTE_W1 + 1]
    yb = ybuf.at[slot]
    for c in range(PACK_CHUNKS):
        lo0, hi0 = _unpack_pair(yb[pl.ds(c, CMB_TM, stride=SUBLANES), :])
        lo1, hi1 = _unpack_pair(yb[pl.ds(CMB_TM * SUBLANES + c, CMB_TM, stride=SUBLANES), :])
        lo_cols = slice(c * LANES, (c + 1) * LANES)
        hi_cols = slice(HALF + c * LANES, HALF + (c + 1) * LANES)
        o_ref[:, lo_cols] = h_ref[:, lo_cols] + (w0 * lo0 + w1 * lo1)
        o_ref[:, hi_cols] = h_ref[:, hi_cols] + (w0 * hi0 + w1 * hi1)


def _combine(dest4, h, route, ys):
    T = h.shape[0]
    tm = CMB_TM
    per_piece = SLOT_TILE // tm
    assert per_piece & (per_piece - 1) == 0
    n = T // tm
    cur = lambda i: i // per_piece
    nxt = lambda i: jnp.minimum(i + 1, n - 1) // per_piece
    return pl.pallas_call(
        _combine_kernel,
        grid=(n,),
        in_specs=[
            _slot_spec(0, cur), _slot_spec(1, cur),
            _slot_spec(0, nxt), _slot_spec(1, nxt),
            pl.BlockSpec((tm, D_MODEL), lambda i: (i, 0)),
            pl.BlockSpec((tm, LANES), lambda i: (i, 0)),
            pl.BlockSpec(memory_space=pl.ANY),
        ],
        out_specs=pl.BlockSpec((tm, D_MODEL), lambda i: (i, 0)),
        out_shape=jax.ShapeDtypeStruct((T, D_MODEL), F32),
        scratch_shapes=[pltpu.VMEM((2, TOP_K * tm * SUBLANES, LANES), U32),
                        pltpu.SemaphoreType.DMA((2,))],
        compiler_params=_cparams(1),
        name="combine",
    )(dest4, dest4, dest4, dest4, h, route, ys)


def _rope_tables(S):
    inv = (np.float32(ROPE_THETA) ** (-np.arange(0, HEAD_DIM, 2, dtype=np.float32)
                                      / np.float32(HEAD_DIM))).astype(np.float32)
    ang = (np.arange(S, dtype=np.float32)[:, None] * inv[None, :]).astype(np.float64)
    cos, sin = np.cos(ang).astype(np.float32), np.sin(ang).astype(np.float32)
    return (jnp.asarray(np.concatenate([cos, cos], axis=-1)),
            jnp.asarray(np.concatenate([-sin, sin], axis=-1)))


def kernel(x, norm_mix_w, w_in, w_pool, pool_scale, q_norm_w, k_norm_w, sink_logits, w_out,
           norm_ffn_w, w_group_router, b_group_router, w_expert_router, b_expert_router,
           w_gate, w_up, w_down):
    B, S, D = x.shape
    T = B * S
    depth = w_in.shape[0]
    cos, sin = _rope_tables(S)
    coef_np, invc_np, bias_np = _band_constants(S)
    coef = jnp.asarray(coef_np, BF16)
    invc = jnp.asarray(invc_np, F32)
    bias = jnp.asarray(bias_np, F32)
    pad_lanes = LANES - N_GROUPS - N_EXPERTS

    h = x.reshape(T, D)
    for l in range(depth):
        u, q, k, v = _in_proj(h, norm_mix_w[l].reshape(1, D), w_in[l].astype(BF16),
                              q_norm_w[l].reshape(1, HEAD_DIM), k_norm_w[l].reshape(1, HEAD_DIM),
                              cos, sin, S)
        ab = _mixers(sink_logits[l], u.reshape(B, S, -1), q.reshape(B, S, -1),
                     k.reshape(B, S, -1), v.reshape(B, S, -1), coef, invc, bias,
                     w_pool[l].astype(BF16), pool_scale[l].reshape(1, POOL_WIDTH))
        w_router = jnp.concatenate(
            [w_group_router[l], w_expert_router[l], jnp.zeros((D, pad_lanes), F32)], axis=1)
        b_router = jnp.concatenate(
            [b_group_router[l], b_expert_router[l], jnp.zeros((pad_lanes,), F32)]).reshape(1, LANES)
        hmix, xn_packed, route, counts = _out_proj(
            ab.reshape(T, MIX_WIDTH), h, w_out[l].astype(BF16), norm_ffn_w[l].reshape(1, D),
            w_router.astype(BF16), b_router)
        n_blocks = -(-(T * TOP_K + N_EXPERTS * (MOE_BLOCK - 1)) // MOE_BLOCK)
        dest, tab, etab = _plan(counts[0].astype(jnp.int32), route, n_blocks)
        xs = _dispatch(etab, dest, xn_packed, n_blocks)
        ys = _experts(tab, etab, xs, w_gate[l], w_up[l], w_down[l])
        h = _combine(dest, hmix, route, ys)
    return h.reshape(B, S, D)
```

```python
import functools

import numpy as np
import jax
import jax.numpy as jnp
from jax import lax
from jax.experimental import pallas as pl
from jax.experimental.pallas import tpu as pltpu

D_MODEL = 2048
POOL_WIDTH = 1024
POOL_WINDOWS = (2, 4, 8, 16)
N_POOL_GROUPS = 4
POOL_GROUP = 256
HEAD_DIM = 128
N_Q_HEADS = 8
N_KV_HEADS = 2
Q_PER_KV = 4
ATTN_WIDTH = 1024
KV_WIDTH = 256
IN_PROJ_WIDTH = 2560
MIX_WIDTH = 2048
WINDOW = 128
BLOCK = 128
BAND = 3 * BLOCK
ROPE_THETA = 10000.0
N_GROUPS = 8
EXPERTS_PER_GROUP = 8
N_EXPERTS = 64
TOP_K = 2
D_EXPERT = 512
MOE_BLOCK = 256
EPS = 1e-6

LANES = 128
SUBLANES = 8
HALF = D_MODEL // 2
PACK_CHUNKS = HALF // LANES
NEG_BIG = -1e30
VMEM_LIMIT = 56 * 1024 * 1024

BF16 = jnp.bfloat16
F32 = jnp.float32
U32 = jnp.uint32
HI_MASK = 0xFFFF0000


def _cparams(n_axes):
    return pltpu.CompilerParams(dimension_semantics=("arbitrary",) * n_axes,
                                vmem_limit_bytes=VMEM_LIMIT)


def _pack_pair(lo, hi):
    lo_bits = lax.bitcast_convert_type(lo.astype(BF16).astype(F32), U32) >> 16
    hi_bits = lax.bitcast_convert_type(hi.astype(BF16).astype(F32), U32) & jnp.uint32(HI_MASK)
    return lo_bits | hi_bits


def _unpack_pair(words):
    lo = lax.bitcast_convert_type(words << 16, F32)
    hi = lax.bitcast_convert_type(words & jnp.uint32(HI_MASK), F32)
    return lo, hi


IN_TM = 512
ROW_SPLIT = 2


def _in_proj_kernel(x_ref, nw_ref, w_ref, qnw_ref, knw_ref, cos_ref, sin_ref,
                    u_ref, q_ref, k_ref, v_ref):
    sub = IN_TM // ROW_SPLIT
    for part in range(ROW_SPLIT):
        rows = slice(part * sub, (part + 1) * sub)
        x = x_ref[rows, :]
        ms = jnp.mean(x * x, axis=-1, keepdims=True)
        xn = (x * lax.rsqrt(ms + EPS) * nw_ref[...]).astype(BF16)
        cos = cos_ref[rows, :]
        sin = sin_ref[rows, :]

        def head_norm_rope(t, w, scale):
            hms = jnp.mean(t * t, axis=-1, keepdims=True)
            t = t * lax.rsqrt(hms + EPS) * w
            t = t * cos + pltpu.roll(t, HEAD_DIM // 2, axis=1) * sin
            return t * scale

        o_k = POOL_WIDTH + ATTN_WIDTH
        zq = jnp.dot(xn, w_ref[:, POOL_WIDTH:o_k], preferred_element_type=F32)
        zk = jnp.dot(xn, w_ref[:, o_k:o_k + KV_WIDTH], preferred_element_type=F32)
        qnw = qnw_ref[...]
        for h in range(N_Q_HEADS):
            sl = slice(h * HEAD_DIM, (h + 1) * HEAD_DIM)
            q_ref[rows, sl] = head_norm_rope(zq[:, sl], qnw, HEAD_DIM ** -0.5).astype(BF16)
        knw = knw_ref[...]
        for h in range(N_KV_HEADS):
            sl = slice(h * HEAD_DIM, (h + 1) * HEAD_DIM)
            k_ref[rows, sl] = head_norm_rope(zk[:, sl], knw, 1.0).astype(BF16)
        u_ref[rows, :] = jnp.dot(xn, w_ref[:, :POOL_WIDTH],
                                 preferred_element_type=F32).astype(BF16)
        v_ref[rows, :] = jnp.dot(xn, w_ref[:, o_k + KV_WIDTH:],
                                 preferred_element_type=F32).astype(BF16)


def _in_proj(x2, nw, w_in, qnw, knw, cos, sin, seq):
    T = x2.shape[0]
    tm = IN_TM
    pos_blocks = seq // tm
    full = lambda shape: pl.BlockSpec(shape, lambda i: (0,) * len(shape))
    return pl.pallas_call(
        _in_proj_kernel,
        grid=(T // tm,),
        in_specs=[
            pl.BlockSpec((tm, D_MODEL), lambda i: (i, 0)),
            full((1, D_MODEL)),
            full((D_MODEL, IN_PROJ_WIDTH)),
            full((1, HEAD_DIM)),
            full((1, HEAD_DIM)),
            pl.BlockSpec((tm, HEAD_DIM), lambda i: (i % pos_blocks, 0)),
            pl.BlockSpec((tm, HEAD_DIM), lambda i: (i % pos_blocks, 0)),
        ],
        out_specs=[
            pl.BlockSpec((tm, POOL_WIDTH), lambda i: (i, 0)),
            pl.BlockSpec((tm, ATTN_WIDTH), lambda i: (i, 0)),
            pl.BlockSpec((tm, KV_WIDTH), lambda i: (i, 0)),
            pl.BlockSpec((tm, KV_WIDTH), lambda i: (i, 0)),
        ],
        out_shape=[
            jax.ShapeDtypeStruct((T, POOL_WIDTH), BF16),
            jax.ShapeDtypeStruct((T, ATTN_WIDTH), BF16),
            jax.ShapeDtypeStruct((T, KV_WIDTH), BF16),
            jax.ShapeDtypeStruct((T, KV_WIDTH), BF16),
        ],
        compiler_params=_cparams(1),
        name="in_proj",
    )(x2, nw, w_in, qnw, knw, cos, sin)


MIX_TQ = 512
MIX_STAGE_SUBS = 2


def _band_constants(seq):
    nb = seq // BLOCK
    coef = np.zeros((3, N_POOL_GROUPS, BLOCK, BAND), np.float32)
    inv_count = np.zeros((3, N_POOL_GROUPS, BLOCK, 1), np.float32)
    bias = np.zeros((3, BLOCK, BAND), np.float32)
    for kind, n in enumerate((0, 1, nb - 1)):
        start = min(max((n - 1) * BLOCK, 0), seq - BAND)
        t = n * BLOCK + np.arange(BLOCK)[:, None]
        s = start + np.arange(BAND)[None, :]
        bias[kind] = np.where(np.abs(s - t) <= WINDOW, 0.0, NEG_BIG)
        for g, win in enumerate(POOL_WINDOWS):
            half = win // 2
            lo = np.clip(t - half, 0, seq)
            hi = np.clip(t + half, 0, seq)
            count = (hi - lo).astype(np.float32)
            inside = ((s >= lo) & (s < hi)).astype(np.float32)
            coef[kind, g] = inside - count * (s == t)
            inv_count[kind, g] = 1.0 / count
    return coef, inv_count, bias


def _mixers_kernel(sink_ref, u_ref, q_ref, k_ref, v_ref, coef_ref, invc_ref, bias_ref,
                   wpool_ref, pscale_ref, ab_ref, *, seq):
    nb = seq // BLOCK
    j = pl.program_id(1)
    n_sub = MIX_TQ // BLOCK
    subs = []
    for r in range(n_sub):
        n = j * n_sub + r
        start = pl.multiple_of(jnp.clip((n - 1) * BLOCK, 0, seq - BAND), BLOCK)
        kind = jnp.where(n == 0, 0, jnp.where(n == nb - 1, 2, 1))
        subs.append((start, kind, slice(r * BLOCK, (r + 1) * BLOCK)))

    def staged(group):
        scores = {}
        for r, (start, kind, rows) in enumerate(group):
            for hk in range(N_KV_HEADS):
                kb = k_ref[0, pl.ds(start, BAND), hk * HEAD_DIM:(hk + 1) * HEAD_DIM]
                qs = jnp.concatenate(
                    [q_ref[0, rows, h * HEAD_DIM:(h + 1) * HEAD_DIM]
                     for h in range(hk * Q_PER_KV, (hk + 1) * Q_PER_KV)], axis=0)
                scores[r, hk] = lax.dot_general(qs, kb, (((1,), (1,)), ((), ())),
                                                preferred_element_type=F32)
        diffs = {}
        for r, (start, kind, rows) in enumerate(group):
            for g in range(N_POOL_GROUPS):
                ub = u_ref[0, pl.ds(start, BAND), g * POOL_GROUP:(g + 1) * POOL_GROUP]
                d = jnp.dot(coef_ref[kind, g], ub, preferred_element_type=F32) * invc_ref[kind, g]
                diffs[r, g] = d.astype(BF16)
        probs = {}
        for r, (start, kind, rows) in enumerate(group):
            bias = bias_ref[kind]
            for hk in range(N_KV_HEADS):
                for gi in range(Q_PER_KV):
                    h = hk * Q_PER_KV + gi
                    sh = scores[r, hk][gi * BLOCK:(gi + 1) * BLOCK] + bias
                    sink = sink_ref[h]
                    m = jnp.maximum(jnp.max(sh, axis=-1, keepdims=True), sink)
                    p = jnp.exp(sh - m)
                    denom = jnp.sum(p, axis=-1, keepdims=True) + jnp.exp(sink - m)
                    probs[r, h] = (p.astype(BF16), denom)
        for r, (start, kind, rows) in enumerate(group):
            for g in range(N_POOL_GROUPS):
                cols = slice(g * POOL_GROUP, (g + 1) * POOL_GROUP)
                y = jnp.dot(diffs[r, g], wpool_ref[g], preferred_element_type=F32)
                ab_ref[0, rows, cols] = (y * pscale_ref[:, cols]).astype(BF16)
        for r, (start, kind, rows) in enumerate(group):
            for hk in range(N_KV_HEADS):
                vb = v_ref[0, pl.ds(start, BAND), hk * HEAD_DIM:(hk + 1) * HEAD_DIM]
                for gi in range(Q_PER_KV):
                    h = hk * Q_PER_KV + gi
                    p, denom = probs[r, h]
                    o = jnp.dot(p, vb, preferred_element_type=F32) / denom
                    ocols = slice(POOL_WIDTH + h * HEAD_DIM, POOL_WIDTH + (h + 1) * HEAD_DIM)
                    ab_ref[0, rows, ocols] = o.astype(BF16)

    for first in range(0, n_sub, MIX_STAGE_SUBS):
        staged(subs[first:first + MIX_STAGE_SUBS])


def _mixers(sink, u, q, k, v, coef, invc, bias, wpool, pscale):
    B, S, _ = u.shape
    tq = MIX_TQ
    full = lambda shape: pl.BlockSpec(shape, lambda b, j: (0,) * len(shape))
    return pl.pallas_call(
        functools.partial(_mixers_kernel, seq=S),
        grid=(B, S // tq),
        in_specs=[
            pl.BlockSpec(memory_space=pltpu.SMEM),
            pl.BlockSpec((1, S, POOL_WIDTH), lambda b, j: (b, 0, 0)),
            pl.BlockSpec((1, tq, ATTN_WIDTH), lambda b, j: (b, j, 0)),
            pl.BlockSpec((1, S, KV_WIDTH), lambda b, j: (b, 0, 0)),
            pl.BlockSpec((1, S, KV_WIDTH), lambda b, j: (b, 0, 0)),
            full((3, N_POOL_GROUPS, BLOCK, BAND)),
            full((3, N_POOL_GROUPS, BLOCK, 1)),
            full((3, BLOCK, BAND)),
            full((N_POOL_GROUPS, POOL_GROUP, POOL_GROUP)),
            full((1, POOL_WIDTH)),
        ],
        out_specs=pl.BlockSpec((1, tq, MIX_WIDTH), lambda b, j: (b, j, 0)),
        out_shape=jax.ShapeDtypeStruct((B, S, MIX_WIDTH), BF16),
        compiler_params=_cparams(2),
        name="mixers",
    )(sink, u, q, k, v, coef, invc, bias, wpool, pscale)


OUT_TM = 512
ROUTE_E0, ROUTE_E1, ROUTE_W0, ROUTE_W1 = 0, 1, 2, 3


def _out_proj_kernel(ab_ref, x_ref, w_ref, nw_ref, wr_ref, br_ref,
                     h_ref, xn_ref, route_ref, counts_ref, logits_keep, *, n_tiles):
    i = pl.program_id(0)

    @pl.when(i == 0)
    def _():
        counts_ref[...] = jnp.zeros_like(counts_ref)
        logits_keep[...] = jnp.zeros_like(logits_keep)

    sub = OUT_TM // ROW_SPLIT
    lane = lax.broadcasted_iota(jnp.int32, (sub, LANES), 1)
    prev_slot = (i + 1) % 2
    cur_slot = i % 2

    def first_argmax(vals):
        m = jnp.max(vals, axis=-1, keepdims=True)
        idx = jnp.min(jnp.where(vals == m, lane, LANES), axis=-1, keepdims=True)
        return m, idx

    def project_tile():
        for part in range(ROW_SPLIT):
            rows = slice(part * sub, (part + 1) * sub)
            h = x_ref[rows, :] + jnp.dot(ab_ref[rows, :], w_ref[...], preferred_element_type=F32)
            h_ref[rows, :] = h
            ms = jnp.mean(h * h, axis=-1, keepdims=True)
            xn = h * lax.rsqrt(ms + EPS) * nw_ref[...]
            packed = _pack_pair(xn[:, :HALF], xn[:, HALF:])
            for c in range(PACK_CHUNKS):
                xn_ref[pl.ds(part * sub * SUBLANES + c, sub, stride=SUBLANES), :] = (
                    packed[:, c * LANES:(c + 1) * LANES])
            logits_keep[cur_slot, rows, :] = (
                jnp.dot(xn.astype(BF16), wr_ref[...], preferred_element_type=F32) + br_ref[...])

    def route_previous_tile():
        new_counts = jnp.zeros((1, LANES), F32)
        for part in range(ROW_SPLIT):
            new_counts = new_counts + route_rows(slice(part * sub, (part + 1) * sub))
        counts_ref[...] += jnp.where(i > 0, new_counts, 0.0)

    def route_rows(rows):
        logits = logits_keep[prev_slot, rows, :]
        gl = jnp.where(lane < N_GROUPS, logits, NEG_BIG)
        gmax, gidx = first_argmax(gl)
        gsum = jnp.sum(jnp.where(lane < N_GROUPS, jnp.exp(gl - gmax), 0.0),
                       axis=-1, keepdims=True)
        g_w = 1.0 / gsum
        e_lo = N_GROUPS + gidx * EXPERTS_PER_GROUP
        el = jnp.where((lane >= e_lo) & (lane < e_lo + EXPERTS_PER_GROUP), logits, NEG_BIG)
        m1, i1 = first_argmax(el)
        m2, i2 = first_argmax(jnp.where(lane == i1, NEG_BIG, el))
        t = jnp.exp(m2 - m1)
        p1 = 1.0 / (1.0 + t)
        p2 = t * p1
        e1 = i1 - N_GROUPS
        e2 = i2 - N_GROUPS
        route_ref[rows, :] = jnp.where(
            lane == ROUTE_E0, e1.astype(F32),
            jnp.where(lane == ROUTE_E1, e2.astype(F32),
                      jnp.where(lane == ROUTE_W0, g_w * p1,
                                jnp.where(lane == ROUTE_W1, g_w * p2, 0.0))))
        chosen = ((lane == e1) | (lane == e2)).astype(F32)
        return jnp.sum(chosen, axis=0, keepdims=True)

    @pl.when(i < n_tiles)
    def _():
        route_previous_tile()
        project_tile()

    @pl.when(i == n_tiles)
    def _():
        route_previous_tile()


def _out_proj(ab, x2, w_out, nw, w_router, b_router):
    T = x2.shape[0]
    tm = OUT_TM
    n_tiles = T // tm
    full = lambda shape: pl.BlockSpec(shape, lambda i: (0,) * len(shape))
    cur = lambda rows, width: pl.BlockSpec((rows, width),
                                           lambda i: (jnp.minimum(i, n_tiles - 1), 0))
    row = lambda width: cur(tm, width)
    return pl.pallas_call(
        functools.partial(_out_proj_kernel, n_tiles=n_tiles),
        grid=(n_tiles + 1,),
        in_specs=[row(MIX_WIDTH), row(D_MODEL), full((MIX_WIDTH, D_MODEL)), full((1, D_MODEL)),
                  full((D_MODEL, LANES)), full((1, LANES))],
        out_specs=[row(D_MODEL),
                   cur(tm * SUBLANES, LANES),
                   pl.BlockSpec((tm, LANES), lambda i: (jnp.maximum(i - 1, 0), 0)),
                   full((1, LANES))],
        scratch_shapes=[pltpu.VMEM((2, tm, LANES), F32)],
        out_shape=[jax.ShapeDtypeStruct((T, D_MODEL), F32),
                   jax.ShapeDtypeStruct((T * SUBLANES, LANES), U32),
                   jax.ShapeDtypeStruct((T, LANES), F32),
                   jax.ShapeDtypeStruct((1, LANES), F32)],
        compiler_params=_cparams(1),
        name="out_proj",
    )(ab, x2, w_out, nw, w_router, b_router)


PLAN_TM = 1024
SLOT_TILE = 1024


TAB_EXPERT, TAB_FIRST, TAB_BUFFER, TAB_NEXT, TAB_VALID = 0, 1, 2, 3, 4
TAB_ROWS = 5
ETAB_PAD_LO, ETAB_PAD_HI, ETAB_MISC = 0, 1, 2
ETAB_ROWS = 3
MISC_N_USED, MISC_RUN_EXPERT = 0, 1
FETCH_AHEAD = 2
TABLE_ROWS = SUBLANES


def _slot_tables(counts_ref, tab_ref, etab_ref, run_expert, run_of_block, pstart_s, n_blocks):
    log_block = MOE_BLOCK.bit_length() - 1

    def clear(r, carry):
        run_expert[r] = -1
        return carry
    lax.fori_loop(0, run_expert.shape[0], clear, 0)

    def per_expert(e, carry):
        slot, blk, run, buf = carry
        c = counts_ref[e]
        n_blk = lax.shift_right_logical(c + (MOE_BLOCK - 1), log_block)
        pstart_s[e] = slot
        etab_ref[ETAB_PAD_LO, e] = slot + c
        etab_ref[ETAB_PAD_HI, e] = slot + n_blk * MOE_BLOCK

        def per_block(b, inner):
            tab_ref[TAB_EXPERT, b] = e
            tab_ref[TAB_FIRST, b] = (b == blk).astype(jnp.int32)
            tab_ref[TAB_BUFFER, b] = buf
            tab_ref[TAB_VALID, b] = jnp.minimum(c - (b - blk) * MOE_BLOCK, MOE_BLOCK)
            run_of_block[b] = run
            return inner
        lax.fori_loop(blk, blk + n_blk, per_block, 0)

        @pl.when(n_blk > 0)
        def _():
            run_expert[run] = e
        has_run = (n_blk > 0).astype(jnp.int32)
        next_buf = jnp.where(buf + has_run == FETCH_AHEAD, 0, buf + has_run)
        return slot + n_blk * MOE_BLOCK, blk + n_blk, run + has_run, next_buf

    zero = jnp.int32(0)
    _, n_used, _, _ = lax.fori_loop(0, N_EXPERTS, per_expert, (zero, zero, zero, zero))

    def per_used_block(b, carry):
        tab_ref[TAB_NEXT, b] = run_expert[run_of_block[b] + FETCH_AHEAD]
        return carry
    lax.fori_loop(0, n_used, per_used_block, 0)

    def per_unused_block(b, carry):
        for row in (TAB_EXPERT, TAB_FIRST, TAB_BUFFER, TAB_VALID):
            tab_ref[row, b] = 0
        tab_ref[TAB_NEXT, b] = -1
        return carry
    lax.fori_loop(n_used, n_blocks, per_unused_block, 0)

    def clear_misc(e, carry):
        etab_ref[ETAB_MISC, e] = 0
        return carry
    lax.fori_loop(0, N_EXPERTS, clear_misc, 0)
    etab_ref[ETAB_MISC, MISC_N_USED] = n_used
    for r in range(FETCH_AHEAD):
        etab_ref[ETAB_MISC, MISC_RUN_EXPERT + r] = run_expert[r]


def _plan_kernel(counts_ref, route_ref, tri_ref, dest_ref, tab_ref, etab_ref,
                 carry_ref, pstart_ref, run_expert, run_of_block, pstart_s, *, n_blocks):
    @pl.when(pl.program_id(0) == 0)
    def _():
        carry_ref[...] = jnp.zeros_like(carry_ref)
        _slot_tables(counts_ref, tab_ref, etab_ref, run_expert, run_of_block, pstart_s, n_blocks)
        lane_row = lax.broadcasted_iota(jnp.int32, (1, LANES), 1)

        def place(e, row):
            return jnp.where(lane_row == e, pstart_s[e].astype(F32), row)
        pstart_ref[...] = lax.fori_loop(0, N_EXPERTS, place, jnp.zeros((1, LANES), F32))

    route = route_ref[...]
    lane = lax.broadcasted_iota(jnp.int32, route.shape, 1)
    e0 = route[:, ROUTE_E0:ROUTE_E0 + 1].astype(jnp.int32)
    e1 = route[:, ROUTE_E1:ROUTE_E1 + 1].astype(jnp.int32)
    oh0 = lane == e0
    oh1 = lane == e1
    both = (oh0 | oh1).astype(F32)
    earlier = jnp.dot(tri_ref[...], both.astype(BF16), preferred_element_type=F32)
    base = pstart_ref[...] + carry_ref[...] + earlier
    d0 = jnp.sum(jnp.where(oh0, base, 0.0), axis=-1, keepdims=True)
    d1 = jnp.sum(jnp.where(oh1, base, 0.0), axis=-1, keepdims=True)
    by_token = jnp.where(lane == 0, d0, jnp.where(lane == 1, d1, 0.0))
    slots = by_token.T[:TABLE_ROWS].astype(jnp.int32)
    for piece in range(PLAN_TM // SLOT_TILE):
        dest_ref[:, piece, 0, :] = slots[:, piece * SLOT_TILE:(piece + 1) * SLOT_TILE]
    carry_ref[...] += jnp.sum(both, axis=0, keepdims=True)


def _plan(counts, route, n_blocks):
    T = route.shape[0]
    tm = PLAN_TM
    tri = jnp.asarray(np.tril(np.ones((tm, tm), np.float32), -1), BF16)
    smem_out = pl.BlockSpec(memory_space=pltpu.SMEM)
    grid_spec = pltpu.PrefetchScalarGridSpec(
        num_scalar_prefetch=1,
        grid=(T // tm,),
        in_specs=[pl.BlockSpec((tm, LANES), lambda i, c: (i, 0)),
                  pl.BlockSpec((tm, tm), lambda i, c: (0, 0))],
        out_specs=[pl.BlockSpec((TABLE_ROWS, tm // SLOT_TILE, 1, SLOT_TILE),
                                lambda i, c: (0, i, 0, 0)), smem_out, smem_out],
        scratch_shapes=[pltpu.VMEM((1, LANES), F32), pltpu.VMEM((1, LANES), F32),
                        pltpu.SMEM((N_EXPERTS + FETCH_AHEAD,), jnp.int32),
                        pltpu.SMEM((n_blocks,), jnp.int32),
                        pltpu.SMEM((N_EXPERTS,), jnp.int32)],
    )
    return pl.pallas_call(
        functools.partial(_plan_kernel, n_blocks=n_blocks),
        grid_spec=grid_spec,
        out_shape=[jax.ShapeDtypeStruct((TABLE_ROWS, T // SLOT_TILE, 1, SLOT_TILE), jnp.int32),
                   jax.ShapeDtypeStruct((TAB_ROWS, n_blocks), jnp.int32),
                   jax.ShapeDtypeStruct((ETAB_ROWS, N_EXPERTS), jnp.int32)],
        compiler_params=_cparams(1),
        name="plan",
    )(counts, route, tri)


DISP_TM = 1024
DMA_GROUP = 8
ROW_WORDS = SUBLANES


def _packed_row(ref, row):
    return ref.at[pl.ds(pl.multiple_of(row * ROW_WORDS, ROW_WORDS), ROW_WORDS)]


def _dispatch_kernel(etab_ref, dest0_ref, dest1_ref, xn_ref, xs_hbm,
                     zeros, sem, tail_sem, *, experts_per_step, tails_per_step, n_blocks):
    i = pl.program_id(0)

    @pl.when(i == 0)
    def _():
        zeros[...] = jnp.zeros_like(zeros)

    def scatter_group(t, carry):
        base = pl.multiple_of(t * DMA_GROUP, DMA_GROUP)
        window = xn_ref.at[pl.ds(base * ROW_WORDS, DMA_GROUP * ROW_WORDS)]
        for k in range(DMA_GROUP):
            src = window.at[pl.ds(k * ROW_WORDS, ROW_WORDS)]
            for choice, dest_ref in enumerate((dest0_ref, dest1_ref)):
                dst = _packed_row(xs_hbm, dest_ref[0, 0, 0, base + k])
                pltpu.make_async_copy(src, dst, sem).start(priority=choice)
        return carry

    lax.fori_loop(0, DISP_TM // DMA_GROUP, scatter_group, 0)

    def pad_fill(start):
        for j in range(experts_per_step):
            e = jnp.minimum(i * experts_per_step + j, N_EXPERTS - 1)
            live = i * experts_per_step + j < N_EXPERTS
            pos = etab_ref[ETAB_PAD_LO, e]
            n_pad = jnp.where(live, etab_ref[ETAB_PAD_HI, e] - pos, 0)
            for bit in reversed(range(MOE_BLOCK.bit_length() - 1)):
                size = 1 << bit
                take = jnp.bitwise_and(lax.shift_right_logical(n_pad, bit), 1)
                dst = xs_hbm.at[pl.ds(pl.multiple_of(pos * ROW_WORDS, ROW_WORDS), size * ROW_WORDS)]
                cp = pltpu.make_async_copy(zeros.at[pl.ds(0, size * ROW_WORDS)], dst, sem)

                @pl.when(take == 1)
                def _():
                    cp.start() if start else cp.wait()
                pos = pos + take * size

    def tail_copies(start):
        for j in range(tails_per_step):
            blk = etab_ref[ETAB_MISC, MISC_N_USED] + i * tails_per_step + j
            rows = pl.ds(pl.multiple_of(jnp.minimum(blk, n_blocks - 1) * (MOE_BLOCK * ROW_WORDS),
                                        MOE_BLOCK * ROW_WORDS), MOE_BLOCK * ROW_WORDS)
            cp = pltpu.make_async_copy(zeros, xs_hbm.at[rows], tail_sem)

            @pl.when(blk < n_blocks)
            def _():
                cp.start() if start else cp.wait()

    pad_fill(True)
    tail_copies(True)

    tile_rows = DISP_TM * ROW_WORDS
    for _ in range(TOP_K):
        pltpu.make_async_copy(xn_ref, xs_hbm.at[pl.ds(0, tile_rows)], sem).wait()
    pad_fill(False)
    tail_copies(False)


def _slot_spec(choice, index):
    return pl.BlockSpec((1, 1, 1, SLOT_TILE), lambda i, *_: (choice, index(i), 0, 0),
                        memory_space=pltpu.SMEM)


def _dispatch(etab, dest4, xn_packed, n_blocks):
    assert DISP_TM == SLOT_TILE
    n = dest4.shape[1]
    min_used = (n * DISP_TM * TOP_K) // MOE_BLOCK
    tails_per_step = -(-(n_blocks - min_used) // n)
    grid_spec = pltpu.PrefetchScalarGridSpec(
        num_scalar_prefetch=1,
        grid=(n,),
        in_specs=[_slot_spec(0, lambda i: i), _slot_spec(1, lambda i: i),
                  pl.BlockSpec((DISP_TM * ROW_WORDS, LANES), lambda i, *_: (i, 0))],
        out_specs=pl.BlockSpec(memory_space=pl.ANY),
        scratch_shapes=[pltpu.VMEM((MOE_BLOCK * ROW_WORDS, LANES), U32),
                        pltpu.SemaphoreType.DMA(()), pltpu.SemaphoreType.DMA(())],
    )
    return pl.pallas_call(
        functools.partial(_dispatch_kernel, experts_per_step=-(-N_EXPERTS // n),
                          tails_per_step=tails_per_step, n_blocks=n_blocks),
        grid_spec=grid_spec,
        out_shape=jax.ShapeDtypeStruct((n_blocks * MOE_BLOCK * ROW_WORDS, LANES), U32),
        compiler_params=_cparams(1),
        name="dispatch",
    )(etab, dest4, dest4, xn_packed)


CAST_ROWS = 256
FETCH_SPLIT = 4
BLOCKS_PER_STEP = 4


def _experts_kernel(tab_ref, etab_ref,
                    xs_ref, wg_hbm, wu_hbm, wd_hbm, ys_ref,
                    fg, fu, fd, wg, wu, wd, sem):
    n_used = etab_ref[ETAB_MISC, MISC_N_USED]
    for j in range(BLOCKS_PER_STEP):
        rows = pl.ds(j * MOE_BLOCK * SUBLANES, MOE_BLOCK * SUBLANES)
        _expert_block(pl.program_id(0) * BLOCKS_PER_STEP + j, n_used, tab_ref, etab_ref,
                      xs_ref.at[rows], wg_hbm, wu_hbm, wd_hbm, ys_ref.at[rows],
                      fg, fu, fd, wg, wu, wd, sem)


def _expert_block(b, n_used, tab_ref, etab_ref, xs_ref, wg_hbm, wu_hbm, wd_hbm, ys_ref,
                  fg, fu, fd, wg, wu, wd, sem):

    def fetch(e, p):
        copies = []
        for k, (src, dst) in enumerate(((wg_hbm, fg), (wu_hbm, fu), (wd_hbm, fd))):
            rows = src.shape[1] // FETCH_SPLIT
            for c in range(FETCH_SPLIT):
                sl = pl.ds(c * rows, rows)
                copies.append(pltpu.make_async_copy(src.at[e, sl], dst.at[p, sl], sem.at[p, k]))
        return copies

    def start_fetch(e, p):
        for k, cp in enumerate(fetch(e, p)):
            cp.start(priority=k % 2)

    @pl.when(b == 0)
    def _():
        for r in range(FETCH_AHEAD):
            e = etab_ref[ETAB_MISC, MISC_RUN_EXPERT + r]

            @pl.when(e >= 0)
            def _():
                start_fetch(e, r)

    @pl.when((b < n_used) & (tab_ref[TAB_FIRST, b] == 1))
    def _():
        p = tab_ref[TAB_BUFFER, b]
        for cp in fetch(tab_ref[TAB_EXPERT, b], p):
            cp.wait()

        def cast(src, dst, n_rows):
            def body(i, carry):
                rows = pl.ds(pl.multiple_of(i * CAST_ROWS, CAST_ROWS), CAST_ROWS)
                dst[rows, :] = src[p, rows, :].astype(BF16)
                return carry
            lax.fori_loop(0, n_rows // CAST_ROWS, body, 0)

        cast(fg, wg, D_MODEL)
        cast(fu, wu, D_MODEL)
        cast(fd, wd, D_EXPERT)

        @pl.when(tab_ref[TAB_NEXT, b] >= 0)
        def _():
            start_fetch(tab_ref[TAB_NEXT, b], p)

    def swiglu_rows(m_rows):
        lo_parts, hi_parts = [], []
        for c in range(PACK_CHUNKS):
            lo, hi = _unpack_pair(xs_ref[pl.ds(c, m_rows, stride=SUBLANES), :])
            lo_parts.append(lo.astype(BF16))
            hi_parts.append(hi.astype(BF16))
        xb = jnp.concatenate(lo_parts + hi_parts, axis=1)
        g = jnp.dot(xb, wg[...], preferred_element_type=F32)
        u = jnp.dot(xb, wu[...], preferred_element_type=F32)
        hmid = (g * jax.nn.sigmoid(g) * u).astype(BF16)
        y = jnp.dot(hmid, wd[...], preferred_element_type=F32)
        packed = _pack_pair(y[:, :HALF], y[:, HALF:])
        for c in range(PACK_CHUNKS):
            ys_ref[pl.ds(c, m_rows, stride=SUBLANES), :] = packed[:, c * LANES:(c + 1) * LANES]
        if m_rows < MOE_BLOCK:
            rest = pl.ds(m_rows * SUBLANES, (MOE_BLOCK - m_rows) * SUBLANES)
            ys_ref[rest, :] = jnp.zeros(((MOE_BLOCK - m_rows) * SUBLANES, LANES), U32)

    valid = tab_ref[TAB_VALID, b]

    @pl.when((b < n_used) & (valid > MOE_BLOCK // 2))
    def _():
        swiglu_rows(MOE_BLOCK)

    @pl.when((b < n_used) & (valid <= MOE_BLOCK // 2))
    def _():
        swiglu_rows(MOE_BLOCK // 2)

    @pl.when(b >= n_used)
    def _():
        ys_ref[...] = jnp.zeros(ys_ref.shape, ys_ref.dtype)


def _experts(tab, etab, xs, w_gate, w_up, w_down):
    n_blocks = tab.shape[1]
    assert n_blocks % BLOCKS_PER_STEP == 0
    blk = lambda: pl.BlockSpec((BLOCKS_PER_STEP * MOE_BLOCK * SUBLANES, LANES),
                               lambda s, *_: (s, 0))
    grid_spec = pltpu.PrefetchScalarGridSpec(
        num_scalar_prefetch=2,
        grid=(n_blocks // BLOCKS_PER_STEP,),
        in_specs=[blk(),
                  pl.BlockSpec(memory_space=pl.ANY),
                  pl.BlockSpec(memory_space=pl.ANY),
                  pl.BlockSpec(memory_space=pl.ANY)],
        out_specs=blk(),
        scratch_shapes=[pltpu.VMEM((FETCH_AHEAD, D_MODEL, D_EXPERT), F32),
                        pltpu.VMEM((FETCH_AHEAD, D_MODEL, D_EXPERT), F32),
                        pltpu.VMEM((FETCH_AHEAD, D_EXPERT, D_MODEL), F32),
                        pltpu.VMEM((D_MODEL, D_EXPERT), BF16),
                        pltpu.VMEM((D_MODEL, D_EXPERT), BF16),
                        pltpu.VMEM((D_EXPERT, D_MODEL), BF16),
                        pltpu.SemaphoreType.DMA((FETCH_AHEAD, 3))],
    )
    return pl.pallas_call(
        _experts_kernel,
        grid_spec=grid_spec,
        out_shape=jax.ShapeDtypeStruct((n_blocks * MOE_BLOCK * SUBLANES, LANES), U32),
        compiler_params=_cparams(1),
        name="experts",
    )(tab, etab, xs, w_gate, w_up, w_down)


CMB_TM = 256


def _combine_kernel(d0_cur_ref, d1_cur_ref, d0_next_ref, d1_next_ref, h_ref, route_ref, ys_hbm,
                    o_ref, ybuf, sem):
    i = pl.program_id(0)
    n = pl.num_programs(0)
    slot = i % 2
    n_rows = TOP_K * CMB_TM

    def gather(idx_refs, tile, s):
        in_piece = jnp.bitwise_and(tile, SLOT_TILE // CMB_TM - 1) * CMB_TM

        def gather_group(t, carry):
            base = pl.multiple_of(t * DMA_GROUP, DMA_GROUP)
            for choice, idx_ref in enumerate(idx_refs):
                first = (choice * CMB_TM + base) * ROW_WORDS
                window = ybuf.at[s, pl.ds(first, DMA_GROUP * ROW_WORDS)]
                for k in range(DMA_GROUP):
                    row = idx_ref[0, 0, 0, in_piece + base + k]
                    pltpu.make_async_copy(_packed_row(ys_hbm, row),
                                          window.at[pl.ds(k * ROW_WORDS, ROW_WORDS)],
                                          sem.at[s]).start(priority=k % 2)
            return carry
        lax.fori_loop(0, CMB_TM // DMA_GROUP, gather_group, 0)

    @pl.when(i == 0)
    def _():
        gather((d0_cur_ref, d1_cur_ref), i, 0)

    @pl.when(i + 1 < n)
    def _():
        gather((d0_next_ref, d1_next_ref), i + 1, 1 - slot)

    pltpu.make_async_copy(ys_hbm.at[pl.ds(0, n_rows * ROW_WORDS)], ybuf.at[slot],
                          sem.at[slot]).wait()
    route = route_ref[...]
    w0 = route[:, ROUTE_W0:ROUTE_W0 + 1]
    w1 = route[:, ROUTE_W1:ROUTE_W1 + 1]
    yb = ybuf.at[slot]
    for c in range(PACK_CHUNKS):
        lo0, hi0 = _unpack_pair(yb[pl.ds(c, CMB_TM, stride=SUBLANES), :])
        lo1, hi1 = _unpack_pair(yb[pl.ds(CMB_TM * SUBLANES + c, CMB_TM, stride=SUBLANES), :])
        lo_cols = slice(c * LANES, (c + 1) * LANES)
        hi_cols = slice(HALF + c * LANES, HALF + (c + 1) * LANES)
        o_ref[:, lo_cols] = h_ref[:, lo_cols] + (w0 * lo0 + w1 * lo1)
        o_ref[:, hi_cols] = h_ref[:, hi_cols] + (w0 * hi0 + w1 * hi1)


def _combine(dest4, h, route, ys):
    T = h.shape[0]
    tm = CMB_TM
    per_piece = SLOT_TILE // tm
    assert per_piece & (per_piece - 1) == 0
    n = T // tm
    cur = lambda i: i // per_piece
    nxt = lambda i: jnp.minimum(i + 1, n - 1) // per_piece
    return pl.pallas_call(
        _combine_kernel,
        grid=(n,),
        in_specs=[
            _slot_spec(0, cur), _slot_spec(1, cur),
            _slot_spec(0, nxt), _slot_spec(1, nxt),
            pl.BlockSpec((tm, D_MODEL), lambda i: (i, 0)),
            pl.BlockSpec((tm, LANES), lambda i: (i, 0)),
            pl.BlockSpec(memory_space=pl.ANY),
        ],
        out_specs=pl.BlockSpec((tm, D_MODEL), lambda i: (i, 0)),
        out_shape=jax.ShapeDtypeStruct((T, D_MODEL), F32),
        scratch_shapes=[pltpu.VMEM((2, TOP_K * tm * SUBLANES, LANES), U32),
                        pltpu.SemaphoreType.DMA((2,))],
        compiler_params=_cparams(1),
        name="combine",
    )(dest4, dest4, dest4, dest4, h, route, ys)


def _rope_tables(S):
    inv = (np.float32(ROPE_THETA) ** (-np.arange(0, HEAD_DIM, 2, dtype=np.float32)
                                      / np.float32(HEAD_DIM))).astype(np.float32)
    ang = (np.arange(S, dtype=np.float32)[:, None] * inv[None, :]).astype(np.float64)
    cos, sin = np.cos(ang).astype(np.float32), np.sin(ang).astype(np.float32)
    return (jnp.asarray(np.concatenate([cos, cos], axis=-1)),
            jnp.asarray(np.concatenate([-sin, sin], axis=-1)))


def kernel(x, norm_mix_w, w_in, w_pool, pool_scale, q_norm_w, k_norm_w, sink_logits, w_out,
           norm_ffn_w, w_group_router, b_group_router, w_expert_router, b_expert_router,
           w_gate, w_up, w_down):
    B, S, D = x.shape
    T = B * S
    depth = w_in.shape[0]
    cos, sin = _rope_tables(S)
    coef_np, invc_np, bias_np = _band_constants(S)
    coef = jnp.asarray(coef_np, BF16)
    invc = jnp.asarray(invc_np, F32)
    bias = jnp.asarray(bias_np, F32)
    pad_lanes = LANES - N_GROUPS - N_EXPERTS

    h = x.reshape(T, D)
    for l in range(depth):
        u, q, k, v = _in_proj(h, norm_mix_w[l].reshape(1, D), w_in[l].astype(BF16),
                              q_norm_w[l].reshape(1, HEAD_DIM), k_norm_w[l].reshape(1, HEAD_DIM),
                              cos, sin, S)
        ab = _mixers(sink_logits[l], u.reshape(B, S, -1), q.reshape(B, S, -1),
                     k.reshape(B, S, -1), v.reshape(B, S, -1), coef, invc, bias,
                     w_pool[l].astype(BF16), pool_scale[l].reshape(1, POOL_WIDTH))
        w_router = jnp.concatenate(
            [w_group_router[l], w_expert_router[l], jnp.zeros((D, pad_lanes), F32)], axis=1)
        b_router = jnp.concatenate(
            [b_group_router[l], b_expert_router[l], jnp.zeros((pad_lanes,), F32)]).reshape(1, LANES)
        hmix, xn_packed, route, counts = _out_proj(
            ab.reshape(T, MIX_WIDTH), h, w_out[l].astype(BF16), norm_ffn_w[l].reshape(1, D),
            w_router.astype(BF16), b_router)
        n_blocks = -(-(T * TOP_K + N_EXPERTS * (MOE_BLOCK - 1)) // MOE_BLOCK)
        dest, tab, etab = _plan(counts[0].astype(jnp.int32), route, n_blocks)
        xs = _dispatch(etab, dest, xn_packed, n_blocks)
        ys = _experts(tab, etab, xs, w_gate[l], w_up[l], w_down[l])
        h = _combine(dest, hmix, route, ys)
    return h.reshape(B, S, D)
```

```python
import functools

import numpy as np
import jax
import jax.numpy as jnp
from jax import lax
from jax.experimental import pallas as pl
from jax.experimental.pallas import tpu as pltpu

D_MODEL = 2048
POOL_WIDTH = 1024
POOL_WINDOWS = (2, 4, 8, 16)
N_POOL_GROUPS = 4
POOL_GROUP = 256
HEAD_DIM = 128
N_Q_HEADS = 8
N_KV_HEADS = 2
Q_PER_KV = 4
ATTN_WIDTH = 1024
KV_WIDTH = 256
IN_PROJ_WIDTH = 2560
MIX_WIDTH = 2048
WINDOW = 128
BLOCK = 128
BAND = 3 * BLOCK
ROPE_THETA = 10000.0
N_GROUPS = 8
EXPERTS_PER_GROUP = 8
N_EXPERTS = 64
TOP_K = 2
D_EXPERT = 512
MOE_BLOCK = 256
EPS = 1e-6

LANES = 128
SUBLANES = 8
HALF = D_MODEL // 2
PACK_CHUNKS = HALF // LANES
NEG_BIG = -1e30
VMEM_LIMIT = 56 * 1024 * 1024

BF16 = jnp.bfloat16
F32 = jnp.float32
U32 = jnp.uint32
HI_MASK = 0xFFFF0000


def _cparams(n_axes):
    return pltpu.CompilerParams(dimension_semantics=("arbitrary",) * n_axes,
                                vmem_limit_bytes=VMEM_LIMIT)


def _pack_pair(lo, hi):
    lo_bits = lax.bitcast_convert_type(lo.astype(BF16).astype(F32), U32) >> 16
    hi_bits = lax.bitcast_convert_type(hi.astype(BF16).astype(F32), U32) & jnp.uint32(HI_MASK)
    return lo_bits | hi_bits


def _unpack_pair(words):
    lo = lax.bitcast_convert_type(words << 16, F32)
    hi = lax.bitcast_convert_type(words & jnp.uint32(HI_MASK), F32)
    return lo, hi


IN_TM = 512
ROW_SPLIT = 2


def _in_proj_kernel(x_ref, nw_ref, w_ref, qnw_ref, knw_ref, cos_ref, sin_ref,
                    u_ref, q_ref, k_ref, v_ref):
    sub = IN_TM // ROW_SPLIT
    for part in range(ROW_SPLIT):
        rows = slice(part * sub, (part + 1) * sub)
        x = x_ref[rows, :]
        ms = jnp.mean(x * x, axis=-1, keepdims=True)
        xn = (x * lax.rsqrt(ms + EPS) * nw_ref[...]).astype(BF16)
        cos = cos_ref[rows, :]
        sin = sin_ref[rows, :]

        def head_norm_rope(t, w, scale):
            hms = jnp.mean(t * t, axis=-1, keepdims=True)
            t = t * lax.rsqrt(hms + EPS) * w
            t = t * cos + pltpu.roll(t, HEAD_DIM // 2, axis=1) * sin
            return t * scale

        o_k = POOL_WIDTH + ATTN_WIDTH
        zq = jnp.dot(xn, w_ref[:, POOL_WIDTH:o_k], preferred_element_type=F32)
        zk = jnp.dot(xn, w_ref[:, o_k:o_k + KV_WIDTH], preferred_element_type=F32)
        qnw = qnw_ref[...]
        for h in range(N_Q_HEADS):
            sl = slice(h * HEAD_DIM, (h + 1) * HEAD_DIM)
            q_ref[rows, sl] = head_norm_rope(zq[:, sl], qnw, HEAD_DIM ** -0.5).astype(BF16)
        knw = knw_ref[...]
        for h in range(N_KV_HEADS):
            sl = slice(h * HEAD_DIM, (h + 1) * HEAD_DIM)
            k_ref[rows, sl] = head_norm_rope(zk[:, sl], knw, 1.0).astype(BF16)
        u_ref[rows, :] = jnp.dot(xn, w_ref[:, :POOL_WIDTH],
                                 preferred_element_type=F32).astype(BF16)
        v_ref[rows, :] = jnp.dot(xn, w_ref[:, o_k + KV_WIDTH:],
                                 preferred_element_type=F32).astype(BF16)


def _in_proj(x2, nw, w_in, qnw, knw, cos, sin, seq):
    T = x2.shape[0]
    tm = IN_TM
    pos_blocks = seq // tm
    full = lambda shape: pl.BlockSpec(shape, lambda i: (0,) * len(shape))
    return pl.pallas_call(
        _in_proj_kernel,
        grid=(T // tm,),
        in_specs=[
            pl.BlockSpec((tm, D_MODEL), lambda i: (i, 0)),
            full((1, D_MODEL)),
            full((D_MODEL, IN_PROJ_WIDTH)),
            full((1, HEAD_DIM)),
            full((1, HEAD_DIM)),
            pl.BlockSpec((tm, HEAD_DIM), lambda i: (i % pos_blocks, 0)),
            pl.BlockSpec((tm, HEAD_DIM), lambda i: (i % pos_blocks, 0)),
        ],
        out_specs=[
            pl.BlockSpec((tm, POOL_WIDTH), lambda i: (i, 0)),
            pl.BlockSpec((tm, ATTN_WIDTH), lambda i: (i, 0)),
            pl.BlockSpec((tm, KV_WIDTH), lambda i: (i, 0)),
            pl.BlockSpec((tm, KV_WIDTH), lambda i: (i, 0)),
        ],
        out_shape=[
            jax.ShapeDtypeStruct((T, POOL_WIDTH), BF16),
            jax.ShapeDtypeStruct((T, ATTN_WIDTH), BF16),
            jax.ShapeDtypeStruct((T, KV_WIDTH), BF16),
            jax.ShapeDtypeStruct((T, KV_WIDTH), BF16),
        ],
        compiler_params=_cparams(1),
        name="in_proj",
    )(x2, nw, w_in, qnw, knw, cos, sin)


MIX_TQ = 512
MIX_STAGE_SUBS = 2


def _band_constants(seq):
    nb = seq // BLOCK
    coef = np.zeros((3, N_POOL_GROUPS, BLOCK, BAND), np.float32)
    inv_count = np.zeros((3, N_POOL_GROUPS, BLOCK, 1), np.float32)
    bias = np.zeros((3, BLOCK, BAND), np.float32)
    for kind, n in enumerate((0, 1, nb - 1)):
        start = min(max((n - 1) * BLOCK, 0), seq - BAND)
        t = n * BLOCK + np.arange(BLOCK)[:, None]
        s = start + np.arange(BAND)[None, :]
        bias[kind] = np.where(np.abs(s - t) <= WINDOW, 0.0, NEG_BIG)
        for g, win in enumerate(POOL_WINDOWS):
            half = win // 2
            lo = np.clip(t - half, 0, seq)
            hi = np.clip(t + half, 0, seq)
            count = (hi - lo).astype(np.float32)
            inside = ((s >= lo) & (s < hi)).astype(np.float32)
            coef[kind, g] = inside - count * (s == t)
            inv_count[kind, g] = 1.0 / count
    return coef, inv_count, bias


def _mixers_kernel(sink_ref, u_ref, q_ref, k_ref, v_ref, coef_ref, invc_ref, bias_ref,
                   wpool_ref, pscale_ref, ab_ref, *, seq):
    nb = seq // BLOCK
    j = pl.program_id(1)
    n_sub = MIX_TQ // BLOCK
    subs = []
    for r in range(n_sub):
        n = j * n_sub + r
        start = pl.multiple_of(jnp.clip((n - 1) * BLOCK, 0, seq - BAND), BLOCK)
        kind = jnp.where(n == 0, 0, jnp.where(n == nb - 1, 2, 1))
        subs.append((start, kind, slice(r * BLOCK, (r + 1) * BLOCK)))

    def staged(group):
        scores = {}
        for r, (start, kind, rows) in enumerate(group):
            for hk in range(N_KV_HEADS):
                kb = k_ref[0, pl.ds(start, BAND), hk * HEAD_DIM:(hk + 1) * HEAD_DIM]
                qs = jnp.concatenate(
                    [q_ref[0, rows, h * HEAD_DIM:(h + 1) * HEAD_DIM]
                     for h in range(hk * Q_PER_KV, (hk + 1) * Q_PER_KV)], axis=0)
                scores[r, hk] = lax.dot_general(qs, kb, (((1,), (1,)), ((), ())),
                                                preferred_element_type=F32)
        diffs = {}
        for r, (start, kind, rows) in enumerate(group):
            for g in range(N_POOL_GROUPS):
                ub = u_ref[0, pl.ds(start, BAND), g * POOL_GROUP:(g + 1) * POOL_GROUP]
                d = jnp.dot(coef_ref[kind, g], ub, preferred_element_type=F32) * invc_ref[kind, g]
                diffs[r, g] = d.astype(BF16)
        probs = {}
        for r, (start, kind, rows) in enumerate(group):
            bias = bias_ref[kind]
            for hk in range(N_KV_HEADS):
                for gi in range(Q_PER_KV):
                    h = hk * Q_PER_KV + gi
                    sh = scores[r, hk][gi * BLOCK:(gi + 1) * BLOCK] + bias
                    sink = sink_ref[h]
                    m = jnp.maximum(jnp.max(sh, axis=-1, keepdims=True), sink)
                    p = jnp.exp(sh - m)
                    denom = jnp.sum(p, axis=-1, keepdims=True) + jnp.exp(sink - m)
                    probs[r, h] = (p.astype(BF16), denom)
        for r, (start, kind, rows) in enumerate(group):
            for g in range(N_POOL_GROUPS):
                cols = slice(g * POOL_GROUP, (g + 1) * POOL_GROUP)
                y = jnp.dot(diffs[r, g], wpool_ref[g], preferred_element_type=F32)
                ab_ref[0, rows, cols] = (y * pscale_ref[:, cols]).astype(BF16)
        for r, (start, kind, rows) in enumerate(group):
            for hk in range(N_KV_HEADS):
                vb = v_ref[0, pl.ds(start, BAND), hk * HEAD_DIM:(hk + 1) * HEAD_DIM]
                for gi in range(Q_PER_KV):
                    h = hk * Q_PER_KV + gi
                    p, denom = probs[r, h]
                    o = jnp.dot(p, vb, preferred_element_type=F32) / denom
                    ocols = slice(POOL_WIDTH + h * HEAD_DIM, POOL_WIDTH + (h + 1) * HEAD_DIM)
                    ab_ref[0, rows, ocols] = o.astype(BF16)

    for first in range(0, n_sub, MIX_STAGE_SUBS):
        staged(subs[first:first + MIX_STAGE_SUBS])


def _mixers(sink, u, q, k, v, coef, invc, bias, wpool, pscale):
    B, S, _ = u.shape
    tq = MIX_TQ
    full = lambda shape: pl.BlockSpec(shape, lambda b, j: (0,) * len(shape))
    return pl.pallas_call(
        functools.partial(_mixers_kernel, seq=S),
        grid=(B, S // tq),
        in_specs=[
            pl.BlockSpec(memory_space=pltpu.SMEM),
            pl.BlockSpec((1, S, POOL_WIDTH), lambda b, j: (b, 0, 0)),
            pl.BlockSpec((1, tq, ATTN_WIDTH), lambda b, j: (b, j, 0)),
            pl.BlockSpec((1, S, KV_WIDTH), lambda b, j: (b, 0, 0)),
            pl.BlockSpec((1, S, KV_WIDTH), lambda b, j: (b, 0, 0)),
            full((3, N_POOL_GROUPS, BLOCK, BAND)),
            full((3, N_POOL_GROUPS, BLOCK, 1)),
            full((3, BLOCK, BAND)),
            full((N_POOL_GROUPS, POOL_GROUP, POOL_GROUP)),
            full((1, POOL_WIDTH)),
        ],
        out_specs=pl.BlockSpec((1, tq, MIX_WIDTH), lambda b, j: (b, j, 0)),
        out_shape=jax.ShapeDtypeStruct((B, S, MIX_WIDTH), BF16),
        compiler_params=_cparams(2),
        name="mixers",
    )(sink, u, q, k, v, coef, invc, bias, wpool, pscale)


OUT_TM = 512
ROUTE_E0, ROUTE_E1, ROUTE_W0, ROUTE_W1 = 0, 1, 2, 3


def _out_proj_kernel(ab_ref, x_ref, w_ref, nw_ref, wr_ref, br_ref,
                     h_ref, xn_ref, route_ref, counts_ref, logits_keep, *, n_tiles):
    i = pl.program_id(0)

    @pl.when(i == 0)
    def _():
        counts_ref[...] = jnp.zeros_like(counts_ref)
        logits_keep[...] = jnp.zeros_like(logits_keep)

    sub = OUT_TM // ROW_SPLIT
    lane = lax.broadcasted_iota(jnp.int32, (sub, LANES), 1)
    prev_slot = (i + 1) % 2
    cur_slot = i % 2

    def first_argmax(vals):
        m = jnp.max(vals, axis=-1, keepdims=True)
        idx = jnp.min(jnp.where(vals == m, lane, LANES), axis=-1, keepdims=True)
        return m, idx

    def project_tile():
        h_parts = []
        for part in range(ROW_SPLIT):
            rows = slice(part * sub, (part + 1) * sub)
            h = x_ref[rows, :] + jnp.dot(ab_ref[rows, :], w_ref[...], preferred_element_type=F32)
            h_ref[rows, :] = h
            h_parts.append(h)
        for part, h in enumerate(h_parts):
            rows = slice(part * sub, (part + 1) * sub)
            ms = jnp.mean(h * h, axis=-1, keepdims=True)
            xn = h * lax.rsqrt(ms + EPS) * nw_ref[...]
            packed = _pack_pair(xn[:, :HALF], xn[:, HALF:])
            for c in range(PACK_CHUNKS):
                xn_ref[pl.ds(part * sub * SUBLANES + c, sub, stride=SUBLANES), :] = (
                    packed[:, c * LANES:(c + 1) * LANES])
            logits_keep[cur_slot, rows, :] = (
                jnp.dot(xn.astype(BF16), wr_ref[...], preferred_element_type=F32) + br_ref[...])

    def route_previous_tile():
        new_counts = jnp.zeros((1, LANES), F32)
        for part in range(ROW_SPLIT):
            new_counts = new_counts + route_rows(slice(part * sub, (part + 1) * sub))
        counts_ref[...] += jnp.where(i > 0, new_counts, 0.0)

    def route_rows(rows):
        logits = logits_keep[prev_slot, rows, :]
        gl = jnp.where(lane < N_GROUPS, logits, NEG_BIG)
        gmax, gidx = first_argmax(gl)
        gsum = jnp.sum(jnp.where(lane < N_GROUPS, jnp.exp(gl - gmax), 0.0),
                       axis=-1, keepdims=True)
        g_w = 1.0 / gsum
        e_lo = N_GROUPS + gidx * EXPERTS_PER_GROUP
        el = jnp.where((lane >= e_lo) & (lane < e_lo + EXPERTS_PER_GROUP), logits, NEG_BIG)
        m1, i1 = first_argmax(el)
        m2, i2 = first_argmax(jnp.where(lane == i1, NEG_BIG, el))
        t = jnp.exp(m2 - m1)
        p1 = 1.0 / (1.0 + t)
        p2 = t * p1
        e1 = i1 - N_GROUPS
        e2 = i2 - N_GROUPS
        route_ref[rows, :] = jnp.where(
            lane == ROUTE_E0, e1.astype(F32),
            jnp.where(lane == ROUTE_E1, e2.astype(F32),
                      jnp.where(lane == ROUTE_W0, g_w * p1,
                                jnp.where(lane == ROUTE_W1, g_w * p2, 0.0))))
        chosen = ((lane == e1) | (lane == e2)).astype(F32)
        return jnp.sum(chosen, axis=0, keepdims=True)

    @pl.when(i < n_tiles)
    def _():
        route_previous_tile()
        project_tile()

    @pl.when(i == n_tiles)
    def _():
        route_previous_tile()


def _out_proj(ab, x2, w_out, nw, w_router, b_router):
    T = x2.shape[0]
    tm = OUT_TM
    n_tiles = T // tm
    full = lambda shape: pl.BlockSpec(shape, lambda i: (0,) * len(shape))
    cur = lambda rows, width: pl.BlockSpec((rows, width),
                                           lambda i: (jnp.minimum(i, n_tiles - 1), 0))
    row = lambda width: cur(tm, width)
    return pl.pallas_call(
        functools.partial(_out_proj_kernel, n_tiles=n_tiles),
        grid=(n_tiles + 1,),
        in_specs=[row(MIX_WIDTH), row(D_MODEL), full((MIX_WIDTH, D_MODEL)), full((1, D_MODEL)),
                  full((D_MODEL, LANES)), full((1, LANES))],
        out_specs=[row(D_MODEL),
                   cur(tm * SUBLANES, LANES),
                   pl.BlockSpec((tm, LANES), lambda i: (jnp.maximum(i - 1, 0), 0)),
                   full((1, LANES))],
        scratch_shapes=[pltpu.VMEM((2, tm, LANES), F32)],
        out_shape=[jax.ShapeDtypeStruct((T, D_MODEL), F32),
                   jax.ShapeDtypeStruct((T * SUBLANES, LANES), U32),
                   jax.ShapeDtypeStruct((T, LANES), F32),
                   jax.ShapeDtypeStruct((1, LANES), F32)],
        compiler_params=_cparams(1),
        name="out_proj",
    )(ab, x2, w_out, nw, w_router, b_router)


PLAN_TM = 1024
SLOT_TILE = 1024


TAB_EXPERT, TAB_FIRST, TAB_BUFFER, TAB_NEXT, TAB_VALID = 0, 1, 2, 3, 4
TAB_ROWS = 5
ETAB_PAD_LO, ETAB_PAD_HI, ETAB_MISC = 0, 1, 2
ETAB_ROWS = 3
MISC_N_USED, MISC_RUN_EXPERT = 0, 1
FETCH_AHEAD = 2
TABLE_ROWS = SUBLANES


def _slot_tables(counts_ref, tab_ref, etab_ref, run_expert, run_of_block, pstart_s, n_blocks):
    log_block = MOE_BLOCK.bit_length() - 1

    def clear(r, carry):
        run_expert[r] = -1
        return carry
    lax.fori_loop(0, run_expert.shape[0], clear, 0)

    def per_expert(e, carry):
        slot, blk, run, buf = carry
        c = counts_ref[e]
        n_blk = lax.shift_right_logical(c + (MOE_BLOCK - 1), log_block)
        pstart_s[e] = slot
        etab_ref[ETAB_PAD_LO, e] = slot + c
        etab_ref[ETAB_PAD_HI, e] = slot + n_blk * MOE_BLOCK

        def per_block(b, inner):
            tab_ref[TAB_EXPERT, b] = e
            tab_ref[TAB_FIRST, b] = (b == blk).astype(jnp.int32)
            tab_ref[TAB_BUFFER, b] = buf
            tab_ref[TAB_VALID, b] = jnp.minimum(c - (b - blk) * MOE_BLOCK, MOE_BLOCK)
            run_of_block[b] = run
            return inner
        lax.fori_loop(blk, blk + n_blk, per_block, 0)

        @pl.when(n_blk > 0)
        def _():
            run_expert[run] = e
        has_run = (n_blk > 0).astype(jnp.int32)
        next_buf = jnp.where(buf + has_run == FETCH_AHEAD, 0, buf + has_run)
        return slot + n_blk * MOE_BLOCK, blk + n_blk, run + has_run, next_buf

    zero = jnp.int32(0)
    _, n_used, _, _ = lax.fori_loop(0, N_EXPERTS, per_expert, (zero, zero, zero, zero))

    def per_used_block(b, carry):
        tab_ref[TAB_NEXT, b] = run_expert[run_of_block[b] + FETCH_AHEAD]
        return carry
    lax.fori_loop(0, n_used, per_used_block, 0)

    def per_unused_block(b, carry):
        for row in (TAB_EXPERT, TAB_FIRST, TAB_BUFFER, TAB_VALID):
            tab_ref[row, b] = 0
        tab_ref[TAB_NEXT, b] = -1
        return carry
    lax.fori_loop(n_used, n_blocks, per_unused_block, 0)

    def clear_misc(e, carry):
        etab_ref[ETAB_MISC, e] = 0
        return carry
    lax.fori_loop(0, N_EXPERTS, clear_misc, 0)
    etab_ref[ETAB_MISC, MISC_N_USED] = n_used
    for r in range(FETCH_AHEAD):
        etab_ref[ETAB_MISC, MISC_RUN_EXPERT + r] = run_expert[r]


def _plan_kernel(counts_ref, route_ref, tri_ref, dest_ref, tab_ref, etab_ref,
                 carry_ref, pstart_ref, run_expert, run_of_block, pstart_s, *, n_blocks):
    @pl.when(pl.program_id(0) == 0)
    def _():
        carry_ref[...] = jnp.zeros_like(carry_ref)
        _slot_tables(counts_ref, tab_ref, etab_ref, run_expert, run_of_block, pstart_s, n_blocks)
        lane_row = lax.broadcasted_iota(jnp.int32, (1, LANES), 1)

        def place(e, row):
            return jnp.where(lane_row == e, pstart_s[e].astype(F32), row)
        pstart_ref[...] = lax.fori_loop(0, N_EXPERTS, place, jnp.zeros((1, LANES), F32))

    route = route_ref[...]
    lane = lax.broadcasted_iota(jnp.int32, route.shape, 1)
    e0 = route[:, ROUTE_E0:ROUTE_E0 + 1].astype(jnp.int32)
    e1 = route[:, ROUTE_E1:ROUTE_E1 + 1].astype(jnp.int32)
    oh0 = lane == e0
    oh1 = lane == e1
    both = (oh0 | oh1).astype(F32)
    earlier = jnp.dot(tri_ref[...], both.astype(BF16), preferred_element_type=F32)
    base = pstart_ref[...] + carry_ref[...] + earlier
    d0 = jnp.sum(jnp.where(oh0, base, 0.0), axis=-1, keepdims=True)
    d1 = jnp.sum(jnp.where(oh1, base, 0.0), axis=-1, keepdims=True)
    by_token = jnp.where(lane == 0, d0, jnp.where(lane == 1, d1, 0.0))
    slots = by_token.T[:TABLE_ROWS].astype(jnp.int32)
    for piece in range(PLAN_TM // SLOT_TILE):
        dest_ref[:, piece, 0, :] = slots[:, piece * SLOT_TILE:(piece + 1) * SLOT_TILE]
    carry_ref[...] += jnp.sum(both, axis=0, keepdims=True)


def _plan(counts, route, n_blocks):
    T = route.shape[0]
    tm = PLAN_TM
    tri = jnp.asarray(np.tril(np.ones((tm, tm), np.float32), -1), BF16)
    smem_out = pl.BlockSpec(memory_space=pltpu.SMEM)
    grid_spec = pltpu.PrefetchScalarGridSpec(
        num_scalar_prefetch=1,
        grid=(T // tm,),
        in_specs=[pl.BlockSpec((tm, LANES), lambda i, c: (i, 0)),
                  pl.BlockSpec((tm, tm), lambda i, c: (0, 0))],
        out_specs=[pl.BlockSpec((TABLE_ROWS, tm // SLOT_TILE, 1, SLOT_TILE),
                                lambda i, c: (0, i, 0, 0)), smem_out, smem_out],
        scratch_shapes=[pltpu.VMEM((1, LANES), F32), pltpu.VMEM((1, LANES), F32),
                        pltpu.SMEM((N_EXPERTS + FETCH_AHEAD,), jnp.int32),
                        pltpu.SMEM((n_blocks,), jnp.int32),
                        pltpu.SMEM((N_EXPERTS,), jnp.int32)],
    )
    return pl.pallas_call(
        functools.partial(_plan_kernel, n_blocks=n_blocks),
        grid_spec=grid_spec,
        out_shape=[jax.ShapeDtypeStruct((TABLE_ROWS, T // SLOT_TILE, 1, SLOT_TILE), jnp.int32),
                   jax.ShapeDtypeStruct((TAB_ROWS, n_blocks), jnp.int32),
                   jax.ShapeDtypeStruct((ETAB_ROWS, N_EXPERTS), jnp.int32)],
        compiler_params=_cparams(1),
        name="plan",
    )(counts, route, tri)


DISP_TM = 1024
DMA_GROUP = 8
ROW_WORDS = SUBLANES


def _packed_row(ref, row):
    return ref.at[pl.ds(pl.multiple_of(row * ROW_WORDS, ROW_WORDS), ROW_WORDS)]


def _dispatch_kernel(etab_ref, dest0_ref, dest1_ref, xn_ref, xs_hbm,
                     zeros, sem, tail_sem, *, experts_per_step, tails_per_step, n_blocks):
    i = pl.program_id(0)

    @pl.when(i == 0)
    def _():
        zeros[...] = jnp.zeros_like(zeros)

    def scatter_group(t, carry):
        base = pl.multiple_of(t * DMA_GROUP, DMA_GROUP)
        window = xn_ref.at[pl.ds(base * ROW_WORDS, DMA_GROUP * ROW_WORDS)]
        for k in range(DMA_GROUP):
            src = window.at[pl.ds(k * ROW_WORDS, ROW_WORDS)]
            for choice, dest_ref in enumerate((dest0_ref, dest1_ref)):
                dst = _packed_row(xs_hbm, dest_ref[0, 0, 0, base + k])
                pltpu.make_async_copy(src, dst, sem).start(priority=choice)
        return carry

    lax.fori_loop(0, DISP_TM // DMA_GROUP, scatter_group, 0)

    def pad_fill(start):
        for j in range(experts_per_step):
            e = jnp.minimum(i * experts_per_step + j, N_EXPERTS - 1)
            live = i * experts_per_step + j < N_EXPERTS
            pos = etab_ref[ETAB_PAD_LO, e]
            n_pad = jnp.where(live, etab_ref[ETAB_PAD_HI, e] - pos, 0)
            for bit in reversed(range(MOE_BLOCK.bit_length() - 1)):
                size = 1 << bit
                take = jnp.bitwise_and(lax.shift_right_logical(n_pad, bit), 1)
                dst = xs_hbm.at[pl.ds(pl.multiple_of(pos * ROW_WORDS, ROW_WORDS), size * ROW_WORDS)]
                cp = pltpu.make_async_copy(zeros.at[pl.ds(0, size * ROW_WORDS)], dst, sem)

                @pl.when(take == 1)
                def _():
                    cp.start() if start else cp.wait()
                pos = pos + take * size

    def tail_copies(start):
        for j in range(tails_per_step):
            blk = etab_ref[ETAB_MISC, MISC_N_USED] + i * tails_per_step + j
            rows = pl.ds(pl.multiple_of(jnp.minimum(blk, n_blocks - 1) * (MOE_BLOCK * ROW_WORDS),
                                        MOE_BLOCK * ROW_WORDS), MOE_BLOCK * ROW_WORDS)
            cp = pltpu.make_async_copy(zeros, xs_hbm.at[rows], tail_sem)

            @pl.when(blk < n_blocks)
            def _():
                cp.start() if start else cp.wait()

    pad_fill(True)
    tail_copies(True)

    tile_rows = DISP_TM * ROW_WORDS
    for _ in range(TOP_K):
        pltpu.make_async_copy(xn_ref, xs_hbm.at[pl.ds(0, tile_rows)], sem).wait()
    pad_fill(False)
    tail_copies(False)


def _slot_spec(choice, index):
    return pl.BlockSpec((1, 1, 1, SLOT_TILE), lambda i, *_: (choice, index(i), 0, 0),
                        memory_space=pltpu.SMEM)


def _dispatch(etab, dest4, xn_packed, n_blocks):
    assert DISP_TM == SLOT_TILE
    n = dest4.shape[1]
    min_used = (n * DISP_TM * TOP_K) // MOE_BLOCK
    tails_per_step = -(-(n_blocks - min_used) // n)
    grid_spec = pltpu.PrefetchScalarGridSpec(
        num_scalar_prefetch=1,
        grid=(n,),
        in_specs=[_slot_spec(0, lambda i: i), _slot_spec(1, lambda i: i),
                  pl.BlockSpec((DISP_TM * ROW_WORDS, LANES), lambda i, *_: (i, 0))],
        out_specs=pl.BlockSpec(memory_space=pl.ANY),
        scratch_shapes=[pltpu.VMEM((MOE_BLOCK * ROW_WORDS, LANES), U32),
                        pltpu.SemaphoreType.DMA(()), pltpu.SemaphoreType.DMA(())],
    )
    return pl.pallas_call(
        functools.partial(_dispatch_kernel, experts_per_step=-(-N_EXPERTS // n),
                          tails_per_step=tails_per_step, n_blocks=n_blocks),
        grid_spec=grid_spec,
        out_shape=jax.ShapeDtypeStruct((n_blocks * MOE_BLOCK * ROW_WORDS, LANES), U32),
        compiler_params=_cparams(1),
        name="dispatch",
    )(etab, dest4, dest4, xn_packed)


CAST_ROWS = 256
FETCH_SPLIT = 4
BLOCKS_PER_STEP = 4


def _experts_kernel(tab_ref, etab_ref,
                    xs_ref, wg_hbm, wu_hbm, wd_hbm, ys_ref,
                    fg, fu, fd, wg, wu, wd, sem):
    n_used = etab_ref[ETAB_MISC, MISC_N_USED]
    for j in range(BLOCKS_PER_STEP):
        rows = pl.ds(j * MOE_BLOCK * SUBLANES, MOE_BLOCK * SUBLANES)
        _expert_block(pl.program_id(0) * BLOCKS_PER_STEP + j, n_used, tab_ref, etab_ref,
                      xs_ref.at[rows], wg_hbm, wu_hbm, wd_hbm, ys_ref.at[rows],
                      fg, fu, fd, wg, wu, wd, sem)


def _expert_block(b, n_used, tab_ref, etab_ref, xs_ref, wg_hbm, wu_hbm, wd_hbm, ys_ref,
                  fg, fu, fd, wg, wu, wd, sem):

    def fetch(e, p):
        copies = []
        for k, (src, dst) in enumerate(((wg_hbm, fg), (wu_hbm, fu), (wd_hbm, fd))):
            rows = src.shape[1] // FETCH_SPLIT
            for c in range(FETCH_SPLIT):
                sl = pl.ds(c * rows, rows)
                copies.append(pltpu.make_async_copy(src.at[e, sl], dst.at[p, sl], sem.at[p, k]))
        return copies

    def start_fetch(e, p):
        for k, cp in enumerate(fetch(e, p)):
            cp.start(priority=k % 2)

    @pl.when(b == 0)
    def _():
        for r in range(FETCH_AHEAD):
            e = etab_ref[ETAB_MISC, MISC_RUN_EXPERT + r]

            @pl.when(e >= 0)
            def _():
                start_fetch(e, r)

    @pl.when((b < n_used) & (tab_ref[TAB_FIRST, b] == 1))
    def _():
        p = tab_ref[TAB_BUFFER, b]
        for cp in fetch(tab_ref[TAB_EXPERT, b], p):
            cp.wait()

        def cast(src, dst, n_rows):
            def body(i, carry):
                rows = pl.ds(pl.multiple_of(i * CAST_ROWS, CAST_ROWS), CAST_ROWS)
                dst[rows, :] = src[p, rows, :].astype(BF16)
                return carry
            lax.fori_loop(0, n_rows // CAST_ROWS, body, 0)

        cast(fg, wg, D_MODEL)
        cast(fu, wu, D_MODEL)
        cast(fd, wd, D_EXPERT)

        @pl.when(tab_ref[TAB_NEXT, b] >= 0)
        def _():
            start_fetch(tab_ref[TAB_NEXT, b], p)

    def swiglu_rows(m_rows):
        lo_parts, hi_parts = [], []
        for c in range(PACK_CHUNKS):
            lo, hi = _unpack_pair(xs_ref[pl.ds(c, m_rows, stride=SUBLANES), :])
            lo_parts.append(lo.astype(BF16))
            hi_parts.append(hi.astype(BF16))
        xb = jnp.concatenate(lo_parts + hi_parts, axis=1)
        g = jnp.dot(xb, wg[...], preferred_element_type=F32)
        u = jnp.dot(xb, wu[...], preferred_element_type=F32)
        hmid = (g * jax.nn.sigmoid(g) * u).astype(BF16)
        y = jnp.dot(hmid, wd[...], preferred_element_type=F32)
        packed = _pack_pair(y[:, :HALF], y[:, HALF:])
        for c in range(PACK_CHUNKS):
            ys_ref[pl.ds(c, m_rows, stride=SUBLANES), :] = packed[:, c * LANES:(c + 1) * LANES]
        if m_rows < MOE_BLOCK:
            rest = pl.ds(m_rows * SUBLANES, (MOE_BLOCK - m_rows) * SUBLANES)
            ys_ref[rest, :] = jnp.zeros(((MOE_BLOCK - m_rows) * SUBLANES, LANES), U32)

    valid = tab_ref[TAB_VALID, b]

    @pl.when((b < n_used) & (valid > MOE_BLOCK // 2))
    def _():
        swiglu_rows(MOE_BLOCK)

    @pl.when((b < n_used) & (valid <= MOE_BLOCK // 2))
    def _():
        swiglu_rows(MOE_BLOCK // 2)

    @pl.when(b >= n_used)
    def _():
        ys_ref[...] = jnp.zeros(ys_ref.shape, ys_ref.dtype)


def _experts(tab, etab, xs, w_gate, w_up, w_down):
    n_blocks = tab.shape[1]
    assert n_blocks % BLOCKS_PER_STEP == 0
    blk = lambda: pl.BlockSpec((BLOCKS_PER_STEP * MOE_BLOCK * SUBLANES, LANES),
                               lambda s, *_: (s, 0))
    grid_spec = pltpu.PrefetchScalarGridSpec(
        num_scalar_prefetch=2,
        grid=(n_blocks // BLOCKS_PER_STEP,),
        in_specs=[blk(),
                  pl.BlockSpec(memory_space=pl.ANY),
                  pl.BlockSpec(memory_space=pl.ANY),
                  pl.BlockSpec(memory_space=pl.ANY)],
        out_specs=blk(),
        scratch_shapes=[pltpu.VMEM((FETCH_AHEAD, D_MODEL, D_EXPERT), F32),
                        pltpu.VMEM((FETCH_AHEAD, D_MODEL, D_EXPERT), F32),
                        pltpu.VMEM((FETCH_AHEAD, D_EXPERT, D_MODEL), F32),
                        pltpu.VMEM((D_MODEL, D_EXPERT), BF16),
                        pltpu.VMEM((D_MODEL, D_EXPERT), BF16),
                        pltpu.VMEM((D_EXPERT, D_MODEL), BF16),
                        pltpu.SemaphoreType.DMA((FETCH_AHEAD, 3))],
    )
    return pl.pallas_call(
        _experts_kernel,
        grid_spec=grid_spec,
        out_shape=jax.ShapeDtypeStruct((n_blocks * MOE_BLOCK * SUBLANES, LANES), U32),
        compiler_params=_cparams(1),
        name="experts",
    )(tab, etab, xs, w_gate, w_up, w_down)


CMB_TM = 256


def _combine_kernel(d0_cur_ref, d1_cur_ref, d0_next_ref, d1_next_ref, h_ref, route_ref, ys_hbm,
                    o_ref, ybuf, sem):
    i = pl.program_id(0)
    n = pl.num_programs(0)
    slot = i % 2
    n_rows = TOP_K * CMB_TM

    def gather(idx_refs, tile, s):
        in_piece = jnp.bitwise_and(tile, SLOT_TILE // CMB_TM - 1) * CMB_TM

        def gather_group(t, carry):
            base = pl.multiple_of(t * DMA_GROUP, DMA_GROUP)
            for choice, idx_ref in enumerate(idx_refs):
                first = (choice * CMB_TM + base) * ROW_WORDS
                window = ybuf.at[s, pl.ds(first, DMA_GROUP * ROW_WORDS)]
                for k in range(DMA_GROUP):
                    row = idx_ref[0, 0, 0, in_piece + base + k]
                    pltpu.make_async_copy(_packed_row(ys_hbm, row),
                                          window.at[pl.ds(k * ROW_WORDS, ROW_WORDS)],
                                          sem.at[s]).start(priority=k % 2)
            return carry
        lax.fori_loop(0, CMB_TM // DMA_GROUP, gather_group, 0)

    @pl.when(i == 0)
    def _():
        gather((d0_cur_ref, d1_cur_ref), i, 0)

    @pl.when(i + 1 < n)
    def _():
        gather((d0_next_ref, d1_next_ref), i + 1, 1 - slot)

    pltpu.make_async_copy(ys_hbm.at[pl.ds(0, n_rows * ROW_WORDS)], ybuf.at[slot],
                          sem.at[slot]).wait()
    route = route_ref[...]
    w0 = route[:, ROUTE_W0:ROUTE_W0 + 1]
    w1 = route[:, ROUTE_W1:ROUTE_W1 + 1]
    yb = ybuf.at[slot]
    for c in range(PACK_CHUNKS):
        lo0, hi0 = _unpack_pair(yb[pl.ds(c, CMB_TM, stride=SUBLANES), :])
        lo1, hi1 = _unpack_pair(yb[pl.ds(CMB_TM * SUBLANES + c, CMB_TM, stride=SUBLANES), :])
        lo_cols = slice(c * LANES, (c + 1) * LANES)
        hi_cols = slice(HALF + c * LANES, HALF + (c + 1) * LANES)
        o_ref[:, lo_cols] = h_ref[:, lo_cols] + (w0 * lo0 + w1 * lo1)
        o_ref[:, hi_cols] = h_ref[:, hi_cols] + (w0 * hi0 + w1 * hi1)


def _combine(dest4, h, route, ys):
    T = h.shape[0]
    tm = CMB_TM
    per_piece = SLOT_TILE // tm
    assert per_piece & (per_piece - 1) == 0
    n = T // tm
    cur = lambda i: i // per_piece
    nxt = lambda i: jnp.minimum(i + 1, n - 1) // per_piece
    return pl.pallas_call(
        _combine_kernel,
        grid=(n,),
        in_specs=[
            _slot_spec(0, cur), _slot_spec(1, cur),
            _slot_spec(0, nxt), _slot_spec(1, nxt),
            pl.BlockSpec((tm, D_MODEL), lambda i: (i, 0)),
            pl.BlockSpec((tm, LANES), lambda i: (i, 0)),
            pl.BlockSpec(memory_space=pl.ANY),
        ],
        out_specs=pl.BlockSpec((tm, D_MODEL), lambda i: (i, 0)),
        out_shape=jax.ShapeDtypeStruct((T, D_MODEL), F32),
        scratch_shapes=[pltpu.VMEM((2, TOP_K * tm * SUBLANES, LANES), U32),
                        pltpu.SemaphoreType.DMA((2,))],
        compiler_params=_cparams(1),
        name="combine",
    )(dest4, dest4, dest4, dest4, h, route, ys)


def _rope_tables(S):
    inv = (np.float32(ROPE_THETA) ** (-np.arange(0, HEAD_DIM, 2, dtype=np.float32)
                                      / np.float32(HEAD_DIM))).astype(np.float32)
    ang = (np.arange(S, dtype=np.float32)[:, None] * inv[None, :]).astype(np.float64)
    cos, sin = np.cos(ang).astype(np.float32), np.sin(ang).astype(np.float32)
    return (jnp.asarray(np.concatenate([cos, cos], axis=-1)),
            jnp.asarray(np.concatenate([-sin, sin], axis=-1)))


def kernel(x, norm_mix_w, w_in, w_pool, pool_scale, q_norm_w, k_norm_w, sink_logits, w_out,
           norm_ffn_w, w_group_router, b_group_router, w_expert_router, b_expert_router,
           w_gate, w_up, w_down):
    B, S, D = x.shape
    T = B * S
    depth = w_in.shape[0]
    cos, sin = _rope_tables(S)
    coef_np, invc_np, bias_np = _band_constants(S)
    coef = jnp.asarray(coef_np, BF16)
    invc = jnp.asarray(invc_np, F32)
    bias = jnp.asarray(bias_np, F32)
    pad_lanes = LANES - N_GROUPS - N_EXPERTS

    h = x.reshape(T, D)
    for l in range(depth):
        u, q, k, v = _in_proj(h, norm_mix_w[l].reshape(1, D), w_in[l].astype(BF16),
                              q_norm_w[l].reshape(1, HEAD_DIM), k_norm_w[l].reshape(1, HEAD_DIM),
                              cos, sin, S)
        ab = _mixers(sink_logits[l], u.reshape(B, S, -1), q.reshape(B, S, -1),
                     k.reshape(B, S, -1), v.reshape(B, S, -1), coef, invc, bias,
                     w_pool[l].astype(BF16), pool_scale[l].reshape(1, POOL_WIDTH))
        w_router = jnp.concatenate(
            [w_group_router[l], w_expert_router[l], jnp.zeros((D, pad_lanes), F32)], axis=1)
        b_router = jnp.concatenate(
            [b_group_router[l], b_expert_router[l], jnp.zeros((pad_lanes,), F32)]).reshape(1, LANES)
        hmix, xn_packed, route, counts = _out_proj(
            ab.reshape(T, MIX_WIDTH), h, w_out[l].astype(BF16), norm_ffn_w[l].reshape(1, D),
            w_router.astype(BF16), b_router)
        n_blocks = -(-(T * TOP_K + N_EXPERTS * (MOE_BLOCK - 1)) // MOE_BLOCK)
        dest, tab, etab = _plan(counts[0].astype(jnp.int32), route, n_blocks)
        xs = _dispatch(etab, dest, xn_packed, n_blocks)
        ys = _experts(tab, etab, xs, w_gate[l], w_up[l], w_down[l])
        h = _combine(dest, hmix, route, ys)
    return h.reshape(B, S, D)
```
